```python
import math
import jax, jax.numpy as jnp
from jax import lax
import numpy as np

D_MODEL = 2048
BATCH = 1
SEQ = 8192
DEPTH = 2

D_MIX = D_MODEL
SSD_HEADS = 16
SSD_HEAD_DIM = 64
SSD_WIDTH = SSD_HEADS * SSD_HEAD_DIM
SSD_GROUPS = 2
SSD_STATE = 128
SSD_CONV = 4
SSD_CHUNK = 256
SSD_XBC = SSD_WIDTH + 2 * SSD_GROUPS * SSD_STATE
MOBA_HEADS = 8
MOBA_HEAD_DIM = 64
MOBA_WIDTH = MOBA_HEADS * MOBA_HEAD_DIM
MOBA_BLOCK = 256
MOBA_TOPK = 3
DIFF_HEADS = 4
DIFF_QK_DIM = 64
DIFF_V_DIM = 2 * DIFF_QK_DIM
DIFF_QK_WIDTH = DIFF_HEADS * 2 * DIFF_QK_DIM
DIFF_WIDTH = DIFF_HEADS * DIFF_V_DIM
ATTN_HEADS = MOBA_HEADS + DIFF_HEADS
Q_BLOCK = 128
REL_BUCKETS = 32
REL_MAX_DIST = 128
D_FF_DENSE = 5632
N_EXPERTS = 8
TOP_K = 2
D_FF_EXPERT = 7168
N_DENSE = (DEPTH + 1) // 2
N_MOE = DEPTH // 2
EPS = 1e-6
IN_SPLITS = (SSD_WIDTH, SSD_XBC, SSD_HEADS, MOBA_WIDTH, MOBA_WIDTH, MOBA_WIDTH, DIFF_QK_WIDTH, DIFF_QK_WIDTH, DIFF_WIDTH)
D_IN_PROJ = sum(IN_SPLITS)

kernel_name = 'hybrid_ssd_moba_diffattn_adaln_moe'


def rmsnorm(x, g):
    xf = x.astype(jnp.float32)
    y = xf * lax.rsqrt(jnp.mean(xf * xf, axis=-1, keepdims=True) + EPS)
    return (y * g.astype(jnp.float32)).astype(x.dtype)


def pad_seq(u, multiple, mode='constant'):
    extra = (-u.shape[1]) % multiple
    if extra == 0:
        return u
    widths = [(0, 0)] * u.ndim
    widths[1] = (0, extra)
    return jnp.pad(u, widths, mode=mode)


def rel_bucket(dist):
    n = jnp.maximum(dist, 0)
    max_exact = REL_BUCKETS // 2
    nf = jnp.maximum(n, 1).astype(jnp.float32)
    large = max_exact + (jnp.log(nf / max_exact) / math.log(REL_MAX_DIST / max_exact)
                         * (REL_BUCKETS - max_exact)).astype(jnp.int32)
    large = jnp.minimum(large, REL_BUCKETS - 1)
    return jnp.where(n < max_exact, n, large)


def ssd_mixer(z, xbc, dt, conv_w, conv_b, dt_bias, a_log, d_skip, norm_g):
    f32 = jnp.float32
    L = z.shape[1]
    z, xbc, dt = (pad_seq(u, SSD_CHUNK) for u in (z, xbc, dt))
    B_, Lp, _ = z.shape
    nc = Lp // SSD_CHUNK
    R = SSD_HEADS // SSD_GROUPS
    xbc = lax.conv_general_dilated(xbc, conv_w[:, None, :].astype(xbc.dtype), (1,), [(SSD_CONV - 1, 0)],
                                   dimension_numbers=('NWC', 'WIO', 'NWC'), feature_group_count=SSD_XBC)
    xbc = jax.nn.silu(xbc + conv_b)
    xs, bm, cm = jnp.split(xbc.astype(f32), [SSD_WIDTH, SSD_WIDTH + SSD_GROUPS * SSD_STATE], axis=-1)
    xs = xs.reshape(B_, nc, SSD_CHUNK, SSD_GROUPS, R, SSD_HEAD_DIM)
    bm = bm.reshape(B_, nc, SSD_CHUNK, SSD_GROUPS, SSD_STATE)
    cm = cm.reshape(B_, nc, SSD_CHUNK, SSD_GROUPS, SSD_STATE)
    dt = jax.nn.softplus(dt.astype(f32) + dt_bias.astype(f32)).reshape(B_, nc, SSD_CHUNK, SSD_GROUPS, R)
    a = -jnp.exp(a_log.astype(f32)).reshape(SSD_GROUPS, R)
    a_cs = jnp.cumsum(jnp.moveaxis(dt * a, 2, -1), axis=-1)
    xdt = xs * dt[..., None]
    tril = jnp.tril(jnp.ones((SSD_CHUNK, SSD_CHUNK), dtype=bool))
    decay = jnp.exp(jnp.where(tril, a_cs[..., :, None] - a_cs[..., None, :], -jnp.inf))
    cb = jnp.einsum('bclgn,bcsgn->bcgls', cm, bm)
    y_diag = jnp.einsum('bcgrls,bcsgrp->bclgrp', cb[:, :, :, None] * decay, xdt)
    to_end = jnp.exp(a_cs[..., -1:] - a_cs)
    chunk_states = jnp.einsum('bclgn,bcgrl,bclgrp->bcgrpn', bm, to_end, xdt)
    chunk_decay = jnp.exp(a_cs[..., -1])

    def carry_state(h, inp):
        s_c, d_c = inp
        return h * d_c[..., None, None] + s_c, h

    h0 = jnp.zeros((B_, SSD_GROUPS, R, SSD_HEAD_DIM, SSD_STATE), f32)
    _, h_prev = lax.scan(carry_state, h0, (jnp.moveaxis(chunk_states, 1, 0), jnp.moveaxis(chunk_decay, 1, 0)))
    h_prev = jnp.moveaxis(h_prev, 0, 1)
    y_off = jnp.einsum('bclgn,bcgrpn,bcgrl->bclgrp', cm, h_prev, jnp.exp(a_cs))
    y = y_diag + y_off + xs * d_skip.astype(f32).reshape(SSD_GROUPS, R)[:, :, None]
    y = y.reshape(B_, Lp, SSD_WIDTH) * jax.nn.silu(z.astype(f32))
    y = y.reshape(B_, Lp, SSD_GROUPS, SSD_WIDTH // SSD_GROUPS)
    y = y * lax.rsqrt(jnp.mean(y * y, axis=-1, keepdims=True) + EPS)
    y = y.reshape(B_, Lp, SSD_WIDTH) * norm_g.astype(f32)
    return y[:, :L].astype(z.dtype)


def moba_mixer(q, k, v, positions, bias_tbl, norm_g):
    f32 = jnp.float32
    L = q.shape[1]
    q, k, v = (pad_seq(u, MOBA_BLOCK) for u in (q, k, v))
    pos = pad_seq(positions, MOBA_BLOCK, mode='edge')
    B_, Lp = q.shape[:2]
    H, S, Dh = MOBA_HEADS, MOBA_BLOCK, MOBA_HEAD_DIM
    nb = Lp // S
    k_sel = min(MOBA_TOPK, nb)
    scale = Dh ** -0.5
    qh = jnp.transpose(q, (0, 2, 1, 3))
    kb = jnp.transpose(k, (0, 2, 1, 3)).reshape(B_, H, nb, S, Dh)
    vb = jnp.transpose(v, (0, 2, 1, 3)).reshape(B_, H, nb, S, Dh)
    k_mean = jnp.mean(kb.astype(f32), axis=3)
    pos_b = pos.reshape(B_, nb, S)
    bi = jnp.arange(B_)[:, None, None, None]
    hi = jnp.arange(H)[None, :, None, None]
    hb = hi[..., None]
    blk = jnp.arange(nb)
    offs_q = jnp.arange(Q_BLOCK)
    offs_k = jnp.arange(S)

    def query_block(qi):
        start = qi * Q_BLOCK
        own = start // S
        qb = lax.dynamic_slice_in_dim(qh, start, Q_BLOCK, axis=2)
        qpos = lax.dynamic_slice_in_dim(pos, start, Q_BLOCK, axis=1)
        gate = jnp.einsum('bhqd,bhjd->bhqj', qb.astype(f32), k_mean)
        gate = jnp.where(blk < own, gate, -jnp.inf)
        _, idx = lax.top_k(gate, k_sel)
        valid = idx < own
        k_g = kb[bi, hi, idx]
        v_g = vb[bi, hi, idx]
        kpos = pos_b[bi, idx]
        s_past = jnp.einsum('bhqd,bhqksd->bhqks', qb, k_g).astype(f32) * scale
        s_past = s_past + bias_tbl[hb, rel_bucket(qpos[:, None, :, None, None] - kpos)].astype(f32)
        s_past = jnp.where(valid[..., None], s_past, -jnp.inf)
        k_own = lax.dynamic_index_in_dim(kb, own, axis=2, keepdims=False)
        v_own = lax.dynamic_index_in_dim(vb, own, axis=2, keepdims=False)
        kpos_own = lax.dynamic_index_in_dim(pos_b, own, axis=1, keepdims=False)
        s_own = jnp.einsum('bhqd,bhsd->bhqs', qb, k_own).astype(f32) * scale
        bias_own = bias_tbl[:, rel_bucket(qpos[:, :, None] - kpos_own[:, None, :])]
        s_own = s_own + jnp.transpose(bias_own, (1, 0, 2, 3)).astype(f32)
        causal = (own * S + offs_k)[None, :] <= (start + offs_q)[:, None]
        s_own = jnp.where(causal, s_own, -jnp.inf)
        logits = jnp.concatenate([s_past.reshape(B_, H, Q_BLOCK, k_sel * S), s_own], axis=-1)
        p = jax.nn.softmax(logits, axis=-1).astype(v.dtype)
        p_past = p[..., : k_sel * S].reshape(B_, H, Q_BLOCK, k_sel, S)
        p_own = p[..., k_sel * S:]
        return (jnp.einsum('bhqks,bhqksd->bhqd', p_past, v_g)
                + jnp.einsum('bhqs,bhsd->bhqd', p_own, v_own))

    out = lax.map(query_block, jnp.arange(Lp // Q_BLOCK))
    out = jnp.transpose(out, (1, 0, 3, 2, 4)).reshape(B_, Lp, H, Dh)[:, :L]
    return rmsnorm(out, norm_g.reshape(H, Dh)).reshape(B_, L, MOBA_WIDTH)


def diff_mixer(q, k, v, positions, bias_tbl, lam_params, subln_g, lambda_init):
    f32 = jnp.float32
    B_, L = q.shape[:2]
    scale = DIFF_QK_DIM ** -0.5
    lp = lam_params.astype(f32)
    lam = jnp.exp(jnp.sum(lp[0] * lp[1])) - jnp.exp(jnp.sum(lp[2] * lp[3])) + lambda_init
    qh = jnp.transpose(q, (0, 2, 3, 1, 4))
    kh = jnp.transpose(k, (0, 2, 3, 1, 4))
    vh = jnp.transpose(v, (0, 2, 1, 3))
    offs_q = jnp.arange(Q_BLOCK)
    key_idx = jnp.arange(L)

    def query_block(qi):
        start = qi * Q_BLOCK
        qb = lax.dynamic_slice_in_dim(qh, start, Q_BLOCK, axis=3)
        qpos = lax.dynamic_slice_in_dim(positions, start, Q_BLOCK, axis=1)
        s = jnp.einsum('bhmqd,bhmkd->bhmqk', qb, kh).astype(f32) * scale
        bias = bias_tbl[:, rel_bucket(qpos[:, :, None] - positions[:, None, :])]
        s = s + jnp.transpose(bias, (1, 0, 2, 3))[:, :, None].astype(f32)
        causal = key_idx[None, :] <= (start + offs_q)[:, None]
        p = jax.nn.softmax(jnp.where(causal, s, -jnp.inf), axis=-1)
        attn = p[:, :, 0] - lam * p[:, :, 1]
        return jnp.einsum('bhqk,bhkd->bhqd', attn.astype(v.dtype), vh)

    out = lax.map(query_block, jnp.arange(L // Q_BLOCK))
    out = jnp.transpose(out, (1, 0, 3, 2, 4)).reshape(B_, L, DIFF_HEADS, DIFF_V_DIM)
    out = rmsnorm(out, subln_g) * (1.0 - lambda_init)
    return out.reshape(B_, L, DIFF_WIDTH)


def token_mixers(h, positions, rel_bias, w_in, conv_w, conv_b, dt_bias, a_log, d_skip, ssd_norm_g,
                 moba_norm_g, diff_lambda, diff_subln_g, w_out, lambda_init):
    B_, L, _ = h.shape
    proj = jnp.einsum('bld,de->ble', h, w_in)
    points = np.cumsum(IN_SPLITS)[:-1].tolist()
    z, xbc, dt, mq, mk, mv, dq, dk, dv = jnp.split(proj, points, axis=-1)
    y_ssd = ssd_mixer(z, xbc, dt, conv_w, conv_b, dt_bias, a_log, d_skip, ssd_norm_g)
    mshape = (B_, L, MOBA_HEADS, MOBA_HEAD_DIM)
    y_moba = moba_mixer(mq.reshape(mshape), mk.reshape(mshape), mv.reshape(mshape), positions,
                        rel_bias[:, :MOBA_HEADS].T, moba_norm_g)
    dshape = (B_, L, DIFF_HEADS, 2, DIFF_QK_DIM)
    y_diff = diff_mixer(dq.reshape(dshape), dk.reshape(dshape), dv.reshape(B_, L, DIFF_HEADS, DIFF_V_DIM),
                        positions, rel_bias[:, MOBA_HEADS:].T, diff_lambda, diff_subln_g, lambda_init)
    y = jnp.concatenate([y_ssd, y_moba, y_diff], axis=-1)
    return jnp.einsum('ble,ed->bld', y, w_out)


def swiglu(h, w1, w3, w2):
    a = jnp.einsum('bld,df->blf', h, w1)
    b = jnp.einsum('bld,df->blf', h, w3)
    return jnp.einsum('blf,fd->bld', jax.nn.silu(a) * b, w2)


def moe_ffn(h, router_w, w1, w3, w2):
    logits = jnp.einsum('bld,de->ble', h, router_w).astype(jnp.float32)
    top_v, top_i = lax.top_k(logits, TOP_K)
    top_w = jax.nn.softmax(top_v, axis=-1)
    comb = jnp.sum(jax.nn.one_hot(top_i, N_EXPERTS, dtype=jnp.float32) * top_w[..., None], axis=-2)
    out = jnp.zeros_like(h)
    for e in range(N_EXPERTS):
        out = out + comb[..., e:e + 1].astype(h.dtype) * swiglu(h, w1[e], w3[e], w2[e])
    return out


def setup_inputs(seed: int = 0) -> dict:
    key = jax.random.key(seed)
    ks = jax.random.split(key, 32)
    f32 = jnp.float32

    def normal(k, shape, scale):
        return jax.random.normal(k, shape, f32) * scale

    def gain(k, shape):
        return 1.0 + 0.02 * jax.random.normal(k, shape, f32)

    x = normal(ks[0], (BATCH, SEQ, D_MODEL), 1.0)
    c = normal(ks[1], (BATCH, D_MODEL), 1.0)
    positions = jnp.broadcast_to(jnp.arange(SEQ, dtype=jnp.int32), (BATCH, SEQ))
    rel_bias = normal(ks[2], (REL_BUCKETS, ATTN_HEADS), 0.5)
    w_ada = normal(ks[3], (DEPTH, D_MODEL, 6 * D_MODEL), D_MODEL ** -0.5)
    b_ada = normal(ks[4], (DEPTH, 6 * D_MODEL), 0.02)
    norm_mix_g = gain(ks[5], (DEPTH, D_MODEL))
    w_in = normal(ks[6], (DEPTH, D_MODEL, D_IN_PROJ), D_MODEL ** -0.5)
    conv_w = normal(ks[7], (DEPTH, SSD_CONV, SSD_XBC), SSD_CONV ** -0.5)
    conv_b = normal(ks[8], (DEPTH, SSD_XBC), 0.02)
    u = jax.random.uniform(ks[9], (DEPTH, SSD_HEADS), f32)
    dt0 = jnp.exp(u * (math.log(0.1) - math.log(0.001)) + math.log(0.001))
    dt_bias = dt0 + jnp.log(-jnp.expm1(-dt0))
    a_log = jnp.log(jax.random.uniform(ks[10], (DEPTH, SSD_HEADS), f32, 1.0, 16.0))
    d_skip = gain(ks[11], (DEPTH, SSD_HEADS))
    ssd_norm_g = gain(ks[12], (DEPTH, SSD_WIDTH))
    moba_norm_g = gain(ks[13], (DEPTH, MOBA_WIDTH))
    diff_lambda = normal(ks[14], (DEPTH, 4, DIFF_QK_DIM), 0.1)
    diff_subln_g = gain(ks[15], (DEPTH, DIFF_V_DIM))
    w_out = normal(ks[16], (DEPTH, D_MIX, D_MODEL), D_MIX ** -0.5)
    norm_ffn_g = gain(ks[17], (DEPTH, D_MODEL))
    dense_w1 = normal(ks[18], (N_DENSE, D_MODEL, D_FF_DENSE), D_MODEL ** -0.5)
    dense_w3 = normal(ks[19], (N_DENSE, D_MODEL, D_FF_DENSE), D_MODEL ** -0.5)
    dense_w2 = normal(ks[20], (N_DENSE, D_FF_DENSE, D_MODEL), D_FF_DENSE ** -0.5)
    router_w = normal(ks[21], (N_MOE, D_MODEL, N_EXPERTS), D_MODEL ** -0.5)
    expert_w1 = normal(ks[22], (N_MOE, N_EXPERTS, D_MODEL, D_FF_EXPERT), D_MODEL ** -0.5)
    expert_w3 = normal(ks[23], (N_MOE, N_EXPERTS, D_MODEL, D_FF_EXPERT), D_MODEL ** -0.5)
    expert_w2 = normal(ks[24], (N_MOE, N_EXPERTS, D_FF_EXPERT, D_MODEL), D_FF_EXPERT ** -0.5)
    final_g = gain(ks[25], (D_MODEL,))
    return {'x': x, 'c': c, 'positions': positions, 'rel_bias': rel_bias, 'w_ada': w_ada, 'b_ada': b_ada,
            'norm_mix_g': norm_mix_g, 'w_in': w_in, 'conv_w': conv_w, 'conv_b': conv_b, 'dt_bias': dt_bias,
            'a_log': a_log, 'd_skip': d_skip, 'ssd_norm_g': ssd_norm_g, 'moba_norm_g': moba_norm_g,
            'diff_lambda': diff_lambda, 'diff_subln_g': diff_subln_g, 'w_out': w_out, 'norm_ffn_g': norm_ffn_g,
            'dense_w1': dense_w1, 'dense_w3': dense_w3, 'dense_w2': dense_w2, 'router_w': router_w,
            'expert_w1': expert_w1, 'expert_w3': expert_w3, 'expert_w2': expert_w2, 'final_g': final_g}


def reference(x, c, positions, rel_bias, w_ada, b_ada, norm_mix_g, w_in, conv_w, conv_b, dt_bias, a_log,
              d_skip, ssd_norm_g, moba_norm_g, diff_lambda, diff_subln_g, w_out, norm_ffn_g, dense_w1,
              dense_w3, dense_w2, router_w, expert_w1, expert_w3, expert_w2, final_g):
    c_act = jax.nn.silu(c)
    for layer in range(DEPTH):
        lambda_init = 0.8 - 0.6 * math.exp(-0.3 * layer)
        mod = jnp.einsum('bd,de->be', c_act, w_ada[layer]) + b_ada[layer]
        shift1, scale1, gate1, shift2, scale2, gate2 = jnp.split(mod[:, None, :], 6, axis=-1)
        h = rmsnorm(x, norm_mix_g[layer]) * (1.0 + scale1) + shift1
        y = token_mixers(h, positions, rel_bias, w_in[layer], conv_w[layer], conv_b[layer], dt_bias[layer],
                         a_log[layer], d_skip[layer], ssd_norm_g[layer], moba_norm_g[layer],
                         diff_lambda[layer], diff_subln_g[layer], w_out[layer], lambda_init)
        x = x + gate1 * y
        h = rmsnorm(x, norm_ffn_g[layer]) * (1.0 + scale2) + shift2
        i = layer // 2
        if layer % 2 == 0:
            y = swiglu(h, dense_w1[i], dense_w3[i], dense_w2[i])
        else:
            y = moe_ffn(h, router_w[i], expert_w1[i], expert_w3[i], expert_w2[i])
        x = x + gate2 * y
    return rmsnorm(x, final_g)
```

```python
import functools
import math

import jax
import jax.numpy as jnp
from jax import lax
from jax.experimental import pallas as pl
from jax.experimental.pallas import tpu as pltpu

F32 = jnp.float32
BF16 = jnp.bfloat16
HIGHEST = lax.Precision.HIGHEST

D_MODEL = 2048
DEPTH = 2
SSD_HEADS = 16
SSD_HEAD_DIM = 64
SSD_WIDTH = SSD_HEADS * SSD_HEAD_DIM
SSD_GROUPS = 2
SSD_STATE = 128
SSD_CONV = 4
SSD_CHUNK = 256
SSD_BC = 2 * SSD_GROUPS * SSD_STATE
MOBA_HEADS = 8
MOBA_HEAD_DIM = 64
MOBA_WIDTH = MOBA_HEADS * MOBA_HEAD_DIM
MOBA_BLOCK = 256
MOBA_TOPK = 3
DIFF_HEADS = 4
DIFF_QK_DIM = 64
DIFF_V_DIM = 128
DIFF_WIDTH = DIFF_HEADS * DIFF_V_DIM
REL_BUCKETS = 32
REL_MAX_DIST = 128
D_FF_DENSE = 5632
N_EXPERTS = 8
D_FF_EXPERT = 7168
EPS = 1e-6

LANES = 128
SUBLANES = 8
VMEM_LIMIT = 56 * 1024 * 1024

COL_Z = 0
COL_X = SSD_WIDTH
COL_BC = COL_X + SSD_WIDTH
COL_MQ = COL_BC + SSD_BC
COL_MK = COL_MQ + MOBA_WIDTH
COL_MV = COL_MK + MOBA_WIDTH
COL_DQ = COL_MV + MOBA_WIDTH
COL_DK = COL_DQ + DIFF_WIDTH
COL_DV = COL_DK + DIFF_WIDTH
COL_DT = COL_DV + DIFF_WIDTH
PROJ_W = COL_DT + LANES
ORIG_DT = SSD_WIDTH + SSD_WIDTH + SSD_BC

ATT_T = 256
TOK_T = 256
MOE_TM = 256
NEG_BIG = -1e9


def _silu(x):
    return x * (1.0 / (1.0 + jnp.exp(-x)))


def _softplus(x):
    return jnp.maximum(x, 0.0) + jnp.log1p(jnp.exp(-jnp.abs(x)))


def _params(*sem):
    return pltpu.CompilerParams(dimension_semantics=sem, vmem_limit_bytes=VMEM_LIMIT)


def _ada_kernel(c_ref, w_ref, b_ref, o_ref):
    ca = _silu(c_ref[...])
    o_ref[0] = jnp.sum(ca * w_ref[0], axis=0, keepdims=True) + b_ref[0]


def ada_modulation(c, w_ada, b_ada):
    depth, d, n = w_ada.shape
    tn = 1024
    return pl.pallas_call(
        _ada_kernel,
        grid=(depth, n // tn),
        in_specs=[pl.BlockSpec((d, 1), lambda l, j: (0, 0)),
                  pl.BlockSpec((1, d, tn), lambda l, j: (l, 0, j)),
                  pl.BlockSpec((1, 1, tn), lambda l, j: (l, 0, j))],
        out_specs=pl.BlockSpec((1, 1, tn), lambda l, j: (l, 0, j)),
        out_shape=jax.ShapeDtypeStruct((depth, 1, n), F32),
        compiler_params=_params("parallel", "parallel"),
        name="ada_modulation",
    )(c.reshape(d, 1), w_ada, b_ada.reshape(depth, 1, n))


def _norm_mod(x, g, sc, sh):
    ms = jnp.mean(x * x, axis=-1, keepdims=True)
    return (x * lax.rsqrt(ms + EPS) * g) * (1.0 + sc) + sh


def _norm_kernel(x_ref, g_ref, sc_ref, sh_ref, o_ref):
    o_ref[...] = _norm_mod(x_ref[...], g_ref[...], sc_ref[...], sh_ref[...]).astype(o_ref.dtype)


def norm_modulate(x, g, sc, sh, out_dtype):
    L, d = x.shape
    tm = 512
    vec = pl.BlockSpec((1, d), lambda i: (0, 0))
    return pl.pallas_call(
        _norm_kernel,
        grid=(L // tm,),
        in_specs=[pl.BlockSpec((tm, d), lambda i: (i, 0)), vec, vec, vec],
        out_specs=pl.BlockSpec((tm, d), lambda i: (i, 0)),
        out_shape=jax.ShapeDtypeStruct((L, d), out_dtype),
        compiler_params=_params("parallel"),
        name="norm_modulate",
    )(x, g, sc, sh)


def _inproj_kernel(a_ref, w_ref, o32_ref, o16_ref):
    acc = jnp.dot(a_ref[...], w_ref[...], preferred_element_type=F32)
    o32_ref[...] = acc
    o16_ref[...] = acc.astype(BF16)


def in_projection(h, w):
    L, k = h.shape
    n = w.shape[1]
    tm, tn = 512, 1920
    return pl.pallas_call(
        _inproj_kernel,
        grid=(n // tn, L // tm),
        in_specs=[pl.BlockSpec((tm, k), lambda j, i: (i, 0)),
                  pl.BlockSpec((k, tn), lambda j, i: (0, j))],
        out_specs=[pl.BlockSpec((tm, tn), lambda j, i: (i, j)),
                   pl.BlockSpec((tm, tn), lambda j, i: (i, j))],
        out_shape=[jax.ShapeDtypeStruct((L, n), F32), jax.ShapeDtypeStruct((L, n), BF16)],
        compiler_params=_params("parallel", "parallel"),
        name="in_projection",
    )(h, w)


def _outproj_kernel(ys_ref, ym_ref, yd_ref, ws_ref, wm_ref, wd_ref, x_ref, gate_ref, o_ref):
    acc = jnp.dot(ys_ref[...], ws_ref[...], preferred_element_type=F32)
    acc += jnp.dot(ym_ref[...], wm_ref[...], preferred_element_type=F32)
    acc += jnp.dot(yd_ref[...], wd_ref[...], preferred_element_type=F32)
    o_ref[...] = x_ref[...] + gate_ref[...] * acc


def out_projection(y_ssd, y_moba, y_diff, w_out, x, gate):
    L, d = x.shape
    tm, tn = 512, 1024
    return pl.pallas_call(
        _outproj_kernel,
        grid=(d // tn, L // tm),
        in_specs=[pl.BlockSpec((tm, SSD_WIDTH), lambda j, i: (i, 0)),
                  pl.BlockSpec((tm, MOBA_WIDTH), lambda j, i: (i, 0)),
                  pl.BlockSpec((tm, DIFF_WIDTH), lambda j, i: (i, 0)),
                  pl.BlockSpec((SSD_WIDTH, tn), lambda j, i: (0, j)),
                  pl.BlockSpec((MOBA_WIDTH, tn), lambda j, i: (SSD_WIDTH // MOBA_WIDTH, j)),
                  pl.BlockSpec((DIFF_WIDTH, tn), lambda j, i: ((SSD_WIDTH + MOBA_WIDTH) // DIFF_WIDTH, j)),
                  pl.BlockSpec((tm, tn), lambda j, i: (i, j)),
                  pl.BlockSpec((1, tn), lambda j, i: (0, j))],
        out_specs=pl.BlockSpec((tm, tn), lambda j, i: (i, j)),
        out_shape=jax.ShapeDtypeStruct((L, d), F32),
        compiler_params=_params("parallel", "parallel"),
        name="out_projection",
    )(y_ssd, y_moba, y_diff, w_out, w_out, w_out, x, gate)


def _swiglu_up_kernel(te_ref, tv_ref, a_ref, w1_ref, w3_ref, o_ref):
    t = pl.program_id(1)

    @pl.when(tv_ref[t] != 0)
    def _():
        a = a_ref[...]
        u = jnp.dot(a, w1_ref[0], preferred_element_type=F32)
        v = jnp.dot(a, w3_ref[0], preferred_element_type=F32)
        o_ref[...] = (_silu(u) * v).astype(o_ref.dtype)

    @pl.when(tv_ref[t] == 0)
    def _():
        o_ref[...] = jnp.zeros_like(o_ref)


def swiglu_up(a, w1, w3, tile_expert, tile_valid, tm, tn):
    rows, k = a.shape
    f = w1.shape[2]
    grid_spec = pltpu.PrefetchScalarGridSpec(
        num_scalar_prefetch=2,
        grid=(f // tn, rows // tm),
        in_specs=[pl.BlockSpec((tm, k), lambda j, t, te, tv: (t, 0)),
                  pl.BlockSpec((1, k, tn), lambda j, t, te, tv: (te[t], 0, j)),
                  pl.BlockSpec((1, k, tn), lambda j, t, te, tv: (te[t], 0, j))],
        out_specs=pl.BlockSpec((tm, tn), lambda j, t, te, tv: (t, j)),
    )
    return pl.pallas_call(
        _swiglu_up_kernel,
        grid_spec=grid_spec,
        out_shape=jax.ShapeDtypeStruct((rows, f), BF16),
        compiler_params=_params("parallel", "arbitrary"),
        name="swiglu_up",
    )(tile_expert, tile_valid, a, w1, w3)


def _swiglu_down_kernel(te_ref, tv_ref, g_ref, w2_ref, *rest, residual):
    t = pl.program_id(1)
    if residual:
        x_ref, gate_ref, o_ref = rest
    else:
        (o_ref,) = rest

    @pl.when(tv_ref[t] != 0)
    def _():
        acc = jnp.dot(g_ref[...], w2_ref[0], preferred_element_type=F32)
        if residual:
            acc = x_ref[...] + gate_ref[...] * acc
        o_ref[...] = acc.astype(o_ref.dtype)

    @pl.when(tv_ref[t] == 0)
    def _():
        o_ref[...] = jnp.zeros_like(o_ref)


def swiglu_down(g, w2, tile_expert, tile_valid, tm, tn, x=None, gate=None):
    rows, f = g.shape
    d = w2.shape[2]
    residual = x is not None
    in_specs = [pl.BlockSpec((tm, f), lambda j, t, te, tv: (t, 0)),
                pl.BlockSpec((1, f, tn), lambda j, t, te, tv: (te[t], 0, j))]
    args = [g, w2]
    if residual:
        in_specs += [pl.BlockSpec((tm, tn), lambda j, t, te, tv: (t, j)),
                     pl.BlockSpec((1, tn), lambda j, t, te, tv: (0, j))]
        args += [x, gate]
    grid_spec = pltpu.PrefetchScalarGridSpec(
        num_scalar_prefetch=2,
        grid=(d // tn, rows // tm),
        in_specs=in_specs,
        out_specs=pl.BlockSpec((tm, tn), lambda j, t, te, tv: (t, j)),
    )
    return pl.pallas_call(
        functools.partial(_swiglu_down_kernel, residual=residual),
        grid_spec=grid_spec,
        out_shape=jax.ShapeDtypeStruct((rows, d), F32 if residual else BF16),
        compiler_params=_params("parallel", "arbitrary"),
        name="swiglu_down",
    )(tile_expert, tile_valid, *args)


def _causal_conv(cur, tail_ref, w_ref, b_ref):
    t = cur.shape[0]
    tail = tail_ref[...]
    w = w_ref[...]
    row8 = lax.broadcasted_iota(jnp.int32, (SUBLANES, cur.shape[1]), 0)
    acc = cur * w[SSD_CONV - 1:SSD_CONV]
    top = cur[0:SUBLANES] * w[SSD_CONV - 1:SSD_CONV]
    for s in range(1, SSD_CONV):
        wk = w[SSD_CONV - 1 - s:SSD_CONV - s]
        rolled = pltpu.roll(cur, s, axis=0)
        acc += rolled * wk
        top += jnp.where(row8 < s, pltpu.roll(tail, s, axis=0), rolled[0:SUBLANES]) * wk
    tail_ref[...] = cur[t - SUBLANES:t]
    return jnp.concatenate([top, acc[SUBLANES:]], axis=0) + b_ref[...]


def _ssd_kernel(z_ref, x_ref, bc_ref, dt_ref, cwx_ref, cwb_ref, cbx_ref, cbb_ref, dtb_ref, alog_ref,
                dskip_ref, ng_ref, expand_ref, o_ref, tailx_ref, tailb_ref, state_ref, ybuf_ref):
    t = SSD_CHUNK
    hg = SSD_HEADS // SSD_GROUPS
    gw = SSD_WIDTH // SSD_GROUPS

    @pl.when(pl.program_id(0) == 0)
    def _():
        tailx_ref[...] = jnp.zeros_like(tailx_ref)
        tailb_ref[...] = jnp.zeros_like(tailb_ref)
        state_ref[...] = jnp.zeros_like(state_ref)

    xs = _silu(_causal_conv(x_ref[...], tailx_ref, cwx_ref, cbx_ref))
    bcm = _silu(_causal_conv(bc_ref[...], tailb_ref, cwb_ref, cbb_ref))
    dt = _softplus(dt_ref[...] + dtb_ref[...])
    a = -jnp.exp(alog_ref[...])
    row = lax.broadcasted_iota(jnp.int32, (t, t), 0)
    col = lax.broadcasted_iota(jnp.int32, (t, t), 1)
    tril = row >= col
    a_cs = jnp.dot(jnp.where(tril, 1.0, 0.0), dt * a, precision=HIGHEST, preferred_element_type=F32)
    a_last = a_cs[t - 1:t]
    per_head = jnp.concatenate(
        [dt, jnp.exp(a_last - a_cs), jnp.exp(a_cs), jnp.broadcast_to(jnp.exp(a_last), (SUBLANES, LANES))], axis=0)
    spread = jnp.dot(per_head, expand_ref[...], precision=HIGHEST, preferred_element_type=F32)
    dt_x, to_end_x, ea_x, cd_x = spread[0:t], spread[t:2 * t], spread[2 * t:3 * t], spread[3 * t:3 * t + 1]
    xdt = xs * dt_x
    xdt_b = xdt.astype(BF16)
    xw_b = (xdt * to_end_x).astype(BF16)
    a_cs_t = a_cs.T

    y_off = []
    for g in range(SSD_GROUPS):
        bm = bcm[:, g * SSD_STATE:(g + 1) * SSD_STATE]
        cm_b = bcm[:, (SSD_GROUPS + g) * SSD_STATE:(SSD_GROUPS + g + 1) * SSD_STATE].astype(BF16)
        cb = lax.dot_general(cm_b, bm.astype(BF16), (((1,), (1,)), ((), ())), preferred_element_type=F32)
        h_prev = state_ref[g]
        y_off.append(jnp.dot(cm_b, h_prev.astype(BF16), preferred_element_type=F32)
                     * ea_x[:, g * gw:(g + 1) * gw])
        st_new = jnp.dot(bm.T.astype(BF16), xw_b[:, g * gw:(g + 1) * gw], preferred_element_type=F32)
        state_ref[g] = h_prev * cd_x[:, g * gw:(g + 1) * gw] + st_new
        for r in range(0, hg, 2):
            pair = []
            for h in (g * hg + r, g * hg + r + 1):
                diff = a_cs[:, h:h + 1] - a_cs_t[h:h + 1, :]
                m = (cb * jnp.exp(jnp.where(tril, diff, -jnp.inf))).astype(BF16)
                pair.append(jnp.dot(m, xdt_b[:, h * SSD_HEAD_DIM:(h + 1) * SSD_HEAD_DIM],
                                    preferred_element_type=F32))
            lo = (g * hg + r) * SSD_HEAD_DIM
            ybuf_ref[:, lo:lo + 2 * SSD_HEAD_DIM] = jnp.concatenate(pair, axis=1)

    y = ybuf_ref[...] + jnp.concatenate(y_off, axis=1) + xs * dskip_ref[...]
    y = y * _silu(z_ref[...])
    outs = []
    for g in range(SSD_GROUPS):
        yg = y[:, g * gw:(g + 1) * gw]
        outs.append(yg * lax.rsqrt(jnp.mean(yg * yg, axis=-1, keepdims=True) + EPS))
    o_ref[...] = (jnp.concatenate(outs, axis=1) * ng_ref[...]).astype(o_ref.dtype)


def ssd_mixer(proj, conv_w, conv_b, dt_bias, a_log, d_skip, norm_g):
    L = proj.shape[0]
    t = SSD_CHUNK
    assert L % t == 0

    def pad_lanes(v):
        return jnp.pad(v, (0, LANES - v.shape[0])).reshape(1, LANES)

    expand = (jnp.arange(SSD_WIDTH)[None, :] // SSD_HEAD_DIM == jnp.arange(LANES)[:, None]).astype(F32)
    full = lambda shape: pl.BlockSpec(shape, lambda c: (0,) * len(shape))
    return pl.pallas_call(
        _ssd_kernel,
        grid=(L // t,),
        in_specs=[pl.BlockSpec((t, SSD_WIDTH), lambda c: (c, COL_Z // SSD_WIDTH)),
                  pl.BlockSpec((t, SSD_WIDTH), lambda c: (c, COL_X // SSD_WIDTH)),
                  pl.BlockSpec((t, SSD_BC), lambda c: (c, COL_BC // SSD_BC)),
                  pl.BlockSpec((t, LANES), lambda c: (c, COL_DT // LANES)),
                  full((SSD_CONV, SSD_WIDTH)), full((SSD_CONV, SSD_BC)),
                  full((1, SSD_WIDTH)), full((1, SSD_BC)),
                  full((1, LANES)), full((1, LANES)), full((1, SSD_WIDTH)), full((1, SSD_WIDTH)),
                  full((LANES, SSD_WIDTH))],
        out_specs=pl.BlockSpec((t, SSD_WIDTH), lambda c: (c, 0)),
        out_shape=jax.ShapeDtypeStruct((L, SSD_WIDTH), BF16),
        scratch_shapes=[pltpu.VMEM((SUBLANES, SSD_WIDTH), F32), pltpu.VMEM((SUBLANES, SSD_BC), F32),
                        pltpu.VMEM((SSD_GROUPS, SSD_STATE, SSD_WIDTH // SSD_GROUPS), F32),
                        pltpu.VMEM((t, SSD_WIDTH), F32)],
        compiler_params=_params("arbitrary"),
        name="ssd_mixer",
    )(proj, proj, proj, proj,
      conv_w[:, :SSD_WIDTH], conv_w[:, SSD_WIDTH:], conv_b[:SSD_WIDTH].reshape(1, -1),
      conv_b[SSD_WIDTH:].reshape(1, -1), pad_lanes(dt_bias), pad_lanes(a_log),
      jnp.repeat(d_skip, SSD_HEAD_DIM).reshape(1, -1), norm_g.reshape(1, -1), expand)


def _moba_prep_kernel(q_ref, k_ref, v_ref, qa_ref, ka_ref, va_ref, kmean_ref):
    own = pl.program_id(0)
    t = MOBA_BLOCK
    dh = MOBA_HEAD_DIM
    nbl = LANES // 4

    @pl.when(own == 0)
    def _():
        kmean_ref[...] = jnp.zeros_like(kmean_ref)

    lane = lax.broadcasted_iota(jnp.int32, (t, LANES), 1)
    lane_s = lax.broadcasted_iota(jnp.int32, (t, nbl), 1)
    zeros_pad = jnp.zeros((t, LANES - dh - nbl), F32)
    own_onehot = jnp.where(lane_s == own, 1.0, 0.0)
    for h in range(MOBA_HEADS):
        qh = q_ref[:, h * dh:(h + 1) * dh]
        kh = k_ref[:, h * dh:(h + 1) * dh]
        vh = v_ref[:, h * dh:(h + 1) * dh]
        gate = lax.dot_general(qh, kmean_ref[h], (((1,), (1,)), ((), ())),
                               precision=HIGHEST, preferred_element_type=F32)
        gate = jnp.where(lane < own, gate, -jnp.inf)
        sel = jnp.zeros((t, LANES), jnp.bool_)
        for _ in range(MOBA_TOPK):
            m = jnp.max(gate, axis=1, keepdims=True)
            idx = jnp.min(jnp.where(gate == m, lane, LANES), axis=1, keepdims=True)
            hit = (lane == idx) & (m > -jnp.inf)
            sel = sel | hit
            gate = jnp.where(lane == idx, -jnp.inf, gate)
        offs = jnp.where(sel | (lane >= own), 0.0, NEG_BIG)
        qa_ref[h] = jnp.concatenate([qh * (dh ** -0.5), offs[:, :nbl], zeros_pad], axis=1).astype(BF16)
        ka_ref[h] = jnp.concatenate([kh, own_onehot, zeros_pad], axis=1).astype(BF16)
        va_ref[h] = jnp.concatenate([vh, jnp.zeros((t, LANES - dh), F32)], axis=1).astype(BF16)
        kmean_ref[h, pl.ds(own, 1), :] = jnp.mean(kh, axis=0, keepdims=True)


def moba_prep(proj):
    L = proj.shape[0]
    t = MOBA_BLOCK
    assert L % t == 0 and L // t <= LANES // 4
    out = jax.ShapeDtypeStruct((MOBA_HEADS, L, LANES), BF16)
    ospec = pl.BlockSpec((MOBA_HEADS, t, LANES), lambda i: (0, i, 0))
    return pl.pallas_call(
        _moba_prep_kernel,
        grid=(L // t,),
        in_specs=[pl.BlockSpec((t, MOBA_WIDTH), lambda i: (i, COL_MQ // MOBA_WIDTH)),
                  pl.BlockSpec((t, MOBA_WIDTH), lambda i: (i, COL_MK // MOBA_WIDTH)),
                  pl.BlockSpec((t, MOBA_WIDTH), lambda i: (i, COL_MV // MOBA_WIDTH))],
        out_specs=[ospec, ospec, ospec],
        out_shape=[out, out, out],
        scratch_shapes=[pltpu.VMEM((MOBA_HEADS, LANES, MOBA_HEAD_DIM), F32)],
        compiler_params=_params("arbitrary"),
        name="moba_prep",
    )(proj, proj, proj)


def _attention_kernel(near_ref, q_ref, k_ref, v_ref, posc_ref, posr_ref, tbl_ref, g_ref, lam_ref, o_ref,
                      m_ref, l_ref, acc_ref, *, maps, batched, head_off, q_scale, norm_dim, post_scale,
                      lambda_init):
    h = pl.program_id(0)
    qi = pl.program_id(1)
    nq = pl.num_programs(1)
    t = ATT_T
    m_ref[...] = jnp.full_like(m_ref, -jnp.inf)
    l_ref[...] = jnp.zeros_like(l_ref)
    acc_ref[...] = jnp.zeros_like(acc_ref)
    q = q_ref[0] if batched else q_ref[...]
    if q_scale != 1.0:
        q = q * q_scale
    trow = tbl_ref[pl.ds(head_off + h, 1), :]
    posc = posc_ref[...]

    def tile(ki, general):
        rows = pl.ds(pl.multiple_of(ki * t, t), t)
        k = k_ref[0, rows, :] if batched else k_ref[rows, :]
        v = v_ref[0, rows, :] if batched else v_ref[rows, :]
        if general:
            dist = jnp.clip(posc - posr_ref[pl.ds(ki, 1), :], 0, LANES - 1)
            tb = jnp.broadcast_to(trow, (t, LANES))
            bias = jnp.concatenate(
                [jnp.take_along_axis(tb, dist[:, j * LANES:(j + 1) * LANES], axis=1) for j in range(t // LANES)],
                axis=1)
            row = lax.broadcasted_iota(jnp.int32, (t, t), 0)
            col = lax.broadcasted_iota(jnp.int32, (t, t), 1)
            causal = col + ki * t <= row + qi * t
        else:
            bias = trow[:, LANES - 1:LANES]
        for mi, (lo, hi) in enumerate(maps):
            s = lax.dot_general(q[:, lo:hi], k[:, lo:hi], (((1,), (1,)), ((), ())),
                                preferred_element_type=F32) + bias
            if general:
                s = jnp.where(causal, s, -jnp.inf)
            m_old = m_ref[mi]
            m_new = jnp.maximum(m_old, jnp.max(s, axis=1, keepdims=True))
            alpha = jnp.exp(m_old - m_new)
            p = jnp.exp(s - m_new)
            l_ref[mi] = alpha * l_ref[mi] + jnp.sum(p, axis=1, keepdims=True)
            acc_ref[mi] = alpha * acc_ref[mi] + jnp.dot(p.astype(BF16), v, preferred_element_type=F32)
            m_ref[mi] = m_new

    def body(ki, carry):
        flag = near_ref[qi * nq + ki]

        @pl.when(flag == 0)
        def _():
            tile(ki, False)

        @pl.when(flag != 0)
        def _():
            tile(ki, True)

        return carry

    lax.fori_loop(0, qi + 1, body, 0)

    o = acc_ref[0] / l_ref[0]
    if lambda_init is not None:
        lp = lam_ref[...]
        lam = (jnp.exp(jnp.sum(lp[0:1] * lp[1:2], axis=1, keepdims=True))
               - jnp.exp(jnp.sum(lp[2:3] * lp[3:4], axis=1, keepdims=True)) + lambda_init)
        o = o - lam * (acc_ref[1] / l_ref[1])
    ms = jnp.sum(o * o, axis=1, keepdims=True) * (1.0 / norm_dim)
    y = o * lax.rsqrt(ms + EPS) * g_ref[pl.ds(h, 1), :]
    if post_scale != 1.0:
        y = y * post_scale
    if batched:
        o_ref[0] = y.astype(o_ref.dtype)
    else:
        o_ref[...] = y.astype(o_ref.dtype)


def _attention_call(kernel, heads, L, near, in_specs, out_spec, out_shape, n_maps, args, name):
    t = ATT_T
    full = lambda shape: pl.BlockSpec(shape, lambda h, i, nr: (0,) * len(shape))
    grid_spec = pltpu.PrefetchScalarGridSpec(
        num_scalar_prefetch=1,
        grid=(heads, L // t),
        in_specs=in_specs + [pl.BlockSpec((t, 1), lambda h, i, nr: (i, 0)),
                             full((L // t, t)), full(args[-3].shape), full(args[-2].shape),
                             full(args[-1].shape)],
        out_specs=out_spec,
        scratch_shapes=[pltpu.VMEM((n_maps, t, 1), F32), pltpu.VMEM((n_maps, t, 1), F32),
                        pltpu.VMEM((n_maps, t, LANES), F32)],
    )
    return pl.pallas_call(kernel, grid_spec=grid_spec, out_shape=out_shape,
                          compiler_params=_params("parallel", "arbitrary"), name=name)(near, *args)


def moba_attention(qa, ka, va, near, pos_col, pos_rows, table, norm_g):
    heads, L, _ = qa.shape
    t = ATT_T
    g = jnp.pad(norm_g.reshape(heads, MOBA_HEAD_DIM), ((0, 0), (0, LANES - MOBA_HEAD_DIM)))
    kernel = functools.partial(_attention_kernel, maps=((0, LANES),), batched=True, head_off=0, q_scale=1.0,
                               norm_dim=MOBA_HEAD_DIM, post_scale=1.0, lambda_init=None)
    kv = pl.BlockSpec((1, L, LANES), lambda h, i, nr: (h, 0, 0))
    out = _attention_call(
        kernel, heads, L, near,
        [pl.BlockSpec((1, t, LANES), lambda h, i, nr: (h, i, 0)), kv, kv],
        pl.BlockSpec((1, t, LANES), lambda h, i, nr: (h, i, 0)),
        jax.ShapeDtypeStruct((heads, L, LANES), BF16), 1,
        [qa, ka, va, pos_col, pos_rows, table, g, jnp.zeros((4, DIFF_QK_DIM), F32)], "moba_attention")
    return jnp.transpose(out[:, :, :MOBA_HEAD_DIM], (1, 0, 2)).reshape(L, MOBA_WIDTH)


def diff_attention(proj16, near, pos_col, pos_rows, table, lam_params, subln_g, lambda_init):
    L = proj16.shape[0]
    t = ATT_T
    g = jnp.broadcast_to(subln_g.reshape(1, DIFF_V_DIM), (DIFF_HEADS, DIFF_V_DIM))
    kernel = functools.partial(_attention_kernel, maps=((0, DIFF_QK_DIM), (DIFF_QK_DIM, 2 * DIFF_QK_DIM)),
                               batched=False, head_off=MOBA_HEADS, q_scale=DIFF_QK_DIM ** -0.5,
                               norm_dim=DIFF_V_DIM, post_scale=1.0 - lambda_init, lambda_init=lambda_init)
    return _attention_call(
        kernel, DIFF_HEADS, L, near,
        [pl.BlockSpec((t, LANES), lambda h, i, nr: (i, COL_DQ // LANES + h)),
         pl.BlockSpec((L, LANES), lambda h, i, nr: (0, COL_DK // LANES + h)),
         pl.BlockSpec((L, LANES), lambda h, i, nr: (0, COL_DV // LANES + h))],
        pl.BlockSpec((t, LANES), lambda h, i, nr: (i, h)),
        jax.ShapeDtypeStruct((L, DIFF_WIDTH), BF16), 2,
        [proj16, proj16, proj16, pos_col, pos_rows, table, g, lam_params], "diff_attention")


def _rel_bucket(dist):
    n = jnp.maximum(dist, 0)
    max_exact = REL_BUCKETS // 2
    nf = jnp.maximum(n, 1).astype(F32)
    large = max_exact + (jnp.log(nf / max_exact) / math.log(REL_MAX_DIST / max_exact)
                         * (REL_BUCKETS - max_exact)).astype(jnp.int32)
    return jnp.where(n < max_exact, n, jnp.minimum(large, REL_BUCKETS - 1))


def attention_tables(positions, rel_bias):
    L = positions.shape[0]
    t = ATT_T
    buckets = _rel_bucket(jnp.arange(LANES, dtype=jnp.int32))
    table = rel_bias[buckets].T
    pos_rows = positions.reshape(L // t, t)
    lo, hi = jnp.min(pos_rows, axis=1), jnp.max(pos_rows, axis=1)
    near = (lo[:, None] - hi[None, :] < LANES) | jnp.eye(L // t, dtype=bool)
    return table, positions.reshape(L, 1), pos_rows, near.astype(jnp.int32).reshape(-1)


def _router_kernel(x_ref, g_ref, sc_ref, sh_ref, rw_ref, h_ref, comb_ref, rank_ref, cum_ref, total_ref, cnt_ref):
    t = TOK_T

    @pl.when(pl.program_id(0) == 0)
    def _():
        cnt_ref[...] = jnp.zeros_like(cnt_ref)

    h = _norm_mod(x_ref[...], g_ref[...], sc_ref[...], sh_ref[...])
    h_ref[...] = h.astype(BF16)
    lane = lax.broadcasted_iota(jnp.int32, (t, LANES), 1)
    logits = jnp.dot(h, rw_ref[...], precision=HIGHEST, preferred_element_type=F32)
    logits = jnp.where(lane < N_EXPERTS, logits, -jnp.inf)
    m1 = jnp.max(logits, axis=1, keepdims=True)
    i1 = jnp.min(jnp.where(logits == m1, lane, LANES), axis=1, keepdims=True)
    rest = jnp.where(lane == i1, -jnp.inf, logits)
    m2 = jnp.max(rest, axis=1, keepdims=True)
    i2 = jnp.min(jnp.where(rest == m2, lane, LANES), axis=1, keepdims=True)
    e2 = jnp.exp(m2 - m1)
    denom = 1.0 + e2
    comb_ref[...] = jnp.where(lane == i1, 1.0 / denom, 0.0) + jnp.where(lane == i2, e2 / denom, 0.0)
    sel = jnp.where((lane == i1) | (lane == i2), 1.0, 0.0)
    row = lax.broadcasted_iota(jnp.int32, (t, t), 0)
    col = lax.broadcasted_iota(jnp.int32, (t, t), 1)
    before = jnp.dot(jnp.where(row > col, 1.0, 0.0).astype(BF16), sel.astype(BF16), preferred_element_type=F32)
    carry = cnt_ref[...]
    cum_ref[0] = carry
    rank_ref[...] = jnp.where(sel > 0.0, before + carry, -1.0)
    carry = carry + jnp.sum(sel, axis=0, keepdims=True)
    cnt_ref[...] = carry
    total_ref[...] = carry


def moe_router(x, g, sc, sh, router_w):
    L, d = x.shape
    t = TOK_T
    rw = jnp.pad(router_w, ((0, 0), (0, LANES - N_EXPERTS)))
    vec = pl.BlockSpec((1, d), lambda i: (0, 0))
    tok = pl.BlockSpec((t, LANES), lambda i: (i, 0))
    return pl.pallas_call(
        _router_kernel,
        grid=(L // t,),
        in_specs=[pl.BlockSpec((t, d), lambda i: (i, 0)), vec, vec, vec,
                  pl.BlockSpec((d, LANES), lambda i: (0, 0))],
        out_specs=[pl.BlockSpec((t, d), lambda i: (i, 0)), tok, tok,
                   pl.BlockSpec((1, 1, LANES), lambda i: (i, 0, 0)),
                   pl.BlockSpec((1, LANES), lambda i: (0, 0))],
        out_shape=[jax.ShapeDtypeStruct((L, d), BF16), jax.ShapeDtypeStruct((L, LANES), F32),
                   jax.ShapeDtypeStruct((L, LANES), F32), jax.ShapeDtypeStruct((L // t, 1, LANES), F32),
                   jax.ShapeDtypeStruct((1, LANES), F32)],
        scratch_shapes=[pltpu.VMEM((1, LANES), F32)],
        compiler_params=_params("arbitrary"),
        name="moe_router",
    )(x, g, sc, sh, rw)


def _item_lists(hit, n_items):
    rows, cols = hit.shape
    n_real = jnp.sum(hit.astype(jnp.int32))
    idx = jnp.nonzero(hit.reshape(-1), size=n_items, fill_value=0)[0].astype(jnp.int32)
    k = jnp.arange(n_items, dtype=jnp.int32)
    real = k < n_real
    idx = jnp.where(real, idx, idx[jnp.maximum(n_real - 1, 0)])
    r, c = idx // cols, idx % cols
    prev_r = jnp.concatenate([jnp.full((1,), -1, jnp.int32), r[:-1]])
    next_r = jnp.concatenate([r[1:], jnp.full((1,), -1, jnp.int32)])
    first = real & (r != prev_r)
    last = real & ((r != next_r) | (k == n_real - 1))
    flags = real.astype(jnp.int32) + 2 * first.astype(jnp.int32) + 4 * last.astype(jnp.int32)
    return r, c, flags


def moe_plan(rank, cum, total, L):
    tm, tb = MOE_TM, TOK_T
    nb = L // tb
    n_tiles = 2 * L // tm + N_EXPERTS
    counts = total[0, :N_EXPERTS].astype(jnp.int32)
    tiles_e = (counts + tm - 1) // tm
    tile_end = jnp.cumsum(tiles_e)
    tile_start = tile_end - tiles_e
    tid = jnp.arange(n_tiles, dtype=jnp.int32)
    tile_valid = tid < tile_end[-1]
    tile_expert = jnp.minimum(jnp.searchsorted(tile_end, tid, side="right"), N_EXPERTS - 1).astype(jnp.int32)
    local_row = (tid - tile_start[tile_expert]) * tm
    r = rank[:, :N_EXPERTS].astype(jnp.int32)
    dest = jnp.where(r >= 0, r + (tile_start * tm)[None, :], -1)
    cum_i = cum[:, 0, :N_EXPERTS].astype(jnp.int32)
    cum_next = jnp.concatenate([cum_i[1:], counts[None, :]], axis=0)
    lo = cum_i[:, tile_expert].T
    hi = cum_next[:, tile_expert].T
    hit = tile_valid[:, None] & (lo < (local_row + tm)[:, None]) & (hi > local_row[:, None])
    n_items = n_tiles + N_EXPERTS * nb
    pad_hit = hit | ((~tile_valid)[:, None] & (jnp.arange(nb) == 0)[None, :])
    gather_items = _item_lists(pad_hit, n_items)
    kb, tt, fl = _item_lists(hit.T, n_items)
    return dict(tile_expert=tile_expert, tile_valid=tile_valid.astype(jnp.int32),
                dest_rows=dest.T, dest_cols=dest.T.reshape(N_EXPERTS, L, 1),
                gather_items=gather_items, combine_items=(tt, kb, fl), n_tiles=n_tiles, n_items=n_items)


def _moe_gather_kernel(it_ref, ib_ref, if_ref, te_ref, dest_ref, h_ref, o_ref, acc_ref):
    i = pl.program_id(0)
    flag = if_ref[i]
    tile = it_ref[i]
    tm = MOE_TM

    @pl.when((flag & 2) != 0)
    def _():
        acc_ref[...] = jnp.zeros_like(acc_ref)

    @pl.when((flag & 1) != 0)
    def _():
        d = dest_ref[pl.ds(te_ref[tile], 1), :]
        rows = tile * tm + lax.broadcasted_iota(jnp.int32, (tm, 1), 0)
        onehot = jnp.where(d == rows, 1.0, 0.0).astype(BF16)
        acc_ref[...] += jnp.dot(onehot, h_ref[...], preferred_element_type=F32)

    @pl.when((flag & 4) != 0)
    def _():
        o_ref[...] = acc_ref[...].astype(o_ref.dtype)


def moe_gather(h, plan):
    L, d = h.shape
    tm, tb = MOE_TM, TOK_T
    it, ib, fl = plan["gather_items"]
    grid_spec = pltpu.PrefetchScalarGridSpec(
        num_scalar_prefetch=4,
        grid=(plan["n_items"],),
        in_specs=[pl.BlockSpec((N_EXPERTS, tb), lambda i, it, ib, fl, te: (0, ib[i])),
                  pl.BlockSpec((tb, d), lambda i, it, ib, fl, te: (ib[i], 0))],
        out_specs=pl.BlockSpec((tm, d), lambda i, it, ib, fl, te: (it[i], 0)),
        scratch_shapes=[pltpu.VMEM((tm, d), F32)],
    )
    return pl.pallas_call(
        _moe_gather_kernel,
        grid_spec=grid_spec,
        out_shape=jax.ShapeDtypeStruct((plan["n_tiles"] * tm, d), BF16),
        compiler_params=_params("arbitrary"),
        name="moe_gather",
    )(it, ib, fl, plan["tile_expert"], plan["dest_rows"], h)


def _moe_combine_kernel(it_ref, ib_ref, if_ref, te_ref, dest_ref, w_ref, ys_ref, x_ref, gate_ref, o_ref, acc_ref):
    i = pl.program_id(0)
    flag = if_ref[i]
    tile = it_ref[i]
    tm = MOE_TM

    @pl.when((flag & 2) != 0)
    def _():
        acc_ref[...] = jnp.zeros_like(acc_ref)

    @pl.when((flag & 1) != 0)
    def _():
        cols = tile * tm + lax.broadcasted_iota(jnp.int32, (1, tm), 1)
        onehot = jnp.where(dest_ref[0] == cols, 1.0, 0.0).astype(BF16)
        acc_ref[...] += w_ref[0] * jnp.dot(onehot, ys_ref[...], preferred_element_type=F32)

    @pl.when((flag & 4) != 0)
    def _():
        o_ref[...] = x_ref[...] + gate_ref[...] * acc_ref[...]


def moe_combine(ys, comb, x, gate, plan):
    L, d = x.shape
    tm, tb = MOE_TM, TOK_T
    it, ib, fl = plan["combine_items"]
    w_cols = comb[:, :N_EXPERTS].T.reshape(N_EXPERTS, L, 1)
    col = lambda i, it, ib, fl, te: (te[it[i]], ib[i], 0)
    grid_spec = pltpu.PrefetchScalarGridSpec(
        num_scalar_prefetch=4,
        grid=(plan["n_items"],),
        in_specs=[pl.BlockSpec((1, tb, 1), col), pl.BlockSpec((1, tb, 1), col),
                  pl.BlockSpec((tm, d), lambda i, it, ib, fl, te: (it[i], 0)),
                  pl.BlockSpec((tb, d), lambda i, it, ib, fl, te: (ib[i], 0)),
                  pl.BlockSpec((1, d), lambda i, it, ib, fl, te: (0, 0))],
        out_specs=pl.BlockSpec((tb, d), lambda i, it, ib, fl, te: (ib[i], 0)),
        scratch_shapes=[pltpu.VMEM((tb, d), F32)],
    )
    return pl.pallas_call(
        _moe_combine_kernel,
        grid_spec=grid_spec,
        out_shape=jax.ShapeDtypeStruct((L, d), F32),
        compiler_params=_params("arbitrary"),
        name="moe_combine",
    )(it, ib, fl, plan["tile_expert"], plan["dest_cols"], w_cols, ys, x, gate)


def moe_ffn(x, g, sc, sh, gate, router_w, w1, w3, w2):
    L = x.shape[0]
    h, comb, rank, cum, total = moe_router(x, g, sc, sh, router_w)
    plan = moe_plan(rank, cum, total, L)
    xs = moe_gather(h, plan)
    te, tv = plan["tile_expert"], plan["tile_valid"]
    act = swiglu_up(xs, w1, w3, te, tv, MOE_TM, 1024)
    ys = swiglu_down(act, w2, te, tv, MOE_TM, 1024)
    return moe_combine(ys, comb, x, gate, plan)


def dense_ffn(x, g, sc, sh, gate, w1, w3, w2):
    L = x.shape[0]
    tm = 512
    h = norm_modulate(x, g, sc, sh, BF16)
    te = jnp.zeros((L // tm,), jnp.int32)
    tv = jnp.ones((L // tm,), jnp.int32)
    act = swiglu_up(h, w1[None], w3[None], te, tv, tm, 1408)
    return swiglu_down(act, w2[None], te, tv, tm, 1024, x=x, gate=gate)


def _permute_in_proj(w):
    d = w.shape[0]
    return jnp.concatenate([w[:, :ORIG_DT], w[:, ORIG_DT + SSD_HEADS:], w[:, ORIG_DT:ORIG_DT + SSD_HEADS],
                            jnp.zeros((d, LANES - SSD_HEADS), w.dtype)], axis=1).astype(BF16)


def kernel(x, c, positions, rel_bias, w_ada, b_ada, norm_mix_g, w_in, conv_w, conv_b, dt_bias, a_log, d_skip, ssd_norm_g, moba_norm_g, diff_lambda, diff_subln_g, w_out, norm_ffn_g, dense_w1, dense_w3, dense_w2, router_w, expert_w1, expert_w3, expert_w2, final_g):
    batch, L, d = x.shape
    assert batch == 1 and d == D_MODEL
    x = x[0]
    mod = ada_modulation(c, w_ada, b_ada)
    table, pos_col, pos_rows, near = attention_tables(positions[0], rel_bias)
    row = lambda v: v.reshape(1, -1)
    for layer in range(DEPTH):
        lambda_init = 0.8 - 0.6 * math.exp(-0.3 * layer)
        shift1, scale1, gate1, shift2, scale2, gate2 = (mod[layer, :, j * d:(j + 1) * d] for j in range(6))
        h = norm_modulate(x, row(norm_mix_g[layer]), scale1, shift1, BF16)
        proj, proj16 = in_projection(h, _permute_in_proj(w_in[layer]))
        y_ssd = ssd_mixer(proj, conv_w[layer], conv_b[layer], dt_bias[layer], a_log[layer], d_skip[layer],
                          ssd_norm_g[layer])
        qa, ka, va = moba_prep(proj)
        y_moba = moba_attention(qa, ka, va, near, pos_col, pos_rows, table, moba_norm_g[layer])
        y_diff = diff_attention(proj16, near, pos_col, pos_rows, table, diff_lambda[layer],
                                diff_subln_g[layer], lambda_init)
        x = out_projection(y_ssd, y_moba, y_diff, w_out[layer].astype(BF16), x, gate1)
        i = layer // 2
        g2 = row(norm_ffn_g[layer])
        if layer % 2 == 0:
            x = dense_ffn(x, g2, scale2, shift2, gate2, dense_w1[i].astype(BF16), dense_w3[i].astype(BF16),
                          dense_w2[i].astype(BF16))
        else:
            x = moe_ffn(x, g2, scale2, shift2, gate2, router_w[i], expert_w1[i].astype(BF16),
                        expert_w3[i].astype(BF16), expert_w2[i].astype(BF16))
    zero = jnp.zeros((1, d), F32)
    return norm_modulate(x, row(final_g), zero, zero, F32)[None]
```

```python
import functools
import math

import jax
import jax.numpy as jnp
from jax import lax
from jax.experimental import pallas as pl
from jax.experimental.pallas import tpu as pltpu

F32 = jnp.float32
BF16 = jnp.bfloat16
HIGHEST = lax.Precision.HIGHEST

D_MODEL = 2048
DEPTH = 2
SSD_HEADS = 16
SSD_HEAD_DIM = 64
SSD_WIDTH = SSD_HEADS * SSD_HEAD_DIM
SSD_GROUPS = 2
SSD_STATE = 128
SSD_CONV = 4
SSD_CHUNK = 256
SSD_BC = 2 * SSD_GROUPS * SSD_STATE
MOBA_HEADS = 8
MOBA_HEAD_DIM = 64
MOBA_WIDTH = MOBA_HEADS * MOBA_HEAD_DIM
MOBA_BLOCK = 256
MOBA_TOPK = 3
DIFF_HEADS = 4
DIFF_QK_DIM = 64
DIFF_V_DIM = 128
DIFF_WIDTH = DIFF_HEADS * DIFF_V_DIM
REL_BUCKETS = 32
REL_MAX_DIST = 128
D_FF_DENSE = 5632
N_EXPERTS = 8
D_FF_EXPERT = 7168
EPS = 1e-6

LANES = 128
SUBLANES = 8
VMEM_LIMIT = 56 * 1024 * 1024

COL_Z = 0
COL_X = SSD_WIDTH
COL_BC = COL_X + SSD_WIDTH
COL_MQ = COL_BC + SSD_BC
COL_MK = COL_MQ + MOBA_WIDTH
COL_MV = COL_MK + MOBA_WIDTH
COL_DQ = COL_MV + MOBA_WIDTH
COL_DK = COL_DQ + DIFF_WIDTH
COL_DV = COL_DK + DIFF_WIDTH
COL_DT = COL_DV + DIFF_WIDTH
PROJ_W = COL_DT + LANES
ORIG_DT = SSD_WIDTH + SSD_WIDTH + SSD_BC

ATT_T = 256
ATT_CHUNK = 32
TOK_T = 256
MOE_TM = 256
NEG_BIG = -1e9


def _silu(x):
    return x * (1.0 / (1.0 + jnp.exp(-x)))


def _softplus(x):
    return jnp.maximum(x, 0.0) + jnp.log1p(jnp.exp(-jnp.abs(x)))


def _params(*sem):
    return pltpu.CompilerParams(dimension_semantics=sem, vmem_limit_bytes=VMEM_LIMIT)


def _ada_kernel(c_ref, w_ref, b_ref, o_ref):
    ca = _silu(c_ref[...])
    o_ref[0] = jnp.sum(ca * w_ref[0], axis=0, keepdims=True) + b_ref[0]


def ada_modulation(c, w_ada, b_ada):
    depth, d, n = w_ada.shape
    tn = 1024
    return pl.pallas_call(
        _ada_kernel,
        grid=(depth, n // tn),
        in_specs=[pl.BlockSpec((d, 1), lambda l, j: (0, 0)),
                  pl.BlockSpec((1, d, tn), lambda l, j: (l, 0, j)),
                  pl.BlockSpec((1, 1, tn), lambda l, j: (l, 0, j))],
        out_specs=pl.BlockSpec((1, 1, tn), lambda l, j: (l, 0, j)),
        out_shape=jax.ShapeDtypeStruct((depth, 1, n), F32),
        compiler_params=_params("parallel", "parallel"),
        name="ada_modulation",
    )(c.reshape(d, 1), w_ada, b_ada.reshape(depth, 1, n))


def _norm_mod(x, g, sc, sh):
    ms = jnp.mean(x * x, axis=-1, keepdims=True)
    return (x * lax.rsqrt(ms + EPS) * g) * (1.0 + sc) + sh


def _norm_kernel(x_ref, g_ref, sc_ref, sh_ref, o_ref):
    o_ref[...] = _norm_mod(x_ref[...], g_ref[...], sc_ref[...], sh_ref[...]).astype(o_ref.dtype)


def norm_modulate(x, g, sc, sh, out_dtype):
    L, d = x.shape
    tm = 512
    vec = pl.BlockSpec((1, d), lambda i: (0, 0))
    return pl.pallas_call(
        _norm_kernel,
        grid=(L // tm,),
        in_specs=[pl.BlockSpec((tm, d), lambda i: (i, 0)), vec, vec, vec],
        out_specs=pl.BlockSpec((tm, d), lambda i: (i, 0)),
        out_shape=jax.ShapeDtypeStruct((L, d), out_dtype),
        compiler_params=_params("parallel"),
        name="norm_modulate",
    )(x, g, sc, sh)


def _inproj_kernel(a_ref, w_ref, o32_ref, o16_ref):
    acc = jnp.dot(a_ref[...], w_ref[...], preferred_element_type=F32)
    o32_ref[...] = acc
    o16_ref[...] = acc.astype(BF16)


def in_projection(h, w):
    L, k = h.shape
    n = w.shape[1]
    tm, tn = 512, 1920
    return pl.pallas_call(
        _inproj_kernel,
        grid=(n // tn, L // tm),
        in_specs=[pl.BlockSpec((tm, k), lambda j, i: (i, 0)),
                  pl.BlockSpec((k, tn), lambda j, i: (0, j))],
        out_specs=[pl.BlockSpec((tm, tn), lambda j, i: (i, j)),
                   pl.BlockSpec((tm, tn), lambda j, i: (i, j))],
        out_shape=[jax.ShapeDtypeStruct((L, n), F32), jax.ShapeDtypeStruct((L, n), BF16)],
        compiler_params=_params("parallel", "parallel"),
        name="in_projection",
    )(h, w)


def _outproj_kernel(ys_ref, ym_ref, yd_ref, ws_ref, wm_ref, wd_ref, x_ref, gate_ref, o_ref):
    acc = jnp.dot(ys_ref[...], ws_ref[...], preferred_element_type=F32)
    acc += jnp.dot(ym_ref[...], wm_ref[...], preferred_element_type=F32)
    acc += jnp.dot(yd_ref[...], wd_ref[...], preferred_element_type=F32)
    o_ref[...] = x_ref[...] + gate_ref[...] * acc


def out_projection(y_ssd, y_moba, y_diff, w_out, x, gate):
    L, d = x.shape
    tm, tn = 512, 1024
    return pl.pallas_call(
        _outproj_kernel,
        grid=(d // tn, L // tm),
        in_specs=[pl.BlockSpec((tm, SSD_WIDTH), lambda j, i: (i, 0)),
                  pl.BlockSpec((tm, MOBA_WIDTH), lambda j, i: (i, 0)),
                  pl.BlockSpec((tm, DIFF_WIDTH), lambda j, i: (i, 0)),
                  pl.BlockSpec((SSD_WIDTH, tn), lambda j, i: (0, j)),
                  pl.BlockSpec((MOBA_WIDTH, tn), lambda j, i: (SSD_WIDTH // MOBA_WIDTH, j)),
                  pl.BlockSpec((DIFF_WIDTH, tn), lambda j, i: ((SSD_WIDTH + MOBA_WIDTH) // DIFF_WIDTH, j)),
                  pl.BlockSpec((tm, tn), lambda j, i: (i, j)),
                  pl.BlockSpec((1, tn), lambda j, i: (0, j))],
        out_specs=pl.BlockSpec((tm, tn), lambda j, i: (i, j)),
        out_shape=jax.ShapeDtypeStruct((L, d), F32),
        compiler_params=_params("parallel", "parallel"),
        name="out_projection",
    )(y_ssd, y_moba, y_diff, w_out, w_out, w_out, x, gate)


def _swiglu_up_kernel(te_ref, tv_ref, a_ref, w1_ref, w3_ref, o_ref):
    t = pl.program_id(1)

    @pl.when(tv_ref[t] != 0)
    def _():
        a = a_ref[...]
        u = jnp.dot(a, w1_ref[0], preferred_element_type=F32)
        v = jnp.dot(a, w3_ref[0], preferred_element_type=F32)
        o_ref[...] = (_silu(u) * v).astype(o_ref.dtype)

    @pl.when(tv_ref[t] == 0)
    def _():
        o_ref[...] = jnp.zeros_like(o_ref)


def swiglu_up(a, w1, w3, tile_expert, tile_valid, tm, tn):
    rows, k = a.shape
    f = w1.shape[2]
    grid_spec = pltpu.PrefetchScalarGridSpec(
        num_scalar_prefetch=2,
        grid=(f // tn, rows // tm),
        in_specs=[pl.BlockSpec((tm, k), lambda j, t, te, tv: (t, 0)),
                  pl.BlockSpec((1, k, tn), lambda j, t, te, tv: (te[t], 0, j)),
                  pl.BlockSpec((1, k, tn), lambda j, t, te, tv: (te[t], 0, j))],
        out_specs=pl.BlockSpec((tm, tn), lambda j, t, te, tv: (t, j)),
    )
    return pl.pallas_call(
        _swiglu_up_kernel,
        grid_spec=grid_spec,
        out_shape=jax.ShapeDtypeStruct((rows, f), BF16),
        compiler_params=_params("parallel", "arbitrary"),
        name="swiglu_up",
    )(tile_expert, tile_valid, a, w1, w3)


def _swiglu_down_kernel(te_ref, tv_ref, g_ref, w2_ref, *rest, residual):
    t = pl.program_id(1)
    if residual:
        x_ref, gate_ref, o_ref = rest
    else:
        (o_ref,) = rest

    @pl.when(tv_ref[t] != 0)
    def _():
        acc = jnp.dot(g_ref[...], w2_ref[0], preferred_element_type=F32)
        if residual:
            acc = x_ref[...] + gate_ref[...] * acc
        o_ref[...] = acc.astype(o_ref.dtype)

    @pl.when(tv_ref[t] == 0)
    def _():
        o_ref[...] = jnp.zeros_like(o_ref)


def swiglu_down(g, w2, tile_expert, tile_valid, tm, tn, x=None, gate=None):
    rows, f = g.shape
    d = w2.shape[2]
    residual = x is not None
    in_specs = [pl.BlockSpec((tm, f), lambda j, t, te, tv: (t, 0)),
                pl.BlockSpec((1, f, tn), lambda j, t, te, tv: (te[t], 0, j))]
    args = [g, w2]
    if residual:
        in_specs += [pl.BlockSpec((tm, tn), lambda j, t, te, tv: (t, j)),
                     pl.BlockSpec((1, tn), lambda j, t, te, tv: (0, j))]
        args += [x, gate]
    grid_spec = pltpu.PrefetchScalarGridSpec(
        num_scalar_prefetch=2,
        grid=(d // tn, rows // tm),
        in_specs=in_specs,
        out_specs=pl.BlockSpec((tm, tn), lambda j, t, te, tv: (t, j)),
    )
    return pl.pallas_call(
        functools.partial(_swiglu_down_kernel, residual=residual),
        grid_spec=grid_spec,
        out_shape=jax.ShapeDtypeStruct((rows, d), F32 if residual else BF16),
        compiler_params=_params("parallel", "arbitrary"),
        name="swiglu_down",
    )(tile_expert, tile_valid, *args)


def _causal_conv(cur, tail_ref, w_ref, b_ref):
    t = cur.shape[0]
    tail = tail_ref[...]
    w = w_ref[...]
    row8 = lax.broadcasted_iota(jnp.int32, (SUBLANES, cur.shape[1]), 0)
    acc = cur * w[SSD_CONV - 1:SSD_CONV]
    top = cur[0:SUBLANES] * w[SSD_CONV - 1:SSD_CONV]
    for s in range(1, SSD_CONV):
        wk = w[SSD_CONV - 1 - s:SSD_CONV - s]
        rolled = pltpu.roll(cur, s, axis=0)
        acc += rolled * wk
        top += jnp.where(row8 < s, pltpu.roll(tail, s, axis=0), rolled[0:SUBLANES]) * wk
    tail_ref[...] = cur[t - SUBLANES:t]
    return jnp.concatenate([top, acc[SUBLANES:]], axis=0) + b_ref[...]


def _ssd_kernel(z_ref, x_ref, bc_ref, dt_ref, cwx_ref, cwb_ref, cbx_ref, cbb_ref, dtb_ref, alog_ref,
                dskip_ref, ng_ref, expand_ref, o_ref, tailx_ref, tailb_ref, state_ref, ybuf_ref):
    t = SSD_CHUNK
    hg = SSD_HEADS // SSD_GROUPS
    gw = SSD_WIDTH // SSD_GROUPS

    @pl.when(pl.program_id(0) == 0)
    def _():
        tailx_ref[...] = jnp.zeros_like(tailx_ref)
        tailb_ref[...] = jnp.zeros_like(tailb_ref)
        state_ref[...] = jnp.zeros_like(state_ref)

    xs = _silu(_causal_conv(x_ref[...], tailx_ref, cwx_ref, cbx_ref))
    bcm = _silu(_causal_conv(bc_ref[...], tailb_ref, cwb_ref, cbb_ref))
    dt = _softplus(dt_ref[...] + dtb_ref[...])
    a = -jnp.exp(alog_ref[...])
    row = lax.broadcasted_iota(jnp.int32, (t, t), 0)
    col = lax.broadcasted_iota(jnp.int32, (t, t), 1)
    tril = row >= col
    a_cs = jnp.dot(jnp.where(tril, 1.0, 0.0), dt * a, precision=HIGHEST, preferred_element_type=F32)
    a_last = a_cs[t - 1:t]
    per_head = jnp.concatenate(
        [dt, jnp.exp(a_last - a_cs), jnp.exp(a_cs), jnp.broadcast_to(jnp.exp(a_last), (SUBLANES, LANES))], axis=0)
    spread = jnp.dot(per_head, expand_ref[...], precision=HIGHEST, preferred_element_type=F32)
    dt_x, to_end_x, ea_x, cd_x = spread[0:t], spread[t:2 * t], spread[2 * t:3 * t], spread[3 * t:3 * t + 1]
    xdt = xs * dt_x
    xdt_b = xdt.astype(BF16)
    xw_b = (xdt * to_end_x).astype(BF16)
    a_cs_t = a_cs.T

    y_off = []
    for g in range(SSD_GROUPS):
        bm = bcm[:, g * SSD_STATE:(g + 1) * SSD_STATE]
        cm_b = bcm[:, (SSD_GROUPS + g) * SSD_STATE:(SSD_GROUPS + g + 1) * SSD_STATE].astype(BF16)
        cb = lax.dot_general(cm_b, bm.astype(BF16), (((1,), (1,)), ((), ())), preferred_element_type=F32)
        h_prev = state_ref[g]
        y_off.append(jnp.dot(cm_b, h_prev.astype(BF16), preferred_element_type=F32)
                     * ea_x[:, g * gw:(g + 1) * gw])
        st_new = jnp.dot(bm.T.astype(BF16), xw_b[:, g * gw:(g + 1) * gw], preferred_element_type=F32)
        state_ref[g] = h_prev * cd_x[:, g * gw:(g + 1) * gw] + st_new
        for r in range(0, hg, 2):
            pair = []
            for h in (g * hg + r, g * hg + r + 1):
                diff = a_cs[:, h:h + 1] - a_cs_t[h:h + 1, :]
                m = (cb * jnp.exp(jnp.where(tril, diff, -jnp.inf))).astype(BF16)
                pair.append(jnp.dot(m, xdt_b[:, h * SSD_HEAD_DIM:(h + 1) * SSD_HEAD_DIM],
                                    preferred_element_type=F32))
            lo = (g * hg + r) * SSD_HEAD_DIM
            ybuf_ref[:, lo:lo + 2 * SSD_HEAD_DIM] = jnp.concatenate(pair, axis=1)

    y = ybuf_ref[...] + jnp.concatenate(y_off, axis=1) + xs * dskip_ref[...]
    y = y * _silu(z_ref[...])
    outs = []
    for g in range(SSD_GROUPS):
        yg = y[:, g * gw:(g + 1) * gw]
        outs.append(yg * lax.rsqrt(jnp.mean(yg * yg, axis=-1, keepdims=True) + EPS))
    o_ref[...] = (jnp.concatenate(outs, axis=1) * ng_ref[...]).astype(o_ref.dtype)


def ssd_mixer(proj, conv_w, conv_b, dt_bias, a_log, d_skip, norm_g):
    L = proj.shape[0]
    t = SSD_CHUNK
    assert L % t == 0

    def pad_lanes(v):
        return jnp.pad(v, (0, LANES - v.shape[0])).reshape(1, LANES)

    expand = (jnp.arange(SSD_WIDTH)[None, :] // SSD_HEAD_DIM == jnp.arange(LANES)[:, None]).astype(F32)
    full = lambda shape: pl.BlockSpec(shape, lambda c: (0,) * len(shape))
    return pl.pallas_call(
        _ssd_kernel,
        grid=(L // t,),
        in_specs=[pl.BlockSpec((t, SSD_WIDTH), lambda c: (c, COL_Z // SSD_WIDTH)),
                  pl.BlockSpec((t, SSD_WIDTH), lambda c: (c, COL_X // SSD_WIDTH)),
                  pl.BlockSpec((t, SSD_BC), lambda c: (c, COL_BC // SSD_BC)),
                  pl.BlockSpec((t, LANES), lambda c: (c, COL_DT // LANES)),
                  full((SSD_CONV, SSD_WIDTH)), full((SSD_CONV, SSD_BC)),
                  full((1, SSD_WIDTH)), full((1, SSD_BC)),
                  full((1, LANES)), full((1, LANES)), full((1, SSD_WIDTH)), full((1, SSD_WIDTH)),
                  full((LANES, SSD_WIDTH))],
        out_specs=pl.BlockSpec((t, SSD_WIDTH), lambda c: (c, 0)),
        out_shape=jax.ShapeDtypeStruct((L, SSD_WIDTH), BF16),
        scratch_shapes=[pltpu.VMEM((SUBLANES, SSD_WIDTH), F32), pltpu.VMEM((SUBLANES, SSD_BC), F32),
                        pltpu.VMEM((SSD_GROUPS, SSD_STATE, SSD_WIDTH // SSD_GROUPS), F32),
                        pltpu.VMEM((t, SSD_WIDTH), F32)],
        compiler_params=_params("arbitrary"),
        name="ssd_mixer",
    )(proj, proj, proj, proj,
      conv_w[:, :SSD_WIDTH], conv_w[:, SSD_WIDTH:], conv_b[:SSD_WIDTH].reshape(1, -1),
      conv_b[SSD_WIDTH:].reshape(1, -1), pad_lanes(dt_bias), pad_lanes(a_log),
      jnp.repeat(d_skip, SSD_HEAD_DIM).reshape(1, -1), norm_g.reshape(1, -1), expand)


def _moba_prep_kernel(q_ref, k_ref, v_ref, qa_ref, ka_ref, va_ref, kmean_ref):
    own = pl.program_id(0)
    t = MOBA_BLOCK
    dh = MOBA_HEAD_DIM
    nbl = LANES // 4

    @pl.when(own == 0)
    def _():
        kmean_ref[...] = jnp.zeros_like(kmean_ref)

    lane = lax.broadcasted_iota(jnp.int32, (t, LANES), 1)
    lane_s = lax.broadcasted_iota(jnp.int32, (t, nbl), 1)
    zeros_pad = jnp.zeros((t, LANES - dh - nbl), F32)
    own_onehot = jnp.where(lane_s == own, 1.0, 0.0)
    q_t = q_ref[...].T
    va_ref[:, 0] = v_ref[...].T.reshape(MOBA_HEADS, dh, t).astype(BF16)
    for h in range(MOBA_HEADS):
        qh = q_ref[:, h * dh:(h + 1) * dh]
        kh = k_ref[:, h * dh:(h + 1) * dh]
        gate = lax.dot_general(qh, kmean_ref[h], (((1,), (1,)), ((), ())),
                               precision=HIGHEST, preferred_element_type=F32)
        gate = jnp.where(lane < own, gate, -jnp.inf)
        sel = jnp.zeros((t, LANES), jnp.bool_)
        for _ in range(MOBA_TOPK):
            m = jnp.max(gate, axis=1, keepdims=True)
            idx = jnp.min(jnp.where(gate == m, lane, LANES), axis=1, keepdims=True)
            hit = (lane == idx) & (m > -jnp.inf)
            sel = sel | hit
            gate = jnp.where(lane == idx, -jnp.inf, gate)
        offs_t = jnp.where(sel | (lane >= own), 0.0, NEG_BIG).T
        qa_ref[h] = jnp.concatenate([q_t[h * dh:(h + 1) * dh] * (dh ** -0.5), offs_t[:nbl],
                                     jnp.zeros((LANES - dh - nbl, t), F32)], axis=0).astype(BF16)
        ka_ref[h] = jnp.concatenate([kh, own_onehot, zeros_pad], axis=1).astype(BF16)
        kmean_ref[h, pl.ds(own, 1), :] = jnp.mean(kh, axis=0, keepdims=True)


def moba_prep(proj):
    L = proj.shape[0]
    t = MOBA_BLOCK
    assert L % t == 0 and L // t <= LANES // 4
    return pl.pallas_call(
        _moba_prep_kernel,
        grid=(L // t,),
        in_specs=[pl.BlockSpec((t, MOBA_WIDTH), lambda i: (i, COL_MQ // MOBA_WIDTH)),
                  pl.BlockSpec((t, MOBA_WIDTH), lambda i: (i, COL_MK // MOBA_WIDTH)),
                  pl.BlockSpec((t, MOBA_WIDTH), lambda i: (i, COL_MV // MOBA_WIDTH))],
        out_specs=[pl.BlockSpec((MOBA_HEADS, LANES, t), lambda i: (0, 0, i)),
                   pl.BlockSpec((MOBA_HEADS, t, LANES), lambda i: (0, i, 0)),
                   pl.BlockSpec((MOBA_HEADS, 1, MOBA_HEAD_DIM, t), lambda i: (0, i, 0, 0))],
        out_shape=[jax.ShapeDtypeStruct((MOBA_HEADS, LANES, L), BF16),
                   jax.ShapeDtypeStruct((MOBA_HEADS, L, LANES), BF16),
                   jax.ShapeDtypeStruct((MOBA_HEADS, L // t, MOBA_HEAD_DIM, t), BF16)],
        scratch_shapes=[pltpu.VMEM((MOBA_HEADS, LANES, MOBA_HEAD_DIM), F32)],
        compiler_params=_params("arbitrary"),
        name="moba_prep",
    )(proj, proj, proj)


def _col_reduce(x, op):
    while x.shape[0] > SUBLANES:
        half = x.shape[0] // 2
        x = op(x[:half], x[half:])
    return jnp.max(x, axis=0, keepdims=True) if op is jnp.maximum else jnp.sum(x, axis=0, keepdims=True)


def _attention_kernel(near_ref, q_ref, k_ref, v_ref, posr_ref, posc_ref, tbl_ref, g_ref, lam_ref, o_ref,
                      m_ref, l_ref, al_ref, acc_ref, sa_ref, sb_ref, pa_ref, pb_ref, *, moba, lambda_init):
    hp = pl.program_id(0)
    qi = pl.program_id(1)
    nq = pl.num_programs(1)
    t = ATT_T
    m_ref[...] = jnp.full_like(m_ref, -jnp.inf)
    l_ref[...] = jnp.zeros_like(l_ref)
    acc_ref[...] = jnp.zeros_like(acc_ref)
    if moba:
        q_t = [q_ref[0], q_ref[1]]
        heads = [2 * hp, 2 * hp + 1]
    else:
        q = q_ref[...] * (DIFF_QK_DIM ** -0.5)
        half = lax.broadcasted_iota(jnp.int32, q.shape, 0) < DIFF_QK_DIM
        q_t = [jnp.where(half, q, jnp.zeros_like(q)), jnp.where(half, jnp.zeros_like(q), q)]
        heads = [MOBA_HEADS + hp]
    trow = [tbl_ref[pl.ds(hd, 1), :] for hd in heads]
    posq = posr_ref[pl.ds(qi, 1), :]

    def scores(ki, dst_ref):
        rows = pl.ds(pl.multiple_of(ki * t, t), t)
        for s in range(2):
            k = k_ref[s, rows, :] if moba else k_ref[rows, :]
            dst_ref[s] = jnp.dot(k, q_t[s], preferred_element_type=F32)

    def accumulate(ki, p_ref):
        for s in range(2):
            pv = jnp.dot(v_ref[s if moba else 0, ki], p_ref[s], preferred_element_type=F32)
            acc_ref[s] = al_ref[s] * acc_ref[s] + pv

    scores(0, sa_ref)
    pb_ref[...] = jnp.zeros_like(pb_ref)
    al_ref[...] = jnp.ones_like(al_ref)

    def tile(ki, general, cur_ref, nxt_ref, p_ref, p_prev_ref):
        scores(jnp.minimum(ki + 1, qi), nxt_ref)
        accumulate(jnp.maximum(ki - 1, 0), p_prev_ref)
        ch = ATT_CHUNK
        mx = [None, None]
        for r in range(0, t, ch):
            if general:
                dist = jnp.clip(posq - posc_ref[ki, r:r + ch, :], 0, LANES - 1)
                bias = [jnp.concatenate(
                    [jnp.take_along_axis(jnp.broadcast_to(tr, (ch, LANES)), dist[:, j * LANES:(j + 1) * LANES],
                                         axis=1) for j in range(t // LANES)], axis=1) for tr in trow]
                key = lax.broadcasted_iota(jnp.int32, (ch, t), 0) + r
                qry = lax.broadcasted_iota(jnp.int32, (ch, t), 1)
                causal = key + ki * t <= qry + qi * t
            for s in range(2):
                blk = cur_ref[s, r:r + ch, :]
                if general:
                    blk = jnp.where(causal, blk + bias[s if moba else 0], -jnp.inf)
                    cur_ref[s, r:r + ch, :] = blk
                mx[s] = blk if mx[s] is None else jnp.maximum(mx[s], blk)
        for s in range(2):
            m_old = m_ref[s]
            m_tile = _col_reduce(mx[s], jnp.maximum)
            if general:
                m_new = jnp.maximum(m_old, m_tile)
                shift = m_new
            else:
                c = trow[s if moba else 0][:, LANES - 1:LANES]
                m_new = jnp.maximum(m_old, m_tile + c)
                shift = m_new - c
            alpha = jnp.exp(m_old - m_new)
            sm = None
            for r in range(0, t, ch):
                p = jnp.exp(cur_ref[s, r:r + ch, :] - shift)
                p_ref[s, r:r + ch, :] = p.astype(BF16)
                sm = p if sm is None else sm + p
            l_ref[s] = alpha * l_ref[s] + _col_reduce(sm, jnp.add)
            al_ref[s] = alpha
            m_ref[s] = m_new

    def step(ki, *bufs):
        flag = near_ref[qi * nq + ki]

        @pl.when(flag == 0)
        def _():
            tile(ki, False, *bufs)

        @pl.when(flag != 0)
        def _():
            tile(ki, True, *bufs)

    def pair(j, carry):
        step(2 * j, sa_ref, sb_ref, pa_ref, pb_ref)
        step(2 * j + 1, sb_ref, sa_ref, pb_ref, pa_ref)
        return carry

    lax.fori_loop(0, (qi + 1) // 2, pair, 0)

    @pl.when(qi % 2 == 0)
    def _():
        step(qi, sa_ref, sb_ref, pa_ref, pb_ref)
        accumulate(qi, pa_ref)

    @pl.when(qi % 2 == 1)
    def _():
        accumulate(qi, pb_ref)

    if moba:
        outs = []
        for s in range(2):
            o = acc_ref[s] / l_ref[s]
            ms = jnp.sum(o * o, axis=0, keepdims=True) * (1.0 / MOBA_HEAD_DIM)
            outs.append(o * lax.rsqrt(ms + EPS) * g_ref[s])
        y_t = jnp.concatenate(outs, axis=0)
    else:
        lp = lam_ref[...]
        lam = (jnp.exp(jnp.sum(lp[0:1] * lp[1:2], axis=1, keepdims=True))
               - jnp.exp(jnp.sum(lp[2:3] * lp[3:4], axis=1, keepdims=True)) + lambda_init)
        o = acc_ref[0] / l_ref[0] - lam * (acc_ref[1] / l_ref[1])
        ms = jnp.sum(o * o, axis=0, keepdims=True) * (1.0 / DIFF_V_DIM)
        y_t = (o * lax.rsqrt(ms + EPS) * g_ref[0]) * (1.0 - lambda_init)
    o_ref[...] = y_t.T.astype(o_ref.dtype)


def _attention_call(kernel, steps, dv, L, near, in_specs, args, name):
    t = ATT_T
    full = lambda a: pl.BlockSpec(a.shape, lambda h, i, nr: (0,) * a.ndim)
    grid_spec = pltpu.PrefetchScalarGridSpec(
        num_scalar_prefetch=1,
        grid=(steps, L // t),
        in_specs=in_specs + [full(a) for a in args[len(in_specs):]],
        out_specs=pl.BlockSpec((t, LANES), lambda h, i, nr: (i, h)),
        scratch_shapes=[pltpu.VMEM((2, 1, t), F32), pltpu.VMEM((2, 1, t), F32), pltpu.VMEM((2, 1, t), F32),
                        pltpu.VMEM((2, dv, t), F32), pltpu.VMEM((2, t, t), F32), pltpu.VMEM((2, t, t), F32),
                        pltpu.VMEM((2, t, t), BF16), pltpu.VMEM((2, t, t), BF16)],
    )
    return pl.pallas_call(kernel, grid_spec=grid_spec, out_shape=jax.ShapeDtypeStruct((L, steps * LANES), BF16),
                          compiler_params=_params("parallel", "arbitrary"), name=name)(near, *args)


def moba_attention(qa_t, ka, va_t, near, pos_rows, pos_cols, table, norm_g):
    heads, L, _ = ka.shape
    t = ATT_T
    kernel = functools.partial(_attention_kernel, moba=True, lambda_init=None)
    g = norm_g.reshape(heads, MOBA_HEAD_DIM, 1)
    return _attention_call(
        kernel, heads // 2, MOBA_HEAD_DIM, L, near,
        [pl.BlockSpec((2, LANES, t), lambda h, i, nr: (h, 0, i)),
         pl.BlockSpec((2, L, LANES), lambda h, i, nr: (h, 0, 0)),
         pl.BlockSpec((2, L // t, MOBA_HEAD_DIM, t), lambda h, i, nr: (h, 0, 0, 0)),
         pl.BlockSpec(pos_rows.shape, lambda h, i, nr: (0, 0)),
         pl.BlockSpec(pos_cols.shape, lambda h, i, nr: (0, 0, 0)),
         pl.BlockSpec(table.shape, lambda h, i, nr: (0, 0)),
         pl.BlockSpec((2, MOBA_HEAD_DIM, 1), lambda h, i, nr: (h, 0, 0))],
        [qa_t, ka, va_t, pos_rows, pos_cols, table, g, jnp.zeros((4, DIFF_QK_DIM), F32)], "moba_attention")


def diff_attention(proj16, near, pos_rows, pos_cols, table, lam_params, subln_g, lambda_init):
    L = proj16.shape[0]
    t = ATT_T
    q_t = proj16[:, COL_DQ:COL_DQ + DIFF_WIDTH].T
    v_t = jnp.transpose(proj16[:, COL_DV:COL_DV + DIFF_WIDTH].reshape(L // t, t, DIFF_HEADS, DIFF_V_DIM),
                        (2, 0, 3, 1))
    kernel = functools.partial(_attention_kernel, moba=False, lambda_init=lambda_init)
    return _attention_call(
        kernel, DIFF_HEADS, DIFF_V_DIM, L, near,
        [pl.BlockSpec((LANES, t), lambda h, i, nr: (h, i)),
         pl.BlockSpec((L, LANES), lambda h, i, nr: (0, COL_DK // LANES + h)),
         pl.BlockSpec((1, L // t, DIFF_V_DIM, t), lambda h, i, nr: (h, 0, 0, 0))],
        [q_t, proj16, v_t, pos_rows, pos_cols, table, subln_g.reshape(1, DIFF_V_DIM, 1), lam_params],
        "diff_attention")


def _rel_bucket(dist):
    n = jnp.maximum(dist, 0)
    max_exact = REL_BUCKETS // 2
    nf = jnp.maximum(n, 1).astype(F32)
    large = max_exact + (jnp.log(nf / max_exact) / math.log(REL_MAX_DIST / max_exact)
                         * (REL_BUCKETS - max_exact)).astype(jnp.int32)
    return jnp.where(n < max_exact, n, jnp.minimum(large, REL_BUCKETS - 1))


def attention_tables(positions, rel_bias):
    L = positions.shape[0]
    t = ATT_T
    buckets = _rel_bucket(jnp.arange(LANES, dtype=jnp.int32))
    table = rel_bias[buckets].T
    pos_rows = positions.reshape(L // t, t)
    lo, hi = jnp.min(pos_rows, axis=1), jnp.max(pos_rows, axis=1)
    near = (lo[:, None] - hi[None, :] < LANES) | jnp.eye(L // t, dtype=bool)
    return table, pos_rows, positions.reshape(L // t, t, 1), near.astype(jnp.int32).reshape(-1)


def _router_kernel(x_ref, g_ref, sc_ref, sh_ref, rw_ref, h_ref, comb_ref, rank_ref, cum_ref, total_ref, cnt_ref):
    t = TOK_T

    @pl.when(pl.program_id(0) == 0)
    def _():
        cnt_ref[...] = jnp.zeros_like(cnt_ref)

    h = _norm_mod(x_ref[...], g_ref[...], sc_ref[...], sh_ref[...])
    h_ref[...] = h.astype(BF16)
    lane = lax.broadcasted_iota(jnp.int32, (t, LANES), 1)
    logits = jnp.dot(h, rw_ref[...], precision=HIGHEST, preferred_element_type=F32)
    logits = jnp.where(lane < N_EXPERTS, logits, -jnp.inf)
    m1 = jnp.max(logits, axis=1, keepdims=True)
    i1 = jnp.min(jnp.where(logits == m1, lane, LANES), axis=1, keepdims=True)
    rest = jnp.where(lane == i1, -jnp.inf, logits)
    m2 = jnp.max(rest, axis=1, keepdims=True)
    i2 = jnp.min(jnp.where(rest == m2, lane, LANES), axis=1, keepdims=True)
    e2 = jnp.exp(m2 - m1)
    denom = 1.0 + e2
    comb_ref[...] = jnp.where(lane == i1, 1.0 / denom, 0.0) + jnp.where(lane == i2, e2 / denom, 0.0)
    sel = jnp.where((lane == i1) | (lane == i2), 1.0, 0.0)
    row = lax.broadcasted_iota(jnp.int32, (t, t), 0)
    col = lax.broadcasted_iota(jnp.int32, (t, t), 1)
    before = jnp.dot(jnp.where(row > col, 1.0, 0.0).astype(BF16), sel.astype(BF16), preferred_element_type=F32)
    carry = cnt_ref[...]
    cum_ref[0] = carry
    rank_ref[...] = jnp.where(sel > 0.0, before + carry, -1.0)
    carry = carry + jnp.sum(sel, axis=0, keepdims=True)
    cnt_ref[...] = carry
    total_ref[...] = carry


def moe_router(x, g, sc, sh, router_w):
    L, d = x.shape
    t = TOK_T
    rw = jnp.pad(router_w, ((0, 0), (0, LANES - N_EXPERTS)))
    vec = pl.BlockSpec((1, d), lambda i: (0, 0))
    tok = pl.BlockSpec((t, LANES), lambda i: (i, 0))
    return pl.pallas_call(
        _router_kernel,
        grid=(L // t,),
        in_specs=[pl.BlockSpec((t, d), lambda i: (i, 0)), vec, vec, vec,
                  pl.BlockSpec((d, LANES), lambda i: (0, 0))],
        out_specs=[pl.BlockSpec((t, d), lambda i: (i, 0)), tok, tok,
                   pl.BlockSpec((1, 1, LANES), lambda i: (i, 0, 0)),
                   pl.BlockSpec((1, LANES), lambda i: (0, 0))],
        out_shape=[jax.ShapeDtypeStruct((L, d), BF16), jax.ShapeDtypeStruct((L, LANES), F32),
                   jax.ShapeDtypeStruct((L, LANES), F32), jax.ShapeDtypeStruct((L // t, 1, LANES), F32),
                   jax.ShapeDtypeStruct((1, LANES), F32)],
        scratch_shapes=[pltpu.VMEM((1, LANES), F32)],
        compiler_params=_params("arbitrary"),
        name="moe_router",
    )(x, g, sc, sh, rw)


def _item_lists(hit, n_items):
    rows, cols = hit.shape
    n_real = jnp.sum(hit.astype(jnp.int32))
    idx = jnp.nonzero(hit.reshape(-1), size=n_items, fill_value=0)[0].astype(jnp.int32)
    k = jnp.arange(n_items, dtype=jnp.int32)
    real = k < n_real
    idx = jnp.where(real, idx, idx[jnp.maximum(n_real - 1, 0)])
    r, c = idx // cols, idx % cols
    prev_r = jnp.concatenate([jnp.full((1,), -1, jnp.int32), r[:-1]])
    next_r = jnp.concatenate([r[1:], jnp.full((1,), -1, jnp.int32)])
    first = real & (r != prev_r)
    last = real & ((r != next_r) | (k == n_real - 1))
    flags = real.astype(jnp.int32) + 2 * first.astype(jnp.int32) + 4 * last.astype(jnp.int32)
    return r, c, flags


def moe_plan(rank, cum, total, L):
    tm, tb = MOE_TM, TOK_T
    nb = L // tb
    n_tiles = 2 * L // tm + N_EXPERTS
    counts = total[0, :N_EXPERTS].astype(jnp.int32)
    tiles_e = (counts + tm - 1) // tm
    tile_end = jnp.cumsum(tiles_e)
    tile_start = tile_end - tiles_e
    tid = jnp.arange(n_tiles, dtype=jnp.int32)
    tile_valid = tid < tile_end[-1]
    tile_expert = jnp.minimum(jnp.searchsorted(tile_end, tid, side="right"), N_EXPERTS - 1).astype(jnp.int32)
    local_row = (tid - tile_start[tile_expert]) * tm
    r = rank[:, :N_EXPERTS].astype(jnp.int32)
    dest = jnp.where(r >= 0, r + (tile_start * tm)[None, :], -1)
    cum_i = cum[:, 0, :N_EXPERTS].astype(jnp.int32)
    cum_next = jnp.concatenate([cum_i[1:], counts[None, :]], axis=0)
    lo = cum_i[:, tile_expert].T
    hi = cum_next[:, tile_expert].T
    hit = tile_valid[:, None] & (lo < (local_row + tm)[:, None]) & (hi > local_row[:, None])
    n_items = n_tiles + N_EXPERTS * nb
    pad_hit = hit | ((~tile_valid)[:, None] & (jnp.arange(nb) == 0)[None, :])
    gather_items = _item_lists(pad_hit, n_items)
    kb, tt, fl = _item_lists(hit.T, n_items)
    return dict(tile_expert=tile_expert, tile_valid=tile_valid.astype(jnp.int32),
                dest_rows=dest.T, dest_cols=dest.T.reshape(N_EXPERTS, L, 1),
                gather_items=gather_items, combine_items=(tt, kb, fl), n_tiles=n_tiles, n_items=n_items)


def _moe_gather_kernel(it_ref, ib_ref, if_ref, te_ref, dest_ref, h_ref, o_ref, acc_ref):
    i = pl.program_id(0)
    flag = if_ref[i]
    tile = it_ref[i]
    tm = MOE_TM

    @pl.when((flag & 2) != 0)
    def _():
        acc_ref[...] = jnp.zeros_like(acc_ref)

    @pl.when((flag & 1) != 0)
    def _():
        d = dest_ref[pl.ds(te_ref[tile], 1), :]
        rows = tile * tm + lax.broadcasted_iota(jnp.int32, (tm, 1), 0)
        onehot = jnp.where(d == rows, 1.0, 0.0).astype(BF16)
        acc_ref[...] += jnp.dot(onehot, h_ref[...], preferred_element_type=F32)

    @pl.when((flag & 4) != 0)
    def _():
        o_ref[...] = acc_ref[...].astype(o_ref.dtype)


def moe_gather(h, plan):
    L, d = h.shape
    tm, tb = MOE_TM, TOK_T
    it, ib, fl = plan["gather_items"]
    grid_spec = pltpu.PrefetchScalarGridSpec(
        num_scalar_prefetch=4,
        grid=(plan["n_items"],),
        in_specs=[pl.BlockSpec((N_EXPERTS, tb), lambda i, it, ib, fl, te: (0, ib[i])),
                  pl.BlockSpec((tb, d), lambda i, it, ib, fl, te: (ib[i], 0))],
        out_specs=pl.BlockSpec((tm, d), lambda i, it, ib, fl, te: (it[i], 0)),
        scratch_shapes=[pltpu.VMEM((tm, d), F32)],
    )
    return pl.pallas_call(
        _moe_gather_kernel,
        grid_spec=grid_spec,
        out_shape=jax.ShapeDtypeStruct((plan["n_tiles"] * tm, d), BF16),
        compiler_params=_params("arbitrary"),
        name="moe_gather",
    )(it, ib, fl, plan["tile_expert"], plan["dest_rows"], h)


def _moe_combine_kernel(it_ref, ib_ref, if_ref, te_ref, dest_ref, w_ref, ys_ref, x_ref, gate_ref, o_ref, acc_ref):
    i = pl.program_id(0)
    flag = if_ref[i]
    tile = it_ref[i]
    tm = MOE_TM

    @pl.when((flag & 2) != 0)
    def _():
        acc_ref[...] = jnp.zeros_like(acc_ref)

    @pl.when((flag & 1) != 0)
    def _():
        cols = tile * tm + lax.broadcasted_iota(jnp.int32, (1, tm), 1)
        onehot = jnp.where(dest_ref[0] == cols, 1.0, 0.0).astype(BF16)
        acc_ref[...] += w_ref[0] * jnp.dot(onehot, ys_ref[...], preferred_element_type=F32)

    @pl.when((flag & 4) != 0)
    def _():
        o_ref[...] = x_ref[...] + gate_ref[...] * acc_ref[...]


def moe_combine(ys, comb, x, gate, plan):
    L, d = x.shape
    tm, tb = MOE_TM, TOK_T
    it, ib, fl = plan["combine_items"]
    w_cols = comb[:, :N_EXPERTS].T.reshape(N_EXPERTS, L, 1)
    col = lambda i, it, ib, fl, te: (te[it[i]], ib[i], 0)
    grid_spec = pltpu.PrefetchScalarGridSpec(
        num_scalar_prefetch=4,
        grid=(plan["n_items"],),
        in_specs=[pl.BlockSpec((1, tb, 1), col), pl.BlockSpec((1, tb, 1), col),
                  pl.BlockSpec((tm, d), lambda i, it, ib, fl, te: (it[i], 0)),
                  pl.BlockSpec((tb, d), lambda i, it, ib, fl, te: (ib[i], 0)),
                  pl.BlockSpec((1, d), lambda i, it, ib, fl, te: (0, 0))],
        out_specs=pl.BlockSpec((tb, d), lambda i, it, ib, fl, te: (ib[i], 0)),
        scratch_shapes=[pltpu.VMEM((tb, d), F32)],
    )
    return pl.pallas_call(
        _moe_combine_kernel,
        grid_spec=grid_spec,
        out_shape=jax.ShapeDtypeStruct((L, d), F32),
        compiler_params=_params("arbitrary"),
        name="moe_combine",
    )(it, ib, fl, plan["tile_expert"], plan["dest_cols"], w_cols, ys, x, gate)


def moe_ffn(x, g, sc, sh, gate, router_w, w1, w3, w2):
    L = x.shape[0]
    h, comb, rank, cum, total = moe_router(x, g, sc, sh, router_w)
    plan = moe_plan(rank, cum, total, L)
    xs = moe_gather(h, plan)
    te, tv = plan["tile_expert"], plan["tile_valid"]
    act = swiglu_up(xs, w1, w3, te, tv, MOE_TM, 1024)
    ys = swiglu_down(act, w2, te, tv, MOE_TM, 1024)
    return moe_combine(ys, comb, x, gate, plan)


def dense_ffn(x, g, sc, sh, gate, w1, w3, w2):
    L = x.shape[0]
    tm = 512
    h = norm_modulate(x, g, sc, sh, BF16)
    te = jnp.zeros((L // tm,), jnp.int32)
    tv = jnp.ones((L // tm,), jnp.int32)
    act = swiglu_up(h, w1[None], w3[None], te, tv, tm, 1408)
    return swiglu_down(act, w2[None], te, tv, tm, 1024, x=x, gate=gate)


def _permute_in_proj(w):
    d = w.shape[0]
    return jnp.concatenate([w[:, :ORIG_DT], w[:, ORIG_DT + SSD_HEADS:], w[:, ORIG_DT:ORIG_DT + SSD_HEADS],
                            jnp.zeros((d, LANES - SSD_HEADS), w.dtype)], axis=1).astype(BF16)


def kernel(x, c, positions, rel_bias, w_ada, b_ada, norm_mix_g, w_in, conv_w, conv_b, dt_bias, a_log, d_skip, ssd_norm_g, moba_norm_g, diff_lambda, diff_subln_g, w_out, norm_ffn_g, dense_w1, dense_w3, dense_w2, router_w, expert_w1, expert_w3, expert_w2, final_g):
    batch, L, d = x.shape
    assert batch == 1 and d == D_MODEL
    x = x[0]
    mod = ada_modulation(c, w_ada, b_ada)
    table, pos_rows, pos_cols, near = attention_tables(positions[0], rel_bias)
    row = lambda v: v.reshape(1, -1)
    for layer in range(DEPTH):
        lambda_init = 0.8 - 0.6 * math.exp(-0.3 * layer)
        shift1, scale1, gate1, shift2, scale2, gate2 = (mod[layer, :, j * d:(j + 1) * d] for j in range(6))
        h = norm_modulate(x, row(norm_mix_g[layer]), scale1, shift1, BF16)
        proj, proj16 = in_projection(h, _permute_in_proj(w_in[layer]))
        y_ssd = ssd_mixer(proj, conv_w[layer], conv_b[layer], dt_bias[layer], a_log[layer], d_skip[layer],
                          ssd_norm_g[layer])
        qa, ka, va = moba_prep(proj)
        y_moba = moba_attention(qa, ka, va, near, pos_rows, pos_cols, table, moba_norm_g[layer])
        y_diff = diff_attention(proj16, near, pos_rows, pos_cols, table, diff_lambda[layer],
                                diff_subln_g[layer], lambda_init)
        x = out_projection(y_ssd, y_moba, y_diff, w_out[layer].astype(BF16), x, gate1)
        i = layer // 2
        g2 = row(norm_ffn_g[layer])
        if layer % 2 == 0:
            x = dense_ffn(x, g2, scale2, shift2, gate2, dense_w1[i].astype(BF16), dense_w3[i].astype(BF16),
                          dense_w2[i].astype(BF16))
        else:
            x = moe_ffn(x, g2, scale2, shift2, gate2, router_w[i], expert_w1[i].astype(BF16),
                        expert_w3[i].astype(BF16), expert_w2[i].astype(BF16))
    zero = jnp.zeros((1, d), F32)
    return norm_modulate(x, row(final_g), zero, zero, F32)[None]
```

```python
import functools
import math

import jax
import jax.numpy as jnp
from jax import lax
from jax.experimental import pallas as pl
from jax.experimental.pallas import tpu as pltpu

F32 = jnp.float32
BF16 = jnp.bfloat16
HIGHEST = lax.Precision.HIGHEST

D_MODEL = 2048
DEPTH = 2
SSD_HEADS = 16
SSD_HEAD_DIM = 64
SSD_WIDTH = SSD_HEADS * SSD_HEAD_DIM
SSD_GROUPS = 2
SSD_STATE = 128
SSD_CONV = 4
SSD_CHUNK = 256
SSD_BC = 2 * SSD_GROUPS * SSD_STATE
MOBA_HEADS = 8
MOBA_HEAD_DIM = 64
MOBA_WIDTH = MOBA_HEADS * MOBA_HEAD_DIM
MOBA_BLOCK = 256
MOBA_TOPK = 3
DIFF_HEADS = 4
DIFF_QK_DIM = 64
DIFF_V_DIM = 128
DIFF_WIDTH = DIFF_HEADS * DIFF_V_DIM
REL_BUCKETS = 32
REL_MAX_DIST = 128
D_FF_DENSE = 5632
N_EXPERTS = 8
D_FF_EXPERT = 7168
EPS = 1e-6

LANES = 128
SUBLANES = 8
VMEM_LIMIT = 56 * 1024 * 1024

COL_Z = 0
COL_X = SSD_WIDTH
COL_BC = COL_X + SSD_WIDTH
COL_MQ = COL_BC + SSD_BC
COL_MK = COL_MQ + MOBA_WIDTH
COL_MV = COL_MK + MOBA_WIDTH
COL_DQ = COL_MV + MOBA_WIDTH
COL_DK = COL_DQ + DIFF_WIDTH
COL_DV = COL_DK + DIFF_WIDTH
COL_DT = COL_DV + DIFF_WIDTH
PROJ_W = COL_DT + LANES
ORIG_DT = SSD_WIDTH + SSD_WIDTH + SSD_BC

ATT_T = 256
ATT_CHUNK = 32
TOK_T = 256
MOE_TM = 256
NEG_BIG = -1e9
MOBA_MAX_BLOCKS = 32
LOG2E = math.log2(math.e)
MOBA_Q_SCALE = MOBA_HEAD_DIM ** -0.5 * LOG2E
DIFF_Q_SCALE = DIFF_QK_DIM ** -0.5 * LOG2E


def _silu(x):
    return x * (1.0 / (1.0 + jnp.exp(-x)))


def _softplus(x):
    return jnp.maximum(x, 0.0) + jnp.log1p(jnp.exp(-jnp.abs(x)))


def _params(*sem):
    return pltpu.CompilerParams(dimension_semantics=sem, vmem_limit_bytes=VMEM_LIMIT)


def _ada_kernel(c_ref, w_ref, b_ref, o_ref):
    ca = _silu(c_ref[...])
    o_ref[0] = jnp.sum(ca * w_ref[0], axis=0, keepdims=True) + b_ref[0]


def ada_modulation(c, w_ada, b_ada):
    depth, d, n = w_ada.shape
    tn = 1024
    return pl.pallas_call(
        _ada_kernel,
        grid=(depth, n // tn),
        in_specs=[pl.BlockSpec((d, 1), lambda l, j: (0, 0)),
                  pl.BlockSpec((1, d, tn), lambda l, j: (l, 0, j)),
                  pl.BlockSpec((1, 1, tn), lambda l, j: (l, 0, j))],
        out_specs=pl.BlockSpec((1, 1, tn), lambda l, j: (l, 0, j)),
        out_shape=jax.ShapeDtypeStruct((depth, 1, n), F32),
        compiler_params=_params("parallel", "parallel"),
        name="ada_modulation",
    )(c.reshape(d, 1), w_ada, b_ada.reshape(depth, 1, n))


def _norm_mod(x, g, sc, sh):
    ms = jnp.mean(x * x, axis=-1, keepdims=True)
    return (x * lax.rsqrt(ms + EPS) * g) * (1.0 + sc) + sh


def _norm_kernel(x_ref, g_ref, sc_ref, sh_ref, o_ref):
    o_ref[...] = _norm_mod(x_ref[...], g_ref[...], sc_ref[...], sh_ref[...]).astype(o_ref.dtype)


def norm_modulate(x, g, sc, sh, out_dtype):
    L, d = x.shape
    tm = 512
    vec = pl.BlockSpec((1, d), lambda i: (0, 0))
    return pl.pallas_call(
        _norm_kernel,
        grid=(L // tm,),
        in_specs=[pl.BlockSpec((tm, d), lambda i: (i, 0)), vec, vec, vec],
        out_specs=pl.BlockSpec((tm, d), lambda i: (i, 0)),
        out_shape=jax.ShapeDtypeStruct((L, d), out_dtype),
        compiler_params=_params("parallel"),
        name="norm_modulate",
    )(x, g, sc, sh)


def _inproj_kernel(a_ref, w_ref, o32_ref, o16_ref):
    acc = jnp.dot(a_ref[...], w_ref[...], preferred_element_type=F32)
    o32_ref[...] = acc
    o16_ref[...] = acc.astype(BF16)


def in_projection(h, w):
    L, k = h.shape
    n = w.shape[1]
    tm, tn = 512, 1920
    return pl.pallas_call(
        _inproj_kernel,
        grid=(n // tn, L // tm),
        in_specs=[pl.BlockSpec((tm, k), lambda j, i: (i, 0)),
                  pl.BlockSpec((k, tn), lambda j, i: (0, j))],
        out_specs=[pl.BlockSpec((tm, tn), lambda j, i: (i, j)),
                   pl.BlockSpec((tm, tn), lambda j, i: (i, j))],
        out_shape=[jax.ShapeDtypeStruct((L, n), F32), jax.ShapeDtypeStruct((L, n), BF16)],
        compiler_params=_params("parallel", "parallel"),
        name="in_projection",
    )(h, w)


def _outproj_kernel(ys_ref, ym_ref, yd_ref, ws_ref, wm_ref, wd_ref, x_ref, gate_ref, o_ref, wb_ref):
    @pl.when(pl.program_id(1) == 0)
    def _():
        wb_ref[0:SSD_WIDTH] = ws_ref[...].astype(BF16)
        wb_ref[SSD_WIDTH:SSD_WIDTH + MOBA_WIDTH] = wm_ref[...].astype(BF16)
        wb_ref[SSD_WIDTH + MOBA_WIDTH:] = wd_ref[...].astype(BF16)

    acc = jnp.dot(ys_ref[...], wb_ref[0:SSD_WIDTH], preferred_element_type=F32)
    acc += jnp.dot(ym_ref[...], wb_ref[SSD_WIDTH:SSD_WIDTH + MOBA_WIDTH], preferred_element_type=F32)
    acc += jnp.dot(yd_ref[...], wb_ref[SSD_WIDTH + MOBA_WIDTH:], preferred_element_type=F32)
    o_ref[...] = x_ref[...] + gate_ref[...] * acc


def out_projection(y_ssd, y_moba, y_diff, w_out, x, gate):
    L, d = x.shape
    tm, tn = 512, 1024
    return pl.pallas_call(
        _outproj_kernel,
        grid=(d // tn, L // tm),
        in_specs=[pl.BlockSpec((tm, SSD_WIDTH), lambda j, i: (i, 0)),
                  pl.BlockSpec((tm, MOBA_WIDTH), lambda j, i: (i, 0)),
                  pl.BlockSpec((tm, DIFF_WIDTH), lambda j, i: (i, 0)),
                  pl.BlockSpec((SSD_WIDTH, tn), lambda j, i: (0, j)),
                  pl.BlockSpec((MOBA_WIDTH, tn), lambda j, i: (SSD_WIDTH // MOBA_WIDTH, j)),
                  pl.BlockSpec((DIFF_WIDTH, tn), lambda j, i: ((SSD_WIDTH + MOBA_WIDTH) // DIFF_WIDTH, j)),
                  pl.BlockSpec((tm, tn), lambda j, i: (i, j)),
                  pl.BlockSpec((1, tn), lambda j, i: (0, j))],
        out_specs=pl.BlockSpec((tm, tn), lambda j, i: (i, j)),
        out_shape=jax.ShapeDtypeStruct((L, d), F32),
        scratch_shapes=[pltpu.VMEM((w_out.shape[0], tn), BF16)],
        compiler_params=_params("arbitrary", "arbitrary"),
        name="out_projection",
    )(y_ssd, y_moba, y_diff, w_out, w_out, w_out, x, gate)


TILE_VALID = 1
TILE_NEW_WEIGHTS = 2


def tile_flags(tile_expert, tile_valid):
    prev = jnp.concatenate([jnp.full((1,), -1, jnp.int32), tile_expert[:-1]])
    return tile_valid * TILE_VALID + (tile_expert != prev).astype(jnp.int32) * TILE_NEW_WEIGHTS


def _swiglu_up_kernel(te_ref, tf_ref, a_ref, w1_ref, w3_ref, o_ref, w1b_ref, w3b_ref):
    flags = tf_ref[pl.program_id(1)]

    @pl.when((flags & TILE_NEW_WEIGHTS) != 0)
    def _():
        w1b_ref[...] = w1_ref[0].astype(BF16)
        w3b_ref[...] = w3_ref[0].astype(BF16)

    @pl.when((flags & TILE_VALID) != 0)
    def _():
        a = a_ref[...]
        u = jnp.dot(a, w1b_ref[...], preferred_element_type=F32)
        v = jnp.dot(a, w3b_ref[...], preferred_element_type=F32)
        o_ref[...] = (_silu(u) * v).astype(o_ref.dtype)

    @pl.when((flags & TILE_VALID) == 0)
    def _():
        o_ref[...] = jnp.zeros_like(o_ref)


def swiglu_up(a, w1, w3, tile_expert, flags, tm, tn):
    rows, k = a.shape
    f = w1.shape[2]
    grid_spec = pltpu.PrefetchScalarGridSpec(
        num_scalar_prefetch=2,
        grid=(f // tn, rows // tm),
        in_specs=[pl.BlockSpec((tm, k), lambda j, t, te, tf: (t, 0)),
                  pl.BlockSpec((1, k, tn), lambda j, t, te, tf: (te[t], 0, j)),
                  pl.BlockSpec((1, k, tn), lambda j, t, te, tf: (te[t], 0, j))],
        out_specs=pl.BlockSpec((tm, tn), lambda j, t, te, tf: (t, j)),
        scratch_shapes=[pltpu.VMEM((k, tn), BF16), pltpu.VMEM((k, tn), BF16)],
    )
    return pl.pallas_call(
        _swiglu_up_kernel,
        grid_spec=grid_spec,
        out_shape=jax.ShapeDtypeStruct((rows, f), BF16),
        compiler_params=_params("arbitrary", "arbitrary"),
        name="swiglu_up",
    )(tile_expert, flags, a, w1, w3)


def _swiglu_down_kernel(te_ref, tf_ref, g_ref, w2_ref, *rest, residual):
    flags = tf_ref[pl.program_id(1)]
    if residual:
        x_ref, gate_ref, o_ref, wb_ref = rest
    else:
        o_ref, wb_ref = rest

    @pl.when((flags & TILE_NEW_WEIGHTS) != 0)
    def _():
        wb_ref[...] = w2_ref[0].astype(BF16)

    @pl.when((flags & TILE_VALID) != 0)
    def _():
        acc = jnp.dot(g_ref[...], wb_ref[...], preferred_element_type=F32)
        if residual:
            acc = x_ref[...] + gate_ref[...] * acc
        o_ref[...] = acc.astype(o_ref.dtype)

    @pl.when((flags & TILE_VALID) == 0)
    def _():
        o_ref[...] = jnp.zeros_like(o_ref)


def swiglu_down(g, w2, tile_expert, flags, tm, tn, x=None, gate=None):
    rows, f = g.shape
    d = w2.shape[2]
    residual = x is not None
    in_specs = [pl.BlockSpec((tm, f), lambda j, t, te, tf: (t, 0)),
                pl.BlockSpec((1, f, tn), lambda j, t, te, tf: (te[t], 0, j))]
    args = [g, w2]
    if residual:
        in_specs += [pl.BlockSpec((tm, tn), lambda j, t, te, tf: (t, j)),
                     pl.BlockSpec((1, tn), lambda j, t, te, tf: (0, j))]
        args += [x, gate]
    grid_spec = pltpu.PrefetchScalarGridSpec(
        num_scalar_prefetch=2,
        grid=(d // tn, rows // tm),
        in_specs=in_specs,
        out_specs=pl.BlockSpec((tm, tn), lambda j, t, te, tf: (t, j)),
        scratch_shapes=[pltpu.VMEM((f, tn), BF16)],
    )
    return pl.pallas_call(
        functools.partial(_swiglu_down_kernel, residual=residual),
        grid_spec=grid_spec,
        out_shape=jax.ShapeDtypeStruct((rows, d), F32 if residual else BF16),
        compiler_params=_params("arbitrary", "arbitrary"),
        name="swiglu_down",
    )(tile_expert, flags, *args)


def _causal_conv(cur, tail_ref, w_ref, b_ref):
    t = cur.shape[0]
    tail = tail_ref[...]
    w = w_ref[...]
    row8 = lax.broadcasted_iota(jnp.int32, (SUBLANES, cur.shape[1]), 0)
    acc = cur * w[SSD_CONV - 1:SSD_CONV]
    top = cur[0:SUBLANES] * w[SSD_CONV - 1:SSD_CONV]
    for s in range(1, SSD_CONV):
        wk = w[SSD_CONV - 1 - s:SSD_CONV - s]
        rolled = pltpu.roll(cur, s, axis=0)
        acc += rolled * wk
        top += jnp.where(row8 < s, pltpu.roll(tail, s, axis=0), rolled[0:SUBLANES]) * wk
    tail_ref[...] = cur[t - SUBLANES:t]
    return jnp.concatenate([top, acc[SUBLANES:]], axis=0) + b_ref[...]


def _ssd_kernel(z_ref, x_ref, bc_ref, dt_ref, cwx_ref, cwb_ref, cbx_ref, cbb_ref, dtb_ref, alog_ref,
                dskip_ref, ng_ref, expand_ref, o_ref, tailx_ref, tailb_ref, state_ref, ybuf_ref):
    t = SSD_CHUNK
    hg = SSD_HEADS // SSD_GROUPS
    gw = SSD_WIDTH // SSD_GROUPS

    @pl.when(pl.program_id(0) == 0)
    def _():
        tailx_ref[...] = jnp.zeros_like(tailx_ref)
        tailb_ref[...] = jnp.zeros_like(tailb_ref)
        state_ref[...] = jnp.zeros_like(state_ref)

    xs = _silu(_causal_conv(x_ref[...], tailx_ref, cwx_ref, cbx_ref))
    bcm = _silu(_causal_conv(bc_ref[...], tailb_ref, cwb_ref, cbb_ref))
    dt = _softplus(dt_ref[...] + dtb_ref[...])
    a = -jnp.exp(alog_ref[...])
    row = lax.broadcasted_iota(jnp.int32, (t, t), 0)
    col = lax.broadcasted_iota(jnp.int32, (t, t), 1)
    tril = row >= col
    a_cs = jnp.dot(jnp.where(tril, 1.0, 0.0), dt * a, precision=HIGHEST, preferred_element_type=F32)
    a_last = a_cs[t - 1:t]
    per_head = jnp.concatenate(
        [dt, jnp.exp(a_last - a_cs), jnp.exp(a_cs), jnp.broadcast_to(jnp.exp(a_last), (SUBLANES, LANES))], axis=0)
    spread = jnp.dot(per_head, expand_ref[...], precision=HIGHEST, preferred_element_type=F32)
    dt_x, to_end_x, ea_x, cd_x = spread[0:t], spread[t:2 * t], spread[2 * t:3 * t], spread[3 * t:3 * t + 1]
    xdt = xs * dt_x
    xdt_b = xdt.astype(BF16)
    xw_b = (xdt * to_end_x).astype(BF16)
    a_cs_t = a_cs.T

    y_off = []
    for g in range(SSD_GROUPS):
        bm = bcm[:, g * SSD_STATE:(g + 1) * SSD_STATE]
        cm_b = bcm[:, (SSD_GROUPS + g) * SSD_STATE:(SSD_GROUPS + g + 1) * SSD_STATE].astype(BF16)
        cb = lax.dot_general(cm_b, bm.astype(BF16), (((1,), (1,)), ((), ())), preferred_element_type=F32)
        h_prev = state_ref[g]
        y_off.append(jnp.dot(cm_b, h_prev.astype(BF16), preferred_element_type=F32)
                     * ea_x[:, g * gw:(g + 1) * gw])
        st_new = jnp.dot(bm.T.astype(BF16), xw_b[:, g * gw:(g + 1) * gw], preferred_element_type=F32)
        state_ref[g] = h_prev * cd_x[:, g * gw:(g + 1) * gw] + st_new
        for r in range(0, hg, 2):
            pair = []
            for h in (g * hg + r, g * hg + r + 1):
                diff = a_cs[:, h:h + 1] - a_cs_t[h:h + 1, :]
                m = (cb * jnp.exp(jnp.where(tril, diff, -jnp.inf))).astype(BF16)
                pair.append(jnp.dot(m, xdt_b[:, h * SSD_HEAD_DIM:(h + 1) * SSD_HEAD_DIM],
                                    preferred_element_type=F32))
            lo = (g * hg + r) * SSD_HEAD_DIM
            ybuf_ref[:, lo:lo + 2 * SSD_HEAD_DIM] = jnp.concatenate(pair, axis=1)

    y = ybuf_ref[...] + jnp.concatenate(y_off, axis=1) + xs * dskip_ref[...]
    y = y * _silu(z_ref[...])
    outs = []
    for g in range(SSD_GROUPS):
        yg = y[:, g * gw:(g + 1) * gw]
        outs.append(yg * lax.rsqrt(jnp.mean(yg * yg, axis=-1, keepdims=True) + EPS))
    o_ref[...] = (jnp.concatenate(outs, axis=1) * ng_ref[...]).astype(o_ref.dtype)


def ssd_mixer(proj, conv_w, conv_b, dt_bias, a_log, d_skip, norm_g):
    L = proj.shape[0]
    t = SSD_CHUNK
    assert L % t == 0

    def pad_lanes(v):
        return jnp.pad(v, (0, LANES - v.shape[0])).reshape(1, LANES)

    expand = (jnp.arange(SSD_WIDTH)[None, :] // SSD_HEAD_DIM == jnp.arange(LANES)[:, None]).astype(F32)
    full = lambda shape: pl.BlockSpec(shape, lambda c: (0,) * len(shape))
    return pl.pallas_call(
        _ssd_kernel,
        grid=(L // t,),
        in_specs=[pl.BlockSpec((t, SSD_WIDTH), lambda c: (c, COL_Z // SSD_WIDTH)),
                  pl.BlockSpec((t, SSD_WIDTH), lambda c: (c, COL_X // SSD_WIDTH)),
                  pl.BlockSpec((t, SSD_BC), lambda c: (c, COL_BC // SSD_BC)),
                  pl.BlockSpec((t, LANES), lambda c: (c, COL_DT // LANES)),
                  full((SSD_CONV, SSD_WIDTH)), full((SSD_CONV, SSD_BC)),
                  full((1, SSD_WIDTH)), full((1, SSD_BC)),
                  full((1, LANES)), full((1, LANES)), full((1, SSD_WIDTH)), full((1, SSD_WIDTH)),
                  full((LANES, SSD_WIDTH))],
        out_specs=pl.BlockSpec((t, SSD_WIDTH), lambda c: (c, 0)),
        out_shape=jax.ShapeDtypeStruct((L, SSD_WIDTH), BF16),
        scratch_shapes=[pltpu.VMEM((SUBLANES, SSD_WIDTH), F32), pltpu.VMEM((SUBLANES, SSD_BC), F32),
                        pltpu.VMEM((SSD_GROUPS, SSD_STATE, SSD_WIDTH // SSD_GROUPS), F32),
                        pltpu.VMEM((t, SSD_WIDTH), F32)],
        compiler_params=_params("arbitrary"),
        name="ssd_mixer",
    )(proj, proj, proj, proj,
      conv_w[:, :SSD_WIDTH], conv_w[:, SSD_WIDTH:], conv_b[:SSD_WIDTH].reshape(1, -1),
      conv_b[SSD_WIDTH:].reshape(1, -1), pad_lanes(dt_bias), pad_lanes(a_log),
      jnp.repeat(d_skip, SSD_HEAD_DIM).reshape(1, -1), norm_g.reshape(1, -1), expand)


def _moba_prep_kernel(q_ref, k_ref, v_ref, qa_ref, ka_ref, va_ref, kmean_ref):
    own = pl.program_id(0)
    t = MOBA_BLOCK
    dh = MOBA_HEAD_DIM
    nbl = MOBA_MAX_BLOCKS

    @pl.when(own == 0)
    def _():
        kmean_ref[...] = jnp.zeros_like(kmean_ref)

    lane = lax.broadcasted_iota(jnp.int32, (t, LANES), 1)
    onehot = jnp.where(lane == dh + own, 1.0, 0.0)
    blk = lax.broadcasted_iota(jnp.int32, (nbl, t), 0)
    q_t = q_ref[...].T
    k = k_ref[...]
    k_mean = jnp.mean(k, axis=0, keepdims=True)
    va_ref[:, 0] = v_ref[...].T.reshape(MOBA_HEADS, dh, t).astype(BF16)
    for h in range(MOBA_HEADS):
        qh_t = q_t[h * dh:(h + 1) * dh]
        gate = jnp.dot(kmean_ref[h], qh_t, precision=HIGHEST, preferred_element_type=F32)
        gate = jnp.where(blk < own, gate, -jnp.inf)
        sel = blk >= own
        for _ in range(MOBA_TOPK):
            m = jnp.max(gate, axis=0, keepdims=True)
            idx = jnp.min(jnp.where(gate == m, blk, nbl), axis=0, keepdims=True)
            sel = sel | ((blk == idx) & (m > -jnp.inf))
            gate = jnp.where(blk == idx, -jnp.inf, gate)
        offs_t = jnp.where(sel, 0.0, NEG_BIG)
        qa_ref[h] = jnp.concatenate([qh_t * MOBA_Q_SCALE, offs_t, jnp.zeros((LANES - dh - nbl, t), F32)],
                                    axis=0).astype(BF16)
        pair = k[:, (h // 2) * LANES:(h // 2 + 1) * LANES]
        if h % 2:
            pair = pltpu.roll(pair, dh, axis=1)
        ka_ref[h] = jnp.where(lane < dh, pair, onehot).astype(BF16)
        kmean_ref[h, pl.ds(own, 1), :] = k_mean[:, h * dh:(h + 1) * dh]


def moba_prep(proj):
    L = proj.shape[0]
    t = MOBA_BLOCK
    assert L % t == 0 and L // t <= MOBA_MAX_BLOCKS
    return pl.pallas_call(
        _moba_prep_kernel,
        grid=(L // t,),
        in_specs=[pl.BlockSpec((t, MOBA_WIDTH), lambda i: (i, COL_MQ // MOBA_WIDTH)),
                  pl.BlockSpec((t, MOBA_WIDTH), lambda i: (i, COL_MK // MOBA_WIDTH)),
                  pl.BlockSpec((t, MOBA_WIDTH), lambda i: (i, COL_MV // MOBA_WIDTH))],
        out_specs=[pl.BlockSpec((MOBA_HEADS, LANES, t), lambda i: (0, 0, i)),
                   pl.BlockSpec((MOBA_HEADS, t, LANES), lambda i: (0, i, 0)),
                   pl.BlockSpec((MOBA_HEADS, 1, MOBA_HEAD_DIM, t), lambda i: (0, i, 0, 0))],
        out_shape=[jax.ShapeDtypeStruct((MOBA_HEADS, LANES, L), BF16),
                   jax.ShapeDtypeStruct((MOBA_HEADS, L, LANES), BF16),
                   jax.ShapeDtypeStruct((MOBA_HEADS, L // t, MOBA_HEAD_DIM, t), BF16)],
        scratch_shapes=[pltpu.VMEM((MOBA_HEADS, MOBA_MAX_BLOCKS, MOBA_HEAD_DIM), F32)],
        compiler_params=_params("arbitrary"),
        name="moba_prep",
    )(proj, proj, proj)


def _col_reduce(x, op):
    while x.shape[0] > SUBLANES:
        half = x.shape[0] // 2
        x = op(x[:half], x[half:])
    return jnp.max(x, axis=0, keepdims=True) if op is jnp.maximum else jnp.sum(x, axis=0, keepdims=True)


def _attention_kernel(near_ref, q_ref, k_ref, v_ref, posr_ref, posc_ref, tbl_ref, g_ref, lam_ref, o_ref,
                      m_ref, l_ref, al_ref, acc_ref, sa_ref, sb_ref, pa_ref, pb_ref, *, moba, lambda_init):
    hp = pl.program_id(0)
    qi = pl.program_id(1)
    nq = pl.num_programs(1)
    t = ATT_T
    m_ref[...] = jnp.full_like(m_ref, -jnp.inf)
    l_ref[...] = jnp.zeros_like(l_ref)
    acc_ref[...] = jnp.zeros_like(acc_ref)
    if moba:
        q_t = [q_ref[0], q_ref[1]]
        heads = [2 * hp, 2 * hp + 1]
    else:
        q = (q_ref[...].astype(F32) * DIFF_Q_SCALE).astype(BF16)
        half = lax.broadcasted_iota(jnp.int32, q.shape, 0) < DIFF_QK_DIM
        q_t = [jnp.where(half, q, jnp.zeros_like(q)), jnp.where(half, jnp.zeros_like(q), q)]
        heads = [MOBA_HEADS + hp]
    trow = [tbl_ref[pl.ds(hd, 1), :] for hd in heads]
    posq = posr_ref[pl.ds(qi, 1), :]

    def scores(ki, dst_ref):
        rows = pl.ds(pl.multiple_of(ki * t, t), t)
        for s in range(2):
            k = k_ref[s, rows, :] if moba else k_ref[rows, :]
            dst_ref[s] = jnp.dot(k, q_t[s], preferred_element_type=F32)

    def accumulate(ki, p_ref):
        for s in range(2):
            pv = jnp.dot(v_ref[s if moba else 0, ki], p_ref[s], preferred_element_type=F32)
            acc_ref[s] = al_ref[s] * acc_ref[s] + pv

    scores(0, sa_ref)
    pb_ref[...] = jnp.zeros_like(pb_ref)
    al_ref[...] = jnp.ones_like(al_ref)

    def tile(ki, general, cur_ref, nxt_ref, p_ref, p_prev_ref):
        scores(jnp.minimum(ki + 1, qi), nxt_ref)
        accumulate(jnp.maximum(ki - 1, 0), p_prev_ref)
        ch = ATT_CHUNK
        mx = [None, None]
        for r in range(0, t, ch):
            if general:
                dist = jnp.clip(posq - posc_ref[ki, r:r + ch, :], 0, LANES - 1)
                bias = [jnp.concatenate(
                    [jnp.take_along_axis(jnp.broadcast_to(tr, (ch, LANES)), dist[:, j * LANES:(j + 1) * LANES],
                                         axis=1) for j in range(t // LANES)], axis=1) for tr in trow]
                key = lax.broadcasted_iota(jnp.int32, (ch, t), 0) + r
                qry = lax.broadcasted_iota(jnp.int32, (ch, t), 1)
                causal = key + ki * t <= qry + qi * t
            for s in range(2):
                blk = cur_ref[s, r:r + ch, :]
                if general:
                    blk = jnp.where(causal, blk + bias[s if moba else 0], -jnp.inf)
                    cur_ref[s, r:r + ch, :] = blk
                mx[s] = blk if mx[s] is None else jnp.maximum(mx[s], blk)
        for s in range(2):
            m_old = m_ref[s]
            m_tile = _col_reduce(mx[s], jnp.maximum)
            if general:
                m_new = jnp.maximum(m_old, m_tile)
                shift = m_new
            else:
                c = trow[s if moba else 0][:, LANES - 1:LANES]
                m_new = jnp.maximum(m_old, m_tile + c)
                shift = m_new - c
            alpha = jnp.exp2(m_old - m_new)
            sm = None
            for r in range(0, t, ch):
                p = jnp.exp2(cur_ref[s, r:r + ch, :] - shift)
                p_ref[s, r:r + ch, :] = p.astype(BF16)
                sm = p if sm is None else sm + p
            l_ref[s] = alpha * l_ref[s] + _col_reduce(sm, jnp.add)
            al_ref[s] = alpha
            m_ref[s] = m_new

    def step(ki, *bufs):
        flag = near_ref[qi * nq + ki]

        @pl.when(flag == 0)
        def _():
            tile(ki, False, *bufs)

        @pl.when(flag != 0)
        def _():
            tile(ki, True, *bufs)

    def pair(j, carry):
        step(2 * j, sa_ref, sb_ref, pa_ref, pb_ref)
        step(2 * j + 1, sb_ref, sa_ref, pb_ref, pa_ref)
        return carry

    lax.fori_loop(0, (qi + 1) // 2, pair, 0)

    @pl.when(qi % 2 == 0)
    def _():
        step(qi, sa_ref, sb_ref, pa_ref, pb_ref)
        accumulate(qi, pa_ref)

    @pl.when(qi % 2 == 1)
    def _():
        accumulate(qi, pb_ref)

    if moba:
        outs = []
        for s in range(2):
            o = acc_ref[s] / l_ref[s]
            ms = jnp.sum(o * o, axis=0, keepdims=True) * (1.0 / MOBA_HEAD_DIM)
            outs.append(o * lax.rsqrt(ms + EPS) * g_ref[s])
        y_t = jnp.concatenate(outs, axis=0)
    else:
        lp = lam_ref[...]
        lam = (jnp.exp(jnp.sum(lp[0:1] * lp[1:2], axis=1, keepdims=True))
               - jnp.exp(jnp.sum(lp[2:3] * lp[3:4], axis=1, keepdims=True)) + lambda_init)
        o = acc_ref[0] / l_ref[0] - lam * (acc_ref[1] / l_ref[1])
        ms = jnp.sum(o * o, axis=0, keepdims=True) * (1.0 / DIFF_V_DIM)
        y_t = (o * lax.rsqrt(ms + EPS) * g_ref[0]) * (1.0 - lambda_init)
    o_ref[...] = y_t.T.astype(o_ref.dtype)


def _attention_call(kernel, steps, dv, L, near, in_specs, args, name):
    t = ATT_T
    full = lambda a: pl.BlockSpec(a.shape, lambda h, i, nr: (0,) * a.ndim)
    grid_spec = pltpu.PrefetchScalarGridSpec(
        num_scalar_prefetch=1,
        grid=(steps, L // t),
        in_specs=in_specs + [full(a) for a in args[len(in_specs):]],
        out_specs=pl.BlockSpec((t, LANES), lambda h, i, nr: (i, h)),
        scratch_shapes=[pltpu.VMEM((2, 1, t), F32), pltpu.VMEM((2, 1, t), F32), pltpu.VMEM((2, 1, t), F32),
                        pltpu.VMEM((2, dv, t), F32), pltpu.VMEM((2, t, t), F32), pltpu.VMEM((2, t, t), F32),
                        pltpu.VMEM((2, t, t), BF16), pltpu.VMEM((2, t, t), BF16)],
    )
    return pl.pallas_call(kernel, grid_spec=grid_spec, out_shape=jax.ShapeDtypeStruct((L, steps * LANES), BF16),
                          compiler_params=_params("parallel", "arbitrary"), name=name)(near, *args)


def moba_attention(qa_t, ka, va_t, near, pos_rows, pos_cols, table, norm_g):
    heads, L, _ = ka.shape
    t = ATT_T
    kernel = functools.partial(_attention_kernel, moba=True, lambda_init=None)
    g = norm_g.reshape(heads, MOBA_HEAD_DIM, 1)
    return _attention_call(
        kernel, heads // 2, MOBA_HEAD_DIM, L, near,
        [pl.BlockSpec((2, LANES, t), lambda h, i, nr: (h, 0, i)),
         pl.BlockSpec((2, L, LANES), lambda h, i, nr: (h, 0, 0)),
         pl.BlockSpec((2, L // t, MOBA_HEAD_DIM, t), lambda h, i, nr: (h, 0, 0, 0)),
         pl.BlockSpec(pos_rows.shape, lambda h, i, nr: (0, 0)),
         pl.BlockSpec(pos_cols.shape, lambda h, i, nr: (0, 0, 0)),
         pl.BlockSpec(table.shape, lambda h, i, nr: (0, 0)),
         pl.BlockSpec((2, MOBA_HEAD_DIM, 1), lambda h, i, nr: (h, 0, 0))],
        [qa_t, ka, va_t, pos_rows, pos_cols, table, g, jnp.zeros((4, DIFF_QK_DIM), F32)], "moba_attention")


def diff_attention(proj16, near, pos_rows, pos_cols, table, lam_params, subln_g, lambda_init):
    L = proj16.shape[0]
    t = ATT_T
    q_t = proj16[:, COL_DQ:COL_DQ + DIFF_WIDTH].T
    v_t = jnp.transpose(proj16[:, COL_DV:COL_DV + DIFF_WIDTH].reshape(L // t, t, DIFF_HEADS, DIFF_V_DIM),
                        (2, 0, 3, 1))
    kernel = functools.partial(_attention_kernel, moba=False, lambda_init=lambda_init)
    return _attention_call(
        kernel, DIFF_HEADS, DIFF_V_DIM, L, near,
        [pl.BlockSpec((LANES, t), lambda h, i, nr: (h, i)),
         pl.BlockSpec((L, LANES), lambda h, i, nr: (0, COL_DK // LANES + h)),
         pl.BlockSpec((1, L // t, DIFF_V_DIM, t), lambda h, i, nr: (h, 0, 0, 0))],
        [q_t, proj16, v_t, pos_rows, pos_cols, table, subln_g.reshape(1, DIFF_V_DIM, 1), lam_params],
        "diff_attention")


def _rel_bucket(dist):
    n = jnp.maximum(dist, 0)
    max_exact = REL_BUCKETS // 2
    nf = jnp.maximum(n, 1).astype(F32)
    large = max_exact + (jnp.log(nf / max_exact) / math.log(REL_MAX_DIST / max_exact)
                         * (REL_BUCKETS - max_exact)).astype(jnp.int32)
    return jnp.where(n < max_exact, n, jnp.minimum(large, REL_BUCKETS - 1))


def attention_tables(positions, rel_bias):
    L = positions.shape[0]
    t = ATT_T
    buckets = _rel_bucket(jnp.arange(LANES, dtype=jnp.int32))
    table = rel_bias[buckets].T * LOG2E
    pos_rows = positions.reshape(L // t, t)
    lo, hi = jnp.min(pos_rows, axis=1), jnp.max(pos_rows, axis=1)
    near = (lo[:, None] - hi[None, :] < LANES) | jnp.eye(L // t, dtype=bool)
    return table, pos_rows, positions.reshape(L // t, t, 1), near.astype(jnp.int32).reshape(-1)


def _router_kernel(x_ref, g_ref, sc_ref, sh_ref, rw_ref, h_ref, comb_ref, rank_ref, cum_ref, total_ref, cnt_ref):
    t = TOK_T

    @pl.when(pl.program_id(0) == 0)
    def _():
        cnt_ref[...] = jnp.zeros_like(cnt_ref)

    h = _norm_mod(x_ref[...], g_ref[...], sc_ref[...], sh_ref[...])
    h_ref[...] = h.astype(BF16)
    lane = lax.broadcasted_iota(jnp.int32, (t, LANES), 1)
    logits = jnp.dot(h, rw_ref[...], precision=HIGHEST, preferred_element_type=F32)
    logits = jnp.where(lane < N_EXPERTS, logits, -jnp.inf)
    m1 = jnp.max(logits, axis=1, keepdims=True)
    i1 = jnp.min(jnp.where(logits == m1, lane, LANES), axis=1, keepdims=True)
    rest = jnp.where(lane == i1, -jnp.inf, logits)
    m2 = jnp.max(rest, axis=1, keepdims=True)
    i2 = jnp.min(jnp.where(rest == m2, lane, LANES), axis=1, keepdims=True)
    e2 = jnp.exp(m2 - m1)
    denom = 1.0 + e2
    comb_ref[...] = jnp.where(lane == i1, 1.0 / denom, 0.0) + jnp.where(lane == i2, e2 / denom, 0.0)
    sel = jnp.where((lane == i1) | (lane == i2), 1.0, 0.0)
    row = lax.broadcasted_iota(jnp.int32, (t, t), 0)
    col = lax.broadcasted_iota(jnp.int32, (t, t), 1)
    before = jnp.dot(jnp.where(row > col, 1.0, 0.0).astype(BF16), sel.astype(BF16), preferred_element_type=F32)
    carry = cnt_ref[...]
    cum_ref[0] = carry
    rank_ref[...] = jnp.where(sel > 0.0, before + carry, -1.0)
    carry = carry + jnp.sum(sel, axis=0, keepdims=True)
    cnt_ref[...] = carry
    total_ref[...] = carry


def moe_router(x, g, sc, sh, router_w):
    L, d = x.shape
    t = TOK_T
    rw = jnp.pad(router_w, ((0, 0), (0, LANES - N_EXPERTS)))
    vec = pl.BlockSpec((1, d), lambda i: (0, 0))
    tok = pl.BlockSpec((t, LANES), lambda i: (i, 0))
    return pl.pallas_call(
        _router_kernel,
        grid=(L // t,),
        in_specs=[pl.BlockSpec((t, d), lambda i: (i, 0)), vec, vec, vec,
                  pl.BlockSpec((d, LANES), lambda i: (0, 0))],
        out_specs=[pl.BlockSpec((t, d), lambda i: (i, 0)), tok, tok,
                   pl.BlockSpec((1, 1, LANES), lambda i: (i, 0, 0)),
                   pl.BlockSpec((1, LANES), lambda i: (0, 0))],
        out_shape=[jax.ShapeDtypeStruct((L, d), BF16), jax.ShapeDtypeStruct((L, LANES), F32),
                   jax.ShapeDtypeStruct((L, LANES), F32), jax.ShapeDtypeStruct((L // t, 1, LANES), F32),
                   jax.ShapeDtypeStruct((1, LANES), F32)],
        scratch_shapes=[pltpu.VMEM((1, LANES), F32)],
        compiler_params=_params("arbitrary"),
        name="moe_router",
    )(x, g, sc, sh, rw)


def _item_lists(hit, n_items):
    rows, cols = hit.shape
    n_real = jnp.sum(hit.astype(jnp.int32))
    idx = jnp.nonzero(hit.reshape(-1), size=n_items, fill_value=0)[0].astype(jnp.int32)
    k = jnp.arange(n_items, dtype=jnp.int32)
    real = k < n_real
    idx = jnp.where(real, idx, idx[jnp.maximum(n_real - 1, 0)])
    r, c = idx // cols, idx % cols
    prev_r = jnp.concatenate([jnp.full((1,), -1, jnp.int32), r[:-1]])
    next_r = jnp.concatenate([r[1:], jnp.full((1,), -1, jnp.int32)])
    first = real & (r != prev_r)
    last = real & ((r != next_r) | (k == n_real - 1))
    flags = real.astype(jnp.int32) + 2 * first.astype(jnp.int32) + 4 * last.astype(jnp.int32)
    return r, c, flags


def moe_plan(rank, cum, total, L):
    tm, tb = MOE_TM, TOK_T
    nb = L // tb
    n_tiles = 2 * L // tm + N_EXPERTS
    counts = total[0, :N_EXPERTS].astype(jnp.int32)
    tiles_e = (counts + tm - 1) // tm
    tile_end = jnp.cumsum(tiles_e)
    tile_start = tile_end - tiles_e
    tid = jnp.arange(n_tiles, dtype=jnp.int32)
    tile_valid = tid < tile_end[-1]
    tile_expert = jnp.minimum(jnp.searchsorted(tile_end, tid, side="right"), N_EXPERTS - 1).astype(jnp.int32)
    local_row = (tid - tile_start[tile_expert]) * tm
    r = rank[:, :N_EXPERTS].astype(jnp.int32)
    dest = jnp.where(r >= 0, r + (tile_start * tm)[None, :], -1)
    cum_i = cum[:, 0, :N_EXPERTS].astype(jnp.int32)
    cum_next = jnp.concatenate([cum_i[1:], counts[None, :]], axis=0)
    lo = cum_i[:, tile_expert].T
    hi = cum_next[:, tile_expert].T
    hit = tile_valid[:, None] & (lo < (local_row + tm)[:, None]) & (hi > local_row[:, None])
    n_items = n_tiles + N_EXPERTS * nb
    pad_hit = hit | ((~tile_valid)[:, None] & (jnp.arange(nb) == 0)[None, :])
    gather_items = _item_lists(pad_hit, n_items)
    kb, tt, fl = _item_lists(hit.T, n_items)
    return dict(tile_expert=tile_expert, tile_valid=tile_valid.astype(jnp.int32),
                dest_rows=dest.T, dest_cols=dest.T.reshape(N_EXPERTS, L, 1),
                gather_items=gather_items, combine_items=(tt, kb, fl), n_tiles=n_tiles, n_items=n_items)


def _moe_gather_kernel(it_ref, ib_ref, if_ref, te_ref, dest_ref, h_ref, o_ref, acc_ref):
    i = pl.program_id(0)
    flag = if_ref[i]
    tile = it_ref[i]
    tm = MOE_TM

    @pl.when((flag & 2) != 0)
    def _():
        acc_ref[...] = jnp.zeros_like(acc_ref)

    @pl.when((flag & 1) != 0)
    def _():
        d = dest_ref[pl.ds(te_ref[tile], 1), :]
        rows = tile * tm + lax.broadcasted_iota(jnp.int32, (tm, 1), 0)
        onehot = jnp.where(d == rows, 1.0, 0.0).astype(BF16)
        acc_ref[...] += jnp.dot(onehot, h_ref[...], preferred_element_type=F32)

    @pl.when((flag & 4) != 0)
    def _():
        o_ref[...] = acc_ref[...].astype(o_ref.dtype)


def moe_gather(h, plan):
    L, d = h.shape
    tm, tb = MOE_TM, TOK_T
    it, ib, fl = plan["gather_items"]
    grid_spec = pltpu.PrefetchScalarGridSpec(
        num_scalar_prefetch=4,
        grid=(plan["n_items"],),
        in_specs=[pl.BlockSpec((N_EXPERTS, tb), lambda i, it, ib, fl, te: (0, ib[i])),
                  pl.BlockSpec((tb, d), lambda i, it, ib, fl, te: (ib[i], 0))],
        out_specs=pl.BlockSpec((tm, d), lambda i, it, ib, fl, te: (it[i], 0)),
        scratch_shapes=[pltpu.VMEM((tm, d), F32)],
    )
    return pl.pallas_call(
        _moe_gather_kernel,
        grid_spec=grid_spec,
        out_shape=jax.ShapeDtypeStruct((plan["n_tiles"] * tm, d), BF16),
        compiler_params=_params("arbitrary"),
        name="moe_gather",
    )(it, ib, fl, plan["tile_expert"], plan["dest_rows"], h)


def _moe_combine_kernel(it_ref, ib_ref, if_ref, te_ref, dest_ref, w_ref, ys_ref, x_ref, gate_ref, o_ref, acc_ref):
    i = pl.program_id(0)
    flag = if_ref[i]
    tile = it_ref[i]
    tm = MOE_TM

    @pl.when((flag & 2) != 0)
    def _():
        acc_ref[...] = jnp.zeros_like(acc_ref)

    @pl.when((flag & 1) != 0)
    def _():
        cols = tile * tm + lax.broadcasted_iota(jnp.int32, (1, tm), 1)
        onehot = jnp.where(dest_ref[0] == cols, 1.0, 0.0).astype(BF16)
        acc_ref[...] += w_ref[0] * jnp.dot(onehot, ys_ref[...], preferred_element_type=F32)

    @pl.when((flag & 4) != 0)
    def _():
        o_ref[...] = x_ref[...] + gate_ref[...] * acc_ref[...]


def moe_combine(ys, comb, x, gate, plan):
    L, d = x.shape
    tm, tb = MOE_TM, TOK_T
    it, ib, fl = plan["combine_items"]
    w_cols = comb[:, :N_EXPERTS].T.reshape(N_EXPERTS, L, 1)
    col = lambda i, it, ib, fl, te: (te[it[i]], ib[i], 0)
    grid_spec = pltpu.PrefetchScalarGridSpec(
        num_scalar_prefetch=4,
        grid=(plan["n_items"],),
        in_specs=[pl.BlockSpec((1, tb, 1), col), pl.BlockSpec((1, tb, 1), col),
                  pl.BlockSpec((tm, d), lambda i, it, ib, fl, te: (it[i], 0)),
                  pl.BlockSpec((tb, d), lambda i, it, ib, fl, te: (ib[i], 0)),
                  pl.BlockSpec((1, d), lambda i, it, ib, fl, te: (0, 0))],
        out_specs=pl.BlockSpec((tb, d), lambda i, it, ib, fl, te: (ib[i], 0)),
        scratch_shapes=[pltpu.VMEM((tb, d), F32)],
    )
    return pl.pallas_call(
        _moe_combine_kernel,
        grid_spec=grid_spec,
        out_shape=jax.ShapeDtypeStruct((L, d), F32),
        compiler_params=_params("arbitrary"),
        name="moe_combine",
    )(it, ib, fl, plan["tile_expert"], plan["dest_cols"], w_cols, ys, x, gate)


def moe_ffn(x, g, sc, sh, gate, router_w, w1, w3, w2):
    L = x.shape[0]
    h, comb, rank, cum, total = moe_router(x, g, sc, sh, router_w)
    plan = moe_plan(rank, cum, total, L)
    xs = moe_gather(h, plan)
    te = plan["tile_expert"]
    flags = tile_flags(te, plan["tile_valid"])
    act = swiglu_up(xs, w1, w3, te, flags, MOE_TM, 1024)
    ys = swiglu_down(act, w2, te, flags, MOE_TM, 512)
    return moe_combine(ys, comb, x, gate, plan)


def dense_ffn(x, g, sc, sh, gate, w1, w3, w2):
    L = x.shape[0]
    tm = 512
    h = norm_modulate(x, g, sc, sh, BF16)
    te = jnp.zeros((L // tm,), jnp.int32)
    flags = tile_flags(te, jnp.ones((L // tm,), jnp.int32))
    act = swiglu_up(h, w1[None], w3[None], te, flags, tm, 512)
    return swiglu_down(act, w2[None], te, flags, tm, 512, x=x, gate=gate)


def _permute_in_proj(w):
    d = w.shape[0]
    return jnp.concatenate([w[:, :ORIG_DT], w[:, ORIG_DT + SSD_HEADS:], w[:, ORIG_DT:ORIG_DT + SSD_HEADS],
                            jnp.zeros((d, LANES - SSD_HEADS), w.dtype)], axis=1).astype(BF16)


def kernel(x, c, positions, rel_bias, w_ada, b_ada, norm_mix_g, w_in, conv_w, conv_b, dt_bias, a_log, d_skip, ssd_norm_g, moba_norm_g, diff_lambda, diff_subln_g, w_out, norm_ffn_g, dense_w1, dense_w3, dense_w2, router_w, expert_w1, expert_w3, expert_w2, final_g):
    batch, L, d = x.shape
    assert batch == 1 and d == D_MODEL
    x = x[0]
    mod = ada_modulation(c, w_ada, b_ada)
    table, pos_rows, pos_cols, near = attention_tables(positions[0], rel_bias)
    row = lambda v: v.reshape(1, -1)
    for layer in range(DEPTH):
        lambda_init = 0.8 - 0.6 * math.exp(-0.3 * layer)
        shift1, scale1, gate1, shift2, scale2, gate2 = (mod[layer, :, j * d:(j + 1) * d] for j in range(6))
        h = norm_modulate(x, row(norm_mix_g[layer]), scale1, shift1, BF16)
        proj, proj16 = in_projection(h, _permute_in_proj(w_in[layer]))
        y_ssd = ssd_mixer(proj, conv_w[layer], conv_b[layer], dt_bias[layer], a_log[layer], d_skip[layer],
                          ssd_norm_g[layer])
        qa, ka, va = moba_prep(proj)
        y_moba = moba_attention(qa, ka, va, near, pos_rows, pos_cols, table, moba_norm_g[layer])
        y_diff = diff_attention(proj16, near, pos_rows, pos_cols, table, diff_lambda[layer],
                                diff_subln_g[layer], lambda_init)
        x = out_projection(y_ssd, y_moba, y_diff, w_out[layer], x, gate1)
        i = layer // 2
        g2 = row(norm_ffn_g[layer])
        if layer % 2 == 0:
            x = dense_ffn(x, g2, scale2, shift2, gate2, dense_w1[i], dense_w3[i], dense_w2[i])
        else:
            x = moe_ffn(x, g2, scale2, shift2, gate2, router_w[i], expert_w1[i], expert_w3[i], expert_w2[i])
    zero = jnp.zeros((1, d), F32)
    return norm_modulate(x, row(final_g), zero, zero, F32)[None]
```

```python
import functools
import math

import jax
import jax.numpy as jnp
from jax import lax
from jax.experimental import pallas as pl
from jax.experimental.pallas import tpu as pltpu

F32 = jnp.float32
BF16 = jnp.bfloat16
HIGHEST = lax.Precision.HIGHEST

D_MODEL = 2048
DEPTH = 2
SSD_HEADS = 16
SSD_HEAD_DIM = 64
SSD_WIDTH = SSD_HEADS * SSD_HEAD_DIM
SSD_GROUPS = 2
SSD_STATE = 128
SSD_CONV = 4
SSD_CHUNK = 256
SSD_BC = 2 * SSD_GROUPS * SSD_STATE
MOBA_HEADS = 8
MOBA_HEAD_DIM = 64
MOBA_WIDTH = MOBA_HEADS * MOBA_HEAD_DIM
MOBA_BLOCK = 256
MOBA_TOPK = 3
DIFF_HEADS = 4
DIFF_QK_DIM = 64
DIFF_V_DIM = 128
DIFF_WIDTH = DIFF_HEADS * DIFF_V_DIM
REL_BUCKETS = 32
REL_MAX_DIST = 128
D_FF_DENSE = 5632
N_EXPERTS = 8
D_FF_EXPERT = 7168
EPS = 1e-6

LANES = 128
SUBLANES = 8
VMEM_LIMIT = 56 * 1024 * 1024

COL_Z = 0
COL_X = SSD_WIDTH
COL_BC = COL_X + SSD_WIDTH
COL_MQ = COL_BC + SSD_BC
COL_MK = COL_MQ + MOBA_WIDTH
COL_MV = COL_MK + MOBA_WIDTH
COL_DQ = COL_MV + MOBA_WIDTH
COL_DK = COL_DQ + DIFF_WIDTH
COL_DV = COL_DK + DIFF_WIDTH
COL_DT = COL_DV + DIFF_WIDTH
PROJ_W = COL_DT + LANES
ORIG_DT = SSD_WIDTH + SSD_WIDTH + SSD_BC

ATT_T = 256
ATT_CHUNK = 32
TOK_T = 512
MOE_TM = 256
NEG_BIG = -1e9
MOBA_MAX_BLOCKS = 32
LOG2E = math.log2(math.e)
MOBA_Q_SCALE = MOBA_HEAD_DIM ** -0.5 * LOG2E
DIFF_Q_SCALE = DIFF_QK_DIM ** -0.5 * LOG2E


def _silu(x):
    return x * (1.0 / (1.0 + jnp.exp(-x)))


def _softplus(x):
    return jnp.maximum(x, 0.0) + jnp.log1p(jnp.exp(-jnp.abs(x)))


def _params(*sem):
    return pltpu.CompilerParams(dimension_semantics=sem, vmem_limit_bytes=VMEM_LIMIT)


def _ada_kernel(c_ref, w_ref, b_ref, o_ref):
    ca = _silu(c_ref[...])
    o_ref[0] = jnp.sum(ca * w_ref[0], axis=0, keepdims=True) + b_ref[0]


def ada_modulation(c, w_ada, b_ada):
    depth, d, n = w_ada.shape
    tn = 1024
    return pl.pallas_call(
        _ada_kernel,
        grid=(depth, n // tn),
        in_specs=[pl.BlockSpec((d, 1), lambda l, j: (0, 0)),
                  pl.BlockSpec((1, d, tn), lambda l, j: (l, 0, j)),
                  pl.BlockSpec((1, 1, tn), lambda l, j: (l, 0, j))],
        out_specs=pl.BlockSpec((1, 1, tn), lambda l, j: (l, 0, j)),
        out_shape=jax.ShapeDtypeStruct((depth, 1, n), F32),
        compiler_params=_params("parallel", "parallel"),
        name="ada_modulation",
    )(c.reshape(d, 1), w_ada, b_ada.reshape(depth, 1, n))


def _norm_mod(x, g, sc, sh):
    ms = jnp.mean(x * x, axis=-1, keepdims=True)
    return (x * lax.rsqrt(ms + EPS) * g) * (1.0 + sc) + sh


def _norm_kernel(x_ref, g_ref, sc_ref, sh_ref, o_ref):
    o_ref[...] = _norm_mod(x_ref[...], g_ref[...], sc_ref[...], sh_ref[...]).astype(o_ref.dtype)


def norm_modulate(x, g, sc, sh, out_dtype):
    L, d = x.shape
    tm = 512
    vec = pl.BlockSpec((1, d), lambda i: (0, 0))
    return pl.pallas_call(
        _norm_kernel,
        grid=(L // tm,),
        in_specs=[pl.BlockSpec((tm, d), lambda i: (i, 0)), vec, vec, vec],
        out_specs=pl.BlockSpec((tm, d), lambda i: (i, 0)),
        out_shape=jax.ShapeDtypeStruct((L, d), out_dtype),
        compiler_params=_params("parallel"),
        name="norm_modulate",
    )(x, g, sc, sh)


def _inproj_kernel(a_ref, w_ref, o32_ref, o16_ref):
    acc = jnp.dot(a_ref[...], w_ref[...], preferred_element_type=F32)
    o32_ref[...] = acc
    o16_ref[...] = acc.astype(BF16)


def in_projection(h, w):
    L, k = h.shape
    n = w.shape[1]
    tm, tn = 512, 1920
    return pl.pallas_call(
        _inproj_kernel,
        grid=(n // tn, L // tm),
        in_specs=[pl.BlockSpec((tm, k), lambda j, i: (i, 0)),
                  pl.BlockSpec((k, tn), lambda j, i: (0, j))],
        out_specs=[pl.BlockSpec((tm, tn), lambda j, i: (i, j)),
                   pl.BlockSpec((tm, tn), lambda j, i: (i, j))],
        out_shape=[jax.ShapeDtypeStruct((L, n), F32), jax.ShapeDtypeStruct((L, n), BF16)],
        compiler_params=_params("parallel", "parallel"),
        name="in_projection",
    )(h, w)


def _outproj_kernel(ys_ref, ym_ref, yd_ref, ws_ref, wm_ref, wd_ref, x_ref, gate_ref, o_ref, wb_ref):
    @pl.when(pl.program_id(1) == 0)
    def _():
        wb_ref[0:SSD_WIDTH] = ws_ref[...].astype(BF16)
        wb_ref[SSD_WIDTH:SSD_WIDTH + MOBA_WIDTH] = wm_ref[...].astype(BF16)
        wb_ref[SSD_WIDTH + MOBA_WIDTH:] = wd_ref[...].astype(BF16)

    acc = jnp.dot(ys_ref[...], wb_ref[0:SSD_WIDTH], preferred_element_type=F32)
    acc += jnp.dot(ym_ref[...], wb_ref[SSD_WIDTH:SSD_WIDTH + MOBA_WIDTH], preferred_element_type=F32)
    acc += jnp.dot(yd_ref[...], wb_ref[SSD_WIDTH + MOBA_WIDTH:], preferred_element_type=F32)
    o_ref[...] = x_ref[...] + gate_ref[...] * acc


def out_projection(y_ssd, y_moba, y_diff, w_out, x, gate):
    L, d = x.shape
    tm, tn = 512, 1024
    return pl.pallas_call(
        _outproj_kernel,
        grid=(d // tn, L // tm),
        in_specs=[pl.BlockSpec((tm, SSD_WIDTH), lambda j, i: (i, 0)),
                  pl.BlockSpec((tm, MOBA_WIDTH), lambda j, i: (i, 0)),
                  pl.BlockSpec((tm, DIFF_WIDTH), lambda j, i: (i, 0)),
                  pl.BlockSpec((SSD_WIDTH, tn), lambda j, i: (0, j)),
                  pl.BlockSpec((MOBA_WIDTH, tn), lambda j, i: (SSD_WIDTH // MOBA_WIDTH, j)),
                  pl.BlockSpec((DIFF_WIDTH, tn), lambda j, i: ((SSD_WIDTH + MOBA_WIDTH) // DIFF_WIDTH, j)),
                  pl.BlockSpec((tm, tn), lambda j, i: (i, j)),
                  pl.BlockSpec((1, tn), lambda j, i: (0, j))],
        out_specs=pl.BlockSpec((tm, tn), lambda j, i: (i, j)),
        out_shape=jax.ShapeDtypeStruct((L, d), F32),
        scratch_shapes=[pltpu.VMEM((w_out.shape[0], tn), BF16)],
        compiler_params=_params("arbitrary", "arbitrary"),
        name="out_projection",
    )(y_ssd, y_moba, y_diff, w_out, w_out, w_out, x, gate)


CAST_COLS = 256
TILE_VALID = 1
TILE_NEW_WEIGHTS = 2


def tile_flags(tile_expert, tile_valid):
    prev = jnp.concatenate([jnp.full((1,), -1, jnp.int32), tile_expert[:-1]])
    return tile_valid * TILE_VALID + (tile_expert != prev).astype(jnp.int32) * TILE_NEW_WEIGHTS


def _swiglu_up_kernel(te_ref, tf_ref, a_ref, w1_ref, w3_ref, o_ref, w1b_ref, w3b_ref):
    flags = tf_ref[pl.program_id(1)]
    tn = o_ref.shape[1]
    cw = CAST_COLS

    @pl.when(flags == TILE_VALID + TILE_NEW_WEIGHTS)
    def _():
        a = a_ref[...]
        for c in range(0, tn, cw):
            w1b = w1_ref[0, :, c:c + cw].astype(BF16)
            w3b = w3_ref[0, :, c:c + cw].astype(BF16)
            w1b_ref[:, c:c + cw] = w1b
            w3b_ref[:, c:c + cw] = w3b
            u = jnp.dot(a, w1b, preferred_element_type=F32)
            v = jnp.dot(a, w3b, preferred_element_type=F32)
            o_ref[:, c:c + cw] = (_silu(u) * v).astype(o_ref.dtype)

    @pl.when(flags == TILE_VALID)
    def _():
        a = a_ref[...]
        u = jnp.dot(a, w1b_ref[...], preferred_element_type=F32)
        v = jnp.dot(a, w3b_ref[...], preferred_element_type=F32)
        o_ref[...] = (_silu(u) * v).astype(o_ref.dtype)

    @pl.when((flags & TILE_VALID) == 0)
    def _():
        o_ref[...] = jnp.zeros_like(o_ref)


def swiglu_up(a, w1, w3, tile_expert, flags, tm, tn):
    rows, k = a.shape
    f = w1.shape[2]
    grid_spec = pltpu.PrefetchScalarGridSpec(
        num_scalar_prefetch=2,
        grid=(f // tn, rows // tm),
        in_specs=[pl.BlockSpec((tm, k), lambda j, t, te, tf: (t, 0)),
                  pl.BlockSpec((1, k, tn), lambda j, t, te, tf: (te[t], 0, j)),
                  pl.BlockSpec((1, k, tn), lambda j, t, te, tf: (te[t], 0, j))],
        out_specs=pl.BlockSpec((tm, tn), lambda j, t, te, tf: (t, j)),
        scratch_shapes=[pltpu.VMEM((k, tn), BF16), pltpu.VMEM((k, tn), BF16)],
    )
    return pl.pallas_call(
        _swiglu_up_kernel,
        grid_spec=grid_spec,
        out_shape=jax.ShapeDtypeStruct((rows, f), BF16),
        compiler_params=_params("arbitrary", "arbitrary"),
        name="swiglu_up",
    )(tile_expert, flags, a, w1, w3)


def _swiglu_down_kernel(te_ref, tf_ref, g_ref, w2_ref, *rest, residual):
    flags = tf_ref[pl.program_id(1)]
    if residual:
        x_ref, gate_ref, o_ref, wb_ref = rest
    else:
        o_ref, wb_ref = rest

    tn = o_ref.shape[1]
    cw = CAST_COLS

    def finish(acc, cols):
        if residual:
            acc = x_ref[:, cols] + gate_ref[:, cols] * acc
        o_ref[:, cols] = acc.astype(o_ref.dtype)

    @pl.when(flags == TILE_VALID + TILE_NEW_WEIGHTS)
    def _():
        g = g_ref[...]
        for c in range(0, tn, cw):
            wb = w2_ref[0, :, c:c + cw].astype(BF16)
            wb_ref[:, c:c + cw] = wb
            finish(jnp.dot(g, wb, preferred_element_type=F32), slice(c, c + cw))

    @pl.when(flags == TILE_VALID)
    def _():
        finish(jnp.dot(g_ref[...], wb_ref[...], preferred_element_type=F32), slice(0, tn))

    @pl.when((flags & TILE_VALID) == 0)
    def _():
        o_ref[...] = jnp.zeros_like(o_ref)


def swiglu_down(g, w2, tile_expert, flags, tm, tn, x=None, gate=None):
    rows, f = g.shape
    d = w2.shape[2]
    residual = x is not None
    in_specs = [pl.BlockSpec((tm, f), lambda j, t, te, tf: (t, 0)),
                pl.BlockSpec((1, f, tn), lambda j, t, te, tf: (te[t], 0, j))]
    args = [g, w2]
    if residual:
        in_specs += [pl.BlockSpec((tm, tn), lambda j, t, te, tf: (t, j)),
                     pl.BlockSpec((1, tn), lambda j, t, te, tf: (0, j))]
        args += [x, gate]
    grid_spec = pltpu.PrefetchScalarGridSpec(
        num_scalar_prefetch=2,
        grid=(d // tn, rows // tm),
        in_specs=in_specs,
        out_specs=pl.BlockSpec((tm, tn), lambda j, t, te, tf: (t, j)),
        scratch_shapes=[pltpu.VMEM((f, tn), BF16)],
    )
    return pl.pallas_call(
        functools.partial(_swiglu_down_kernel, residual=residual),
        grid_spec=grid_spec,
        out_shape=jax.ShapeDtypeStruct((rows, d), F32 if residual else BF16),
        compiler_params=_params("arbitrary", "arbitrary"),
        name="swiglu_down",
    )(tile_expert, flags, *args)


def _causal_conv(cur, tail_ref, w_ref, b_ref):
    t = cur.shape[0]
    tail = tail_ref[...]
    w = w_ref[...]
    row8 = lax.broadcasted_iota(jnp.int32, (SUBLANES, cur.shape[1]), 0)
    acc = cur * w[SSD_CONV - 1:SSD_CONV]
    top = cur[0:SUBLANES] * w[SSD_CONV - 1:SSD_CONV]
    for s in range(1, SSD_CONV):
        wk = w[SSD_CONV - 1 - s:SSD_CONV - s]
        rolled = pltpu.roll(cur, s, axis=0)
        acc += rolled * wk
        top += jnp.where(row8 < s, pltpu.roll(tail, s, axis=0), rolled[0:SUBLANES]) * wk
    tail_ref[...] = cur[t - SUBLANES:t]
    return jnp.concatenate([top, acc[SUBLANES:]], axis=0) + b_ref[...]


def _ssd_kernel(z_ref, x_ref, bc_ref, dt_ref, cwx_ref, cwb_ref, cbx_ref, cbb_ref, dtb_ref, alog_ref,
                dskip_ref, ng_ref, expand_ref, o_ref, tailx_ref, tailb_ref, state_ref, ybuf_ref):
    t = SSD_CHUNK
    hg = SSD_HEADS // SSD_GROUPS
    gw = SSD_WIDTH // SSD_GROUPS

    @pl.when(pl.program_id(0) == 0)
    def _():
        tailx_ref[...] = jnp.zeros_like(tailx_ref)
        tailb_ref[...] = jnp.zeros_like(tailb_ref)
        state_ref[...] = jnp.zeros_like(state_ref)

    xs = _silu(_causal_conv(x_ref[...], tailx_ref, cwx_ref, cbx_ref))
    bcm = _silu(_causal_conv(bc_ref[...], tailb_ref, cwb_ref, cbb_ref))
    dt = _softplus(dt_ref[...] + dtb_ref[...])
    a = -jnp.exp(alog_ref[...])
    row = lax.broadcasted_iota(jnp.int32, (t, t), 0)
    col = lax.broadcasted_iota(jnp.int32, (t, t), 1)
    tril = row >= col
    a_cs = jnp.dot(jnp.where(tril, 1.0, 0.0), dt * a, precision=HIGHEST, preferred_element_type=F32)
    a_last = a_cs[t - 1:t]
    per_head = jnp.concatenate(
        [dt, jnp.exp(a_last - a_cs), jnp.exp(a_cs), jnp.broadcast_to(jnp.exp(a_last), (SUBLANES, LANES))], axis=0)
    spread = jnp.dot(per_head, expand_ref[...], precision=HIGHEST, preferred_element_type=F32)
    dt_x, to_end_x, ea_x, cd_x = spread[0:t], spread[t:2 * t], spread[2 * t:3 * t], spread[3 * t:3 * t + 1]
    xdt = xs * dt_x
    xdt_b = xdt.astype(BF16)
    xw_b = (xdt * to_end_x).astype(BF16)
    a_cs_t = a_cs.T

    y_off = []
    for g in range(SSD_GROUPS):
        bm = bcm[:, g * SSD_STATE:(g + 1) * SSD_STATE]
        cm_b = bcm[:, (SSD_GROUPS + g) * SSD_STATE:(SSD_GROUPS + g + 1) * SSD_STATE].astype(BF16)
        cb = lax.dot_general(cm_b, bm.astype(BF16), (((1,), (1,)), ((), ())), preferred_element_type=F32)
        h_prev = state_ref[g]
        y_off.append(jnp.dot(cm_b, h_prev.astype(BF16), preferred_element_type=F32)
                     * ea_x[:, g * gw:(g + 1) * gw])
        st_new = jnp.dot(bm.T.astype(BF16), xw_b[:, g * gw:(g + 1) * gw], preferred_element_type=F32)
        state_ref[g] = h_prev * cd_x[:, g * gw:(g + 1) * gw] + st_new
        for r in range(0, hg, 2):
            pair = []
            for h in (g * hg + r, g * hg + r + 1):
                diff = a_cs[:, h:h + 1] - a_cs_t[h:h + 1, :]
                m = (cb * jnp.exp(jnp.where(tril, diff, -jnp.inf))).astype(BF16)
                pair.append(jnp.dot(m, xdt_b[:, h * SSD_HEAD_DIM:(h + 1) * SSD_HEAD_DIM],
                                    preferred_element_type=F32))
            lo = (g * hg + r) * SSD_HEAD_DIM
            ybuf_ref[:, lo:lo + 2 * SSD_HEAD_DIM] = jnp.concatenate(pair, axis=1)

    y = ybuf_ref[...] + jnp.concatenate(y_off, axis=1) + xs * dskip_ref[...]
    y = y * _silu(z_ref[...])
    outs = []
    for g in range(SSD_GROUPS):
        yg = y[:, g * gw:(g + 1) * gw]
        outs.append(yg * lax.rsqrt(jnp.mean(yg * yg, axis=-1, keepdims=True) + EPS))
    o_ref[...] = (jnp.concatenate(outs, axis=1) * ng_ref[...]).astype(o_ref.dtype)


def ssd_mixer(proj, conv_w, conv_b, dt_bias, a_log, d_skip, norm_g):
    L = proj.shape[0]
    t = SSD_CHUNK
    assert L % t == 0

    def pad_lanes(v):
        return jnp.pad(v, (0, LANES - v.shape[0])).reshape(1, LANES)

    expand = (jnp.arange(SSD_WIDTH)[None, :] // SSD_HEAD_DIM == jnp.arange(LANES)[:, None]).astype(F32)
    full = lambda shape: pl.BlockSpec(shape, lambda c: (0,) * len(shape))
    return pl.pallas_call(
        _ssd_kernel,
        grid=(L // t,),
        in_specs=[pl.BlockSpec((t, SSD_WIDTH), lambda c: (c, COL_Z // SSD_WIDTH)),
                  pl.BlockSpec((t, SSD_WIDTH), lambda c: (c, COL_X // SSD_WIDTH)),
                  pl.BlockSpec((t, SSD_BC), lambda c: (c, COL_BC // SSD_BC)),
                  pl.BlockSpec((t, LANES), lambda c: (c, COL_DT // LANES)),
                  full((SSD_CONV, SSD_WIDTH)), full((SSD_CONV, SSD_BC)),
                  full((1, SSD_WIDTH)), full((1, SSD_BC)),
                  full((1, LANES)), full((1, LANES)), full((1, SSD_WIDTH)), full((1, SSD_WIDTH)),
                  full((LANES, SSD_WIDTH))],
        out_specs=pl.BlockSpec((t, SSD_WIDTH), lambda c: (c, 0)),
        out_shape=jax.ShapeDtypeStruct((L, SSD_WIDTH), BF16),
        scratch_shapes=[pltpu.VMEM((SUBLANES, SSD_WIDTH), F32), pltpu.VMEM((SUBLANES, SSD_BC), F32),
                        pltpu.VMEM((SSD_GROUPS, SSD_STATE, SSD_WIDTH // SSD_GROUPS), F32),
                        pltpu.VMEM((t, SSD_WIDTH), F32)],
        compiler_params=_params("arbitrary"),
        name="ssd_mixer",
    )(proj, proj, proj, proj,
      conv_w[:, :SSD_WIDTH], conv_w[:, SSD_WIDTH:], conv_b[:SSD_WIDTH].reshape(1, -1),
      conv_b[SSD_WIDTH:].reshape(1, -1), pad_lanes(dt_bias), pad_lanes(a_log),
      jnp.repeat(d_skip, SSD_HEAD_DIM).reshape(1, -1), norm_g.reshape(1, -1), expand)


def _moba_prep_kernel(q_ref, k_ref, v_ref, qa_ref, ka_ref, va_ref, kmean_ref):
    own = pl.program_id(0)
    t = MOBA_BLOCK
    dh = MOBA_HEAD_DIM
    nbl = MOBA_MAX_BLOCKS

    @pl.when(own == 0)
    def _():
        kmean_ref[...] = jnp.zeros_like(kmean_ref)

    lane = lax.broadcasted_iota(jnp.int32, (t, LANES), 1)
    onehot = jnp.where(lane == dh + own, 1.0, 0.0)
    blk = lax.broadcasted_iota(jnp.int32, (nbl, t), 0)
    q_t = q_ref[...].T
    k = k_ref[...]
    k_mean = jnp.mean(k, axis=0, keepdims=True)
    va_ref[:, 0] = v_ref[...].T.reshape(MOBA_HEADS, dh, t).astype(BF16)
    for h in range(MOBA_HEADS):
        qh_t = q_t[h * dh:(h + 1) * dh]
        gate = jnp.dot(kmean_ref[h], qh_t, precision=HIGHEST, preferred_element_type=F32)
        gate = jnp.where(blk < own, gate, -jnp.inf)
        sel = blk >= own
        for _ in range(MOBA_TOPK):
            m = jnp.max(gate, axis=0, keepdims=True)
            idx = jnp.min(jnp.where(gate == m, blk, nbl), axis=0, keepdims=True)
            sel = sel | ((blk == idx) & (m > -jnp.inf))
            gate = jnp.where(blk == idx, -jnp.inf, gate)
        offs_t = jnp.where(sel, 0.0, NEG_BIG)
        qa_ref[h] = jnp.concatenate([qh_t * MOBA_Q_SCALE, offs_t, jnp.zeros((LANES - dh - nbl, t), F32)],
                                    axis=0).astype(BF16)
        pair = k[:, (h // 2) * LANES:(h // 2 + 1) * LANES]
        if h % 2:
            pair = pltpu.roll(pair, dh, axis=1)
        ka_ref[h] = jnp.where(lane < dh, pair, onehot).astype(BF16)
        kmean_ref[h, pl.ds(own, 1), :] = k_mean[:, h * dh:(h + 1) * dh]


def moba_prep(proj):
    L = proj.shape[0]
    t = MOBA_BLOCK
    assert L % t == 0 and L // t <= MOBA_MAX_BLOCKS
    return pl.pallas_call(
        _moba_prep_kernel,
        grid=(L // t,),
        in_specs=[pl.BlockSpec((t, MOBA_WIDTH), lambda i: (i, COL_MQ // MOBA_WIDTH)),
                  pl.BlockSpec((t, MOBA_WIDTH), lambda i: (i, COL_MK // MOBA_WIDTH)),
                  pl.BlockSpec((t, MOBA_WIDTH), lambda i: (i, COL_MV // MOBA_WIDTH))],
        out_specs=[pl.BlockSpec((MOBA_HEADS, LANES, t), lambda i: (0, 0, i)),
                   pl.BlockSpec((MOBA_HEADS, t, LANES), lambda i: (0, i, 0)),
                   pl.BlockSpec((MOBA_HEADS, 1, MOBA_HEAD_DIM, t), lambda i: (0, i, 0, 0))],
        out_shape=[jax.ShapeDtypeStruct((MOBA_HEADS, LANES, L), BF16),
                   jax.ShapeDtypeStruct((MOBA_HEADS, L, LANES), BF16),
                   jax.ShapeDtypeStruct((MOBA_HEADS, L // t, MOBA_HEAD_DIM, t), BF16)],
        scratch_shapes=[pltpu.VMEM((MOBA_HEADS, MOBA_MAX_BLOCKS, MOBA_HEAD_DIM), F32)],
        compiler_params=_params("arbitrary"),
        name="moba_prep",
    )(proj, proj, proj)


def _col_reduce(x, op):
    while x.shape[0] > SUBLANES:
        half = x.shape[0] // 2
        x = op(x[:half], x[half:])
    return jnp.max(x, axis=0, keepdims=True) if op is jnp.maximum else jnp.sum(x, axis=0, keepdims=True)


def _attention_kernel(near_ref, q_ref, k_ref, v_ref, posr_ref, posc_ref, tbl_ref, g_ref, lam_ref, o_ref,
                      m_ref, l_ref, al_ref, acc_ref, sa_ref, sb_ref, pa_ref, pb_ref, *, moba, lambda_init):
    hp = pl.program_id(0)
    qi = pl.program_id(1)
    nq = pl.num_programs(1)
    t = ATT_T
    m_ref[...] = jnp.full_like(m_ref, -jnp.inf)
    l_ref[...] = jnp.zeros_like(l_ref)
    acc_ref[...] = jnp.zeros_like(acc_ref)
    if moba:
        q_t = [q_ref[0], q_ref[1]]
        heads = [2 * hp, 2 * hp + 1]
    else:
        q = (q_ref[...].astype(F32) * DIFF_Q_SCALE).astype(BF16)
        half = lax.broadcasted_iota(jnp.int32, q.shape, 0) < DIFF_QK_DIM
        q_t = [jnp.where(half, q, jnp.zeros_like(q)), jnp.where(half, jnp.zeros_like(q), q)]
        heads = [MOBA_HEADS + hp]
    trow = [tbl_ref[pl.ds(hd, 1), :] for hd in heads]
    posq = posr_ref[pl.ds(qi, 1), :]

    def scores(ki, dst_ref):
        rows = pl.ds(pl.multiple_of(ki * t, t), t)
        for s in range(2):
            k = k_ref[s, rows, :] if moba else k_ref[rows, :]
            dst_ref[s] = jnp.dot(k, q_t[s], preferred_element_type=F32)

    def accumulate(ki, p_ref):
        for s in range(2):
            pv = jnp.dot(v_ref[s if moba else 0, ki], p_ref[s], preferred_element_type=F32)
            acc_ref[s] = al_ref[s] * acc_ref[s] + pv

    scores(0, sa_ref)
    pb_ref[...] = jnp.zeros_like(pb_ref)
    al_ref[...] = jnp.ones_like(al_ref)

    def tile(ki, general, cur_ref, nxt_ref, p_ref, p_prev_ref):
        scores(jnp.minimum(ki + 1, qi), nxt_ref)
        accumulate(jnp.maximum(ki - 1, 0), p_prev_ref)
        ch = ATT_CHUNK
        mx = [None, None]
        for r in range(0, t, ch):
            if general:
                dist = jnp.clip(posq - posc_ref[ki, r:r + ch, :], 0, LANES - 1)
                bias = [jnp.concatenate(
                    [jnp.take_along_axis(jnp.broadcast_to(tr, (ch, LANES)), dist[:, j * LANES:(j + 1) * LANES],
                                         axis=1) for j in range(t // LANES)], axis=1) for tr in trow]
                key = lax.broadcasted_iota(jnp.int32, (ch, t), 0) + r
                qry = lax.broadcasted_iota(jnp.int32, (ch, t), 1)
                causal = key + ki * t <= qry + qi * t
            for s in range(2):
                blk = cur_ref[s, r:r + ch, :]
                if general:
                    blk = jnp.where(causal, blk + bias[s if moba else 0], -jnp.inf)
                    cur_ref[s, r:r + ch, :] = blk
                mx[s] = blk if mx[s] is None else jnp.maximum(mx[s], blk)
        for s in range(2):
            m_old = m_ref[s]
            m_tile = _col_reduce(mx[s], jnp.maximum)
            if general:
                m_new = jnp.maximum(m_old, m_tile)
                shift = m_new
            else:
                c = trow[s if moba else 0][:, LANES - 1:LANES]
                m_new = jnp.maximum(m_old, m_tile + c)
                shift = m_new - c
            alpha = jnp.exp2(m_old - m_new)
            sm = None
            for r in range(0, t, ch):
                p = jnp.exp2(cur_ref[s, r:r + ch, :] - shift)
                p_ref[s, r:r + ch, :] = p.astype(BF16)
                sm = p if sm is None else sm + p
            l_ref[s] = alpha * l_ref[s] + _col_reduce(sm, jnp.add)
            al_ref[s] = alpha
            m_ref[s] = m_new

    def step(ki, *bufs):
        flag = near_ref[qi * nq + ki]

        @pl.when(flag == 0)
        def _():
            tile(ki, False, *bufs)

        @pl.when(flag != 0)
        def _():
            tile(ki, True, *bufs)

    def pair(j, carry):
        step(2 * j, sa_ref, sb_ref, pa_ref, pb_ref)
        step(2 * j + 1, sb_ref, sa_ref, pb_ref, pa_ref)
        return carry

    lax.fori_loop(0, (qi + 1) // 2, pair, 0)

    @pl.when(qi % 2 == 0)
    def _():
        step(qi, sa_ref, sb_ref, pa_ref, pb_ref)
        accumulate(qi, pa_ref)

    @pl.when(qi % 2 == 1)
    def _():
        accumulate(qi, pb_ref)

    if moba:
        outs = []
        for s in range(2):
            o = acc_ref[s] / l_ref[s]
            ms = jnp.sum(o * o, axis=0, keepdims=True) * (1.0 / MOBA_HEAD_DIM)
            outs.append(o * lax.rsqrt(ms + EPS) * g_ref[s])
        y_t = jnp.concatenate(outs, axis=0)
    else:
        lp = lam_ref[...]
        lam = (jnp.exp(jnp.sum(lp[0:1] * lp[1:2], axis=1, keepdims=True))
               - jnp.exp(jnp.sum(lp[2:3] * lp[3:4], axis=1, keepdims=True)) + lambda_init)
        o = acc_ref[0] / l_ref[0] - lam * (acc_ref[1] / l_ref[1])
        ms = jnp.sum(o * o, axis=0, keepdims=True) * (1.0 / DIFF_V_DIM)
        y_t = (o * lax.rsqrt(ms + EPS) * g_ref[0]) * (1.0 - lambda_init)
    o_ref[...] = y_t.T.astype(o_ref.dtype)


def _attention_call(kernel, steps, dv, L, near, in_specs, args, name):
    t = ATT_T
    full = lambda a: pl.BlockSpec(a.shape, lambda h, i, nr: (0,) * a.ndim)
    grid_spec = pltpu.PrefetchScalarGridSpec(
        num_scalar_prefetch=1,
        grid=(steps, L // t),
        in_specs=in_specs + [full(a) for a in args[len(in_specs):]],
        out_specs=pl.BlockSpec((t, LANES), lambda h, i, nr: (i, h)),
        scratch_shapes=[pltpu.VMEM((2, 1, t), F32), pltpu.VMEM((2, 1, t), F32), pltpu.VMEM((2, 1, t), F32),
                        pltpu.VMEM((2, dv, t), F32), pltpu.VMEM((2, t, t), F32), pltpu.VMEM((2, t, t), F32),
                        pltpu.VMEM((2, t, t), BF16), pltpu.VMEM((2, t, t), BF16)],
    )
    return pl.pallas_call(kernel, grid_spec=grid_spec, out_shape=jax.ShapeDtypeStruct((L, steps * LANES), BF16),
                          compiler_params=_params("parallel", "arbitrary"), name=name)(near, *args)


def moba_attention(qa_t, ka, va_t, near, pos_rows, pos_cols, table, norm_g):
    heads, L, _ = ka.shape
    t = ATT_T
    kernel = functools.partial(_attention_kernel, moba=True, lambda_init=None)
    g = norm_g.reshape(heads, MOBA_HEAD_DIM, 1)
    return _attention_call(
        kernel, heads // 2, MOBA_HEAD_DIM, L, near,
        [pl.BlockSpec((2, LANES, t), lambda h, i, nr: (h, 0, i)),
         pl.BlockSpec((2, L, LANES), lambda h, i, nr: (h, 0, 0)),
         pl.BlockSpec((2, L // t, MOBA_HEAD_DIM, t), lambda h, i, nr: (h, 0, 0, 0)),
         pl.BlockSpec(pos_rows.shape, lambda h, i, nr: (0, 0)),
         pl.BlockSpec(pos_cols.shape, lambda h, i, nr: (0, 0, 0)),
         pl.BlockSpec(table.shape, lambda h, i, nr: (0, 0)),
         pl.BlockSpec((2, MOBA_HEAD_DIM, 1), lambda h, i, nr: (h, 0, 0))],
        [qa_t, ka, va_t, pos_rows, pos_cols, table, g, jnp.zeros((4, DIFF_QK_DIM), F32)], "moba_attention")


def diff_attention(proj16, near, pos_rows, pos_cols, table, lam_params, subln_g, lambda_init):
    L = proj16.shape[0]
    t = ATT_T
    q_t = proj16[:, COL_DQ:COL_DQ + DIFF_WIDTH].T
    v_t = jnp.transpose(proj16[:, COL_DV:COL_DV + DIFF_WIDTH].reshape(L // t, t, DIFF_HEADS, DIFF_V_DIM),
                        (2, 0, 3, 1))
    kernel = functools.partial(_attention_kernel, moba=False, lambda_init=lambda_init)
    return _attention_call(
        kernel, DIFF_HEADS, DIFF_V_DIM, L, near,
        [pl.BlockSpec((LANES, t), lambda h, i, nr: (h, i)),
         pl.BlockSpec((L, LANES), lambda h, i, nr: (0, COL_DK // LANES + h)),
         pl.BlockSpec((1, L // t, DIFF_V_DIM, t), lambda h, i, nr: (h, 0, 0, 0))],
        [q_t, proj16, v_t, pos_rows, pos_cols, table, subln_g.reshape(1, DIFF_V_DIM, 1), lam_params],
        "diff_attention")


def _rel_bucket(dist):
    n = jnp.maximum(dist, 0)
    max_exact = REL_BUCKETS // 2
    nf = jnp.maximum(n, 1).astype(F32)
    large = max_exact + (jnp.log(nf / max_exact) / math.log(REL_MAX_DIST / max_exact)
                         * (REL_BUCKETS - max_exact)).astype(jnp.int32)
    return jnp.where(n < max_exact, n, jnp.minimum(large, REL_BUCKETS - 1))


def attention_tables(positions, rel_bias):
    L = positions.shape[0]
    t = ATT_T
    buckets = _rel_bucket(jnp.arange(LANES, dtype=jnp.int32))
    table = rel_bias[buckets].T * LOG2E
    pos_rows = positions.reshape(L // t, t)
    lo, hi = jnp.min(pos_rows, axis=1), jnp.max(pos_rows, axis=1)
    near = (lo[:, None] - hi[None, :] < LANES) | jnp.eye(L // t, dtype=bool)
    return table, pos_rows, positions.reshape(L // t, t, 1), near.astype(jnp.int32).reshape(-1)


def _router_kernel(x_ref, g_ref, sc_ref, sh_ref, rw_ref, h_ref, comb_ref, rank_ref, cum_ref, total_ref, cnt_ref):
    t = TOK_T

    @pl.when(pl.program_id(0) == 0)
    def _():
        cnt_ref[...] = jnp.zeros_like(cnt_ref)

    h = _norm_mod(x_ref[...], g_ref[...], sc_ref[...], sh_ref[...])
    h_ref[...] = h.astype(BF16)
    lane = lax.broadcasted_iota(jnp.int32, (t, LANES), 1)
    logits = jnp.dot(h, rw_ref[...], precision=HIGHEST, preferred_element_type=F32)
    logits = jnp.where(lane < N_EXPERTS, logits, -jnp.inf)
    m1 = jnp.max(logits, axis=1, keepdims=True)
    i1 = jnp.min(jnp.where(logits == m1, lane, LANES), axis=1, keepdims=True)
    rest = jnp.where(lane == i1, -jnp.inf, logits)
    m2 = jnp.max(rest, axis=1, keepdims=True)
    i2 = jnp.min(jnp.where(rest == m2, lane, LANES), axis=1, keepdims=True)
    e2 = jnp.exp(m2 - m1)
    denom = 1.0 + e2
    comb_ref[...] = jnp.where(lane == i1, 1.0 / denom, 0.0) + jnp.where(lane == i2, e2 / denom, 0.0)
    sel = jnp.where((lane == i1) | (lane == i2), 1.0, 0.0)
    row = lax.broadcasted_iota(jnp.int32, (t, t), 0)
    col = lax.broadcasted_iota(jnp.int32, (t, t), 1)
    before = jnp.dot(jnp.where(row > col, 1.0, 0.0).astype(BF16), sel.astype(BF16), preferred_element_type=F32)
    carry = cnt_ref[...]
    cum_ref[0] = carry
    rank_ref[...] = jnp.where(sel > 0.0, before + carry, -1.0)
    carry = carry + jnp.sum(sel, axis=0, keepdims=True)
    cnt_ref[...] = carry
    total_ref[...] = carry


def moe_router(x, g, sc, sh, router_w):
    L, d = x.shape
    t = TOK_T
    rw = jnp.pad(router_w, ((0, 0), (0, LANES - N_EXPERTS)))
    vec = pl.BlockSpec((1, d), lambda i: (0, 0))
    tok = pl.BlockSpec((t, LANES), lambda i: (i, 0))
    return pl.pallas_call(
        _router_kernel,
        grid=(L // t,),
        in_specs=[pl.BlockSpec((t, d), lambda i: (i, 0)), vec, vec, vec,
                  pl.BlockSpec((d, LANES), lambda i: (0, 0))],
        out_specs=[pl.BlockSpec((t, d), lambda i: (i, 0)), tok, tok,
                   pl.BlockSpec((1, 1, LANES), lambda i: (i, 0, 0)),
                   pl.BlockSpec((1, LANES), lambda i: (0, 0))],
        out_shape=[jax.ShapeDtypeStruct((L, d), BF16), jax.ShapeDtypeStruct((L, LANES), F32),
                   jax.ShapeDtypeStruct((L, LANES), F32), jax.ShapeDtypeStruct((L // t, 1, LANES), F32),
                   jax.ShapeDtypeStruct((1, LANES), F32)],
        scratch_shapes=[pltpu.VMEM((1, LANES), F32)],
        compiler_params=_params("arbitrary"),
        name="moe_router",
    )(x, g, sc, sh, rw)


def _item_lists(hit, n_items):
    rows, cols = hit.shape
    n_real = jnp.sum(hit.astype(jnp.int32))
    idx = jnp.nonzero(hit.reshape(-1), size=n_items, fill_value=0)[0].astype(jnp.int32)
    k = jnp.arange(n_items, dtype=jnp.int32)
    real = k < n_real
    idx = jnp.where(real, idx, idx[jnp.maximum(n_real - 1, 0)])
    r, c = idx // cols, idx % cols
    prev_r = jnp.concatenate([jnp.full((1,), -1, jnp.int32), r[:-1]])
    next_r = jnp.concatenate([r[1:], jnp.full((1,), -1, jnp.int32)])
    first = real & (r != prev_r)
    last = real & ((r != next_r) | (k == n_real - 1))
    flags = real.astype(jnp.int32) + 2 * first.astype(jnp.int32) + 4 * last.astype(jnp.int32)
    return r, c, flags


def moe_plan(rank, cum, total, L):
    tm, tb = MOE_TM, TOK_T
    nb = L // tb
    n_tiles = 2 * L // tm + N_EXPERTS
    counts = total[0, :N_EXPERTS].astype(jnp.int32)
    tiles_e = (counts + tm - 1) // tm
    tile_end = jnp.cumsum(tiles_e)
    tile_start = tile_end - tiles_e
    tid = jnp.arange(n_tiles, dtype=jnp.int32)
    tile_valid = tid < tile_end[-1]
    tile_expert = jnp.minimum(jnp.searchsorted(tile_end, tid, side="right"), N_EXPERTS - 1).astype(jnp.int32)
    local_row = (tid - tile_start[tile_expert]) * tm
    r = rank[:, :N_EXPERTS].astype(jnp.int32)
    dest = jnp.where(r >= 0, r + (tile_start * tm)[None, :], -1)
    cum_i = cum[:, 0, :N_EXPERTS].astype(jnp.int32)
    cum_next = jnp.concatenate([cum_i[1:], counts[None, :]], axis=0)
    lo = cum_i[:, tile_expert].T
    hi = cum_next[:, tile_expert].T
    hit = tile_valid[:, None] & (lo < (local_row + tm)[:, None]) & (hi > local_row[:, None])
    n_items = n_tiles + N_EXPERTS * nb
    pad_hit = hit | ((~tile_valid)[:, None] & (jnp.arange(nb) == 0)[None, :])
    gather_items = _item_lists(pad_hit, n_items)
    kb, tt, fl = _item_lists(hit.T, n_items)
    return dict(tile_expert=tile_expert, tile_valid=tile_valid.astype(jnp.int32),
                dest_rows=dest.T, dest_cols=dest.T.reshape(N_EXPERTS, L, 1),
                gather_items=gather_items, combine_items=(tt, kb, fl), n_tiles=n_tiles, n_items=n_items)


def _moe_gather_kernel(it_ref, ib_ref, if_ref, te_ref, dest_ref, h_ref, o_ref, acc_ref):
    i = pl.program_id(0)
    flag = if_ref[i]
    tile = it_ref[i]
    tm = MOE_TM

    @pl.when((flag & 2) != 0)
    def _():
        acc_ref[...] = jnp.zeros_like(acc_ref)

    @pl.when((flag & 1) != 0)
    def _():
        d = dest_ref[pl.ds(te_ref[tile], 1), :]
        rows = tile * tm + lax.broadcasted_iota(jnp.int32, (tm, 1), 0)
        onehot = jnp.where(d == rows, 1.0, 0.0).astype(BF16)
        acc_ref[...] += jnp.dot(onehot, h_ref[...], preferred_element_type=F32)

    @pl.when((flag & 4) != 0)
    def _():
        o_ref[...] = acc_ref[...].astype(o_ref.dtype)


def moe_gather(h, plan):
    L, d = h.shape
    tm, tb = MOE_TM, TOK_T
    it, ib, fl = plan["gather_items"]
    grid_spec = pltpu.PrefetchScalarGridSpec(
        num_scalar_prefetch=4,
        grid=(plan["n_items"],),
        in_specs=[pl.BlockSpec((N_EXPERTS, tb), lambda i, it, ib, fl, te: (0, ib[i])),
                  pl.BlockSpec((tb, d), lambda i, it, ib, fl, te: (ib[i], 0))],
        out_specs=pl.BlockSpec((tm, d), lambda i, it, ib, fl, te: (it[i], 0)),
        scratch_shapes=[pltpu.VMEM((tm, d), F32)],
    )
    return pl.pallas_call(
        _moe_gather_kernel,
        grid_spec=grid_spec,
        out_shape=jax.ShapeDtypeStruct((plan["n_tiles"] * tm, d), BF16),
        compiler_params=_params("arbitrary"),
        name="moe_gather",
    )(it, ib, fl, plan["tile_expert"], plan["dest_rows"], h)


def _moe_combine_kernel(it_ref, ib_ref, if_ref, te_ref, dest_ref, w_ref, ys_ref, x_ref, gate_ref, fg_ref, o_ref,
                        acc_ref, *, final_norm):
    i = pl.program_id(0)
    flag = if_ref[i]
    tile = it_ref[i]
    tm = MOE_TM

    @pl.when((flag & 2) != 0)
    def _():
        acc_ref[...] = jnp.zeros_like(acc_ref)

    @pl.when((flag & 1) != 0)
    def _():
        cols = tile * tm + lax.broadcasted_iota(jnp.int32, (1, tm), 1)
        onehot = jnp.where(dest_ref[0] == cols, 1.0, 0.0).astype(BF16)
        acc_ref[...] += w_ref[0] * jnp.dot(onehot, ys_ref[...], preferred_element_type=F32)

    @pl.when((flag & 4) != 0)
    def _():
        y = x_ref[...] + gate_ref[...] * acc_ref[...]
        if final_norm:
            y = y * lax.rsqrt(jnp.mean(y * y, axis=-1, keepdims=True) + EPS) * fg_ref[...]
        o_ref[...] = y


def moe_combine(ys, comb, x, gate, plan, final_g=None):
    L, d = x.shape
    final_norm = final_g is not None
    if not final_norm:
        final_g = jnp.ones((1, d), F32)
    tm, tb = MOE_TM, TOK_T
    it, ib, fl = plan["combine_items"]
    w_cols = comb[:, :N_EXPERTS].T.reshape(N_EXPERTS, L, 1)
    col = lambda i, it, ib, fl, te: (te[it[i]], ib[i], 0)
    grid_spec = pltpu.PrefetchScalarGridSpec(
        num_scalar_prefetch=4,
        grid=(plan["n_items"],),
        in_specs=[pl.BlockSpec((1, tb, 1), col), pl.BlockSpec((1, tb, 1), col),
                  pl.BlockSpec((tm, d), lambda i, it, ib, fl, te: (it[i], 0)),
                  pl.BlockSpec((tb, d), lambda i, it, ib, fl, te: (ib[i], 0)),
                  pl.BlockSpec((1, d), lambda i, it, ib, fl, te: (0, 0)),
                  pl.BlockSpec((1, d), lambda i, it, ib, fl, te: (0, 0))],
        out_specs=pl.BlockSpec((tb, d), lambda i, it, ib, fl, te: (ib[i], 0)),
        scratch_shapes=[pltpu.VMEM((tb, d), F32)],
    )
    return pl.pallas_call(
        functools.partial(_moe_combine_kernel, final_norm=final_norm),
        grid_spec=grid_spec,
        out_shape=jax.ShapeDtypeStruct((L, d), F32),
        compiler_params=_params("arbitrary"),
        name="moe_combine",
    )(it, ib, fl, plan["tile_expert"], plan["dest_cols"], w_cols, ys, x, gate, final_g)


def moe_ffn(x, g, sc, sh, gate, router_w, w1, w3, w2, final_g=None):
    L = x.shape[0]
    h, comb, rank, cum, total = moe_router(x, g, sc, sh, router_w)
    plan = moe_plan(rank, cum, total, L)
    xs = moe_gather(h, plan)
    te = plan["tile_expert"]
    flags = tile_flags(te, plan["tile_valid"])
    act = swiglu_up(xs, w1, w3, te, flags, MOE_TM, 1024)
    ys = swiglu_down(act, w2, te, flags, MOE_TM, 512)
    return moe_combine(ys, comb, x, gate, plan, final_g)


def dense_ffn(x, g, sc, sh, gate, w1, w3, w2):
    L = x.shape[0]
    tm = 512
    h = norm_modulate(x, g, sc, sh, BF16)
    te = jnp.zeros((L // tm,), jnp.int32)
    flags = tile_flags(te, jnp.ones((L // tm,), jnp.int32))
    act = swiglu_up(h, w1[None], w3[None], te, flags, tm, 512)
    return swiglu_down(act, w2[None], te, flags, tm, 512, x=x, gate=gate)


def _permute_in_proj(w):
    d = w.shape[0]
    return jnp.concatenate([w[:, :ORIG_DT], w[:, ORIG_DT + SSD_HEADS:], w[:, ORIG_DT:ORIG_DT + SSD_HEADS],
                            jnp.zeros((d, LANES - SSD_HEADS), w.dtype)], axis=1).astype(BF16)


def kernel(x, c, positions, rel_bias, w_ada, b_ada, norm_mix_g, w_in, conv_w, conv_b, dt_bias, a_log, d_skip, ssd_norm_g, moba_norm_g, diff_lambda, diff_subln_g, w_out, norm_ffn_g, dense_w1, dense_w3, dense_w2, router_w, expert_w1, expert_w3, expert_w2, final_g):
    batch, L, d = x.shape
    assert batch == 1 and d == D_MODEL
    x = x[0]
    mod = ada_modulation(c, w_ada, b_ada)
    table, pos_rows, pos_cols, near = attention_tables(positions[0], rel_bias)
    row = lambda v: v.reshape(1, -1)
    for layer in range(DEPTH):
        lambda_init = 0.8 - 0.6 * math.exp(-0.3 * layer)
        shift1, scale1, gate1, shift2, scale2, gate2 = (mod[layer, :, j * d:(j + 1) * d] for j in range(6))
        h = norm_modulate(x, row(norm_mix_g[layer]), scale1, shift1, BF16)
        proj, proj16 = in_projection(h, _permute_in_proj(w_in[layer]))
        y_ssd = ssd_mixer(proj, conv_w[layer], conv_b[layer], dt_bias[layer], a_log[layer], d_skip[layer],
                          ssd_norm_g[layer])
        qa, ka, va = moba_prep(proj)
        y_moba = moba_attention(qa, ka, va, near, pos_rows, pos_cols, table, moba_norm_g[layer])
        y_diff = diff_attention(proj16, near, pos_rows, pos_cols, table, diff_lambda[layer],
                                diff_subln_g[layer], lambda_init)
        x = out_projection(y_ssd, y_moba, y_diff, w_out[layer], x, gate1)
        i = layer // 2
        g2 = row(norm_ffn_g[layer])
        if layer % 2 == 0:
            x = dense_ffn(x, g2, scale2, shift2, gate2, dense_w1[i], dense_w3[i], dense_w2[i])
        else:
            x = moe_ffn(x, g2, scale2, shift2, gate2, router_w[i], expert_w1[i], expert_w3[i], expert_w2[i],
                        final_g=row(final_g) if layer == DEPTH - 1 else None)
    if (DEPTH - 1) % 2 == 0:
        zero = jnp.zeros((1, d), F32)
        x = norm_modulate(x, row(final_g), zero, zero, F32)
    return x[None]
```

```python
import functools
import math

import jax
import jax.numpy as jnp
from jax import lax
from jax.experimental import pallas as pl
from jax.experimental.pallas import tpu as pltpu

F32 = jnp.float32
BF16 = jnp.bfloat16
HIGHEST = lax.Precision.HIGHEST

D_MODEL = 2048
DEPTH = 2
SSD_HEADS = 16
SSD_HEAD_DIM = 64
SSD_WIDTH = SSD_HEADS * SSD_HEAD_DIM
SSD_GROUPS = 2
SSD_STATE = 128
SSD_CONV = 4
SSD_CHUNK = 256
SSD_BC = 2 * SSD_GROUPS * SSD_STATE
MOBA_HEADS = 8
MOBA_HEAD_DIM = 64
MOBA_WIDTH = MOBA_HEADS * MOBA_HEAD_DIM
MOBA_BLOCK = 256
MOBA_TOPK = 3
DIFF_HEADS = 4
DIFF_QK_DIM = 64
DIFF_V_DIM = 128
DIFF_WIDTH = DIFF_HEADS * DIFF_V_DIM
REL_BUCKETS = 32
REL_MAX_DIST = 128
D_FF_DENSE = 5632
N_EXPERTS = 8
D_FF_EXPERT = 7168
EPS = 1e-6

LANES = 128
SUBLANES = 8
VMEM_LIMIT = 56 * 1024 * 1024

COL_Z = 0
COL_X = SSD_WIDTH
COL_BC = COL_X + SSD_WIDTH
COL_MQ = COL_BC + SSD_BC
COL_MK = COL_MQ + MOBA_WIDTH
COL_MV = COL_MK + MOBA_WIDTH
COL_DQ = COL_MV + MOBA_WIDTH
COL_DK = COL_DQ + DIFF_WIDTH
COL_DV = COL_DK + DIFF_WIDTH
COL_DT = COL_DV + DIFF_WIDTH
PROJ_W = COL_DT + LANES
ORIG_DT = SSD_WIDTH + SSD_WIDTH + SSD_BC

ATT_T = 256
ATT_CHUNK = 32
TOK_T = 512
MOE_TM = 256
NEG_BIG = -1e9
MOBA_MAX_BLOCKS = 32
LOG2E = math.log2(math.e)
MOBA_Q_SCALE = MOBA_HEAD_DIM ** -0.5 * LOG2E
DIFF_Q_SCALE = DIFF_QK_DIM ** -0.5 * LOG2E


def _silu(x):
    return x * (1.0 / (1.0 + jnp.exp(-x)))


def _softplus(x):
    return jnp.maximum(x, 0.0) + jnp.log1p(jnp.exp(-jnp.abs(x)))


def _params(*sem):
    return pltpu.CompilerParams(dimension_semantics=sem, vmem_limit_bytes=VMEM_LIMIT)


def _ada_kernel(c_ref, w_ref, b_ref, o_ref):
    ca = _silu(c_ref[...])
    o_ref[0] = jnp.sum(ca * w_ref[0], axis=0, keepdims=True) + b_ref[0]


def ada_modulation(c, w_ada, b_ada):
    depth, d, n = w_ada.shape
    tn = 1024
    return pl.pallas_call(
        _ada_kernel,
        grid=(depth, n // tn),
        in_specs=[pl.BlockSpec((d, 1), lambda l, j: (0, 0)),
                  pl.BlockSpec((1, d, tn), lambda l, j: (l, 0, j)),
                  pl.BlockSpec((1, 1, tn), lambda l, j: (l, 0, j))],
        out_specs=pl.BlockSpec((1, 1, tn), lambda l, j: (l, 0, j)),
        out_shape=jax.ShapeDtypeStruct((depth, 1, n), F32),
        compiler_params=_params("parallel", "parallel"),
        name="ada_modulation",
    )(c.reshape(d, 1), w_ada, b_ada.reshape(depth, 1, n))


def _norm_mod(x, g, sc, sh):
    ms = jnp.mean(x * x, axis=-1, keepdims=True)
    return (x * lax.rsqrt(ms + EPS) * g) * (1.0 + sc) + sh


def _norm_kernel(x_ref, g_ref, sc_ref, sh_ref, o_ref):
    o_ref[...] = _norm_mod(x_ref[...], g_ref[...], sc_ref[...], sh_ref[...]).astype(o_ref.dtype)


def norm_modulate(x, g, sc, sh, out_dtype):
    L, d = x.shape
    tm = 512
    vec = pl.BlockSpec((1, d), lambda i: (0, 0))
    return pl.pallas_call(
        _norm_kernel,
        grid=(L // tm,),
        in_specs=[pl.BlockSpec((tm, d), lambda i: (i, 0)), vec, vec, vec],
        out_specs=pl.BlockSpec((tm, d), lambda i: (i, 0)),
        out_shape=jax.ShapeDtypeStruct((L, d), out_dtype),
        compiler_params=_params("parallel"),
        name="norm_modulate",
    )(x, g, sc, sh)


def _inproj_kernel(a_ref, w_ref, o32_ref, o16_ref):
    acc = jnp.dot(a_ref[...], w_ref[...], preferred_element_type=F32)
    o32_ref[...] = acc
    o16_ref[...] = acc.astype(BF16)


def in_projection(h, w):
    L, k = h.shape
    n = w.shape[1]
    tm, tn = 512, 1920
    return pl.pallas_call(
        _inproj_kernel,
        grid=(n // tn, L // tm),
        in_specs=[pl.BlockSpec((tm, k), lambda j, i: (i, 0)),
                  pl.BlockSpec((k, tn), lambda j, i: (0, j))],
        out_specs=[pl.BlockSpec((tm, tn), lambda j, i: (i, j)),
                   pl.BlockSpec((tm, tn), lambda j, i: (i, j))],
        out_shape=[jax.ShapeDtypeStruct((L, n), F32), jax.ShapeDtypeStruct((L, n), BF16)],
        compiler_params=_params("parallel", "parallel"),
        name="in_projection",
    )(h, w)


def _outproj_kernel(ys_ref, ym_ref, yd_ref, ws_ref, wm_ref, wd_ref, x_ref, gate_ref, o_ref, wb_ref):
    @pl.when(pl.program_id(1) == 0)
    def _():
        wb_ref[0:SSD_WIDTH] = ws_ref[...].astype(BF16)
        wb_ref[SSD_WIDTH:SSD_WIDTH + MOBA_WIDTH] = wm_ref[...].astype(BF16)
        wb_ref[SSD_WIDTH + MOBA_WIDTH:] = wd_ref[...].astype(BF16)

    acc = jnp.dot(ys_ref[...], wb_ref[0:SSD_WIDTH], preferred_element_type=F32)
    acc += jnp.dot(ym_ref[...], wb_ref[SSD_WIDTH:SSD_WIDTH + MOBA_WIDTH], preferred_element_type=F32)
    acc += jnp.dot(yd_ref[...], wb_ref[SSD_WIDTH + MOBA_WIDTH:], preferred_element_type=F32)
    o_ref[...] = x_ref[...] + gate_ref[...] * acc


def out_projection(y_ssd, y_moba, y_diff, w_out, x, gate):
    L, d = x.shape
    tm, tn = 512, 1024
    return pl.pallas_call(
        _outproj_kernel,
        grid=(d // tn, L // tm),
        in_specs=[pl.BlockSpec((tm, SSD_WIDTH), lambda j, i: (i, 0)),
                  pl.BlockSpec((tm, MOBA_WIDTH), lambda j, i: (i, 0)),
                  pl.BlockSpec((tm, DIFF_WIDTH), lambda j, i: (i, 0)),
                  pl.BlockSpec((SSD_WIDTH, tn), lambda j, i: (0, j)),
                  pl.BlockSpec((MOBA_WIDTH, tn), lambda j, i: (SSD_WIDTH // MOBA_WIDTH, j)),
                  pl.BlockSpec((DIFF_WIDTH, tn), lambda j, i: ((SSD_WIDTH + MOBA_WIDTH) // DIFF_WIDTH, j)),
                  pl.BlockSpec((tm, tn), lambda j, i: (i, j)),
                  pl.BlockSpec((1, tn), lambda j, i: (0, j))],
        out_specs=pl.BlockSpec((tm, tn), lambda j, i: (i, j)),
        out_shape=jax.ShapeDtypeStruct((L, d), F32),
        scratch_shapes=[pltpu.VMEM((w_out.shape[0], tn), BF16)],
        compiler_params=_params("arbitrary", "arbitrary"),
        name="out_projection",
    )(y_ssd, y_moba, y_diff, w_out, w_out, w_out, x, gate)


CAST_COLS = 256
TILE_VALID = 1
TILE_NEW_WEIGHTS = 2


def weight_plan(tile_expert, tile_valid):
    prev = jnp.concatenate([jnp.full((1,), -1, jnp.int32), tile_expert[:-1]])
    first = tile_expert != prev
    ordinal = jnp.cumsum(first.astype(jnp.int32)) - 1
    n_blocks = ordinal[-1] + 1
    block_expert = jnp.zeros_like(tile_expert).at[ordinal].set(tile_expert)
    next_expert = block_expert[(ordinal + 1) % n_blocks]
    flags = tile_valid * TILE_VALID + first.astype(jnp.int32) * TILE_NEW_WEIGHTS
    return tile_expert, flags, ordinal, next_expert, n_blocks.reshape(1)


def _grouped_matmul_kernel(te_ref, tf_ref, to_ref, tx_ref, nb_ref, a_ref, *rest, n_mats, residual):
    w_hbm, rest = rest[:n_mats], rest[n_mats:]
    if residual:
        x_ref, gate_ref, o_ref, wf_ref, wb_ref, sem = rest
    else:
        o_ref, wf_ref, wb_ref, sem = rest
    j = pl.program_id(0)
    t = pl.program_id(1)
    tn = o_ref.shape[1]
    cw = CAST_COLS
    flags = tf_ref[t]
    block = j * nb_ref[0] + to_ref[t]
    slot = block % 2

    def copies(expert, col_tile, dst_slot):
        cols = pl.ds(pl.multiple_of(col_tile * tn, tn), tn)
        return [pltpu.make_async_copy(w.at[expert, :, cols], wf_ref.at[dst_slot, m], sem.at[dst_slot, m])
                for m, w in enumerate(w_hbm)]

    @pl.when((flags & TILE_NEW_WEIGHTS) != 0)
    def _():
        @pl.when(block == 0)
        def _():
            for c in copies(te_ref[t], j, slot):
                c.start()

        wraps = to_ref[t] == nb_ref[0] - 1

        @pl.when(jnp.logical_not(wraps & (j == pl.num_programs(0) - 1)))
        def _():
            for c in copies(tx_ref[t], j + wraps.astype(jnp.int32), 1 - slot):
                c.start()

        for c in copies(te_ref[t], j, slot):
            c.wait()

    def epilogue(accs, cols):
        if n_mats == 2:
            y = _silu(accs[0]) * accs[1]
        else:
            y = accs[0]
            if residual:
                y = x_ref[:, cols] + gate_ref[:, cols] * y
        o_ref[:, cols] = y.astype(o_ref.dtype)

    @pl.when(flags == TILE_VALID + TILE_NEW_WEIGHTS)
    def _():
        a = a_ref[...]
        for c in range(0, tn, cw):
            accs = []
            for m in range(n_mats):
                wb = wf_ref[slot, m, :, c:c + cw].astype(BF16)
                wb_ref[m, :, c:c + cw] = wb
                accs.append(jnp.dot(a, wb, preferred_element_type=F32))
            epilogue(accs, slice(c, c + cw))

    @pl.when(flags == TILE_VALID)
    def _():
        a = a_ref[...]
        epilogue([jnp.dot(a, wb_ref[m], preferred_element_type=F32) for m in range(n_mats)], slice(0, tn))

    @pl.when((flags & TILE_VALID) == 0)
    def _():
        o_ref[...] = jnp.zeros_like(o_ref)


def grouped_matmul(a, weights, plan, tm, tn, out_dtype, x=None, gate=None, name="grouped_matmul"):
    rows, k = a.shape
    n = weights[0].shape[2]
    n_mats = len(weights)
    residual = x is not None
    idx = lambda f: (lambda j, t, *refs: f(j, t))
    in_specs = [pl.BlockSpec((tm, k), idx(lambda j, t: (t, 0)))] + [pl.BlockSpec(memory_space=pl.ANY)] * n_mats
    args = [a, *weights]
    if residual:
        in_specs += [pl.BlockSpec((tm, tn), idx(lambda j, t: (t, j))), pl.BlockSpec((1, tn), idx(lambda j, t: (0, j)))]
        args += [x, gate]
    grid_spec = pltpu.PrefetchScalarGridSpec(
        num_scalar_prefetch=5,
        grid=(n // tn, rows // tm),
        in_specs=in_specs,
        out_specs=pl.BlockSpec((tm, tn), idx(lambda j, t: (t, j))),
        scratch_shapes=[pltpu.VMEM((2, n_mats, k, tn), F32), pltpu.VMEM((n_mats, k, tn), BF16),
                        pltpu.SemaphoreType.DMA((2, n_mats))],
    )
    return pl.pallas_call(
        functools.partial(_grouped_matmul_kernel, n_mats=n_mats, residual=residual),
        grid_spec=grid_spec,
        out_shape=jax.ShapeDtypeStruct((rows, n), out_dtype),
        compiler_params=_params("arbitrary", "arbitrary"),
        name=name,
    )(*plan, *args)


def _causal_conv(cur, tail_ref, w_ref, b_ref):
    t = cur.shape[0]
    tail = tail_ref[...]
    w = w_ref[...]
    row8 = lax.broadcasted_iota(jnp.int32, (SUBLANES, cur.shape[1]), 0)
    acc = cur * w[SSD_CONV - 1:SSD_CONV]
    top = cur[0:SUBLANES] * w[SSD_CONV - 1:SSD_CONV]
    for s in range(1, SSD_CONV):
        wk = w[SSD_CONV - 1 - s:SSD_CONV - s]
        rolled = pltpu.roll(cur, s, axis=0)
        acc += rolled * wk
        top += jnp.where(row8 < s, pltpu.roll(tail, s, axis=0), rolled[0:SUBLANES]) * wk
    tail_ref[...] = cur[t - SUBLANES:t]
    return jnp.concatenate([top, acc[SUBLANES:]], axis=0) + b_ref[...]


def _ssd_kernel(z_ref, x_ref, bc_ref, dt_ref, cwx_ref, cwb_ref, cbx_ref, cbb_ref, dtb_ref, alog_ref,
                dskip_ref, ng_ref, expand_ref, o_ref, tailx_ref, tailb_ref, state_ref, ybuf_ref):
    t = SSD_CHUNK
    hg = SSD_HEADS // SSD_GROUPS
    gw = SSD_WIDTH // SSD_GROUPS

    @pl.when(pl.program_id(0) == 0)
    def _():
        tailx_ref[...] = jnp.zeros_like(tailx_ref)
        tailb_ref[...] = jnp.zeros_like(tailb_ref)
        state_ref[...] = jnp.zeros_like(state_ref)

    xs = _silu(_causal_conv(x_ref[...], tailx_ref, cwx_ref, cbx_ref))
    bcm = _silu(_causal_conv(bc_ref[...], tailb_ref, cwb_ref, cbb_ref))
    dt = _softplus(dt_ref[...] + dtb_ref[...])
    a = -jnp.exp(alog_ref[...])
    row = lax.broadcasted_iota(jnp.int32, (t, t), 0)
    col = lax.broadcasted_iota(jnp.int32, (t, t), 1)
    tril = row >= col
    a_cs = jnp.dot(jnp.where(tril, 1.0, 0.0), dt * a, precision=HIGHEST, preferred_element_type=F32)
    a_last = a_cs[t - 1:t]
    per_head = jnp.concatenate(
        [dt, jnp.exp(a_last - a_cs), jnp.exp(a_cs), jnp.broadcast_to(jnp.exp(a_last), (SUBLANES, LANES))], axis=0)
    spread = jnp.dot(per_head, expand_ref[...], precision=HIGHEST, preferred_element_type=F32)
    dt_x, to_end_x, ea_x, cd_x = spread[0:t], spread[t:2 * t], spread[2 * t:3 * t], spread[3 * t:3 * t + 1]
    xdt = xs * dt_x
    xdt_b = xdt.astype(BF16)
    xw_b = (xdt * to_end_x).astype(BF16)
    a_cs_t = a_cs.T

    y_off = []
    for g in range(SSD_GROUPS):
        bm = bcm[:, g * SSD_STATE:(g + 1) * SSD_STATE]
        cm_b = bcm[:, (SSD_GROUPS + g) * SSD_STATE:(SSD_GROUPS + g + 1) * SSD_STATE].astype(BF16)
        cb = lax.dot_general(cm_b, bm.astype(BF16), (((1,), (1,)), ((), ())), preferred_element_type=F32)
        h_prev = state_ref[g]
        y_off.append(jnp.dot(cm_b, h_prev.astype(BF16), preferred_element_type=F32)
                     * ea_x[:, g * gw:(g + 1) * gw])
        st_new = jnp.dot(bm.T.astype(BF16), xw_b[:, g * gw:(g + 1) * gw], preferred_element_type=F32)
        state_ref[g] = h_prev * cd_x[:, g * gw:(g + 1) * gw] + st_new
        for r in range(0, hg, 2):
            pair = []
            for h in (g * hg + r, g * hg + r + 1):
                diff = a_cs[:, h:h + 1] - a_cs_t[h:h + 1, :]
                m = (cb * jnp.exp(jnp.where(tril, diff, -jnp.inf))).astype(BF16)
                pair.append(jnp.dot(m, xdt_b[:, h * SSD_HEAD_DIM:(h + 1) * SSD_HEAD_DIM],
                                    preferred_element_type=F32))
            lo = (g * hg + r) * SSD_HEAD_DIM
            ybuf_ref[:, lo:lo + 2 * SSD_HEAD_DIM] = jnp.concatenate(pair, axis=1)

    y = ybuf_ref[...] + jnp.concatenate(y_off, axis=1) + xs * dskip_ref[...]
    y = y * _silu(z_ref[...])
    outs = []
    for g in range(SSD_GROUPS):
        yg = y[:, g * gw:(g + 1) * gw]
        outs.append(yg * lax.rsqrt(jnp.mean(yg * yg, axis=-1, keepdims=True) + EPS))
    o_ref[...] = (jnp.concatenate(outs, axis=1) * ng_ref[...]).astype(o_ref.dtype)


def ssd_mixer(proj, conv_w, conv_b, dt_bias, a_log, d_skip, norm_g):
    L = proj.shape[0]
    t = SSD_CHUNK
    assert L % t == 0

    def pad_lanes(v):
        return jnp.pad(v, (0, LANES - v.shape[0])).reshape(1, LANES)

    expand = (jnp.arange(SSD_WIDTH)[None, :] // SSD_HEAD_DIM == jnp.arange(LANES)[:, None]).astype(F32)
    full = lambda shape: pl.BlockSpec(shape, lambda c: (0,) * len(shape))
    return pl.pallas_call(
        _ssd_kernel,
        grid=(L // t,),
        in_specs=[pl.BlockSpec((t, SSD_WIDTH), lambda c: (c, COL_Z // SSD_WIDTH)),
                  pl.BlockSpec((t, SSD_WIDTH), lambda c: (c, COL_X // SSD_WIDTH)),
                  pl.BlockSpec((t, SSD_BC), lambda c: (c, COL_BC // SSD_BC)),
                  pl.BlockSpec((t, LANES), lambda c: (c, COL_DT // LANES)),
                  full((SSD_CONV, SSD_WIDTH)), full((SSD_CONV, SSD_BC)),
                  full((1, SSD_WIDTH)), full((1, SSD_BC)),
                  full((1, LANES)), full((1, LANES)), full((1, SSD_WIDTH)), full((1, SSD_WIDTH)),
                  full((LANES, SSD_WIDTH))],
        out_specs=pl.BlockSpec((t, SSD_WIDTH), lambda c: (c, 0)),
        out_shape=jax.ShapeDtypeStruct((L, SSD_WIDTH), BF16),
        scratch_shapes=[pltpu.VMEM((SUBLANES, SSD_WIDTH), F32), pltpu.VMEM((SUBLANES, SSD_BC), F32),
                        pltpu.VMEM((SSD_GROUPS, SSD_STATE, SSD_WIDTH // SSD_GROUPS), F32),
                        pltpu.VMEM((t, SSD_WIDTH), F32)],
        compiler_params=_params("arbitrary"),
        name="ssd_mixer",
    )(proj, proj, proj, proj,
      conv_w[:, :SSD_WIDTH], conv_w[:, SSD_WIDTH:], conv_b[:SSD_WIDTH].reshape(1, -1),
      conv_b[SSD_WIDTH:].reshape(1, -1), pad_lanes(dt_bias), pad_lanes(a_log),
      jnp.repeat(d_skip, SSD_HEAD_DIM).reshape(1, -1), norm_g.reshape(1, -1), expand)


def _moba_prep_kernel(q_ref, k_ref, v_ref, qa_ref, ka_ref, va_ref, kmean_ref):
    own = pl.program_id(0)
    t = MOBA_BLOCK
    dh = MOBA_HEAD_DIM
    nbl = MOBA_MAX_BLOCKS

    @pl.when(own == 0)
    def _():
        kmean_ref[...] = jnp.zeros_like(kmean_ref)

    lane = lax.broadcasted_iota(jnp.int32, (t, LANES), 1)
    onehot = jnp.where(lane == dh + own, 1.0, 0.0)
    blk = lax.broadcasted_iota(jnp.int32, (nbl, t), 0)
    q_t = q_ref[...].T
    k = k_ref[...]
    k_mean = jnp.mean(k, axis=0, keepdims=True)
    va_ref[:, 0] = v_ref[...].T.reshape(MOBA_HEADS, dh, t).astype(BF16)
    for h in range(MOBA_HEADS):
        qh_t = q_t[h * dh:(h + 1) * dh]
        gate = jnp.dot(kmean_ref[h], qh_t, precision=HIGHEST, preferred_element_type=F32)
        gate = jnp.where(blk < own, gate, -jnp.inf)
        sel = blk >= own
        for _ in range(MOBA_TOPK):
            m = jnp.max(gate, axis=0, keepdims=True)
            idx = jnp.min(jnp.where(gate == m, blk, nbl), axis=0, keepdims=True)
            sel = sel | ((blk == idx) & (m > -jnp.inf))
            gate = jnp.where(blk == idx, -jnp.inf, gate)
        offs_t = jnp.where(sel, 0.0, NEG_BIG)
        qa_ref[h] = jnp.concatenate([qh_t * MOBA_Q_SCALE, offs_t, jnp.zeros((LANES - dh - nbl, t), F32)],
                                    axis=0).astype(BF16)
        pair = k[:, (h // 2) * LANES:(h // 2 + 1) * LANES]
        if h % 2:
            pair = pltpu.roll(pair, dh, axis=1)
        ka_ref[h] = jnp.where(lane < dh, pair, onehot).astype(BF16)
        kmean_ref[h, pl.ds(own, 1), :] = k_mean[:, h * dh:(h + 1) * dh]


def moba_prep(proj):
    L = proj.shape[0]
    t = MOBA_BLOCK
    assert L % t == 0 and L // t <= MOBA_MAX_BLOCKS
    return pl.pallas_call(
        _moba_prep_kernel,
        grid=(L // t,),
        in_specs=[pl.BlockSpec((t, MOBA_WIDTH), lambda i: (i, COL_MQ // MOBA_WIDTH)),
                  pl.BlockSpec((t, MOBA_WIDTH), lambda i: (i, COL_MK // MOBA_WIDTH)),
                  pl.BlockSpec((t, MOBA_WIDTH), lambda i: (i, COL_MV // MOBA_WIDTH))],
        out_specs=[pl.BlockSpec((MOBA_HEADS, LANES, t), lambda i: (0, 0, i)),
                   pl.BlockSpec((MOBA_HEADS, t, LANES), lambda i: (0, i, 0)),
                   pl.BlockSpec((MOBA_HEADS, 1, MOBA_HEAD_DIM, t), lambda i: (0, i, 0, 0))],
        out_shape=[jax.ShapeDtypeStruct((MOBA_HEADS, LANES, L), BF16),
                   jax.ShapeDtypeStruct((MOBA_HEADS, L, LANES), BF16),
                   jax.ShapeDtypeStruct((MOBA_HEADS, L // t, MOBA_HEAD_DIM, t), BF16)],
        scratch_shapes=[pltpu.VMEM((MOBA_HEADS, MOBA_MAX_BLOCKS, MOBA_HEAD_DIM), F32)],
        compiler_params=_params("arbitrary"),
        name="moba_prep",
    )(proj, proj, proj)


def _col_reduce(x, op):
    while x.shape[0] > SUBLANES:
        half = x.shape[0] // 2
        x = op(x[:half], x[half:])
    return jnp.max(x, axis=0, keepdims=True) if op is jnp.maximum else jnp.sum(x, axis=0, keepdims=True)


def _attention_kernel(near_ref, q_ref, k_ref, v_ref, posr_ref, posc_ref, tbl_ref, g_ref, lam_ref, o_ref,
                      m_ref, l_ref, al_ref, acc_ref, sa_ref, sb_ref, pa_ref, pb_ref, *, moba, lambda_init):
    hp = pl.program_id(0)
    qi = pl.program_id(1)
    nq = pl.num_programs(1)
    t = ATT_T
    m_ref[...] = jnp.full_like(m_ref, -jnp.inf)
    l_ref[...] = jnp.zeros_like(l_ref)
    acc_ref[...] = jnp.zeros_like(acc_ref)
    if moba:
        q_t = [q_ref[0], q_ref[1]]
        heads = [2 * hp, 2 * hp + 1]
    else:
        q = (q_ref[...].astype(F32) * DIFF_Q_SCALE).astype(BF16)
        half = lax.broadcasted_iota(jnp.int32, q.shape, 0) < DIFF_QK_DIM
        q_t = [jnp.where(half, q, jnp.zeros_like(q)), jnp.where(half, jnp.zeros_like(q), q)]
        heads = [MOBA_HEADS + hp]
    trow = [tbl_ref[pl.ds(hd, 1), :] for hd in heads]
    posq = posr_ref[pl.ds(qi, 1), :]

    def scores(ki, dst_ref):
        rows = pl.ds(pl.multiple_of(ki * t, t), t)
        for s in range(2):
            k = k_ref[s, rows, :] if moba else k_ref[rows, :]
            dst_ref[s] = jnp.dot(k, q_t[s], preferred_element_type=F32)

    def accumulate(ki, p_ref):
        for s in range(2):
            pv = jnp.dot(v_ref[s if moba else 0, ki], p_ref[s], preferred_element_type=F32)
            acc_ref[s] = al_ref[s] * acc_ref[s] + pv

    scores(0, sa_ref)
    pb_ref[...] = jnp.zeros_like(pb_ref)
    al_ref[...] = jnp.ones_like(al_ref)

    def tile(ki, general, cur_ref, nxt_ref, p_ref, p_prev_ref):
        scores(jnp.minimum(ki + 1, qi), nxt_ref)
        accumulate(jnp.maximum(ki - 1, 0), p_prev_ref)
        ch = ATT_CHUNK
        mx = [None, None]
        for r in range(0, t, ch):
            if general:
                dist = jnp.clip(posq - posc_ref[ki, r:r + ch, :], 0, LANES - 1)
                bias = [jnp.concatenate(
                    [jnp.take_along_axis(jnp.broadcast_to(tr, (ch, LANES)), dist[:, j * LANES:(j + 1) * LANES],
                                         axis=1) for j in range(t // LANES)], axis=1) for tr in trow]
                key = lax.broadcasted_iota(jnp.int32, (ch, t), 0) + r
                qry = lax.broadcasted_iota(jnp.int32, (ch, t), 1)
                causal = key + ki * t <= qry + qi * t
            for s in range(2):
                blk = cur_ref[s, r:r + ch, :]
                if general:
                    blk = jnp.where(causal, blk + bias[s if moba else 0], -jnp.inf)
                    cur_ref[s, r:r + ch, :] = blk
                mx[s] = blk if mx[s] is None else jnp.maximum(mx[s], blk)
        for s in range(2):
            m_old = m_ref[s]
            m_tile = _col_reduce(mx[s], jnp.maximum)
            if general:
                m_new = jnp.maximum(m_old, m_tile)
                shift = m_new
            else:
                c = trow[s if moba else 0][:, LANES - 1:LANES]
                m_new = jnp.maximum(m_old, m_tile + c)
                shift = m_new - c
            alpha = jnp.exp2(m_old - m_new)
            sm = None
            for r in range(0, t, ch):
                p = jnp.exp2(cur_ref[s, r:r + ch, :] - shift)
                p_ref[s, r:r + ch, :] = p.astype(BF16)
                sm = p if sm is None else sm + p
            l_ref[s] = alpha * l_ref[s] + _col_reduce(sm, jnp.add)
            al_ref[s] = alpha
            m_ref[s] = m_new

    def step(ki, *bufs):
        flag = near_ref[qi * nq + ki]

        @pl.when(flag == 0)
        def _():
            tile(ki, False, *bufs)

        @pl.when(flag != 0)
        def _():
            tile(ki, True, *bufs)

    def pair(j, carry):
        step(2 * j, sa_ref, sb_ref, pa_ref, pb_ref)
        step(2 * j + 1, sb_ref, sa_ref, pb_ref, pa_ref)
        return carry

    lax.fori_loop(0, (qi + 1) // 2, pair, 0)

    @pl.when(qi % 2 == 0)
    def _():
        step(qi, sa_ref, sb_ref, pa_ref, pb_ref)
        accumulate(qi, pa_ref)

    @pl.when(qi % 2 == 1)
    def _():
        accumulate(qi, pb_ref)

    if moba:
        outs = []
        for s in range(2):
            o = acc_ref[s] / l_ref[s]
            ms = jnp.sum(o * o, axis=0, keepdims=True) * (1.0 / MOBA_HEAD_DIM)
            outs.append(o * lax.rsqrt(ms + EPS) * g_ref[s])
        y_t = jnp.concatenate(outs, axis=0)
    else:
        lp = lam_ref[...]
        lam = (jnp.exp(jnp.sum(lp[0:1] * lp[1:2], axis=1, keepdims=True))
               - jnp.exp(jnp.sum(lp[2:3] * lp[3:4], axis=1, keepdims=True)) + lambda_init)
        o = acc_ref[0] / l_ref[0] - lam * (acc_ref[1] / l_ref[1])
        ms = jnp.sum(o * o, axis=0, keepdims=True) * (1.0 / DIFF_V_DIM)
        y_t = (o * lax.rsqrt(ms + EPS) * g_ref[0]) * (1.0 - lambda_init)
    o_ref[...] = y_t.T.astype(o_ref.dtype)


def _attention_call(kernel, steps, dv, L, near, in_specs, args, name):
    t = ATT_T
    full = lambda a: pl.BlockSpec(a.shape, lambda h, i, nr: (0,) * a.ndim)
    grid_spec = pltpu.PrefetchScalarGridSpec(
        num_scalar_prefetch=1,
        grid=(steps, L // t),
        in_specs=in_specs + [full(a) for a in args[len(in_specs):]],
        out_specs=pl.BlockSpec((t, LANES), lambda h, i, nr: (i, h)),
        scratch_shapes=[pltpu.VMEM((2, 1, t), F32), pltpu.VMEM((2, 1, t), F32), pltpu.VMEM((2, 1, t), F32),
                        pltpu.VMEM((2, dv, t), F32), pltpu.VMEM((2, t, t), F32), pltpu.VMEM((2, t, t), F32),
                        pltpu.VMEM((2, t, t), BF16), pltpu.VMEM((2, t, t), BF16)],
    )
    return pl.pallas_call(kernel, grid_spec=grid_spec, out_shape=jax.ShapeDtypeStruct((L, steps * LANES), BF16),
                          compiler_params=_params("parallel", "arbitrary"), name=name)(near, *args)


def moba_attention(qa_t, ka, va_t, near, pos_rows, pos_cols, table, norm_g):
    heads, L, _ = ka.shape
    t = ATT_T
    kernel = functools.partial(_attention_kernel, moba=True, lambda_init=None)
    g = norm_g.reshape(heads, MOBA_HEAD_DIM, 1)
    return _attention_call(
        kernel, heads // 2, MOBA_HEAD_DIM, L, near,
        [pl.BlockSpec((2, LANES, t), lambda h, i, nr: (h, 0, i)),
         pl.BlockSpec((2, L, LANES), lambda h, i, nr: (h, 0, 0)),
         pl.BlockSpec((2, L // t, MOBA_HEAD_DIM, t), lambda h, i, nr: (h, 0, 0, 0)),
         pl.BlockSpec(pos_rows.shape, lambda h, i, nr: (0, 0)),
         pl.BlockSpec(pos_cols.shape, lambda h, i, nr: (0, 0, 0)),
         pl.BlockSpec(table.shape, lambda h, i, nr: (0, 0)),
         pl.BlockSpec((2, MOBA_HEAD_DIM, 1), lambda h, i, nr: (h, 0, 0))],
        [qa_t, ka, va_t, pos_rows, pos_cols, table, g, jnp.zeros((4, DIFF_QK_DIM), F32)], "moba_attention")


def diff_attention(proj16, near, pos_rows, pos_cols, table, lam_params, subln_g, lambda_init):
    L = proj16.shape[0]
    t = ATT_T
    q_t = proj16[:, COL_DQ:COL_DQ + DIFF_WIDTH].T
    v_t = jnp.transpose(proj16[:, COL_DV:COL_DV + DIFF_WIDTH].reshape(L // t, t, DIFF_HEADS, DIFF_V_DIM),
                        (2, 0, 3, 1))
    kernel = functools.partial(_attention_kernel, moba=False, lambda_init=lambda_init)
    return _attention_call(
        kernel, DIFF_HEADS, DIFF_V_DIM, L, near,
        [pl.BlockSpec((LANES, t), lambda h, i, nr: (h, i)),
         pl.BlockSpec((L, LANES), lambda h, i, nr: (0, COL_DK // LANES + h)),
         pl.BlockSpec((1, L // t, DIFF_V_DIM, t), lambda h, i, nr: (h, 0, 0, 0))],
        [q_t, proj16, v_t, pos_rows, pos_cols, table, subln_g.reshape(1, DIFF_V_DIM, 1), lam_params],
        "diff_attention")


def _rel_bucket(dist):
    n = jnp.maximum(dist, 0)
    max_exact = REL_BUCKETS // 2
    nf = jnp.maximum(n, 1).astype(F32)
    large = max_exact + (jnp.log(nf / max_exact) / math.log(REL_MAX_DIST / max_exact)
                         * (REL_BUCKETS - max_exact)).astype(jnp.int32)
    return jnp.where(n < max_exact, n, jnp.minimum(large, REL_BUCKETS - 1))


def attention_tables(positions, rel_bias):
    L = positions.shape[0]
    t = ATT_T
    buckets = _rel_bucket(jnp.arange(LANES, dtype=jnp.int32))
    table = rel_bias[buckets].T * LOG2E
    pos_rows = positions.reshape(L // t, t)
    lo, hi = jnp.min(pos_rows, axis=1), jnp.max(pos_rows, axis=1)
    near = (lo[:, None] - hi[None, :] < LANES) | jnp.eye(L // t, dtype=bool)
    return table, pos_rows, positions.reshape(L // t, t, 1), near.astype(jnp.int32).reshape(-1)


def _router_kernel(x_ref, g_ref, sc_ref, sh_ref, rw_ref, h_ref, comb_ref, rank_ref, cum_ref, total_ref, cnt_ref):
    t = TOK_T

    @pl.when(pl.program_id(0) == 0)
    def _():
        cnt_ref[...] = jnp.zeros_like(cnt_ref)

    h = _norm_mod(x_ref[...], g_ref[...], sc_ref[...], sh_ref[...])
    h_ref[...] = h.astype(BF16)
    lane = lax.broadcasted_iota(jnp.int32, (t, LANES), 1)
    logits = jnp.dot(h, rw_ref[...], precision=HIGHEST, preferred_element_type=F32)
    logits = jnp.where(lane < N_EXPERTS, logits, -jnp.inf)
    m1 = jnp.max(logits, axis=1, keepdims=True)
    i1 = jnp.min(jnp.where(logits == m1, lane, LANES), axis=1, keepdims=True)
    rest = jnp.where(lane == i1, -jnp.inf, logits)
    m2 = jnp.max(rest, axis=1, keepdims=True)
    i2 = jnp.min(jnp.where(rest == m2, lane, LANES), axis=1, keepdims=True)
    e2 = jnp.exp(m2 - m1)
    denom = 1.0 + e2
    comb_ref[...] = jnp.where(lane == i1, 1.0 / denom, 0.0) + jnp.where(lane == i2, e2 / denom, 0.0)
    sel = jnp.where((lane == i1) | (lane == i2), 1.0, 0.0)
    row = lax.broadcasted_iota(jnp.int32, (t, t), 0)
    col = lax.broadcasted_iota(jnp.int32, (t, t), 1)
    before = jnp.dot(jnp.where(row > col, 1.0, 0.0).astype(BF16), sel.astype(BF16), preferred_element_type=F32)
    carry = cnt_ref[...]
    cum_ref[0] = carry
    rank_ref[...] = jnp.where(sel > 0.0, before + carry, -1.0)
    carry = carry + jnp.sum(sel, axis=0, keepdims=True)
    cnt_ref[...] = carry
    total_ref[...] = carry


def moe_router(x, g, sc, sh, router_w):
    L, d = x.shape
    t = TOK_T
    rw = jnp.pad(router_w, ((0, 0), (0, LANES - N_EXPERTS)))
    vec = pl.BlockSpec((1, d), lambda i: (0, 0))
    tok = pl.BlockSpec((t, LANES), lambda i: (i, 0))
    return pl.pallas_call(
        _router_kernel,
        grid=(L // t,),
        in_specs=[pl.BlockSpec((t, d), lambda i: (i, 0)), vec, vec, vec,
                  pl.BlockSpec((d, LANES), lambda i: (0, 0))],
        out_specs=[pl.BlockSpec((t, d), lambda i: (i, 0)), tok, tok,
                   pl.BlockSpec((1, 1, LANES), lambda i: (i, 0, 0)),
                   pl.BlockSpec((1, LANES), lambda i: (0, 0))],
        out_shape=[jax.ShapeDtypeStruct((L, d), BF16), jax.ShapeDtypeStruct((L, LANES), F32),
                   jax.ShapeDtypeStruct((L, LANES), F32), jax.ShapeDtypeStruct((L // t, 1, LANES), F32),
                   jax.ShapeDtypeStruct((1, LANES), F32)],
        scratch_shapes=[pltpu.VMEM((1, LANES), F32)],
        compiler_params=_params("arbitrary"),
        name="moe_router",
    )(x, g, sc, sh, rw)


def _item_lists(hit, n_items):
    rows, cols = hit.shape
    n_real = jnp.sum(hit.astype(jnp.int32))
    idx = jnp.nonzero(hit.reshape(-1), size=n_items, fill_value=0)[0].astype(jnp.int32)
    k = jnp.arange(n_items, dtype=jnp.int32)
    real = k < n_real
    idx = jnp.where(real, idx, idx[jnp.maximum(n_real - 1, 0)])
    r, c = idx // cols, idx % cols
    prev_r = jnp.concatenate([jnp.full((1,), -1, jnp.int32), r[:-1]])
    next_r = jnp.concatenate([r[1:], jnp.full((1,), -1, jnp.int32)])
    first = real & (r != prev_r)
    last = real & ((r != next_r) | (k == n_real - 1))
    flags = real.astype(jnp.int32) + 2 * first.astype(jnp.int32) + 4 * last.astype(jnp.int32)
    return r, c, flags


def moe_plan(rank, cum, total, L):
    tm, tb = MOE_TM, TOK_T
    nb = L // tb
    n_tiles = 2 * L // tm + N_EXPERTS
    counts = total[0, :N_EXPERTS].astype(jnp.int32)
    tiles_e = (counts + tm - 1) // tm
    tile_end = jnp.cumsum(tiles_e)
    tile_start = tile_end - tiles_e
    tid = jnp.arange(n_tiles, dtype=jnp.int32)
    tile_valid = tid < tile_end[-1]
    tile_expert = jnp.minimum(jnp.searchsorted(tile_end, tid, side="right"), N_EXPERTS - 1).astype(jnp.int32)
    local_row = (tid - tile_start[tile_expert]) * tm
    r = rank[:, :N_EXPERTS].astype(jnp.int32)
    dest = jnp.where(r >= 0, r + (tile_start * tm)[None, :], -1)
    cum_i = cum[:, 0, :N_EXPERTS].astype(jnp.int32)
    cum_next = jnp.concatenate([cum_i[1:], counts[None, :]], axis=0)
    lo = cum_i[:, tile_expert].T
    hi = cum_next[:, tile_expert].T
    hit = tile_valid[:, None] & (lo < (local_row + tm)[:, None]) & (hi > local_row[:, None])
    n_items = n_tiles + N_EXPERTS * nb
    pad_hit = hit | ((~tile_valid)[:, None] & (jnp.arange(nb) == 0)[None, :])
    gather_items = _item_lists(pad_hit, n_items)
    kb, tt, fl = _item_lists(hit.T, n_items)
    return dict(tile_expert=tile_expert, tile_valid=tile_valid.astype(jnp.int32),
                dest_rows=dest.T, dest_cols=dest.T.reshape(N_EXPERTS, L, 1),
                gather_items=gather_items, combine_items=(tt, kb, fl), n_tiles=n_tiles, n_items=n_items)


def _moe_gather_kernel(it_ref, ib_ref, if_ref, te_ref, dest_ref, h_ref, o_ref, acc_ref):
    i = pl.program_id(0)
    flag = if_ref[i]
    tile = it_ref[i]
    tm = MOE_TM

    @pl.when((flag & 2) != 0)
    def _():
        acc_ref[...] = jnp.zeros_like(acc_ref)

    @pl.when((flag & 1) != 0)
    def _():
        d = dest_ref[pl.ds(te_ref[tile], 1), :]
        rows = tile * tm + lax.broadcasted_iota(jnp.int32, (tm, 1), 0)
        onehot = jnp.where(d == rows, 1.0, 0.0).astype(BF16)
        acc_ref[...] += jnp.dot(onehot, h_ref[...], preferred_element_type=F32)

    @pl.when((flag & 4) != 0)
    def _():
        o_ref[...] = acc_ref[...].astype(o_ref.dtype)


def moe_gather(h, plan):
    L, d = h.shape
    tm, tb = MOE_TM, TOK_T
    it, ib, fl = plan["gather_items"]
    grid_spec = pltpu.PrefetchScalarGridSpec(
        num_scalar_prefetch=4,
        grid=(plan["n_items"],),
        in_specs=[pl.BlockSpec((N_EXPERTS, tb), lambda i, it, ib, fl, te: (0, ib[i])),
                  pl.BlockSpec((tb, d), lambda i, it, ib, fl, te: (ib[i], 0))],
        out_specs=pl.BlockSpec((tm, d), lambda i, it, ib, fl, te: (it[i], 0)),
        scratch_shapes=[pltpu.VMEM((tm, d), F32)],
    )
    return pl.pallas_call(
        _moe_gather_kernel,
        grid_spec=grid_spec,
        out_shape=jax.ShapeDtypeStruct((plan["n_tiles"] * tm, d), BF16),
        compiler_params=_params("arbitrary"),
        name="moe_gather",
    )(it, ib, fl, plan["tile_expert"], plan["dest_rows"], h)


def _moe_combine_kernel(it_ref, ib_ref, if_ref, te_ref, dest_ref, w_ref, ys_ref, x_ref, gate_ref, fg_ref, o_ref,
                        acc_ref, *, final_norm):
    i = pl.program_id(0)
    flag = if_ref[i]
    tile = it_ref[i]
    tm = MOE_TM

    @pl.when((flag & 2) != 0)
    def _():
        acc_ref[...] = jnp.zeros_like(acc_ref)

    @pl.when((flag & 1) != 0)
    def _():
        cols = tile * tm + lax.broadcasted_iota(jnp.int32, (1, tm), 1)
        onehot = jnp.where(dest_ref[0] == cols, 1.0, 0.0).astype(BF16)
        acc_ref[...] += w_ref[0] * jnp.dot(onehot, ys_ref[...], preferred_element_type=F32)

    @pl.when((flag & 4) != 0)
    def _():
        y = x_ref[...] + gate_ref[...] * acc_ref[...]
        if final_norm:
            y = y * lax.rsqrt(jnp.mean(y * y, axis=-1, keepdims=True) + EPS) * fg_ref[...]
        o_ref[...] = y


def moe_combine(ys, comb, x, gate, plan, final_g=None):
    L, d = x.shape
    final_norm = final_g is not None
    if not final_norm:
        final_g = jnp.ones((1, d), F32)
    tm, tb = MOE_TM, TOK_T
    it, ib, fl = plan["combine_items"]
    w_cols = comb[:, :N_EXPERTS].T.reshape(N_EXPERTS, L, 1)
    col = lambda i, it, ib, fl, te: (te[it[i]], ib[i], 0)
    grid_spec = pltpu.PrefetchScalarGridSpec(
        num_scalar_prefetch=4,
        grid=(plan["n_items"],),
        in_specs=[pl.BlockSpec((1, tb, 1), col), pl.BlockSpec((1, tb, 1), col),
                  pl.BlockSpec((tm, d), lambda i, it, ib, fl, te: (it[i], 0)),
                  pl.BlockSpec((tb, d), lambda i, it, ib, fl, te: (ib[i], 0)),
                  pl.BlockSpec((1, d), lambda i, it, ib, fl, te: (0, 0)),
                  pl.BlockSpec((1, d), lambda i, it, ib, fl, te: (0, 0))],
        out_specs=pl.BlockSpec((tb, d), lambda i, it, ib, fl, te: (ib[i], 0)),
        scratch_shapes=[pltpu.VMEM((tb, d), F32)],
    )
    return pl.pallas_call(
        functools.partial(_moe_combine_kernel, final_norm=final_norm),
        grid_spec=grid_spec,
        out_shape=jax.ShapeDtypeStruct((L, d), F32),
        compiler_params=_params("arbitrary"),
        name="moe_combine",
    )(it, ib, fl, plan["tile_expert"], plan["dest_cols"], w_cols, ys, x, gate, final_g)


def moe_ffn(x, g, sc, sh, gate, router_w, w1, w3, w2, final_g=None):
    L = x.shape[0]
    h, comb, rank, cum, total = moe_router(x, g, sc, sh, router_w)
    plan = moe_plan(rank, cum, total, L)
    xs = moe_gather(h, plan)
    wplan = weight_plan(plan["tile_expert"], plan["tile_valid"])
    act = grouped_matmul(xs, [w1, w3], wplan, MOE_TM, 1024, BF16, name="swiglu_up")
    ys = grouped_matmul(act, [w2], wplan, MOE_TM, 512, BF16, name="swiglu_down")
    return moe_combine(ys, comb, x, gate, plan, final_g)


def dense_ffn(x, g, sc, sh, gate, w1, w3, w2):
    L = x.shape[0]
    tm = 512
    h = norm_modulate(x, g, sc, sh, BF16)
    wplan = weight_plan(jnp.zeros((L // tm,), jnp.int32), jnp.ones((L // tm,), jnp.int32))
    act = grouped_matmul(h, [w1[None], w3[None]], wplan, tm, 512, BF16, name="swiglu_up")
    return grouped_matmul(act, [w2[None]], wplan, tm, 512, F32, x=x, gate=gate, name="swiglu_down")


def _permute_in_proj(w):
    d = w.shape[0]
    return jnp.concatenate([w[:, :ORIG_DT], w[:, ORIG_DT + SSD_HEADS:], w[:, ORIG_DT:ORIG_DT + SSD_HEADS],
                            jnp.zeros((d, LANES - SSD_HEADS), w.dtype)], axis=1).astype(BF16)


def kernel(x, c, positions, rel_bias, w_ada, b_ada, norm_mix_g, w_in, conv_w, conv_b, dt_bias, a_log, d_skip, ssd_norm_g, moba_norm_g, diff_lambda, diff_subln_g, w_out, norm_ffn_g, dense_w1, dense_w3, dense_w2, router_w, expert_w1, expert_w3, expert_w2, final_g):
    batch, L, d = x.shape
    assert batch == 1 and d == D_MODEL
    x = x[0]
    mod = ada_modulation(c, w_ada, b_ada)
    table, pos_rows, pos_cols, near = attention_tables(positions[0], rel_bias)
    row = lambda v: v.reshape(1, -1)
    for layer in range(DEPTH):
        lambda_init = 0.8 - 0.6 * math.exp(-0.3 * layer)
        shift1, scale1, gate1, shift2, scale2, gate2 = (mod[layer, :, j * d:(j + 1) * d] for j in range(6))
        h = norm_modulate(x, row(norm_mix_g[layer]), scale1, shift1, BF16)
        proj, proj16 = in_projection(h, _permute_in_proj(w_in[layer]))
        y_ssd = ssd_mixer(proj, conv_w[layer], conv_b[layer], dt_bias[layer], a_log[layer], d_skip[layer],
                          ssd_norm_g[layer])
        qa, ka, va = moba_prep(proj)
        y_moba = moba_attention(qa, ka, va, near, pos_rows, pos_cols, table, moba_norm_g[layer])
        y_diff = diff_attention(proj16, near, pos_rows, pos_cols, table, diff_lambda[layer],
                                diff_subln_g[layer], lambda_init)
        x = out_projection(y_ssd, y_moba, y_diff, w_out[layer], x, gate1)
        i = layer // 2
        g2 = row(norm_ffn_g[layer])
        if layer % 2 == 0:
            x = dense_ffn(x, g2, scale2, shift2, gate2, dense_w1[i], dense_w3[i], dense_w2[i])
        else:
            x = moe_ffn(x, g2, scale2, shift2, gate2, router_w[i], expert_w1[i], expert_w3[i], expert_w2[i],
                        final_g=row(final_g) if layer == DEPTH - 1 else None)
    if (DEPTH - 1) % 2 == 0:
        zero = jnp.zeros((1, d), F32)
        x = norm_modulate(x, row(final_g), zero, zero, F32)
    return x[None]
```

```python
import functools
import math

import jax
import jax.numpy as jnp
from jax import lax
from jax.experimental import pallas as pl
from jax.experimental.pallas import tpu as pltpu

F32 = jnp.float32
BF16 = jnp.bfloat16
HIGHEST = lax.Precision.HIGHEST

D_MODEL = 2048
DEPTH = 2
SSD_HEADS = 16
SSD_HEAD_DIM = 64
SSD_WIDTH = SSD_HEADS * SSD_HEAD_DIM
SSD_GROUPS = 2
SSD_STATE = 128
SSD_CONV = 4
SSD_CHUNK = 256
SSD_BC = 2 * SSD_GROUPS * SSD_STATE
MOBA_HEADS = 8
MOBA_HEAD_DIM = 64
MOBA_WIDTH = MOBA_HEADS * MOBA_HEAD_DIM
MOBA_BLOCK = 256
MOBA_TOPK = 3
DIFF_HEADS = 4
DIFF_QK_DIM = 64
DIFF_V_DIM = 128
DIFF_WIDTH = DIFF_HEADS * DIFF_V_DIM
REL_BUCKETS = 32
REL_MAX_DIST = 128
D_FF_DENSE = 5632
N_EXPERTS = 8
D_FF_EXPERT = 7168
EPS = 1e-6

LANES = 128
SUBLANES = 8
VMEM_LIMIT = 56 * 1024 * 1024

COL_Z = 0
COL_X = SSD_WIDTH
COL_BC = COL_X + SSD_WIDTH
COL_MQ = COL_BC + SSD_BC
COL_MK = COL_MQ + MOBA_WIDTH
COL_MV = COL_MK + MOBA_WIDTH
COL_DQ = COL_MV + MOBA_WIDTH
COL_DK = COL_DQ + DIFF_WIDTH
COL_DV = COL_DK + DIFF_WIDTH
COL_DT = COL_DV + DIFF_WIDTH
PROJ_W = COL_DT + LANES
ORIG_DT = SSD_WIDTH + SSD_WIDTH + SSD_BC

ATT_T = 256
ATT_CHUNK = 32
TOK_T = 512
MOE_TM = 256
NEG_BIG = -1e9
MOBA_MAX_BLOCKS = 32
LOG2E = math.log2(math.e)
MOBA_Q_SCALE = MOBA_HEAD_DIM ** -0.5 * LOG2E
DIFF_Q_SCALE = DIFF_QK_DIM ** -0.5 * LOG2E


def _silu(x):
    return x * (1.0 / (1.0 + jnp.exp(-x)))


def _softplus(x):
    return jnp.maximum(x, 0.0) + jnp.log1p(jnp.exp(-jnp.abs(x)))


def _params(*sem):
    return pltpu.CompilerParams(dimension_semantics=sem, vmem_limit_bytes=VMEM_LIMIT)


def _ada_kernel(c_ref, w_ref, b_ref, o_ref):
    ca = _silu(c_ref[...])
    o_ref[0] = jnp.sum(ca * w_ref[0], axis=0, keepdims=True) + b_ref[0]


def ada_modulation(c, w_ada, b_ada):
    depth, d, n = w_ada.shape
    tn = 2048
    return pl.pallas_call(
        _ada_kernel,
        grid=(depth, n // tn),
        in_specs=[pl.BlockSpec((d, 1), lambda l, j: (0, 0)),
                  pl.BlockSpec((1, d, tn), lambda l, j: (l, 0, j)),
                  pl.BlockSpec((1, 1, tn), lambda l, j: (l, 0, j))],
        out_specs=pl.BlockSpec((1, 1, tn), lambda l, j: (l, 0, j)),
        out_shape=jax.ShapeDtypeStruct((depth, 1, n), F32),
        compiler_params=_params("parallel", "parallel"),
        name="ada_modulation",
    )(c.reshape(d, 1), w_ada, b_ada.reshape(depth, 1, n))


def _norm_mod(x, g, sc, sh):
    ms = jnp.mean(x * x, axis=-1, keepdims=True)
    return (x * lax.rsqrt(ms + EPS) * g) * (1.0 + sc) + sh


def _norm_kernel(x_ref, g_ref, sc_ref, sh_ref, o_ref):
    o_ref[...] = _norm_mod(x_ref[...], g_ref[...], sc_ref[...], sh_ref[...]).astype(o_ref.dtype)


def norm_modulate(x, g, sc, sh, out_dtype):
    L, d = x.shape
    tm = 512
    vec = pl.BlockSpec((1, d), lambda i: (0, 0))
    return pl.pallas_call(
        _norm_kernel,
        grid=(L // tm,),
        in_specs=[pl.BlockSpec((tm, d), lambda i: (i, 0)), vec, vec, vec],
        out_specs=pl.BlockSpec((tm, d), lambda i: (i, 0)),
        out_shape=jax.ShapeDtypeStruct((L, d), out_dtype),
        compiler_params=_params("parallel"),
        name="norm_modulate",
    )(x, g, sc, sh)


def _inproj_kernel(a_ref, w_ref, o32_ref, o16_ref):
    acc = jnp.dot(a_ref[...], w_ref[...], preferred_element_type=F32)
    o32_ref[...] = acc
    o16_ref[...] = acc.astype(BF16)


def in_projection(h, w):
    L, k = h.shape
    n = w.shape[1]
    tm, tn = 1024, 1920
    return pl.pallas_call(
        _inproj_kernel,
        grid=(n // tn, L // tm),
        in_specs=[pl.BlockSpec((tm, k), lambda j, i: (i, 0)),
                  pl.BlockSpec((k, tn), lambda j, i: (0, j))],
        out_specs=[pl.BlockSpec((tm, tn), lambda j, i: (i, j)),
                   pl.BlockSpec((tm, tn), lambda j, i: (i, j))],
        out_shape=[jax.ShapeDtypeStruct((L, n), F32), jax.ShapeDtypeStruct((L, n), BF16)],
        compiler_params=_params("parallel", "parallel"),
        name="in_projection",
    )(h, w)


def _outproj_kernel(ys_ref, ym_ref, yd_ref, ws_ref, wm_ref, wd_ref, x_ref, gate_ref, o_ref, wb_ref):
    @pl.when(pl.program_id(1) == 0)
    def _():
        wb_ref[0:SSD_WIDTH] = ws_ref[...].astype(BF16)
        wb_ref[SSD_WIDTH:SSD_WIDTH + MOBA_WIDTH] = wm_ref[...].astype(BF16)
        wb_ref[SSD_WIDTH + MOBA_WIDTH:] = wd_ref[...].astype(BF16)

    acc = jnp.dot(ys_ref[...], wb_ref[0:SSD_WIDTH], preferred_element_type=F32)
    acc += jnp.dot(ym_ref[...], wb_ref[SSD_WIDTH:SSD_WIDTH + MOBA_WIDTH], preferred_element_type=F32)
    acc += jnp.dot(yd_ref[...], wb_ref[SSD_WIDTH + MOBA_WIDTH:], preferred_element_type=F32)
    o_ref[...] = x_ref[...] + gate_ref[...] * acc


def out_projection(y_ssd, y_moba, y_diff, w_out, x, gate):
    L, d = x.shape
    tm, tn = 512, 1024
    return pl.pallas_call(
        _outproj_kernel,
        grid=(d // tn, L // tm),
        in_specs=[pl.BlockSpec((tm, SSD_WIDTH), lambda j, i: (i, 0)),
                  pl.BlockSpec((tm, MOBA_WIDTH), lambda j, i: (i, 0)),
                  pl.BlockSpec((tm, DIFF_WIDTH), lambda j, i: (i, 0)),
                  pl.BlockSpec((SSD_WIDTH, tn), lambda j, i: (0, j)),
                  pl.BlockSpec((MOBA_WIDTH, tn), lambda j, i: (SSD_WIDTH // MOBA_WIDTH, j)),
                  pl.BlockSpec((DIFF_WIDTH, tn), lambda j, i: ((SSD_WIDTH + MOBA_WIDTH) // DIFF_WIDTH, j)),
                  pl.BlockSpec((tm, tn), lambda j, i: (i, j)),
                  pl.BlockSpec((1, tn), lambda j, i: (0, j))],
        out_specs=pl.BlockSpec((tm, tn), lambda j, i: (i, j)),
        out_shape=jax.ShapeDtypeStruct((L, d), F32),
        scratch_shapes=[pltpu.VMEM((w_out.shape[0], tn), BF16)],
        compiler_params=_params("arbitrary", "arbitrary"),
        name="out_projection",
    )(y_ssd, y_moba, y_diff, w_out, w_out, w_out, x, gate)


CAST_COLS = 256
TILE_VALID = 1
TILE_NEW_WEIGHTS = 2


def weight_plan(tile_expert, tile_valid):
    prev = jnp.concatenate([jnp.full((1,), -1, jnp.int32), tile_expert[:-1]])
    first = tile_expert != prev
    ordinal = jnp.cumsum(first.astype(jnp.int32)) - 1
    n_blocks = ordinal[-1] + 1
    block_expert = jnp.zeros_like(tile_expert).at[ordinal].set(tile_expert)
    next_expert = block_expert[(ordinal + 1) % n_blocks]
    flags = tile_valid * TILE_VALID + first.astype(jnp.int32) * TILE_NEW_WEIGHTS
    return tile_expert, flags, ordinal, next_expert, n_blocks.reshape(1)


def _grouped_matmul_kernel(te_ref, tf_ref, to_ref, tx_ref, nb_ref, a_ref, *rest, n_mats, residual):
    w_hbm, rest = rest[:n_mats], rest[n_mats:]
    if residual:
        x_ref, gate_ref, o_ref, wf_ref, wb_ref, sem = rest
    else:
        o_ref, wf_ref, wb_ref, sem = rest
    j = pl.program_id(0)
    t = pl.program_id(1)
    tn = o_ref.shape[1]
    cw = CAST_COLS
    flags = tf_ref[t]
    block = j * nb_ref[0] + to_ref[t]
    slot = block % 2

    def copies(expert, col_tile, dst_slot):
        cols = pl.ds(pl.multiple_of(col_tile * tn, tn), tn)
        return [pltpu.make_async_copy(w.at[expert, :, cols], wf_ref.at[dst_slot, m], sem.at[dst_slot, m])
                for m, w in enumerate(w_hbm)]

    @pl.when((flags & TILE_NEW_WEIGHTS) != 0)
    def _():
        @pl.when(block == 0)
        def _():
            for c in copies(te_ref[t], j, slot):
                c.start()

        wraps = to_ref[t] == nb_ref[0] - 1

        @pl.when(jnp.logical_not(wraps & (j == pl.num_programs(0) - 1)))
        def _():
            for c in copies(tx_ref[t], j + wraps.astype(jnp.int32), 1 - slot):
                c.start()

        for c in copies(te_ref[t], j, slot):
            c.wait()

    def epilogue(accs, cols):
        if n_mats == 2:
            y = _silu(accs[0]) * accs[1]
        else:
            y = accs[0]
            if residual:
                y = x_ref[:, cols] + gate_ref[:, cols] * y
        o_ref[:, cols] = y.astype(o_ref.dtype)

    @pl.when(flags == TILE_VALID + TILE_NEW_WEIGHTS)
    def _():
        a = a_ref[...]
        for c in range(0, tn, cw):
            accs = []
            for m in range(n_mats):
                wb = wf_ref[slot, m, :, c:c + cw].astype(BF16)
                wb_ref[m, :, c:c + cw] = wb
                accs.append(jnp.dot(a, wb, preferred_element_type=F32))
            epilogue(accs, slice(c, c + cw))

    @pl.when(flags == TILE_VALID)
    def _():
        a = a_ref[...]
        epilogue([jnp.dot(a, wb_ref[m], preferred_element_type=F32) for m in range(n_mats)], slice(0, tn))

    @pl.when((flags & TILE_VALID) == 0)
    def _():
        o_ref[...] = jnp.zeros_like(o_ref)


def grouped_matmul(a, weights, plan, tm, tn, out_dtype, x=None, gate=None, name="grouped_matmul"):
    rows, k = a.shape
    n = weights[0].shape[2]
    n_mats = len(weights)
    residual = x is not None
    idx = lambda f: (lambda j, t, *refs: f(j, t))
    in_specs = [pl.BlockSpec((tm, k), idx(lambda j, t: (t, 0)))] + [pl.BlockSpec(memory_space=pl.ANY)] * n_mats
    args = [a, *weights]
    if residual:
        in_specs += [pl.BlockSpec((tm, tn), idx(lambda j, t: (t, j))), pl.BlockSpec((1, tn), idx(lambda j, t: (0, j)))]
        args += [x, gate]
    grid_spec = pltpu.PrefetchScalarGridSpec(
        num_scalar_prefetch=5,
        grid=(n // tn, rows // tm),
        in_specs=in_specs,
        out_specs=pl.BlockSpec((tm, tn), idx(lambda j, t: (t, j))),
        scratch_shapes=[pltpu.VMEM((2, n_mats, k, tn), F32), pltpu.VMEM((n_mats, k, tn), BF16),
                        pltpu.SemaphoreType.DMA((2, n_mats))],
    )
    return pl.pallas_call(
        functools.partial(_grouped_matmul_kernel, n_mats=n_mats, residual=residual),
        grid_spec=grid_spec,
        out_shape=jax.ShapeDtypeStruct((rows, n), out_dtype),
        compiler_params=_params("arbitrary", "arbitrary"),
        name=name,
    )(*plan, *args)


def _causal_conv(cur, tail_ref, w_ref, b_ref):
    t = cur.shape[0]
    tail = tail_ref[...]
    w = w_ref[...]
    row8 = lax.broadcasted_iota(jnp.int32, (SUBLANES, cur.shape[1]), 0)
    acc = cur * w[SSD_CONV - 1:SSD_CONV]
    top = cur[0:SUBLANES] * w[SSD_CONV - 1:SSD_CONV]
    for s in range(1, SSD_CONV):
        wk = w[SSD_CONV - 1 - s:SSD_CONV - s]
        rolled = pltpu.roll(cur, s, axis=0)
        acc += rolled * wk
        top += jnp.where(row8 < s, pltpu.roll(tail, s, axis=0), rolled[0:SUBLANES]) * wk
    tail_ref[...] = cur[t - SUBLANES:t]
    return jnp.concatenate([top, acc[SUBLANES:]], axis=0) + b_ref[...]


def _bf16_terms(x):
    terms = []
    for _ in range(3):
        part = x.astype(BF16)
        terms.append(part)
        x = x - part.astype(F32)
    return terms


def _dot_exact_rhs01(x, onehot_bf16):
    return sum(jnp.dot(part, onehot_bf16, preferred_element_type=F32) for part in _bf16_terms(x))


def _dot_exact_lhs01(onehot_bf16, x):
    return sum(jnp.dot(onehot_bf16, part, preferred_element_type=F32) for part in _bf16_terms(x))


def _ssd_kernel(z_ref, x_ref, bc_ref, dt_ref, cwx_ref, cwb_ref, cbx_ref, cbb_ref, dtb_ref, alog_ref,
                dskip_ref, ng_ref, expand_ref, o_ref, tailx_ref, tailb_ref, state_ref, ybuf_ref):
    t = SSD_CHUNK
    hg = SSD_HEADS // SSD_GROUPS
    gw = SSD_WIDTH // SSD_GROUPS

    @pl.when(pl.program_id(0) == 0)
    def _():
        tailx_ref[...] = jnp.zeros_like(tailx_ref)
        tailb_ref[...] = jnp.zeros_like(tailb_ref)
        state_ref[...] = jnp.zeros_like(state_ref)

    xs = _silu(_causal_conv(x_ref[...], tailx_ref, cwx_ref, cbx_ref))
    bcm = _silu(_causal_conv(bc_ref[...], tailb_ref, cwb_ref, cbb_ref))
    dt = _softplus(dt_ref[...] + dtb_ref[...])
    a = -jnp.exp(alog_ref[...])
    row = lax.broadcasted_iota(jnp.int32, (t, t), 0)
    col = lax.broadcasted_iota(jnp.int32, (t, t), 1)
    tril = row >= col
    a_cs = _dot_exact_lhs01(jnp.where(tril, 1.0, 0.0).astype(BF16), dt * a)
    a_last = a_cs[t - 1:t]
    per_head = jnp.concatenate(
        [dt, jnp.exp(a_last - a_cs), jnp.exp(a_cs), jnp.broadcast_to(jnp.exp(a_last), (SUBLANES, LANES))], axis=0)
    spread = _dot_exact_rhs01(per_head, expand_ref[...])
    dt_x, to_end_x, ea_x, cd_x = spread[0:t], spread[t:2 * t], spread[2 * t:3 * t], spread[3 * t:3 * t + 1]
    xdt = xs * dt_x
    xdt_b = xdt.astype(BF16)
    xw_b = (xdt * to_end_x).astype(BF16)
    a_cs_t = a_cs.T

    y_off = []
    for g in range(SSD_GROUPS):
        bm = bcm[:, g * SSD_STATE:(g + 1) * SSD_STATE]
        cm_b = bcm[:, (SSD_GROUPS + g) * SSD_STATE:(SSD_GROUPS + g + 1) * SSD_STATE].astype(BF16)
        cb = lax.dot_general(cm_b, bm.astype(BF16), (((1,), (1,)), ((), ())), preferred_element_type=F32)
        h_prev = state_ref[g]
        y_off.append(jnp.dot(cm_b, h_prev.astype(BF16), preferred_element_type=F32)
                     * ea_x[:, g * gw:(g + 1) * gw])
        st_new = jnp.dot(bm.T.astype(BF16), xw_b[:, g * gw:(g + 1) * gw], preferred_element_type=F32)
        state_ref[g] = h_prev * cd_x[:, g * gw:(g + 1) * gw] + st_new
        for r in range(0, hg, 2):
            pair = []
            for h in (g * hg + r, g * hg + r + 1):
                diff = a_cs[:, h:h + 1] - a_cs_t[h:h + 1, :]
                m = (cb * jnp.exp(jnp.where(tril, diff, -jnp.inf))).astype(BF16)
                pair.append(jnp.dot(m, xdt_b[:, h * SSD_HEAD_DIM:(h + 1) * SSD_HEAD_DIM],
                                    preferred_element_type=F32))
            lo = (g * hg + r) * SSD_HEAD_DIM
            ybuf_ref[:, lo:lo + 2 * SSD_HEAD_DIM] = jnp.concatenate(pair, axis=1)

    y = ybuf_ref[...] + jnp.concatenate(y_off, axis=1) + xs * dskip_ref[...]
    y = y * _silu(z_ref[...])
    outs = []
    for g in range(SSD_GROUPS):
        yg = y[:, g * gw:(g + 1) * gw]
        outs.append(yg * lax.rsqrt(jnp.mean(yg * yg, axis=-1, keepdims=True) + EPS))
    o_ref[...] = (jnp.concatenate(outs, axis=1) * ng_ref[...]).astype(o_ref.dtype)


def ssd_mixer(proj, conv_w, conv_b, dt_bias, a_log, d_skip, norm_g):
    L = proj.shape[0]
    t = SSD_CHUNK
    assert L % t == 0

    def pad_lanes(v):
        return jnp.pad(v, (0, LANES - v.shape[0])).reshape(1, LANES)

    expand = (jnp.arange(SSD_WIDTH)[None, :] // SSD_HEAD_DIM == jnp.arange(LANES)[:, None]).astype(BF16)
    full = lambda shape: pl.BlockSpec(shape, lambda c: (0,) * len(shape))
    return pl.pallas_call(
        _ssd_kernel,
        grid=(L // t,),
        in_specs=[pl.BlockSpec((t, SSD_WIDTH), lambda c: (c, COL_Z // SSD_WIDTH)),
                  pl.BlockSpec((t, SSD_WIDTH), lambda c: (c, COL_X // SSD_WIDTH)),
                  pl.BlockSpec((t, SSD_BC), lambda c: (c, COL_BC // SSD_BC)),
                  pl.BlockSpec((t, LANES), lambda c: (c, COL_DT // LANES)),
                  full((SSD_CONV, SSD_WIDTH)), full((SSD_CONV, SSD_BC)),
                  full((1, SSD_WIDTH)), full((1, SSD_BC)),
                  full((1, LANES)), full((1, LANES)), full((1, SSD_WIDTH)), full((1, SSD_WIDTH)),
                  full((LANES, SSD_WIDTH))],
        out_specs=pl.BlockSpec((t, SSD_WIDTH), lambda c: (c, 0)),
        out_shape=jax.ShapeDtypeStruct((L, SSD_WIDTH), BF16),
        scratch_shapes=[pltpu.VMEM((SUBLANES, SSD_WIDTH), F32), pltpu.VMEM((SUBLANES, SSD_BC), F32),
                        pltpu.VMEM((SSD_GROUPS, SSD_STATE, SSD_WIDTH // SSD_GROUPS), F32),
                        pltpu.VMEM((t, SSD_WIDTH), F32)],
        compiler_params=_params("arbitrary"),
        name="ssd_mixer",
    )(proj, proj, proj, proj,
      conv_w[:, :SSD_WIDTH], conv_w[:, SSD_WIDTH:], conv_b[:SSD_WIDTH].reshape(1, -1),
      conv_b[SSD_WIDTH:].reshape(1, -1), pad_lanes(dt_bias), pad_lanes(a_log),
      jnp.repeat(d_skip, SSD_HEAD_DIM).reshape(1, -1), norm_g.reshape(1, -1), expand)


def _moba_prep_kernel(q_ref, k_ref, v_ref, qa_ref, ka_ref, va_ref, kmean_ref):
    own = pl.program_id(0)
    t = MOBA_BLOCK
    dh = MOBA_HEAD_DIM
    nbl = MOBA_MAX_BLOCKS

    @pl.when(own == 0)
    def _():
        kmean_ref[...] = jnp.zeros_like(kmean_ref)

    lane = lax.broadcasted_iota(jnp.int32, (t, LANES), 1)
    onehot = jnp.where(lane == dh + own, 1.0, 0.0)
    blk = lax.broadcasted_iota(jnp.int32, (nbl, t), 0)
    q_t = q_ref[...].T
    k = k_ref[...]
    k_mean = jnp.mean(k, axis=0, keepdims=True)
    va_ref[:, 0] = v_ref[...].T.reshape(MOBA_HEADS, dh, t).astype(BF16)
    for h in range(MOBA_HEADS):
        qh_t = q_t[h * dh:(h + 1) * dh]
        gate = jnp.dot(kmean_ref[h], qh_t, precision=HIGHEST, preferred_element_type=F32)
        gate = jnp.where(blk < own, gate, -jnp.inf)
        sel = blk >= own
        for _ in range(MOBA_TOPK):
            m = jnp.max(gate, axis=0, keepdims=True)
            idx = jnp.min(jnp.where(gate == m, blk, nbl), axis=0, keepdims=True)
            sel = sel | ((blk == idx) & (m > -jnp.inf))
            gate = jnp.where(blk == idx, -jnp.inf, gate)
        offs_t = jnp.where(sel, 0.0, NEG_BIG)
        qa_ref[h] = jnp.concatenate([qh_t * MOBA_Q_SCALE, offs_t, jnp.zeros((LANES - dh - nbl, t), F32)],
                                    axis=0).astype(BF16)
        pair = k[:, (h // 2) * LANES:(h // 2 + 1) * LANES]
        if h % 2:
            pair = pltpu.roll(pair, dh, axis=1)
        ka_ref[h] = jnp.where(lane < dh, pair, onehot).astype(BF16)
        kmean_ref[h, pl.ds(own, 1), :] = k_mean[:, h * dh:(h + 1) * dh]


def moba_prep(proj):
    L = proj.shape[0]
    t = MOBA_BLOCK
    assert L % t == 0 and L // t <= MOBA_MAX_BLOCKS
    return pl.pallas_call(
        _moba_prep_kernel,
        grid=(L // t,),
        in_specs=[pl.BlockSpec((t, MOBA_WIDTH), lambda i: (i, COL_MQ // MOBA_WIDTH)),
                  pl.BlockSpec((t, MOBA_WIDTH), lambda i: (i, COL_MK // MOBA_WIDTH)),
                  pl.BlockSpec((t, MOBA_WIDTH), lambda i: (i, COL_MV // MOBA_WIDTH))],
        out_specs=[pl.BlockSpec((MOBA_HEADS, LANES, t), lambda i: (0, 0, i)),
                   pl.BlockSpec((MOBA_HEADS, t, LANES), lambda i: (0, i, 0)),
                   pl.BlockSpec((MOBA_HEADS, 1, MOBA_HEAD_DIM, t), lambda i: (0, i, 0, 0))],
        out_shape=[jax.ShapeDtypeStruct((MOBA_HEADS, LANES, L), BF16),
                   jax.ShapeDtypeStruct((MOBA_HEADS, L, LANES), BF16),
                   jax.ShapeDtypeStruct((MOBA_HEADS, L // t, MOBA_HEAD_DIM, t), BF16)],
        scratch_shapes=[pltpu.VMEM((MOBA_HEADS, MOBA_MAX_BLOCKS, MOBA_HEAD_DIM), F32)],
        compiler_params=_params("arbitrary"),
        name="moba_prep",
    )(proj, proj, proj)


def _col_reduce(x, op):
    while x.shape[0] > SUBLANES:
        half = x.shape[0] // 2
        x = op(x[:half], x[half:])
    return jnp.max(x, axis=0, keepdims=True) if op is jnp.maximum else jnp.sum(x, axis=0, keepdims=True)


def _attention_kernel(near_ref, q_ref, k_ref, v_ref, posr_ref, posc_ref, tbl_ref, g_ref, lam_ref, o_ref,
                      m_ref, l_ref, al_ref, acc_ref, sa_ref, sb_ref, pa_ref, pb_ref, vt_ref, *, moba, lambda_init):
    hp = pl.program_id(0)
    qi = pl.program_id(1)
    nq = pl.num_programs(1)
    t = ATT_T
    m_ref[...] = jnp.full_like(m_ref, -jnp.inf)
    l_ref[...] = jnp.zeros_like(l_ref)
    acc_ref[...] = jnp.zeros_like(acc_ref)
    if moba:
        q_t = [q_ref[0], q_ref[1]]
        heads = [2 * hp, 2 * hp + 1]
    else:
        q = (q_ref[...].astype(F32) * DIFF_Q_SCALE).T.astype(BF16)
        half = lax.broadcasted_iota(jnp.int32, q.shape, 0) < DIFF_QK_DIM
        q_t = [jnp.where(half, q, jnp.zeros_like(q)), jnp.where(half, jnp.zeros_like(q), q)]
        heads = [MOBA_HEADS + hp]

        @pl.when(qi == 0)
        def _():
            def transpose_tile(i, carry):
                vt_ref[i] = v_ref[pl.ds(pl.multiple_of(i * t, t), t), :].astype(F32).T.astype(BF16)
                return carry

            lax.fori_loop(0, nq, transpose_tile, 0)
    trow = [tbl_ref[pl.ds(hd, 1), :] for hd in heads]
    posq = posr_ref[pl.ds(qi, 1), :]

    def scores(ki, dst_ref):
        rows = pl.ds(pl.multiple_of(ki * t, t), t)
        for s in range(2):
            k = k_ref[s, rows, :] if moba else k_ref[rows, :]
            dst_ref[s] = jnp.dot(k, q_t[s], preferred_element_type=F32)

    def accumulate(ki, p_ref):
        for s in range(2):
            v_t = v_ref[s, ki] if moba else vt_ref[ki]
            pv = jnp.dot(v_t, p_ref[s], preferred_element_type=F32)
            acc_ref[s] = al_ref[s] * acc_ref[s] + pv

    scores(0, sa_ref)
    pb_ref[...] = jnp.zeros_like(pb_ref)
    al_ref[...] = jnp.ones_like(al_ref)

    def tile(ki, general, cur_ref, nxt_ref, p_ref, p_prev_ref):
        scores(jnp.minimum(ki + 1, qi), nxt_ref)
        accumulate(jnp.maximum(ki - 1, 0), p_prev_ref)
        ch = ATT_CHUNK
        mx = [None, None]
        for r in range(0, t, ch):
            if general:
                dist = jnp.clip(posq - posc_ref[ki, r:r + ch, :], 0, LANES - 1)
                bias = [jnp.concatenate(
                    [jnp.take_along_axis(jnp.broadcast_to(tr, (ch, LANES)), dist[:, j * LANES:(j + 1) * LANES],
                                         axis=1) for j in range(t // LANES)], axis=1) for tr in trow]
                key = lax.broadcasted_iota(jnp.int32, (ch, t), 0) + r
                qry = lax.broadcasted_iota(jnp.int32, (ch, t), 1)
                causal = key + ki * t <= qry + qi * t
            for s in range(2):
                blk = cur_ref[s, r:r + ch, :]
                if general:
                    blk = jnp.where(causal, blk + bias[s if moba else 0], -jnp.inf)
                    cur_ref[s, r:r + ch, :] = blk
                mx[s] = blk if mx[s] is None else jnp.maximum(mx[s], blk)
        for s in range(2):
            m_old = m_ref[s]
            m_tile = _col_reduce(mx[s], jnp.maximum)
            if general:
                m_new = jnp.maximum(m_old, m_tile)
                shift = m_new
            else:
                c = trow[s if moba else 0][:, LANES - 1:LANES]
                m_new = jnp.maximum(m_old, m_tile + c)
                shift = m_new - c
            alpha = jnp.exp2(m_old - m_new)
            sm = None
            for r in range(0, t, ch):
                p = jnp.exp2(cur_ref[s, r:r + ch, :] - shift)
                p_ref[s, r:r + ch, :] = p.astype(BF16)
                sm = p if sm is None else sm + p
            l_ref[s] = alpha * l_ref[s] + _col_reduce(sm, jnp.add)
            al_ref[s] = alpha
            m_ref[s] = m_new

    def step(ki, *bufs):
        flag = near_ref[qi * nq + ki]

        @pl.when(flag == 0)
        def _():
            tile(ki, False, *bufs)

        @pl.when(flag != 0)
        def _():
            tile(ki, True, *bufs)

    def pair(j, carry):
        step(2 * j, sa_ref, sb_ref, pa_ref, pb_ref)
        step(2 * j + 1, sb_ref, sa_ref, pb_ref, pa_ref)
        return carry

    lax.fori_loop(0, (qi + 1) // 2, pair, 0)

    @pl.when(qi % 2 == 0)
    def _():
        step(qi, sa_ref, sb_ref, pa_ref, pb_ref)
        accumulate(qi, pa_ref)

    @pl.when(qi % 2 == 1)
    def _():
        accumulate(qi, pb_ref)

    if moba:
        outs = []
        for s in range(2):
            o = acc_ref[s] / l_ref[s]
            ms = jnp.sum(o * o, axis=0, keepdims=True) * (1.0 / MOBA_HEAD_DIM)
            outs.append(o * lax.rsqrt(ms + EPS) * g_ref[s])
        y_t = jnp.concatenate(outs, axis=0)
    else:
        lp = lam_ref[...]
        lam = (jnp.exp(jnp.sum(lp[0:1] * lp[1:2], axis=1, keepdims=True))
               - jnp.exp(jnp.sum(lp[2:3] * lp[3:4], axis=1, keepdims=True)) + lambda_init)
        o = acc_ref[0] / l_ref[0] - lam * (acc_ref[1] / l_ref[1])
        ms = jnp.sum(o * o, axis=0, keepdims=True) * (1.0 / DIFF_V_DIM)
        y_t = (o * lax.rsqrt(ms + EPS) * g_ref[0]) * (1.0 - lambda_init)
    o_ref[...] = y_t.T.astype(o_ref.dtype)


def _attention_call(kernel, steps, dv, L, near, in_specs, args, name, vt_scratch=False):
    t = ATT_T
    full = lambda a: pl.BlockSpec(a.shape, lambda h, i, nr: (0,) * a.ndim)
    grid_spec = pltpu.PrefetchScalarGridSpec(
        num_scalar_prefetch=1,
        grid=(steps, L // t),
        in_specs=in_specs + [full(a) for a in args[len(in_specs):]],
        out_specs=pl.BlockSpec((t, LANES), lambda h, i, nr: (i, h)),
        scratch_shapes=[pltpu.VMEM((2, 1, t), F32), pltpu.VMEM((2, 1, t), F32), pltpu.VMEM((2, 1, t), F32),
                        pltpu.VMEM((2, dv, t), F32), pltpu.VMEM((2, t, t), F32), pltpu.VMEM((2, t, t), F32),
                        pltpu.VMEM((2, t, t), BF16), pltpu.VMEM((2, t, t), BF16),
                        pltpu.VMEM((L // t if vt_scratch else 1, dv, t), BF16)],
    )
    return pl.pallas_call(kernel, grid_spec=grid_spec, out_shape=jax.ShapeDtypeStruct((L, steps * LANES), BF16),
                          compiler_params=_params("parallel", "arbitrary"), name=name)(near, *args)


def moba_attention(qa_t, ka, va_t, near, pos_rows, pos_cols, table, norm_g):
    heads, L, _ = ka.shape
    t = ATT_T
    kernel = functools.partial(_attention_kernel, moba=True, lambda_init=None)
    g = norm_g.reshape(heads, MOBA_HEAD_DIM, 1)
    return _attention_call(
        kernel, heads // 2, MOBA_HEAD_DIM, L, near,
        [pl.BlockSpec((2, LANES, t), lambda h, i, nr: (h, 0, i)),
         pl.BlockSpec((2, L, LANES), lambda h, i, nr: (h, 0, 0)),
         pl.BlockSpec((2, L // t, MOBA_HEAD_DIM, t), lambda h, i, nr: (h, 0, 0, 0)),
         pl.BlockSpec(pos_rows.shape, lambda h, i, nr: (0, 0)),
         pl.BlockSpec(pos_cols.shape, lambda h, i, nr: (0, 0, 0)),
         pl.BlockSpec(table.shape, lambda h, i, nr: (0, 0)),
         pl.BlockSpec((2, MOBA_HEAD_DIM, 1), lambda h, i, nr: (h, 0, 0))],
        [qa_t, ka, va_t, pos_rows, pos_cols, table, g, jnp.zeros((4, DIFF_QK_DIM), F32)], "moba_attention")


def diff_attention(proj16, near, pos_rows, pos_cols, table, lam_params, subln_g, lambda_init):
    L = proj16.shape[0]
    t = ATT_T
    kernel = functools.partial(_attention_kernel, moba=False, lambda_init=lambda_init)
    return _attention_call(
        kernel, DIFF_HEADS, DIFF_V_DIM, L, near,
        [pl.BlockSpec((t, LANES), lambda h, i, nr: (i, COL_DQ // LANES + h)),
         pl.BlockSpec((L, LANES), lambda h, i, nr: (0, COL_DK // LANES + h)),
         pl.BlockSpec((L, LANES), lambda h, i, nr: (0, COL_DV // LANES + h))],
        [proj16, proj16, proj16, pos_rows, pos_cols, table, subln_g.reshape(1, DIFF_V_DIM, 1), lam_params],
        "diff_attention", vt_scratch=True)


def _rel_bucket(dist):
    n = jnp.maximum(dist, 0)
    max_exact = REL_BUCKETS // 2
    nf = jnp.maximum(n, 1).astype(F32)
    large = max_exact + (jnp.log(nf / max_exact) / math.log(REL_MAX_DIST / max_exact)
                         * (REL_BUCKETS - max_exact)).astype(jnp.int32)
    return jnp.where(n < max_exact, n, jnp.minimum(large, REL_BUCKETS - 1))


def attention_tables(positions, rel_bias):
    L = positions.shape[0]
    t = ATT_T
    buckets = _rel_bucket(jnp.arange(LANES, dtype=jnp.int32))
    table = rel_bias[buckets].T * LOG2E
    pos_rows = positions.reshape(L // t, t)
    lo, hi = jnp.min(pos_rows, axis=1), jnp.max(pos_rows, axis=1)
    near = (lo[:, None] - hi[None, :] < LANES) | jnp.eye(L // t, dtype=bool)
    return table, pos_rows, positions.reshape(L // t, t, 1), near.astype(jnp.int32).reshape(-1)


def _router_kernel(x_ref, g_ref, sc_ref, sh_ref, rw_ref, h_ref, comb_ref, rank_ref, cum_ref, total_ref, cnt_ref):
    t = TOK_T

    @pl.when(pl.program_id(0) == 0)
    def _():
        cnt_ref[...] = jnp.zeros_like(cnt_ref)

    h = _norm_mod(x_ref[...], g_ref[...], sc_ref[...], sh_ref[...])
    h_ref[...] = h.astype(BF16)
    lane = lax.broadcasted_iota(jnp.int32, (t, LANES), 1)
    logits = jnp.dot(h, rw_ref[...], precision=HIGHEST, preferred_element_type=F32)
    logits = jnp.where(lane < N_EXPERTS, logits, -jnp.inf)
    m1 = jnp.max(logits, axis=1, keepdims=True)
    i1 = jnp.min(jnp.where(logits == m1, lane, LANES), axis=1, keepdims=True)
    rest = jnp.where(lane == i1, -jnp.inf, logits)
    m2 = jnp.max(rest, axis=1, keepdims=True)
    i2 = jnp.min(jnp.where(rest == m2, lane, LANES), axis=1, keepdims=True)
    e2 = jnp.exp(m2 - m1)
    denom = 1.0 + e2
    comb_ref[...] = jnp.where(lane == i1, 1.0 / denom, 0.0) + jnp.where(lane == i2, e2 / denom, 0.0)
    sel = jnp.where((lane == i1) | (lane == i2), 1.0, 0.0)
    row = lax.broadcasted_iota(jnp.int32, (t, t), 0)
    col = lax.broadcasted_iota(jnp.int32, (t, t), 1)
    before = jnp.dot(jnp.where(row > col, 1.0, 0.0).astype(BF16), sel.astype(BF16), preferred_element_type=F32)
    carry = cnt_ref[...]
    cum_ref[0] = carry
    rank_ref[...] = jnp.where(sel > 0.0, before + carry, -1.0)
    carry = carry + jnp.sum(sel, axis=0, keepdims=True)
    cnt_ref[...] = carry
    total_ref[...] = carry


def moe_router(x, g, sc, sh, router_w):
    L, d = x.shape
    t = TOK_T
    rw = jnp.pad(router_w, ((0, 0), (0, LANES - N_EXPERTS)))
    vec = pl.BlockSpec((1, d), lambda i: (0, 0))
    tok = pl.BlockSpec((t, LANES), lambda i: (i, 0))
    return pl.pallas_call(
        _router_kernel,
        grid=(L // t,),
        in_specs=[pl.BlockSpec((t, d), lambda i: (i, 0)), vec, vec, vec,
                  pl.BlockSpec((d, LANES), lambda i: (0, 0))],
        out_specs=[pl.BlockSpec((t, d), lambda i: (i, 0)), tok, tok,
                   pl.BlockSpec((1, 1, LANES), lambda i: (i, 0, 0)),
                   pl.BlockSpec((1, LANES), lambda i: (0, 0))],
        out_shape=[jax.ShapeDtypeStruct((L, d), BF16), jax.ShapeDtypeStruct((L, LANES), F32),
                   jax.ShapeDtypeStruct((L, LANES), F32), jax.ShapeDtypeStruct((L // t, 1, LANES), F32),
                   jax.ShapeDtypeStruct((1, LANES), F32)],
        scratch_shapes=[pltpu.VMEM((1, LANES), F32)],
        compiler_params=_params("arbitrary"),
        name="moe_router",
    )(x, g, sc, sh, rw)


def _item_lists(hit, n_items):
    rows, cols = hit.shape
    n_real = jnp.sum(hit.astype(jnp.int32))
    idx = jnp.nonzero(hit.reshape(-1), size=n_items, fill_value=0)[0].astype(jnp.int32)
    k = jnp.arange(n_items, dtype=jnp.int32)
    real = k < n_real
    idx = jnp.where(real, idx, idx[jnp.maximum(n_real - 1, 0)])
    r, c = idx // cols, idx % cols
    prev_r = jnp.concatenate([jnp.full((1,), -1, jnp.int32), r[:-1]])
    next_r = jnp.concatenate([r[1:], jnp.full((1,), -1, jnp.int32)])
    first = real & (r != prev_r)
    last = real & ((r != next_r) | (k == n_real - 1))
    flags = real.astype(jnp.int32) + 2 * first.astype(jnp.int32) + 4 * last.astype(jnp.int32)
    return r, c, flags


def moe_plan(rank, cum, total, L):
    tm, tb = MOE_TM, TOK_T
    nb = L // tb
    n_tiles = 2 * L // tm + N_EXPERTS
    counts = total[0, :N_EXPERTS].astype(jnp.int32)
    tiles_e = (counts + tm - 1) // tm
    tile_end = jnp.cumsum(tiles_e)
    tile_start = tile_end - tiles_e
    tid = jnp.arange(n_tiles, dtype=jnp.int32)
    tile_valid = tid < tile_end[-1]
    tile_expert = jnp.minimum(jnp.searchsorted(tile_end, tid, side="right"), N_EXPERTS - 1).astype(jnp.int32)
    local_row = (tid - tile_start[tile_expert]) * tm
    r = rank[:, :N_EXPERTS].astype(jnp.int32)
    dest = jnp.where(r >= 0, r + (tile_start * tm)[None, :], -1)
    cum_i = cum[:, 0, :N_EXPERTS].astype(jnp.int32)
    cum_next = jnp.concatenate([cum_i[1:], counts[None, :]], axis=0)
    lo = cum_i[:, tile_expert].T
    hi = cum_next[:, tile_expert].T
    hit = tile_valid[:, None] & (lo < (local_row + tm)[:, None]) & (hi > local_row[:, None])
    n_items = n_tiles + N_EXPERTS * nb
    pad_hit = hit | ((~tile_valid)[:, None] & (jnp.arange(nb) == 0)[None, :])
    gather_items = _item_lists(pad_hit, n_items)
    kb, tt, fl = _item_lists(hit.T, n_items)
    return dict(tile_expert=tile_expert, tile_valid=tile_valid.astype(jnp.int32),
                dest_rows=dest.T, segment_start=(tile_start * tm).astype(jnp.int32),
                gather_items=gather_items, combine_items=(tt, kb, fl), n_tiles=n_tiles, n_items=n_items)


def _moe_gather_kernel(it_ref, ib_ref, if_ref, te_ref, dest_ref, h_ref, o_ref, acc_ref):
    i = pl.program_id(0)
    flag = if_ref[i]
    tile = it_ref[i]
    tm = MOE_TM

    @pl.when((flag & 2) != 0)
    def _():
        acc_ref[...] = jnp.zeros_like(acc_ref)

    @pl.when((flag & 1) != 0)
    def _():
        d = dest_ref[pl.ds(te_ref[tile], 1), :]
        rows = tile * tm + lax.broadcasted_iota(jnp.int32, (tm, 1), 0)
        onehot = jnp.where(d == rows, 1.0, 0.0).astype(BF16)
        acc_ref[...] += jnp.dot(onehot, h_ref[...], preferred_element_type=F32)

    @pl.when((flag & 4) != 0)
    def _():
        o_ref[...] = acc_ref[...].astype(o_ref.dtype)


def moe_gather(h, plan):
    L, d = h.shape
    tm, tb = MOE_TM, TOK_T
    it, ib, fl = plan["gather_items"]
    grid_spec = pltpu.PrefetchScalarGridSpec(
        num_scalar_prefetch=4,
        grid=(plan["n_items"],),
        in_specs=[pl.BlockSpec((N_EXPERTS, tb), lambda i, it, ib, fl, te: (0, ib[i])),
                  pl.BlockSpec((tb, d), lambda i, it, ib, fl, te: (ib[i], 0))],
        out_specs=pl.BlockSpec((tm, d), lambda i, it, ib, fl, te: (it[i], 0)),
        scratch_shapes=[pltpu.VMEM((tm, d), F32)],
    )
    return pl.pallas_call(
        _moe_gather_kernel,
        grid_spec=grid_spec,
        out_shape=jax.ShapeDtypeStruct((plan["n_tiles"] * tm, d), BF16),
        compiler_params=_params("arbitrary"),
        name="moe_gather",
    )(it, ib, fl, plan["tile_expert"], plan["dest_rows"], h)


def _moe_combine_kernel(it_ref, ib_ref, if_ref, te_ref, ts_ref, rank_ref, w_ref, ys_ref, x_ref, gate_ref, fg_ref,
                        o_ref, acc_ref, *, final_norm):
    i = pl.program_id(0)
    flag = if_ref[i]
    tile = it_ref[i]
    tm = MOE_TM

    @pl.when((flag & 2) != 0)
    def _():
        acc_ref[...] = jnp.zeros_like(acc_ref)

    @pl.when((flag & 1) != 0)
    def _():
        e = te_ref[tile]
        mine = lax.broadcasted_iota(jnp.int32, rank_ref.shape, 1) == e
        rank = jnp.sum(jnp.where(mine, rank_ref[...], 0.0), axis=1, keepdims=True)
        w = jnp.sum(jnp.where(mine, w_ref[...], 0.0), axis=1, keepdims=True)
        first = tile * tm - ts_ref[e]
        cols = (first + lax.broadcasted_iota(jnp.int32, (1, tm), 1)).astype(F32)
        onehot = jnp.where(rank == cols, 1.0, 0.0).astype(BF16)
        acc_ref[...] += w * jnp.dot(onehot, ys_ref[...], preferred_element_type=F32)

    @pl.when((flag & 4) != 0)
    def _():
        y = x_ref[...] + gate_ref[...] * acc_ref[...]
        if final_norm:
            y = y * lax.rsqrt(jnp.mean(y * y, axis=-1, keepdims=True) + EPS) * fg_ref[...]
        o_ref[...] = y


def moe_combine(ys, comb, rank, x, gate, plan, final_g=None):
    L, d = x.shape
    final_norm = final_g is not None
    if not final_norm:
        final_g = jnp.ones((1, d), F32)
    tm, tb = MOE_TM, TOK_T
    it, ib, fl = plan["combine_items"]
    tok = lambda i, it, ib, *_: (ib[i], 0)
    fixed = lambda i, *_: (0, 0)
    grid_spec = pltpu.PrefetchScalarGridSpec(
        num_scalar_prefetch=5,
        grid=(plan["n_items"],),
        in_specs=[pl.BlockSpec((tb, LANES), tok), pl.BlockSpec((tb, LANES), tok),
                  pl.BlockSpec((tm, d), lambda i, it, *_: (it[i], 0)),
                  pl.BlockSpec((tb, d), tok), pl.BlockSpec((1, d), fixed), pl.BlockSpec((1, d), fixed)],
        out_specs=pl.BlockSpec((tb, d), tok),
        scratch_shapes=[pltpu.VMEM((tb, d), F32)],
    )
    return pl.pallas_call(
        functools.partial(_moe_combine_kernel, final_norm=final_norm),
        grid_spec=grid_spec,
        out_shape=jax.ShapeDtypeStruct((L, d), F32),
        compiler_params=_params("arbitrary"),
        name="moe_combine",
    )(it, ib, fl, plan["tile_expert"], plan["segment_start"], rank, comb, ys, x, gate, final_g)


def moe_ffn(x, g, sc, sh, gate, router_w, w1, w3, w2, final_g=None):
    L = x.shape[0]
    h, comb, rank, cum, total = moe_router(x, g, sc, sh, router_w)
    plan = moe_plan(rank, cum, total, L)
    xs = moe_gather(h, plan)
    wplan = weight_plan(plan["tile_expert"], plan["tile_valid"])
    act = grouped_matmul(xs, [w1, w3], wplan, MOE_TM, 1024, BF16, name="swiglu_up")
    ys = grouped_matmul(act, [w2], wplan, MOE_TM, 512, BF16, name="swiglu_down")
    return moe_combine(ys, comb, rank, x, gate, plan, final_g)


def dense_ffn(x, g, sc, sh, gate, w1, w3, w2):
    L = x.shape[0]
    tm = 512
    h = norm_modulate(x, g, sc, sh, BF16)
    wplan = weight_plan(jnp.zeros((L // tm,), jnp.int32), jnp.ones((L // tm,), jnp.int32))
    act = grouped_matmul(h, [w1[None], w3[None]], wplan, tm, 512, BF16, name="swiglu_up")
    return grouped_matmul(act, [w2[None]], wplan, tm, 512, F32, x=x, gate=gate, name="swiglu_down")


def _permute_in_proj(w):
    d = w.shape[0]
    return jnp.concatenate([w[:, :ORIG_DT], w[:, ORIG_DT + SSD_HEADS:], w[:, ORIG_DT:ORIG_DT + SSD_HEADS],
                            jnp.zeros((d, LANES - SSD_HEADS), w.dtype)], axis=1).astype(BF16)


def kernel(x, c, positions, rel_bias, w_ada, b_ada, norm_mix_g, w_in, conv_w, conv_b, dt_bias, a_log, d_skip, ssd_norm_g, moba_norm_g, diff_lambda, diff_subln_g, w_out, norm_ffn_g, dense_w1, dense_w3, dense_w2, router_w, expert_w1, expert_w3, expert_w2, final_g):
    batch, L, d = x.shape
    assert batch == 1 and d == D_MODEL
    x = x[0]
    mod = ada_modulation(c, w_ada, b_ada)
    table, pos_rows, pos_cols, near = attention_tables(positions[0], rel_bias)
    row = lambda v: v.reshape(1, -1)
    for layer in range(DEPTH):
        lambda_init = 0.8 - 0.6 * math.exp(-0.3 * layer)
        shift1, scale1, gate1, shift2, scale2, gate2 = (mod[layer, :, j * d:(j + 1) * d] for j in range(6))
        h = norm_modulate(x, row(norm_mix_g[layer]), scale1, shift1, BF16)
        proj, proj16 = in_projection(h, _permute_in_proj(w_in[layer]))
        y_ssd = ssd_mixer(proj, conv_w[layer], conv_b[layer], dt_bias[layer], a_log[layer], d_skip[layer],
                          ssd_norm_g[layer])
        qa, ka, va = moba_prep(proj)
        y_moba = moba_attention(qa, ka, va, near, pos_rows, pos_cols, table, moba_norm_g[layer])
        y_diff = diff_attention(proj16, near, pos_rows, pos_cols, table, diff_lambda[layer],
                                diff_subln_g[layer], lambda_init)
        x = out_projection(y_ssd, y_moba, y_diff, w_out[layer], x, gate1)
        i = layer // 2
        g2 = row(norm_ffn_g[layer])
        if layer % 2 == 0:
            x = dense_ffn(x, g2, scale2, shift2, gate2, dense_w1[i], dense_w3[i], dense_w2[i])
        else:
            x = moe_ffn(x, g2, scale2, shift2, gate2, router_w[i], expert_w1[i], expert_w3[i], expert_w2[i],
                        final_g=row(final_g) if layer == DEPTH - 1 else None)
    if (DEPTH - 1) % 2 == 0:
        zero = jnp.zeros((1, d), F32)
        x = norm_modulate(x, row(final_g), zero, zero, F32)
    return x[None]
```

```python
import functools
import math

import jax
import jax.numpy as jnp
from jax import lax
from jax.experimental import pallas as pl
from jax.experimental.pallas import tpu as pltpu

F32 = jnp.float32
BF16 = jnp.bfloat16
HIGHEST = lax.Precision.HIGHEST

D_MODEL = 2048
DEPTH = 2
SSD_HEADS = 16
SSD_HEAD_DIM = 64
SSD_WIDTH = SSD_HEADS * SSD_HEAD_DIM
SSD_GROUPS = 2
SSD_STATE = 128
SSD_CONV = 4
SSD_CHUNK = 256
SSD_BC = 2 * SSD_GROUPS * SSD_STATE
MOBA_HEADS = 8
MOBA_HEAD_DIM = 64
MOBA_WIDTH = MOBA_HEADS * MOBA_HEAD_DIM
MOBA_BLOCK = 256
MOBA_TOPK = 3
DIFF_HEADS = 4
DIFF_QK_DIM = 64
DIFF_V_DIM = 128
DIFF_WIDTH = DIFF_HEADS * DIFF_V_DIM
REL_BUCKETS = 32
REL_MAX_DIST = 128
D_FF_DENSE = 5632
N_EXPERTS = 8
D_FF_EXPERT = 7168
EPS = 1e-6

LANES = 128
SUBLANES = 8
VMEM_LIMIT = 56 * 1024 * 1024

COL_Z = 0
COL_X = SSD_WIDTH
COL_BC = COL_X + SSD_WIDTH
COL_MQ = COL_BC + SSD_BC
COL_MK = COL_MQ + MOBA_WIDTH
COL_MV = COL_MK + MOBA_WIDTH
COL_DQ = COL_MV + MOBA_WIDTH
COL_DK = COL_DQ + DIFF_WIDTH
COL_DV = COL_DK + DIFF_WIDTH
COL_DT = COL_DV + DIFF_WIDTH
PROJ_W = COL_DT + LANES
ORIG_DT = SSD_WIDTH + SSD_WIDTH + SSD_BC

ATT_T = 256
ATT_CHUNK = 64
TOK_T = 512
MOE_TM = 256
NEG_BIG = -1e9
MOBA_MAX_BLOCKS = 32
LOG2E = math.log2(math.e)
MOBA_Q_SCALE = MOBA_HEAD_DIM ** -0.5 * LOG2E
DIFF_Q_SCALE = DIFF_QK_DIM ** -0.5 * LOG2E


def _silu(x):
    return x * (1.0 / (1.0 + jnp.exp(-x)))


def _softplus(x):
    return jnp.maximum(x, 0.0) + jnp.log1p(jnp.exp(-jnp.abs(x)))


def _params(*sem):
    return pltpu.CompilerParams(dimension_semantics=sem, vmem_limit_bytes=VMEM_LIMIT)


def _ada_kernel(c_ref, w_ref, b_ref, o_ref):
    ca = _silu(c_ref[...])
    o_ref[0] = jnp.sum(ca * w_ref[0], axis=0, keepdims=True) + b_ref[0]


def ada_modulation(c, w_ada, b_ada):
    depth, d, n = w_ada.shape
    tn = 2048
    return pl.pallas_call(
        _ada_kernel,
        grid=(depth, n // tn),
        in_specs=[pl.BlockSpec((d, 1), lambda l, j: (0, 0)),
                  pl.BlockSpec((1, d, tn), lambda l, j: (l, 0, j)),
                  pl.BlockSpec((1, 1, tn), lambda l, j: (l, 0, j))],
        out_specs=pl.BlockSpec((1, 1, tn), lambda l, j: (l, 0, j)),
        out_shape=jax.ShapeDtypeStruct((depth, 1, n), F32),
        compiler_params=_params("parallel", "parallel"),
        name="ada_modulation",
    )(c.reshape(d, 1), w_ada, b_ada.reshape(depth, 1, n))


def _norm_mod(x, g, sc, sh):
    ms = jnp.mean(x * x, axis=-1, keepdims=True)
    return (x * lax.rsqrt(ms + EPS) * g) * (1.0 + sc) + sh


def _norm_kernel(x_ref, g_ref, sc_ref, sh_ref, o_ref):
    o_ref[...] = _norm_mod(x_ref[...], g_ref[...], sc_ref[...], sh_ref[...]).astype(o_ref.dtype)


def norm_modulate(x, g, sc, sh, out_dtype):
    L, d = x.shape
    tm = 512
    vec = pl.BlockSpec((1, d), lambda i: (0, 0))
    return pl.pallas_call(
        _norm_kernel,
        grid=(L // tm,),
        in_specs=[pl.BlockSpec((tm, d), lambda i: (i, 0)), vec, vec, vec],
        out_specs=pl.BlockSpec((tm, d), lambda i: (i, 0)),
        out_shape=jax.ShapeDtypeStruct((L, d), out_dtype),
        compiler_params=_params("parallel"),
        name="norm_modulate",
    )(x, g, sc, sh)


def _inproj_kernel(a_ref, w_ref, o32_ref, o16_ref):
    acc = jnp.dot(a_ref[...], w_ref[...], preferred_element_type=F32)
    o32_ref[...] = acc
    o16_ref[...] = acc.astype(BF16)


def in_projection(h, w):
    L, k = h.shape
    n = w.shape[1]
    tm, tn = 1024, 1920
    return pl.pallas_call(
        _inproj_kernel,
        grid=(n // tn, L // tm),
        in_specs=[pl.BlockSpec((tm, k), lambda j, i: (i, 0)),
                  pl.BlockSpec((k, tn), lambda j, i: (0, j))],
        out_specs=[pl.BlockSpec((tm, tn), lambda j, i: (i, j)),
                   pl.BlockSpec((tm, tn), lambda j, i: (i, j))],
        out_shape=[jax.ShapeDtypeStruct((L, n), F32), jax.ShapeDtypeStruct((L, n), BF16)],
        compiler_params=_params("parallel", "parallel"),
        name="in_projection",
    )(h, w)


def _outproj_kernel(ys_ref, ym_ref, yd_ref, ws_ref, wm_ref, wd_ref, x_ref, gate_ref, o_ref, wb_ref):
    @pl.when(pl.program_id(1) == 0)
    def _():
        wb_ref[0:SSD_WIDTH] = ws_ref[...].astype(BF16)
        wb_ref[SSD_WIDTH:SSD_WIDTH + MOBA_WIDTH] = wm_ref[...].astype(BF16)
        wb_ref[SSD_WIDTH + MOBA_WIDTH:] = wd_ref[...].astype(BF16)

    acc = jnp.dot(ys_ref[...], wb_ref[0:SSD_WIDTH], preferred_element_type=F32)
    acc += jnp.dot(ym_ref[...], wb_ref[SSD_WIDTH:SSD_WIDTH + MOBA_WIDTH], preferred_element_type=F32)
    acc += jnp.dot(yd_ref[...], wb_ref[SSD_WIDTH + MOBA_WIDTH:], preferred_element_type=F32)
    o_ref[...] = x_ref[...] + gate_ref[...] * acc


def out_projection(y_ssd, y_moba, y_diff, w_out, x, gate):
    L, d = x.shape
    tm, tn = 512, 1024
    return pl.pallas_call(
        _outproj_kernel,
        grid=(d // tn, L // tm),
        in_specs=[pl.BlockSpec((tm, SSD_WIDTH), lambda j, i: (i, 0)),
                  pl.BlockSpec((tm, MOBA_WIDTH), lambda j, i: (i, 0)),
                  pl.BlockSpec((tm, DIFF_WIDTH), lambda j, i: (i, 0)),
                  pl.BlockSpec((SSD_WIDTH, tn), lambda j, i: (0, j)),
                  pl.BlockSpec((MOBA_WIDTH, tn), lambda j, i: (SSD_WIDTH // MOBA_WIDTH, j)),
                  pl.BlockSpec((DIFF_WIDTH, tn), lambda j, i: ((SSD_WIDTH + MOBA_WIDTH) // DIFF_WIDTH, j)),
                  pl.BlockSpec((tm, tn), lambda j, i: (i, j)),
                  pl.BlockSpec((1, tn), lambda j, i: (0, j))],
        out_specs=pl.BlockSpec((tm, tn), lambda j, i: (i, j)),
        out_shape=jax.ShapeDtypeStruct((L, d), F32),
        scratch_shapes=[pltpu.VMEM((w_out.shape[0], tn), BF16)],
        compiler_params=_params("arbitrary", "arbitrary"),
        name="out_projection",
    )(y_ssd, y_moba, y_diff, w_out, w_out, w_out, x, gate)


CAST_COLS = 256
TILE_VALID = 1
TILE_NEW_WEIGHTS = 2


def weight_plan(tile_expert, tile_valid):
    prev = jnp.concatenate([jnp.full((1,), -1, jnp.int32), tile_expert[:-1]])
    first = tile_expert != prev
    ordinal = jnp.cumsum(first.astype(jnp.int32)) - 1
    n_blocks = ordinal[-1] + 1
    block_expert = jnp.zeros_like(tile_expert).at[ordinal].set(tile_expert)
    next_expert = block_expert[(ordinal + 1) % n_blocks]
    flags = tile_valid * TILE_VALID + first.astype(jnp.int32) * TILE_NEW_WEIGHTS
    return tile_expert, flags, ordinal, next_expert, n_blocks.reshape(1)


def _grouped_matmul_kernel(te_ref, tf_ref, to_ref, tx_ref, nb_ref, a_ref, *rest, n_mats, residual):
    w_hbm, rest = rest[:n_mats], rest[n_mats:]
    if residual:
        x_ref, gate_ref, o_ref, wf_ref, wb_ref, sem = rest
    else:
        o_ref, wf_ref, wb_ref, sem = rest
    j = pl.program_id(0)
    t = pl.program_id(1)
    tn = o_ref.shape[1]
    cw = CAST_COLS
    flags = tf_ref[t]
    block = j * nb_ref[0] + to_ref[t]
    slot = block % 2

    def copies(expert, col_tile, dst_slot):
        cols = pl.ds(pl.multiple_of(col_tile * tn, tn), tn)
        return [pltpu.make_async_copy(w.at[expert, :, cols], wf_ref.at[dst_slot, m], sem.at[dst_slot, m])
                for m, w in enumerate(w_hbm)]

    @pl.when((flags & TILE_NEW_WEIGHTS) != 0)
    def _():
        @pl.when(block == 0)
        def _():
            for c in copies(te_ref[t], j, slot):
                c.start()

        wraps = to_ref[t] == nb_ref[0] - 1

        @pl.when(jnp.logical_not(wraps & (j == pl.num_programs(0) - 1)))
        def _():
            for c in copies(tx_ref[t], j + wraps.astype(jnp.int32), 1 - slot):
                c.start()

        for c in copies(te_ref[t], j, slot):
            c.wait()

    def epilogue(accs, cols):
        if n_mats == 2:
            y = _silu(accs[0]) * accs[1]
        else:
            y = accs[0]
            if residual:
                y = x_ref[:, cols] + gate_ref[:, cols] * y
        o_ref[:, cols] = y.astype(o_ref.dtype)

    @pl.when(flags == TILE_VALID + TILE_NEW_WEIGHTS)
    def _():
        a = a_ref[...]
        for c in range(0, tn, cw):
            accs = []
            for m in range(n_mats):
                wb = wf_ref[slot, m, :, c:c + cw].astype(BF16)
                wb_ref[m, :, c:c + cw] = wb
                accs.append(jnp.dot(a, wb, preferred_element_type=F32))
            epilogue(accs, slice(c, c + cw))

    @pl.when(flags == TILE_VALID)
    def _():
        a = a_ref[...]
        epilogue([jnp.dot(a, wb_ref[m], preferred_element_type=F32) for m in range(n_mats)], slice(0, tn))

    @pl.when((flags & TILE_VALID) == 0)
    def _():
        o_ref[...] = jnp.zeros_like(o_ref)


def grouped_matmul(a, weights, plan, tm, tn, out_dtype, x=None, gate=None, name="grouped_matmul"):
    rows, k = a.shape
    n = weights[0].shape[2]
    n_mats = len(weights)
    residual = x is not None
    idx = lambda f: (lambda j, t, *refs: f(j, t))
    in_specs = [pl.BlockSpec((tm, k), idx(lambda j, t: (t, 0)))] + [pl.BlockSpec(memory_space=pl.ANY)] * n_mats
    args = [a, *weights]
    if residual:
        in_specs += [pl.BlockSpec((tm, tn), idx(lambda j, t: (t, j))), pl.BlockSpec((1, tn), idx(lambda j, t: (0, j)))]
        args += [x, gate]
    grid_spec = pltpu.PrefetchScalarGridSpec(
        num_scalar_prefetch=5,
        grid=(n // tn, rows // tm),
        in_specs=in_specs,
        out_specs=pl.BlockSpec((tm, tn), idx(lambda j, t: (t, j))),
        scratch_shapes=[pltpu.VMEM((2, n_mats, k, tn), F32), pltpu.VMEM((n_mats, k, tn), BF16),
                        pltpu.SemaphoreType.DMA((2, n_mats))],
    )
    return pl.pallas_call(
        functools.partial(_grouped_matmul_kernel, n_mats=n_mats, residual=residual),
        grid_spec=grid_spec,
        out_shape=jax.ShapeDtypeStruct((rows, n), out_dtype),
        compiler_params=_params("arbitrary", "arbitrary"),
        name=name,
    )(*plan, *args)


def _causal_conv(cur, tail_ref, w_ref, b_ref):
    t = cur.shape[0]
    tail = tail_ref[...]
    w = w_ref[...]
    row8 = lax.broadcasted_iota(jnp.int32, (SUBLANES, cur.shape[1]), 0)
    acc = cur * w[SSD_CONV - 1:SSD_CONV]
    top = cur[0:SUBLANES] * w[SSD_CONV - 1:SSD_CONV]
    for s in range(1, SSD_CONV):
        wk = w[SSD_CONV - 1 - s:SSD_CONV - s]
        rolled = pltpu.roll(cur, s, axis=0)
        acc += rolled * wk
        top += jnp.where(row8 < s, pltpu.roll(tail, s, axis=0), rolled[0:SUBLANES]) * wk
    tail_ref[...] = cur[t - SUBLANES:t]
    return jnp.concatenate([top, acc[SUBLANES:]], axis=0) + b_ref[...]


def _bf16_terms(x):
    terms = []
    for _ in range(3):
        part = x.astype(BF16)
        terms.append(part)
        x = x - part.astype(F32)
    return terms


def _dot_exact_rhs01(x, onehot_bf16):
    return sum(jnp.dot(part, onehot_bf16, preferred_element_type=F32) for part in _bf16_terms(x))


def _dot_exact_lhs01(onehot_bf16, x):
    return sum(jnp.dot(onehot_bf16, part, preferred_element_type=F32) for part in _bf16_terms(x))


def _ssd_kernel(z_ref, x_ref, bc_ref, dt_ref, cwx_ref, cwb_ref, cbx_ref, cbb_ref, dtb_ref, alog_ref,
                dskip_ref, ng_ref, expand_ref, o_ref, tailx_ref, tailb_ref, state_ref, ybuf_ref):
    t = SSD_CHUNK
    hg = SSD_HEADS // SSD_GROUPS
    gw = SSD_WIDTH // SSD_GROUPS

    @pl.when(pl.program_id(0) == 0)
    def _():
        tailx_ref[...] = jnp.zeros_like(tailx_ref)
        tailb_ref[...] = jnp.zeros_like(tailb_ref)
        state_ref[...] = jnp.zeros_like(state_ref)

    xs = _silu(_causal_conv(x_ref[...], tailx_ref, cwx_ref, cbx_ref))
    bcm = _silu(_causal_conv(bc_ref[...], tailb_ref, cwb_ref, cbb_ref))
    dt = _softplus(dt_ref[...] + dtb_ref[...])
    a = -jnp.exp(alog_ref[...])
    row = lax.broadcasted_iota(jnp.int32, (t, t), 0)
    col = lax.broadcasted_iota(jnp.int32, (t, t), 1)
    tril = row >= col
    a_cs = _dot_exact_lhs01(jnp.where(tril, 1.0, 0.0).astype(BF16), dt * a)
    a_last = a_cs[t - 1:t]
    per_head = jnp.concatenate(
        [dt, jnp.exp(a_last - a_cs), jnp.exp(a_cs), jnp.broadcast_to(jnp.exp(a_last), (SUBLANES, LANES))], axis=0)
    spread = _dot_exact_rhs01(per_head, expand_ref[...])
    dt_x, to_end_x, ea_x, cd_x = spread[0:t], spread[t:2 * t], spread[2 * t:3 * t], spread[3 * t:3 * t + 1]
    xdt = xs * dt_x
    xdt_b = xdt.astype(BF16)
    xw_b = (xdt * to_end_x).astype(BF16)
    a_cs_t = a_cs.T

    y_off = []
    for g in range(SSD_GROUPS):
        bm = bcm[:, g * SSD_STATE:(g + 1) * SSD_STATE]
        cm_b = bcm[:, (SSD_GROUPS + g) * SSD_STATE:(SSD_GROUPS + g + 1) * SSD_STATE].astype(BF16)
        cb = lax.dot_general(cm_b, bm.astype(BF16), (((1,), (1,)), ((), ())), preferred_element_type=F32)
        h_prev = state_ref[g]
        y_off.append(jnp.dot(cm_b, h_prev.astype(BF16), preferred_element_type=F32)
                     * ea_x[:, g * gw:(g + 1) * gw])
        st_new = jnp.dot(bm.T.astype(BF16), xw_b[:, g * gw:(g + 1) * gw], preferred_element_type=F32)
        state_ref[g] = h_prev * cd_x[:, g * gw:(g + 1) * gw] + st_new
        for r in range(0, hg, 2):
            pair = []
            for h in (g * hg + r, g * hg + r + 1):
                diff = a_cs[:, h:h + 1] - a_cs_t[h:h + 1, :]
                m = (cb * jnp.exp(jnp.where(tril, diff, -jnp.inf))).astype(BF16)
                pair.append(jnp.dot(m, xdt_b[:, h * SSD_HEAD_DIM:(h + 1) * SSD_HEAD_DIM],
                                    preferred_element_type=F32))
            lo = (g * hg + r) * SSD_HEAD_DIM
            ybuf_ref[:, lo:lo + 2 * SSD_HEAD_DIM] = jnp.concatenate(pair, axis=1)

    y = ybuf_ref[...] + jnp.concatenate(y_off, axis=1) + xs * dskip_ref[...]
    y = y * _silu(z_ref[...])
    outs = []
    for g in range(SSD_GROUPS):
        yg = y[:, g * gw:(g + 1) * gw]
        outs.append(yg * lax.rsqrt(jnp.mean(yg * yg, axis=-1, keepdims=True) + EPS))
    o_ref[...] = (jnp.concatenate(outs, axis=1) * ng_ref[...]).astype(o_ref.dtype)


def ssd_mixer(proj, conv_w, conv_b, dt_bias, a_log, d_skip, norm_g):
    L = proj.shape[0]
    t = SSD_CHUNK
    assert L % t == 0

    def pad_lanes(v):
        return jnp.pad(v, (0, LANES - v.shape[0])).reshape(1, LANES)

    expand = (jnp.arange(SSD_WIDTH)[None, :] // SSD_HEAD_DIM == jnp.arange(LANES)[:, None]).astype(BF16)
    full = lambda shape: pl.BlockSpec(shape, lambda c: (0,) * len(shape))
    return pl.pallas_call(
        _ssd_kernel,
        grid=(L // t,),
        in_specs=[pl.BlockSpec((t, SSD_WIDTH), lambda c: (c, COL_Z // SSD_WIDTH)),
                  pl.BlockSpec((t, SSD_WIDTH), lambda c: (c, COL_X // SSD_WIDTH)),
                  pl.BlockSpec((t, SSD_BC), lambda c: (c, COL_BC // SSD_BC)),
                  pl.BlockSpec((t, LANES), lambda c: (c, COL_DT // LANES)),
                  full((SSD_CONV, SSD_WIDTH)), full((SSD_CONV, SSD_BC)),
                  full((1, SSD_WIDTH)), full((1, SSD_BC)),
                  full((1, LANES)), full((1, LANES)), full((1, SSD_WIDTH)), full((1, SSD_WIDTH)),
                  full((LANES, SSD_WIDTH))],
        out_specs=pl.BlockSpec((t, SSD_WIDTH), lambda c: (c, 0)),
        out_shape=jax.ShapeDtypeStruct((L, SSD_WIDTH), BF16),
        scratch_shapes=[pltpu.VMEM((SUBLANES, SSD_WIDTH), F32), pltpu.VMEM((SUBLANES, SSD_BC), F32),
                        pltpu.VMEM((SSD_GROUPS, SSD_STATE, SSD_WIDTH // SSD_GROUPS), F32),
                        pltpu.VMEM((t, SSD_WIDTH), F32)],
        compiler_params=_params("arbitrary"),
        name="ssd_mixer",
    )(proj, proj, proj, proj,
      conv_w[:, :SSD_WIDTH], conv_w[:, SSD_WIDTH:], conv_b[:SSD_WIDTH].reshape(1, -1),
      conv_b[SSD_WIDTH:].reshape(1, -1), pad_lanes(dt_bias), pad_lanes(a_log),
      jnp.repeat(d_skip, SSD_HEAD_DIM).reshape(1, -1), norm_g.reshape(1, -1), expand)


def _moba_prep_kernel(q_ref, k_ref, v_ref, qa_ref, ka_ref, va_ref, kmean_ref):
    own = pl.program_id(0)
    t = MOBA_BLOCK
    dh = MOBA_HEAD_DIM
    nbl = MOBA_MAX_BLOCKS

    @pl.when(own == 0)
    def _():
        kmean_ref[...] = jnp.zeros_like(kmean_ref)

    lane = lax.broadcasted_iota(jnp.int32, (t, LANES), 1)
    onehot = jnp.where(lane == dh + own, 1.0, 0.0)
    blk = lax.broadcasted_iota(jnp.int32, (nbl, t), 0)
    q_t = q_ref[...].T
    k = k_ref[...]
    k_mean = jnp.mean(k, axis=0, keepdims=True)
    va_ref[:, 0] = v_ref[...].T.reshape(MOBA_HEADS, dh, t).astype(BF16)
    for h in range(MOBA_HEADS):
        qh_t = q_t[h * dh:(h + 1) * dh]
        gate = jnp.dot(kmean_ref[h], qh_t, precision=HIGHEST, preferred_element_type=F32)
        gate = jnp.where(blk < own, gate, -jnp.inf)
        sel = blk >= own
        for _ in range(MOBA_TOPK):
            m = jnp.max(gate, axis=0, keepdims=True)
            idx = jnp.min(jnp.where(gate == m, blk, nbl), axis=0, keepdims=True)
            sel = sel | ((blk == idx) & (m > -jnp.inf))
            gate = jnp.where(blk == idx, -jnp.inf, gate)
        offs_t = jnp.where(sel, 0.0, NEG_BIG)
        qa_ref[h] = jnp.concatenate([qh_t * MOBA_Q_SCALE, offs_t, jnp.zeros((LANES - dh - nbl, t), F32)],
                                    axis=0).astype(BF16)
        pair = k[:, (h // 2) * LANES:(h // 2 + 1) * LANES]
        if h % 2:
            pair = pltpu.roll(pair, dh, axis=1)
        ka_ref[h] = jnp.where(lane < dh, pair, onehot).astype(BF16)
        kmean_ref[h, pl.ds(own, 1), :] = k_mean[:, h * dh:(h + 1) * dh]


def moba_prep(proj):
    L = proj.shape[0]
    t = MOBA_BLOCK
    assert L % t == 0 and L // t <= MOBA_MAX_BLOCKS
    return pl.pallas_call(
        _moba_prep_kernel,
        grid=(L // t,),
        in_specs=[pl.BlockSpec((t, MOBA_WIDTH), lambda i: (i, COL_MQ // MOBA_WIDTH)),
                  pl.BlockSpec((t, MOBA_WIDTH), lambda i: (i, COL_MK // MOBA_WIDTH)),
                  pl.BlockSpec((t, MOBA_WIDTH), lambda i: (i, COL_MV // MOBA_WIDTH))],
        out_specs=[pl.BlockSpec((MOBA_HEADS, LANES, t), lambda i: (0, 0, i)),
                   pl.BlockSpec((MOBA_HEADS, t, LANES), lambda i: (0, i, 0)),
                   pl.BlockSpec((MOBA_HEADS, 1, MOBA_HEAD_DIM, t), lambda i: (0, i, 0, 0))],
        out_shape=[jax.ShapeDtypeStruct((MOBA_HEADS, LANES, L), BF16),
                   jax.ShapeDtypeStruct((MOBA_HEADS, L, LANES), BF16),
                   jax.ShapeDtypeStruct((MOBA_HEADS, L // t, MOBA_HEAD_DIM, t), BF16)],
        scratch_shapes=[pltpu.VMEM((MOBA_HEADS, MOBA_MAX_BLOCKS, MOBA_HEAD_DIM), F32)],
        compiler_params=_params("arbitrary"),
        name="moba_prep",
    )(proj, proj, proj)


def _col_reduce(x, op):
    while x.shape[0] > SUBLANES:
        half = x.shape[0] // 2
        x = op(x[:half], x[half:])
    return jnp.max(x, axis=0, keepdims=True) if op is jnp.maximum else jnp.sum(x, axis=0, keepdims=True)


def _attention_kernel(near_ref, q_ref, k_ref, v_ref, posr_ref, posc_ref, tbl_ref, g_ref, lam_ref, o_ref,
                      m_ref, l_ref, al_ref, acc_ref, sa_ref, sb_ref, pa_ref, pb_ref, vt_ref, *, moba, lambda_init):
    hp = pl.program_id(0)
    qi = pl.program_id(1)
    nq = pl.num_programs(1)
    t = ATT_T
    m_ref[...] = jnp.full_like(m_ref, -jnp.inf)
    l_ref[...] = jnp.zeros_like(l_ref)
    acc_ref[...] = jnp.zeros_like(acc_ref)
    if moba:
        q_t = [q_ref[0], q_ref[1]]
        heads = [2 * hp, 2 * hp + 1]
    else:
        q = (q_ref[...].astype(F32) * DIFF_Q_SCALE).T.astype(BF16)
        half = lax.broadcasted_iota(jnp.int32, q.shape, 0) < DIFF_QK_DIM
        q_t = [jnp.where(half, q, jnp.zeros_like(q)), jnp.where(half, jnp.zeros_like(q), q)]
        heads = [MOBA_HEADS + hp]

        @pl.when(qi == 0)
        def _():
            def transpose_tile(i, carry):
                vt_ref[i] = v_ref[pl.ds(pl.multiple_of(i * t, t), t), :].astype(F32).T.astype(BF16)
                return carry

            lax.fori_loop(0, nq, transpose_tile, 0)
    trow = [tbl_ref[pl.ds(hd, 1), :] for hd in heads]
    posq = posr_ref[pl.ds(qi, 1), :]

    def scores(ki, dst_ref):
        rows = pl.ds(pl.multiple_of(ki * t, t), t)
        for s in range(2):
            k = k_ref[s, rows, :] if moba else k_ref[rows, :]
            dst_ref[s] = jnp.dot(k, q_t[s], preferred_element_type=F32)

    def accumulate(ki, p_ref):
        for s in range(2):
            v_t = v_ref[s, ki] if moba else vt_ref[ki]
            pv = jnp.dot(v_t, p_ref[s], preferred_element_type=F32)
            acc_ref[s] = al_ref[s] * acc_ref[s] + pv

    scores(0, sa_ref)
    pb_ref[...] = jnp.zeros_like(pb_ref)
    al_ref[...] = jnp.ones_like(al_ref)

    def tile(ki, general, cur_ref, nxt_ref, p_ref, p_prev_ref):
        scores(jnp.minimum(ki + 1, qi), nxt_ref)
        accumulate(jnp.maximum(ki - 1, 0), p_prev_ref)
        ch = ATT_CHUNK
        mx = [None, None]
        for r in range(0, t, ch):
            if general:
                dist = jnp.clip(posq - posc_ref[ki, r:r + ch, :], 0, LANES - 1)
                bias = [jnp.concatenate(
                    [jnp.take_along_axis(jnp.broadcast_to(tr, (ch, LANES)), dist[:, j * LANES:(j + 1) * LANES],
                                         axis=1) for j in range(t // LANES)], axis=1) for tr in trow]
                key = lax.broadcasted_iota(jnp.int32, (ch, t), 0) + r
                qry = lax.broadcasted_iota(jnp.int32, (ch, t), 1)
                causal = key + ki * t <= qry + qi * t
            for s in range(2):
                blk = cur_ref[s, r:r + ch, :]
                if general:
                    blk = jnp.where(causal, blk + bias[s if moba else 0], -jnp.inf)
                    cur_ref[s, r:r + ch, :] = blk
                mx[s] = blk if mx[s] is None else jnp.maximum(mx[s], blk)
        for s in range(2):
            m_old = m_ref[s]
            m_tile = _col_reduce(mx[s], jnp.maximum)
            if general:
                m_new = jnp.maximum(m_old, m_tile)
                shift = m_new
            else:
                c = trow[s if moba else 0][:, LANES - 1:LANES]
                m_new = jnp.maximum(m_old, m_tile + c)
                shift = m_new - c
            alpha = jnp.exp2(m_old - m_new)
            sm = None
            for r in range(0, t, ch):
                p = jnp.exp2(cur_ref[s, r:r + ch, :] - shift)
                p_ref[s, r:r + ch, :] = p.astype(BF16)
                sm = p if sm is None else sm + p
            l_ref[s] = alpha * l_ref[s] + _col_reduce(sm, jnp.add)
            al_ref[s] = alpha
            m_ref[s] = m_new

    def step(ki, *bufs):
        flag = near_ref[qi * nq + ki]

        @pl.when(flag == 0)
        def _():
            tile(ki, False, *bufs)

        @pl.when(flag != 0)
        def _():
            tile(ki, True, *bufs)

    def pair(j, carry):
        step(2 * j, sa_ref, sb_ref, pa_ref, pb_ref)
        step(2 * j + 1, sb_ref, sa_ref, pb_ref, pa_ref)
        return carry

    lax.fori_loop(0, (qi + 1) // 2, pair, 0)

    @pl.when(qi % 2 == 0)
    def _():
        step(qi, sa_ref, sb_ref, pa_ref, pb_ref)
        accumulate(qi, pa_ref)

    @pl.when(qi % 2 == 1)
    def _():
        accumulate(qi, pb_ref)

    if moba:
        outs = []
        for s in range(2):
            o = acc_ref[s] / l_ref[s]
            ms = jnp.sum(o * o, axis=0, keepdims=True) * (1.0 / MOBA_HEAD_DIM)
            outs.append(o * lax.rsqrt(ms + EPS) * g_ref[s])
        y_t = jnp.concatenate(outs, axis=0)
    else:
        lp = lam_ref[...]
        lam = (jnp.exp(jnp.sum(lp[0:1] * lp[1:2], axis=1, keepdims=True))
               - jnp.exp(jnp.sum(lp[2:3] * lp[3:4], axis=1, keepdims=True)) + lambda_init)
        o = acc_ref[0] / l_ref[0] - lam * (acc_ref[1] / l_ref[1])
        ms = jnp.sum(o * o, axis=0, keepdims=True) * (1.0 / DIFF_V_DIM)
        y_t = (o * lax.rsqrt(ms + EPS) * g_ref[0]) * (1.0 - lambda_init)
    o_ref[...] = y_t.T.astype(o_ref.dtype)


def _attention_call(kernel, steps, dv, L, near, in_specs, args, name, vt_scratch=False):
    t = ATT_T
    full = lambda a: pl.BlockSpec(a.shape, lambda h, i, nr: (0,) * a.ndim)
    grid_spec = pltpu.PrefetchScalarGridSpec(
        num_scalar_prefetch=1,
        grid=(steps, L // t),
        in_specs=in_specs + [full(a) for a in args[len(in_specs):]],
        out_specs=pl.BlockSpec((t, LANES), lambda h, i, nr: (i, h)),
        scratch_shapes=[pltpu.VMEM((2, 1, t), F32), pltpu.VMEM((2, 1, t), F32), pltpu.VMEM((2, 1, t), F32),
                        pltpu.VMEM((2, dv, t), F32), pltpu.VMEM((2, t, t), F32), pltpu.VMEM((2, t, t), F32),
                        pltpu.VMEM((2, t, t), BF16), pltpu.VMEM((2, t, t), BF16),
                        pltpu.VMEM((L // t if vt_scratch else 1, dv, t), BF16)],
    )
    return pl.pallas_call(kernel, grid_spec=grid_spec, out_shape=jax.ShapeDtypeStruct((L, steps * LANES), BF16),
                          compiler_params=_params("parallel", "arbitrary"), name=name)(near, *args)


def moba_attention(qa_t, ka, va_t, near, pos_rows, pos_cols, table, norm_g):
    heads, L, _ = ka.shape
    t = ATT_T
    kernel = functools.partial(_attention_kernel, moba=True, lambda_init=None)
    g = norm_g.reshape(heads, MOBA_HEAD_DIM, 1)
    return _attention_call(
        kernel, heads // 2, MOBA_HEAD_DIM, L, near,
        [pl.BlockSpec((2, LANES, t), lambda h, i, nr: (h, 0, i)),
         pl.BlockSpec((2, L, LANES), lambda h, i, nr: (h, 0, 0)),
         pl.BlockSpec((2, L // t, MOBA_HEAD_DIM, t), lambda h, i, nr: (h, 0, 0, 0)),
         pl.BlockSpec(pos_rows.shape, lambda h, i, nr: (0, 0)),
         pl.BlockSpec(pos_cols.shape, lambda h, i, nr: (0, 0, 0)),
         pl.BlockSpec(table.shape, lambda h, i, nr: (0, 0)),
         pl.BlockSpec((2, MOBA_HEAD_DIM, 1), lambda h, i, nr: (h, 0, 0))],
        [qa_t, ka, va_t, pos_rows, pos_cols, table, g, jnp.zeros((4, DIFF_QK_DIM), F32)], "moba_attention")


def diff_attention(proj16, near, pos_rows, pos_cols, table, lam_params, subln_g, lambda_init):
    L = proj16.shape[0]
    t = ATT_T
    kernel = functools.partial(_attention_kernel, moba=False, lambda_init=lambda_init)
    return _attention_call(
        kernel, DIFF_HEADS, DIFF_V_DIM, L, near,
        [pl.BlockSpec((t, LANES), lambda h, i, nr: (i, COL_DQ // LANES + h)),
         pl.BlockSpec((L, LANES), lambda h, i, nr: (0, COL_DK // LANES + h)),
         pl.BlockSpec((L, LANES), lambda h, i, nr: (0, COL_DV // LANES + h))],
        [proj16, proj16, proj16, pos_rows, pos_cols, table, subln_g.reshape(1, DIFF_V_DIM, 1), lam_params],
        "diff_attention", vt_scratch=True)


def _rel_bucket(dist):
    n = jnp.maximum(dist, 0)
    max_exact = REL_BUCKETS // 2
    nf = jnp.maximum(n, 1).astype(F32)
    large = max_exact + (jnp.log(nf / max_exact) / math.log(REL_MAX_DIST / max_exact)
                         * (REL_BUCKETS - max_exact)).astype(jnp.int32)
    return jnp.where(n < max_exact, n, jnp.minimum(large, REL_BUCKETS - 1))


def attention_tables(positions, rel_bias):
    L = positions.shape[0]
    t = ATT_T
    buckets = _rel_bucket(jnp.arange(LANES, dtype=jnp.int32))
    table = rel_bias[buckets].T * LOG2E
    pos_rows = positions.reshape(L // t, t)
    lo, hi = jnp.min(pos_rows, axis=1), jnp.max(pos_rows, axis=1)
    near = (lo[:, None] - hi[None, :] < LANES) | jnp.eye(L // t, dtype=bool)
    return table, pos_rows, positions.reshape(L // t, t, 1), near.astype(jnp.int32).reshape(-1)


def _router_kernel(x_ref, g_ref, sc_ref, sh_ref, rw_ref, h_ref, comb_ref, rank_ref, rank_t_ref, cum_ref, total_ref,
                   cnt_ref):
    t = TOK_T

    @pl.when(pl.program_id(0) == 0)
    def _():
        cnt_ref[...] = jnp.zeros_like(cnt_ref)

    h = _norm_mod(x_ref[...], g_ref[...], sc_ref[...], sh_ref[...])
    h_ref[...] = h.astype(BF16)
    lane = lax.broadcasted_iota(jnp.int32, (t, LANES), 1)
    logits = jnp.dot(h, rw_ref[...], precision=HIGHEST, preferred_element_type=F32)
    logits = jnp.where(lane < N_EXPERTS, logits, -jnp.inf)
    m1 = jnp.max(logits, axis=1, keepdims=True)
    i1 = jnp.min(jnp.where(logits == m1, lane, LANES), axis=1, keepdims=True)
    rest = jnp.where(lane == i1, -jnp.inf, logits)
    m2 = jnp.max(rest, axis=1, keepdims=True)
    i2 = jnp.min(jnp.where(rest == m2, lane, LANES), axis=1, keepdims=True)
    e2 = jnp.exp(m2 - m1)
    denom = 1.0 + e2
    comb_ref[...] = jnp.where(lane == i1, 1.0 / denom, 0.0) + jnp.where(lane == i2, e2 / denom, 0.0)
    sel = jnp.where((lane == i1) | (lane == i2), 1.0, 0.0)
    row = lax.broadcasted_iota(jnp.int32, (t, t), 0)
    col = lax.broadcasted_iota(jnp.int32, (t, t), 1)
    before = jnp.dot(jnp.where(row > col, 1.0, 0.0).astype(BF16), sel.astype(BF16), preferred_element_type=F32)
    carry = cnt_ref[...]
    cum_ref[0] = carry
    rank = jnp.where(sel > 0.0, before + carry, -1.0)
    rank_ref[...] = rank
    rank_t_ref[...] = rank.T
    carry = carry + jnp.sum(sel, axis=0, keepdims=True)
    cnt_ref[...] = carry
    total_ref[...] = carry


def moe_router(x, g, sc, sh, router_w):
    L, d = x.shape
    t = TOK_T
    rw = jnp.pad(router_w, ((0, 0), (0, LANES - N_EXPERTS)))
    vec = pl.BlockSpec((1, d), lambda i: (0, 0))
    tok = pl.BlockSpec((t, LANES), lambda i: (i, 0))
    return pl.pallas_call(
        _router_kernel,
        grid=(L // t,),
        in_specs=[pl.BlockSpec((t, d), lambda i: (i, 0)), vec, vec, vec,
                  pl.BlockSpec((d, LANES), lambda i: (0, 0))],
        out_specs=[pl.BlockSpec((t, d), lambda i: (i, 0)), tok, tok,
                   pl.BlockSpec((LANES, t), lambda i: (0, i)),
                   pl.BlockSpec((1, 1, LANES), lambda i: (i, 0, 0)),
                   pl.BlockSpec((1, LANES), lambda i: (0, 0))],
        out_shape=[jax.ShapeDtypeStruct((L, d), BF16), jax.ShapeDtypeStruct((L, LANES), F32),
                   jax.ShapeDtypeStruct((L, LANES), F32), jax.ShapeDtypeStruct((LANES, L), F32),
                   jax.ShapeDtypeStruct((L // t, 1, LANES), F32), jax.ShapeDtypeStruct((1, LANES), F32)],
        scratch_shapes=[pltpu.VMEM((1, LANES), F32)],
        compiler_params=_params("arbitrary"),
        name="moe_router",
    )(x, g, sc, sh, rw)


def _item_lists(hit, n_items):
    rows, cols = hit.shape
    running = jnp.cumsum(hit.reshape(-1).astype(jnp.int32))
    n_real = running[-1]
    k = jnp.arange(n_items, dtype=jnp.int32)
    real = k < n_real
    idx = jnp.searchsorted(running, jnp.minimum(k + 1, n_real), side="left", method="compare_all")
    idx = jnp.minimum(idx, rows * cols - 1).astype(jnp.int32)
    r, c = idx // cols, idx % cols
    prev_r = jnp.concatenate([jnp.full((1,), -1, jnp.int32), r[:-1]])
    next_r = jnp.concatenate([r[1:], jnp.full((1,), -1, jnp.int32)])
    first = real & (r != prev_r)
    last = real & ((r != next_r) | (k == n_real - 1))
    flags = real.astype(jnp.int32) + 2 * first.astype(jnp.int32) + 4 * last.astype(jnp.int32)
    return r, c, flags


def moe_plan(rank_t, cum, total, L):
    tm, tb = MOE_TM, TOK_T
    nb = L // tb
    n_tiles = 2 * L // tm + N_EXPERTS
    counts = total[0, :N_EXPERTS].astype(jnp.int32)
    tiles_e = (counts + tm - 1) // tm
    tile_end = jnp.cumsum(tiles_e)
    tile_start = tile_end - tiles_e
    tid = jnp.arange(n_tiles, dtype=jnp.int32)
    tile_valid = tid < tile_end[-1]
    tile_expert = jnp.minimum(jnp.searchsorted(tile_end, tid, side="right"), N_EXPERTS - 1).astype(jnp.int32)
    local_row = (tid - tile_start[tile_expert]) * tm
    r = rank_t[:N_EXPERTS].astype(jnp.int32)
    dest = jnp.where(r >= 0, r + (tile_start * tm)[:, None], -1)
    cum_i = cum[:, 0, :N_EXPERTS].astype(jnp.int32)
    cum_next = jnp.concatenate([cum_i[1:], counts[None, :]], axis=0)
    lo = cum_i[:, tile_expert].T
    hi = cum_next[:, tile_expert].T
    hit = tile_valid[:, None] & (lo < (local_row + tm)[:, None]) & (hi > local_row[:, None])
    n_items = n_tiles + N_EXPERTS * nb
    pad_hit = hit | ((~tile_valid)[:, None] & (jnp.arange(nb) == 0)[None, :])
    gather_items = _item_lists(pad_hit, n_items)
    kb, tt, fl = _item_lists(hit.T, n_items)
    return dict(tile_expert=tile_expert, tile_valid=tile_valid.astype(jnp.int32),
                dest_rows=dest, segment_start=(tile_start * tm).astype(jnp.int32),
                gather_items=gather_items, combine_items=(tt, kb, fl), n_tiles=n_tiles, n_items=n_items)


def _moe_gather_kernel(it_ref, ib_ref, if_ref, te_ref, dest_ref, h_ref, o_ref, acc_ref):
    i = pl.program_id(0)
    flag = if_ref[i]
    tile = it_ref[i]
    tm = MOE_TM

    @pl.when((flag & 2) != 0)
    def _():
        acc_ref[...] = jnp.zeros_like(acc_ref)

    @pl.when((flag & 1) != 0)
    def _():
        d = dest_ref[pl.ds(te_ref[tile], 1), :]
        rows = tile * tm + lax.broadcasted_iota(jnp.int32, (tm, 1), 0)
        onehot = jnp.where(d == rows, 1.0, 0.0).astype(BF16)
        acc_ref[...] += jnp.dot(onehot, h_ref[...], preferred_element_type=F32)

    @pl.when((flag & 4) != 0)
    def _():
        o_ref[...] = acc_ref[...].astype(o_ref.dtype)


def moe_gather(h, plan):
    L, d = h.shape
    tm, tb = MOE_TM, TOK_T
    it, ib, fl = plan["gather_items"]
    grid_spec = pltpu.PrefetchScalarGridSpec(
        num_scalar_prefetch=4,
        grid=(plan["n_items"],),
        in_specs=[pl.BlockSpec((N_EXPERTS, tb), lambda i, it, ib, fl, te: (0, ib[i])),
                  pl.BlockSpec((tb, d), lambda i, it, ib, fl, te: (ib[i], 0))],
        out_specs=pl.BlockSpec((tm, d), lambda i, it, ib, fl, te: (it[i], 0)),
        scratch_shapes=[pltpu.VMEM((tm, d), F32)],
    )
    return pl.pallas_call(
        _moe_gather_kernel,
        grid_spec=grid_spec,
        out_shape=jax.ShapeDtypeStruct((plan["n_tiles"] * tm, d), BF16),
        compiler_params=_params("arbitrary"),
        name="moe_gather",
    )(it, ib, fl, plan["tile_expert"], plan["dest_rows"], h)


def _moe_combine_kernel(it_ref, ib_ref, if_ref, te_ref, ts_ref, rank_ref, w_ref, ys_ref, x_ref, gate_ref, fg_ref,
                        o_ref, acc_ref, *, final_norm):
    i = pl.program_id(0)
    flag = if_ref[i]
    tile = it_ref[i]
    tm = MOE_TM

    @pl.when((flag & 2) != 0)
    def _():
        acc_ref[...] = jnp.zeros_like(acc_ref)

    @pl.when((flag & 1) != 0)
    def _():
        e = te_ref[tile]
        mine = lax.broadcasted_iota(jnp.int32, rank_ref.shape, 1) == e
        rank = jnp.sum(jnp.where(mine, rank_ref[...], 0.0), axis=1, keepdims=True)
        w = jnp.sum(jnp.where(mine, w_ref[...], 0.0), axis=1, keepdims=True)
        first = tile * tm - ts_ref[e]
        cols = (first + lax.broadcasted_iota(jnp.int32, (1, tm), 1)).astype(F32)
        onehot = jnp.where(rank == cols, 1.0, 0.0).astype(BF16)
        acc_ref[...] += w * jnp.dot(onehot, ys_ref[...], preferred_element_type=F32)

    @pl.when((flag & 4) != 0)
    def _():
        y = x_ref[...] + gate_ref[...] * acc_ref[...]
        if final_norm:
            y = y * lax.rsqrt(jnp.mean(y * y, axis=-1, keepdims=True) + EPS) * fg_ref[...]
        o_ref[...] = y


def moe_combine(ys, comb, rank, x, gate, plan, final_g=None):
    L, d = x.shape
    final_norm = final_g is not None
    if not final_norm:
        final_g = jnp.ones((1, d), F32)
    tm, tb = MOE_TM, TOK_T
    it, ib, fl = plan["combine_items"]
    tok = lambda i, it, ib, *_: (ib[i], 0)
    fixed = lambda i, *_: (0, 0)
    grid_spec = pltpu.PrefetchScalarGridSpec(
        num_scalar_prefetch=5,
        grid=(plan["n_items"],),
        in_specs=[pl.BlockSpec((tb, LANES), tok), pl.BlockSpec((tb, LANES), tok),
                  pl.BlockSpec((tm, d), lambda i, it, *_: (it[i], 0)),
                  pl.BlockSpec((tb, d), tok), pl.BlockSpec((1, d), fixed), pl.BlockSpec((1, d), fixed)],
        out_specs=pl.BlockSpec((tb, d), tok),
        scratch_shapes=[pltpu.VMEM((tb, d), F32)],
    )
    return pl.pallas_call(
        functools.partial(_moe_combine_kernel, final_norm=final_norm),
        grid_spec=grid_spec,
        out_shape=jax.ShapeDtypeStruct((L, d), F32),
        compiler_params=_params("arbitrary"),
        name="moe_combine",
    )(it, ib, fl, plan["tile_expert"], plan["segment_start"], rank, comb, ys, x, gate, final_g)


def moe_ffn(x, g, sc, sh, gate, router_w, w1, w3, w2, final_g=None):
    L = x.shape[0]
    h, comb, rank, rank_t, cum, total = moe_router(x, g, sc, sh, router_w)
    plan = moe_plan(rank_t, cum, total, L)
    xs = moe_gather(h, plan)
    wplan = weight_plan(plan["tile_expert"], plan["tile_valid"])
    act = grouped_matmul(xs, [w1, w3], wplan, MOE_TM, 1024, BF16, name="swiglu_up")
    ys = grouped_matmul(act, [w2], wplan, MOE_TM, 512, BF16, name="swiglu_down")
    return moe_combine(ys, comb, rank, x, gate, plan, final_g)


def dense_ffn(x, g, sc, sh, gate, w1, w3, w2):
    L = x.shape[0]
    tm = 512
    h = norm_modulate(x, g, sc, sh, BF16)
    wplan = weight_plan(jnp.zeros((L // tm,), jnp.int32), jnp.ones((L // tm,), jnp.int32))
    act = grouped_matmul(h, [w1[None], w3[None]], wplan, tm, 512, BF16, name="swiglu_up")
    return grouped_matmul(act, [w2[None]], wplan, tm, 512, F32, x=x, gate=gate, name="swiglu_down")


def _permute_in_proj(w):
    d = w.shape[0]
    return jnp.concatenate([w[:, :ORIG_DT], w[:, ORIG_DT + SSD_HEADS:], w[:, ORIG_DT:ORIG_DT + SSD_HEADS],
                            jnp.zeros((d, LANES - SSD_HEADS), w.dtype)], axis=1).astype(BF16)


def kernel(x, c, positions, rel_bias, w_ada, b_ada, norm_mix_g, w_in, conv_w, conv_b, dt_bias, a_log, d_skip, ssd_norm_g, moba_norm_g, diff_lambda, diff_subln_g, w_out, norm_ffn_g, dense_w1, dense_w3, dense_w2, router_w, expert_w1, expert_w3, expert_w2, final_g):
    batch, L, d = x.shape
    assert batch == 1 and d == D_MODEL
    x = x[0]
    mod = ada_modulation(c, w_ada, b_ada)
    table, pos_rows, pos_cols, near = attention_tables(positions[0], rel_bias)
    row = lambda v: v.reshape(1, -1)
    for layer in range(DEPTH):
        lambda_init = 0.8 - 0.6 * math.exp(-0.3 * layer)
        shift1, scale1, gate1, shift2, scale2, gate2 = (mod[layer, :, j * d:(j + 1) * d] for j in range(6))
        h = norm_modulate(x, row(norm_mix_g[layer]), scale1, shift1, BF16)
        proj, proj16 = in_projection(h, _permute_in_proj(w_in[layer]))
        y_ssd = ssd_mixer(proj, conv_w[layer], conv_b[layer], dt_bias[layer], a_log[layer], d_skip[layer],
                          ssd_norm_g[layer])
        qa, ka, va = moba_prep(proj)
        y_moba = moba_attention(qa, ka, va, near, pos_rows, pos_cols, table, moba_norm_g[layer])
        y_diff = diff_attention(proj16, near, pos_rows, pos_cols, table, diff_lambda[layer],
                                diff_subln_g[layer], lambda_init)
        x = out_projection(y_ssd, y_moba, y_diff, w_out[layer], x, gate1)
        i = layer // 2
        g2 = row(norm_ffn_g[layer])
        if layer % 2 == 0:
            x = dense_ffn(x, g2, scale2, shift2, gate2, dense_w1[i], dense_w3[i], dense_w2[i])
        else:
            x = moe_ffn(x, g2, scale2, shift2, gate2, router_w[i], expert_w1[i], expert_w3[i], expert_w2[i],
                        final_g=row(final_g) if layer == DEPTH - 1 else None)
    if (DEPTH - 1) % 2 == 0:
        zero = jnp.zeros((1, d), F32)
        x = norm_modulate(x, row(final_g), zero, zero, F32)
    return x[None]
```

```python
import functools
import math

import jax
import jax.numpy as jnp
from jax import lax
from jax.experimental import pallas as pl
from jax.experimental.pallas import tpu as pltpu

F32 = jnp.float32
BF16 = jnp.bfloat16
HIGHEST = lax.Precision.HIGHEST

D_MODEL = 2048
DEPTH = 2
SSD_HEADS = 16
SSD_HEAD_DIM = 64
SSD_WIDTH = SSD_HEADS * SSD_HEAD_DIM
SSD_GROUPS = 2
SSD_STATE = 128
SSD_CONV = 4
SSD_CHUNK = 256
SSD_BC = 2 * SSD_GROUPS * SSD_STATE
MOBA_HEADS = 8
MOBA_HEAD_DIM = 64
MOBA_WIDTH = MOBA_HEADS * MOBA_HEAD_DIM
MOBA_BLOCK = 256
MOBA_TOPK = 3
DIFF_HEADS = 4
DIFF_QK_DIM = 64
DIFF_V_DIM = 128
DIFF_WIDTH = DIFF_HEADS * DIFF_V_DIM
REL_BUCKETS = 32
REL_MAX_DIST = 128
D_FF_DENSE = 5632
N_EXPERTS = 8
D_FF_EXPERT = 7168
EPS = 1e-6

LANES = 128
SUBLANES = 8
VMEM_LIMIT = 56 * 1024 * 1024

COL_Z = 0
COL_X = SSD_WIDTH
COL_BC = COL_X + SSD_WIDTH
COL_MQ = COL_BC + SSD_BC
COL_MK = COL_MQ + MOBA_WIDTH
COL_MV = COL_MK + MOBA_WIDTH
COL_DQ = COL_MV + MOBA_WIDTH
COL_DK = COL_DQ + DIFF_WIDTH
COL_DV = COL_DK + DIFF_WIDTH
COL_DT = COL_DV + DIFF_WIDTH
PROJ_W = COL_DT + LANES
ORIG_DT = SSD_WIDTH + SSD_WIDTH + SSD_BC

ATT_T = 256
ATT_CHUNK = 64
TOK_T = 512
MOE_TM = 256
NEG_BIG = -1e9
MOBA_MAX_BLOCKS = 32
LOG2E = math.log2(math.e)
MOBA_Q_SCALE = MOBA_HEAD_DIM ** -0.5 * LOG2E
DIFF_Q_SCALE = DIFF_QK_DIM ** -0.5 * LOG2E


def _silu(x):
    return x * (1.0 / (1.0 + jnp.exp(-x)))


def _softplus(x):
    return jnp.maximum(x, 0.0) + jnp.log1p(jnp.exp(-jnp.abs(x)))


def _params(*sem):
    return pltpu.CompilerParams(dimension_semantics=sem, vmem_limit_bytes=VMEM_LIMIT)


def _ada_kernel(c_ref, w_ref, b_ref, o_ref):
    ca = _silu(c_ref[...])
    o_ref[0] = jnp.sum(ca * w_ref[0], axis=0, keepdims=True) + b_ref[0]


def ada_modulation(c, w_ada, b_ada):
    depth, d, n = w_ada.shape
    tn = 2048
    return pl.pallas_call(
        _ada_kernel,
        grid=(depth, n // tn),
        in_specs=[pl.BlockSpec((d, 1), lambda l, j: (0, 0)),
                  pl.BlockSpec((1, d, tn), lambda l, j: (l, 0, j)),
                  pl.BlockSpec((1, 1, tn), lambda l, j: (l, 0, j))],
        out_specs=pl.BlockSpec((1, 1, tn), lambda l, j: (l, 0, j)),
        out_shape=jax.ShapeDtypeStruct((depth, 1, n), F32),
        compiler_params=_params("parallel", "parallel"),
        name="ada_modulation",
    )(c.reshape(d, 1), w_ada, b_ada.reshape(depth, 1, n))


def _norm_mod(x, g, sc, sh):
    ms = jnp.mean(x * x, axis=-1, keepdims=True)
    return (x * lax.rsqrt(ms + EPS) * g) * (1.0 + sc) + sh


def _norm_kernel(x_ref, g_ref, sc_ref, sh_ref, o_ref):
    o_ref[...] = _norm_mod(x_ref[...], g_ref[...], sc_ref[...], sh_ref[...]).astype(o_ref.dtype)


def norm_modulate(x, g, sc, sh, out_dtype):
    L, d = x.shape
    tm = 512
    vec = pl.BlockSpec((1, d), lambda i: (0, 0))
    return pl.pallas_call(
        _norm_kernel,
        grid=(L // tm,),
        in_specs=[pl.BlockSpec((tm, d), lambda i: (i, 0)), vec, vec, vec],
        out_specs=pl.BlockSpec((tm, d), lambda i: (i, 0)),
        out_shape=jax.ShapeDtypeStruct((L, d), out_dtype),
        compiler_params=_params("parallel"),
        name="norm_modulate",
    )(x, g, sc, sh)


def _inproj_kernel(a_ref, w_ref, o32_ref, o16_ref):
    acc = jnp.dot(a_ref[...], w_ref[...], preferred_element_type=F32)
    o32_ref[...] = acc
    o16_ref[...] = acc.astype(BF16)


def in_projection(h, w):
    L, k = h.shape
    n = w.shape[1]
    tm, tn = 1024, 1920
    return pl.pallas_call(
        _inproj_kernel,
        grid=(n // tn, L // tm),
        in_specs=[pl.BlockSpec((tm, k), lambda j, i: (i, 0)),
                  pl.BlockSpec((k, tn), lambda j, i: (0, j))],
        out_specs=[pl.BlockSpec((tm, tn), lambda j, i: (i, j)),
                   pl.BlockSpec((tm, tn), lambda j, i: (i, j))],
        out_shape=[jax.ShapeDtypeStruct((L, n), F32), jax.ShapeDtypeStruct((L, n), BF16)],
        compiler_params=_params("parallel", "parallel"),
        name="in_projection",
    )(h, w)


def _outproj_kernel(ys_ref, ym_ref, yd_ref, ws_ref, wm_ref, wd_ref, x_ref, gate_ref, o_ref, wb_ref):
    @pl.when(pl.program_id(1) == 0)
    def _():
        wb_ref[0:SSD_WIDTH] = ws_ref[...].astype(BF16)
        wb_ref[SSD_WIDTH:SSD_WIDTH + MOBA_WIDTH] = wm_ref[...].astype(BF16)
        wb_ref[SSD_WIDTH + MOBA_WIDTH:] = wd_ref[...].astype(BF16)

    acc = jnp.dot(ys_ref[...], wb_ref[0:SSD_WIDTH], preferred_element_type=F32)
    acc += jnp.dot(ym_ref[...], wb_ref[SSD_WIDTH:SSD_WIDTH + MOBA_WIDTH], preferred_element_type=F32)
    acc += jnp.dot(yd_ref[...], wb_ref[SSD_WIDTH + MOBA_WIDTH:], preferred_element_type=F32)
    o_ref[...] = x_ref[...] + gate_ref[...] * acc


def out_projection(y_ssd, y_moba, y_diff, w_out, layer, x, gate):
    L, d = x.shape
    tm, tn = 512, 1024
    return pl.pallas_call(
        _outproj_kernel,
        grid=(d // tn, L // tm),
        in_specs=[pl.BlockSpec((tm, SSD_WIDTH), lambda j, i: (i, 0)),
                  pl.BlockSpec((tm, MOBA_WIDTH), lambda j, i: (i, 0)),
                  pl.BlockSpec((tm, DIFF_WIDTH), lambda j, i: (i, 0)),
                  pl.BlockSpec((None, SSD_WIDTH, tn), lambda j, i: (layer, 0, j)),
                  pl.BlockSpec((None, MOBA_WIDTH, tn), lambda j, i: (layer, SSD_WIDTH // MOBA_WIDTH, j)),
                  pl.BlockSpec((None, DIFF_WIDTH, tn),
                               lambda j, i: (layer, (SSD_WIDTH + MOBA_WIDTH) // DIFF_WIDTH, j)),
                  pl.BlockSpec((tm, tn), lambda j, i: (i, j)),
                  pl.BlockSpec((1, tn), lambda j, i: (0, j))],
        out_specs=pl.BlockSpec((tm, tn), lambda j, i: (i, j)),
        out_shape=jax.ShapeDtypeStruct((L, d), F32),
        scratch_shapes=[pltpu.VMEM((w_out.shape[1], tn), BF16)],
        compiler_params=_params("arbitrary", "arbitrary"),
        name="out_projection",
    )(y_ssd, y_moba, y_diff, w_out, w_out, w_out, x, gate)


CAST_COLS = 256
TILE_VALID = 1
TILE_NEW_WEIGHTS = 2


def weight_plan(tile_expert, tile_valid):
    prev = jnp.concatenate([jnp.full((1,), -1, jnp.int32), tile_expert[:-1]])
    first = tile_expert != prev
    ordinal = jnp.cumsum(first.astype(jnp.int32)) - 1
    n_blocks = ordinal[-1] + 1
    block_expert = jnp.zeros_like(tile_expert).at[ordinal].set(tile_expert)
    next_expert = block_expert[(ordinal + 1) % n_blocks]
    flags = tile_valid * TILE_VALID + first.astype(jnp.int32) * TILE_NEW_WEIGHTS
    return tile_expert, flags, ordinal, next_expert, n_blocks.reshape(1)


def _grouped_matmul_kernel(te_ref, tf_ref, to_ref, tx_ref, nb_ref, a_ref, *rest, n_mats, residual):
    w_hbm, rest = rest[:n_mats], rest[n_mats:]
    if residual:
        x_ref, gate_ref, o_ref, wf_ref, wb_ref, sem = rest
    else:
        o_ref, wf_ref, wb_ref, sem = rest
    j = pl.program_id(0)
    t = pl.program_id(1)
    tn = o_ref.shape[1]
    cw = CAST_COLS
    flags = tf_ref[t]
    block = j * nb_ref[0] + to_ref[t]
    slot = block % 2

    def copies(expert, col_tile, dst_slot):
        cols = pl.ds(pl.multiple_of(col_tile * tn, tn), tn)
        return [pltpu.make_async_copy(w.at[expert, :, cols], wf_ref.at[dst_slot, m], sem.at[dst_slot, m])
                for m, w in enumerate(w_hbm)]

    @pl.when((flags & TILE_NEW_WEIGHTS) != 0)
    def _():
        @pl.when(block == 0)
        def _():
            for c in copies(te_ref[t], j, slot):
                c.start()

        wraps = to_ref[t] == nb_ref[0] - 1

        @pl.when(jnp.logical_not(wraps & (j == pl.num_programs(0) - 1)))
        def _():
            for c in copies(tx_ref[t], j + wraps.astype(jnp.int32), 1 - slot):
                c.start()

        for c in copies(te_ref[t], j, slot):
            c.wait()

    def epilogue(accs, cols):
        if n_mats == 2:
            y = _silu(accs[0]) * accs[1]
        else:
            y = accs[0]
            if residual:
                y = x_ref[:, cols] + gate_ref[:, cols] * y
        o_ref[:, cols] = y.astype(o_ref.dtype)

    @pl.when(flags == TILE_VALID + TILE_NEW_WEIGHTS)
    def _():
        a = a_ref[...]
        for c in range(0, tn, cw):
            accs = []
            for m in range(n_mats):
                wb = wf_ref[slot, m, :, c:c + cw].astype(BF16)
                wb_ref[m, :, c:c + cw] = wb
                accs.append(jnp.dot(a, wb, preferred_element_type=F32))
            epilogue(accs, slice(c, c + cw))

    @pl.when(flags == TILE_VALID)
    def _():
        a = a_ref[...]
        epilogue([jnp.dot(a, wb_ref[m], preferred_element_type=F32) for m in range(n_mats)], slice(0, tn))

    @pl.when((flags & TILE_VALID) == 0)
    def _():
        o_ref[...] = jnp.zeros_like(o_ref)


def grouped_matmul(a, weights, plan, tm, tn, out_dtype, x=None, gate=None, name="grouped_matmul"):
    rows, k = a.shape
    n = weights[0].shape[2]
    n_mats = len(weights)
    residual = x is not None
    idx = lambda f: (lambda j, t, *refs: f(j, t))
    in_specs = [pl.BlockSpec((tm, k), idx(lambda j, t: (t, 0)))] + [pl.BlockSpec(memory_space=pl.ANY)] * n_mats
    args = [a, *weights]
    if residual:
        in_specs += [pl.BlockSpec((tm, tn), idx(lambda j, t: (t, j))), pl.BlockSpec((1, tn), idx(lambda j, t: (0, j)))]
        args += [x, gate]
    grid_spec = pltpu.PrefetchScalarGridSpec(
        num_scalar_prefetch=5,
        grid=(n // tn, rows // tm),
        in_specs=in_specs,
        out_specs=pl.BlockSpec((tm, tn), idx(lambda j, t: (t, j))),
        scratch_shapes=[pltpu.VMEM((2, n_mats, k, tn), F32), pltpu.VMEM((n_mats, k, tn), BF16),
                        pltpu.SemaphoreType.DMA((2, n_mats))],
    )
    return pl.pallas_call(
        functools.partial(_grouped_matmul_kernel, n_mats=n_mats, residual=residual),
        grid_spec=grid_spec,
        out_shape=jax.ShapeDtypeStruct((rows, n), out_dtype),
        compiler_params=_params("arbitrary", "arbitrary"),
        name=name,
    )(*plan, *args)


def _causal_conv(cur, tail_ref, w_ref, b_ref):
    t = cur.shape[0]
    tail = tail_ref[...]
    w = w_ref[...]
    row8 = lax.broadcasted_iota(jnp.int32, (SUBLANES, cur.shape[1]), 0)
    acc = cur * w[SSD_CONV - 1:SSD_CONV]
    top = cur[0:SUBLANES] * w[SSD_CONV - 1:SSD_CONV]
    for s in range(1, SSD_CONV):
        wk = w[SSD_CONV - 1 - s:SSD_CONV - s]
        rolled = pltpu.roll(cur, s, axis=0)
        acc += rolled * wk
        top += jnp.where(row8 < s, pltpu.roll(tail, s, axis=0), rolled[0:SUBLANES]) * wk
    tail_ref[...] = cur[t - SUBLANES:t]
    return jnp.concatenate([top, acc[SUBLANES:]], axis=0) + b_ref[...]


def _bf16_terms(x):
    terms = []
    for _ in range(3):
        part = x.astype(BF16)
        terms.append(part)
        x = x - part.astype(F32)
    return terms


def _dot_exact_rhs01(x, onehot_bf16):
    return sum(jnp.dot(part, onehot_bf16, preferred_element_type=F32) for part in _bf16_terms(x))


def _dot_exact_lhs01(onehot_bf16, x):
    return sum(jnp.dot(onehot_bf16, part, preferred_element_type=F32) for part in _bf16_terms(x))


def _ssd_kernel(z_ref, x_ref, bc_ref, dt_ref, cwx_ref, cwb_ref, cbx_ref, cbb_ref, dtb_ref, alog_ref,
                dskip_ref, ng_ref, expand_ref, o_ref, tailx_ref, tailb_ref, state_ref, ybuf_ref):
    t = SSD_CHUNK
    hg = SSD_HEADS // SSD_GROUPS
    gw = SSD_WIDTH // SSD_GROUPS

    @pl.when(pl.program_id(0) == 0)
    def _():
        tailx_ref[...] = jnp.zeros_like(tailx_ref)
        tailb_ref[...] = jnp.zeros_like(tailb_ref)
        state_ref[...] = jnp.zeros_like(state_ref)

    xs = _silu(_causal_conv(x_ref[...], tailx_ref, cwx_ref, cbx_ref))
    bcm = _silu(_causal_conv(bc_ref[...], tailb_ref, cwb_ref, cbb_ref))
    dt = _softplus(dt_ref[...] + dtb_ref[...])
    a = -jnp.exp(alog_ref[...])
    row = lax.broadcasted_iota(jnp.int32, (t, t), 0)
    col = lax.broadcasted_iota(jnp.int32, (t, t), 1)
    tril = row >= col
    a_cs = _dot_exact_lhs01(jnp.where(tril, 1.0, 0.0).astype(BF16), dt * a)
    a_last = a_cs[t - 1:t]
    per_head = jnp.concatenate(
        [dt, jnp.exp(a_last - a_cs), jnp.exp(a_cs), jnp.broadcast_to(jnp.exp(a_last), (SUBLANES, LANES))], axis=0)
    spread = _dot_exact_rhs01(per_head, expand_ref[...])
    dt_x, to_end_x, ea_x, cd_x = spread[0:t], spread[t:2 * t], spread[2 * t:3 * t], spread[3 * t:3 * t + 1]
    xdt = xs * dt_x
    xdt_b = xdt.astype(BF16)
    xw_b = (xdt * to_end_x).astype(BF16)
    a_cs_t = a_cs.T

    y_off = []
    for g in range(SSD_GROUPS):
        bm = bcm[:, g * SSD_STATE:(g + 1) * SSD_STATE]
        cm_b = bcm[:, (SSD_GROUPS + g) * SSD_STATE:(SSD_GROUPS + g + 1) * SSD_STATE].astype(BF16)
        cb = lax.dot_general(cm_b, bm.astype(BF16), (((1,), (1,)), ((), ())), preferred_element_type=F32)
        h_prev = state_ref[g]
        y_off.append(jnp.dot(cm_b, h_prev.astype(BF16), preferred_element_type=F32)
                     * ea_x[:, g * gw:(g + 1) * gw])
        st_new = jnp.dot(bm.T.astype(BF16), xw_b[:, g * gw:(g + 1) * gw], preferred_element_type=F32)
        state_ref[g] = h_prev * cd_x[:, g * gw:(g + 1) * gw] + st_new
        for r in range(0, hg, 2):
            pair = []
            for h in (g * hg + r, g * hg + r + 1):
                diff = a_cs[:, h:h + 1] - a_cs_t[h:h + 1, :]
                m = (cb * jnp.exp(jnp.where(tril, diff, -jnp.inf))).astype(BF16)
                pair.append(jnp.dot(m, xdt_b[:, h * SSD_HEAD_DIM:(h + 1) * SSD_HEAD_DIM],
                                    preferred_element_type=F32))
            lo = (g * hg + r) * SSD_HEAD_DIM
            ybuf_ref[:, lo:lo + 2 * SSD_HEAD_DIM] = jnp.concatenate(pair, axis=1)

    y = ybuf_ref[...] + jnp.concatenate(y_off, axis=1) + xs * dskip_ref[...]
    y = y * _silu(z_ref[...])
    outs = []
    for g in range(SSD_GROUPS):
        yg = y[:, g * gw:(g + 1) * gw]
        outs.append(yg * lax.rsqrt(jnp.mean(yg * yg, axis=-1, keepdims=True) + EPS))
    o_ref[...] = (jnp.concatenate(outs, axis=1) * ng_ref[...]).astype(o_ref.dtype)


def ssd_mixer(proj, conv_w, conv_b, dt_bias, a_log, d_skip, norm_g):
    L = proj.shape[0]
    t = SSD_CHUNK
    assert L % t == 0

    def pad_lanes(v):
        return jnp.pad(v, (0, LANES - v.shape[0])).reshape(1, LANES)

    expand = (jnp.arange(SSD_WIDTH)[None, :] // SSD_HEAD_DIM == jnp.arange(LANES)[:, None]).astype(BF16)
    full = lambda shape: pl.BlockSpec(shape, lambda c: (0,) * len(shape))
    return pl.pallas_call(
        _ssd_kernel,
        grid=(L // t,),
        in_specs=[pl.BlockSpec((t, SSD_WIDTH), lambda c: (c, COL_Z // SSD_WIDTH)),
                  pl.BlockSpec((t, SSD_WIDTH), lambda c: (c, COL_X // SSD_WIDTH)),
                  pl.BlockSpec((t, SSD_BC), lambda c: (c, COL_BC // SSD_BC)),
                  pl.BlockSpec((t, LANES), lambda c: (c, COL_DT // LANES)),
                  full((SSD_CONV, SSD_WIDTH)), full((SSD_CONV, SSD_BC)),
                  full((1, SSD_WIDTH)), full((1, SSD_BC)),
                  full((1, LANES)), full((1, LANES)), full((1, SSD_WIDTH)), full((1, SSD_WIDTH)),
                  full((LANES, SSD_WIDTH))],
        out_specs=pl.BlockSpec((t, SSD_WIDTH), lambda c: (c, 0)),
        out_shape=jax.ShapeDtypeStruct((L, SSD_WIDTH), BF16),
        scratch_shapes=[pltpu.VMEM((SUBLANES, SSD_WIDTH), F32), pltpu.VMEM((SUBLANES, SSD_BC), F32),
                        pltpu.VMEM((SSD_GROUPS, SSD_STATE, SSD_WIDTH // SSD_GROUPS), F32),
                        pltpu.VMEM((t, SSD_WIDTH), F32)],
        compiler_params=_params("arbitrary"),
        name="ssd_mixer",
    )(proj, proj, proj, proj,
      conv_w[:, :SSD_WIDTH], conv_w[:, SSD_WIDTH:], conv_b[:SSD_WIDTH].reshape(1, -1),
      conv_b[SSD_WIDTH:].reshape(1, -1), pad_lanes(dt_bias), pad_lanes(a_log),
      jnp.repeat(d_skip, SSD_HEAD_DIM).reshape(1, -1), norm_g.reshape(1, -1), expand)


def _moba_prep_kernel(q_ref, k_ref, v_ref, qa_ref, ka_ref, va_ref, kmean_ref):
    own = pl.program_id(0)
    t = MOBA_BLOCK
    dh = MOBA_HEAD_DIM
    nbl = MOBA_MAX_BLOCKS

    @pl.when(own == 0)
    def _():
        kmean_ref[...] = jnp.zeros_like(kmean_ref)

    lane = lax.broadcasted_iota(jnp.int32, (t, LANES), 1)
    onehot = jnp.where(lane == dh + own, 1.0, 0.0)
    blk = lax.broadcasted_iota(jnp.int32, (nbl, t), 0)
    q_t = q_ref[...].T
    k = k_ref[...]
    k_mean = jnp.mean(k, axis=0, keepdims=True)
    va_ref[:, 0] = v_ref[...].T.reshape(MOBA_HEADS, dh, t).astype(BF16)
    for h in range(MOBA_HEADS):
        qh_t = q_t[h * dh:(h + 1) * dh]
        gate = jnp.dot(kmean_ref[h], qh_t, precision=HIGHEST, preferred_element_type=F32)
        gate = jnp.where(blk < own, gate, -jnp.inf)
        sel = blk >= own
        for _ in range(MOBA_TOPK):
            m = jnp.max(gate, axis=0, keepdims=True)
            idx = jnp.min(jnp.where(gate == m, blk, nbl), axis=0, keepdims=True)
            sel = sel | ((blk == idx) & (m > -jnp.inf))
            gate = jnp.where(blk == idx, -jnp.inf, gate)
        offs_t = jnp.where(sel, 0.0, NEG_BIG)
        qa_ref[h] = jnp.concatenate([qh_t * MOBA_Q_SCALE, offs_t, jnp.zeros((LANES - dh - nbl, t), F32)],
                                    axis=0).astype(BF16)
        pair = k[:, (h // 2) * LANES:(h // 2 + 1) * LANES]
        if h % 2:
            pair = pltpu.roll(pair, dh, axis=1)
        ka_ref[h] = jnp.where(lane < dh, pair, onehot).astype(BF16)
        kmean_ref[h, pl.ds(own, 1), :] = k_mean[:, h * dh:(h + 1) * dh]


def moba_prep(proj):
    L = proj.shape[0]
    t = MOBA_BLOCK
    assert L % t == 0 and L // t <= MOBA_MAX_BLOCKS
    return pl.pallas_call(
        _moba_prep_kernel,
        grid=(L // t,),
        in_specs=[pl.BlockSpec((t, MOBA_WIDTH), lambda i: (i, COL_MQ // MOBA_WIDTH)),
                  pl.BlockSpec((t, MOBA_WIDTH), lambda i: (i, COL_MK // MOBA_WIDTH)),
                  pl.BlockSpec((t, MOBA_WIDTH), lambda i: (i, COL_MV // MOBA_WIDTH))],
        out_specs=[pl.BlockSpec((MOBA_HEADS, LANES, t), lambda i: (0, 0, i)),
                   pl.BlockSpec((MOBA_HEADS, t, LANES), lambda i: (0, i, 0)),
                   pl.BlockSpec((MOBA_HEADS, 1, MOBA_HEAD_DIM, t), lambda i: (0, i, 0, 0))],
        out_shape=[jax.ShapeDtypeStruct((MOBA_HEADS, LANES, L), BF16),
                   jax.ShapeDtypeStruct((MOBA_HEADS, L, LANES), BF16),
                   jax.ShapeDtypeStruct((MOBA_HEADS, L // t, MOBA_HEAD_DIM, t), BF16)],
        scratch_shapes=[pltpu.VMEM((MOBA_HEADS, MOBA_MAX_BLOCKS, MOBA_HEAD_DIM), F32)],
        compiler_params=_params("arbitrary"),
        name="moba_prep",
    )(proj, proj, proj)


def _col_reduce(x, op):
    while x.shape[0] > SUBLANES:
        half = x.shape[0] // 2
        x = op(x[:half], x[half:])
    return jnp.max(x, axis=0, keepdims=True) if op is jnp.maximum else jnp.sum(x, axis=0, keepdims=True)


def _attention_kernel(near_ref, q_ref, k_ref, v_ref, posr_ref, posc_ref, tbl_ref, g_ref, lam_ref, o_ref,
                      m_ref, l_ref, al_ref, acc_ref, sa_ref, sb_ref, pa_ref, pb_ref, vt_ref, *, moba, lambda_init):
    hp = pl.program_id(0)
    qi = pl.program_id(1)
    nq = pl.num_programs(1)
    t = ATT_T
    m_ref[...] = jnp.full_like(m_ref, -jnp.inf)
    l_ref[...] = jnp.zeros_like(l_ref)
    acc_ref[...] = jnp.zeros_like(acc_ref)
    if moba:
        q_t = [q_ref[0], q_ref[1]]
        heads = [2 * hp, 2 * hp + 1]
    else:
        q = (q_ref[...].astype(F32) * DIFF_Q_SCALE).T.astype(BF16)
        half = lax.broadcasted_iota(jnp.int32, q.shape, 0) < DIFF_QK_DIM
        q_t = [jnp.where(half, q, jnp.zeros_like(q)), jnp.where(half, jnp.zeros_like(q), q)]
        heads = [MOBA_HEADS + hp]

        @pl.when(qi == 0)
        def _():
            def transpose_tile(i, carry):
                vt_ref[i] = v_ref[pl.ds(pl.multiple_of(i * t, t), t), :].astype(F32).T.astype(BF16)
                return carry

            lax.fori_loop(0, nq, transpose_tile, 0)
    trow = [tbl_ref[pl.ds(hd, 1), :] for hd in heads]
    posq = posr_ref[pl.ds(qi, 1), :]

    def scores(ki, dst_ref):
        rows = pl.ds(pl.multiple_of(ki * t, t), t)
        for s in range(2):
            k = k_ref[s, rows, :] if moba else k_ref[rows, :]
            dst_ref[s] = jnp.dot(k, q_t[s], preferred_element_type=F32)

    def accumulate(ki, p_ref):
        for s in range(2):
            v_t = v_ref[s, ki] if moba else vt_ref[ki]
            pv = jnp.dot(v_t, p_ref[s], preferred_element_type=F32)
            acc_ref[s] = al_ref[s] * acc_ref[s] + pv

    scores(0, sa_ref)
    pb_ref[...] = jnp.zeros_like(pb_ref)
    al_ref[...] = jnp.ones_like(al_ref)

    def tile(ki, general, cur_ref, nxt_ref, p_ref, p_prev_ref):
        scores(jnp.minimum(ki + 1, qi), nxt_ref)
        accumulate(jnp.maximum(ki - 1, 0), p_prev_ref)
        ch = ATT_CHUNK
        mx = [None, None]
        for r in range(0, t, ch):
            if general:
                dist = jnp.clip(posq - posc_ref[ki, r:r + ch, :], 0, LANES - 1)
                bias = [jnp.concatenate(
                    [jnp.take_along_axis(jnp.broadcast_to(tr, (ch, LANES)), dist[:, j * LANES:(j + 1) * LANES],
                                         axis=1) for j in range(t // LANES)], axis=1) for tr in trow]
                key = lax.broadcasted_iota(jnp.int32, (ch, t), 0) + r
                qry = lax.broadcasted_iota(jnp.int32, (ch, t), 1)
                causal = key + ki * t <= qry + qi * t
            for s in range(2):
                blk = cur_ref[s, r:r + ch, :]
                if general:
                    blk = jnp.where(causal, blk + bias[s if moba else 0], -jnp.inf)
                    cur_ref[s, r:r + ch, :] = blk
                mx[s] = blk if mx[s] is None else jnp.maximum(mx[s], blk)
        for s in range(2):
            m_old = m_ref[s]
            m_tile = _col_reduce(mx[s], jnp.maximum)
            if general:
                m_new = jnp.maximum(m_old, m_tile)
                shift = m_new
            else:
                c = trow[s if moba else 0][:, LANES - 1:LANES]
                m_new = jnp.maximum(m_old, m_tile + c)
                shift = m_new - c
            alpha = jnp.exp2(m_old - m_new)
            sm = None
            for r in range(0, t, ch):
                p = jnp.exp2(cur_ref[s, r:r + ch, :] - shift)
                p_ref[s, r:r + ch, :] = p.astype(BF16)
                sm = p if sm is None else sm + p
            l_ref[s] = alpha * l_ref[s] + _col_reduce(sm, jnp.add)
            al_ref[s] = alpha
            m_ref[s] = m_new

    def step(ki, *bufs):
        flag = near_ref[qi * nq + ki]

        @pl.when(flag == 0)
        def _():
            tile(ki, False, *bufs)

        @pl.when(flag != 0)
        def _():
            tile(ki, True, *bufs)

    def pair(j, carry):
        step(2 * j, sa_ref, sb_ref, pa_ref, pb_ref)
        step(2 * j + 1, sb_ref, sa_ref, pb_ref, pa_ref)
        return carry

    lax.fori_loop(0, (qi + 1) // 2, pair, 0)

    @pl.when(qi % 2 == 0)
    def _():
        step(qi, sa_ref, sb_ref, pa_ref, pb_ref)
        accumulate(qi, pa_ref)

    @pl.when(qi % 2 == 1)
    def _():
        accumulate(qi, pb_ref)

    if moba:
        outs = []
        for s in range(2):
            o = acc_ref[s] / l_ref[s]
            ms = jnp.sum(o * o, axis=0, keepdims=True) * (1.0 / MOBA_HEAD_DIM)
            outs.append(o * lax.rsqrt(ms + EPS) * g_ref[s])
        y_t = jnp.concatenate(outs, axis=0)
    else:
        lp = lam_ref[...]
        lam = (jnp.exp(jnp.sum(lp[0:1] * lp[1:2], axis=1, keepdims=True))
               - jnp.exp(jnp.sum(lp[2:3] * lp[3:4], axis=1, keepdims=True)) + lambda_init)
        o = acc_ref[0] / l_ref[0] - lam * (acc_ref[1] / l_ref[1])
        ms = jnp.sum(o * o, axis=0, keepdims=True) * (1.0 / DIFF_V_DIM)
        y_t = (o * lax.rsqrt(ms + EPS) * g_ref[0]) * (1.0 - lambda_init)
    o_ref[...] = y_t.T.astype(o_ref.dtype)


def _attention_call(kernel, steps, dv, L, near, in_specs, args, name, vt_scratch=False):
    t = ATT_T
    full = lambda a: pl.BlockSpec(a.shape, lambda h, i, nr: (0,) * a.ndim)
    grid_spec = pltpu.PrefetchScalarGridSpec(
        num_scalar_prefetch=1,
        grid=(steps, L // t),
        in_specs=in_specs + [full(a) for a in args[len(in_specs):]],
        out_specs=pl.BlockSpec((t, LANES), lambda h, i, nr: (i, h)),
        scratch_shapes=[pltpu.VMEM((2, 1, t), F32), pltpu.VMEM((2, 1, t), F32), pltpu.VMEM((2, 1, t), F32),
                        pltpu.VMEM((2, dv, t), F32), pltpu.VMEM((2, t, t), F32), pltpu.VMEM((2, t, t), F32),
                        pltpu.VMEM((2, t, t), BF16), pltpu.VMEM((2, t, t), BF16),
                        pltpu.VMEM((L // t if vt_scratch else 1, dv, t), BF16)],
    )
    return pl.pallas_call(kernel, grid_spec=grid_spec, out_shape=jax.ShapeDtypeStruct((L, steps * LANES), BF16),
                          compiler_params=_params("parallel", "arbitrary"), name=name)(near, *args)


def moba_attention(qa_t, ka, va_t, near, pos_rows, pos_cols, table, norm_g):
    heads, L, _ = ka.shape
    t = ATT_T
    kernel = functools.partial(_attention_kernel, moba=True, lambda_init=None)
    g = norm_g.reshape(heads, MOBA_HEAD_DIM, 1)
    return _attention_call(
        kernel, heads // 2, MOBA_HEAD_DIM, L, near,
        [pl.BlockSpec((2, LANES, t), lambda h, i, nr: (h, 0, i)),
         pl.BlockSpec((2, L, LANES), lambda h, i, nr: (h, 0, 0)),
         pl.BlockSpec((2, L // t, MOBA_HEAD_DIM, t), lambda h, i, nr: (h, 0, 0, 0)),
         pl.BlockSpec(pos_rows.shape, lambda h, i, nr: (0, 0)),
         pl.BlockSpec(pos_cols.shape, lambda h, i, nr: (0, 0, 0)),
         pl.BlockSpec(table.shape, lambda h, i, nr: (0, 0)),
         pl.BlockSpec((2, MOBA_HEAD_DIM, 1), lambda h, i, nr: (h, 0, 0))],
        [qa_t, ka, va_t, pos_rows, pos_cols, table, g, jnp.zeros((4, DIFF_QK_DIM), F32)], "moba_attention")


def diff_attention(proj16, near, pos_rows, pos_cols, table, lam_params, subln_g, lambda_init):
    L = proj16.shape[0]
    t = ATT_T
    kernel = functools.partial(_attention_kernel, moba=False, lambda_init=lambda_init)
    return _attention_call(
        kernel, DIFF_HEADS, DIFF_V_DIM, L, near,
        [pl.BlockSpec((t, LANES), lambda h, i, nr: (i, COL_DQ // LANES + h)),
         pl.BlockSpec((L, LANES), lambda h, i, nr: (0, COL_DK // LANES + h)),
         pl.BlockSpec((L, LANES), lambda h, i, nr: (0, COL_DV // LANES + h))],
        [proj16, proj16, proj16, pos_rows, pos_cols, table, subln_g.reshape(1, DIFF_V_DIM, 1), lam_params],
        "diff_attention", vt_scratch=True)


def _rel_bucket(dist):
    n = jnp.maximum(dist, 0)
    max_exact = REL_BUCKETS // 2
    nf = jnp.maximum(n, 1).astype(F32)
    large = max_exact + (jnp.log(nf / max_exact) / math.log(REL_MAX_DIST / max_exact)
                         * (REL_BUCKETS - max_exact)).astype(jnp.int32)
    return jnp.where(n < max_exact, n, jnp.minimum(large, REL_BUCKETS - 1))


def attention_tables(positions, rel_bias):
    L = positions.shape[0]
    t = ATT_T
    buckets = _rel_bucket(jnp.arange(LANES, dtype=jnp.int32))
    table = rel_bias[buckets].T * LOG2E
    pos_rows = positions.reshape(L // t, t)
    lo, hi = jnp.min(pos_rows, axis=1), jnp.max(pos_rows, axis=1)
    near = (lo[:, None] - hi[None, :] < LANES) | jnp.eye(L // t, dtype=bool)
    return table, pos_rows, positions.reshape(L // t, t, 1), near.astype(jnp.int32).reshape(-1)


def _router_kernel(x_ref, g_ref, sc_ref, sh_ref, rw_ref, h_ref, comb_ref, rank_ref, rank_t_ref, cum_ref, total_ref,
                   cnt_ref):
    t = TOK_T

    @pl.when(pl.program_id(0) == 0)
    def _():
        cnt_ref[...] = jnp.zeros_like(cnt_ref)

    h = _norm_mod(x_ref[...], g_ref[...], sc_ref[...], sh_ref[...])
    h_ref[...] = h.astype(BF16)
    lane = lax.broadcasted_iota(jnp.int32, (t, LANES), 1)
    logits = jnp.dot(h, rw_ref[...], precision=HIGHEST, preferred_element_type=F32)
    logits = jnp.where(lane < N_EXPERTS, logits, -jnp.inf)
    m1 = jnp.max(logits, axis=1, keepdims=True)
    i1 = jnp.min(jnp.where(logits == m1, lane, LANES), axis=1, keepdims=True)
    rest = jnp.where(lane == i1, -jnp.inf, logits)
    m2 = jnp.max(rest, axis=1, keepdims=True)
    i2 = jnp.min(jnp.where(rest == m2, lane, LANES), axis=1, keepdims=True)
    e2 = jnp.exp(m2 - m1)
    denom = 1.0 + e2
    comb_ref[...] = jnp.where(lane == i1, 1.0 / denom, 0.0) + jnp.where(lane == i2, e2 / denom, 0.0)
    sel = jnp.where((lane == i1) | (lane == i2), 1.0, 0.0)
    row = lax.broadcasted_iota(jnp.int32, (t, t), 0)
    col = lax.broadcasted_iota(jnp.int32, (t, t), 1)
    before = jnp.dot(jnp.where(row > col, 1.0, 0.0).astype(BF16), sel.astype(BF16), preferred_element_type=F32)
    carry = cnt_ref[...]
    cum_ref[0] = carry
    rank = jnp.where(sel > 0.0, before + carry, -1.0)
    rank_ref[...] = rank
    rank_t_ref[...] = rank.T
    carry = carry + jnp.sum(sel, axis=0, keepdims=True)
    cnt_ref[...] = carry
    total_ref[...] = carry


def moe_router(x, g, sc, sh, router_w):
    L, d = x.shape
    t = TOK_T
    rw = jnp.pad(router_w, ((0, 0), (0, LANES - N_EXPERTS)))
    vec = pl.BlockSpec((1, d), lambda i: (0, 0))
    tok = pl.BlockSpec((t, LANES), lambda i: (i, 0))
    return pl.pallas_call(
        _router_kernel,
        grid=(L // t,),
        in_specs=[pl.BlockSpec((t, d), lambda i: (i, 0)), vec, vec, vec,
                  pl.BlockSpec((d, LANES), lambda i: (0, 0))],
        out_specs=[pl.BlockSpec((t, d), lambda i: (i, 0)), tok, tok,
                   pl.BlockSpec((LANES, t), lambda i: (0, i)),
                   pl.BlockSpec((1, 1, LANES), lambda i: (i, 0, 0)),
                   pl.BlockSpec((1, LANES), lambda i: (0, 0))],
        out_shape=[jax.ShapeDtypeStruct((L, d), BF16), jax.ShapeDtypeStruct((L, LANES), F32),
                   jax.ShapeDtypeStruct((L, LANES), F32), jax.ShapeDtypeStruct((LANES, L), F32),
                   jax.ShapeDtypeStruct((L // t, 1, LANES), F32), jax.ShapeDtypeStruct((1, LANES), F32)],
        scratch_shapes=[pltpu.VMEM((1, LANES), F32)],
        compiler_params=_params("arbitrary"),
        name="moe_router",
    )(x, g, sc, sh, rw)


def _item_lists(hit, n_items):
    rows, cols = hit.shape
    running = jnp.cumsum(hit.reshape(-1).astype(jnp.int32))
    n_real = running[-1]
    k = jnp.arange(n_items, dtype=jnp.int32)
    real = k < n_real
    idx = jnp.searchsorted(running, jnp.minimum(k + 1, n_real), side="left", method="compare_all")
    idx = jnp.minimum(idx, rows * cols - 1).astype(jnp.int32)
    r, c = idx // cols, idx % cols
    prev_r = jnp.concatenate([jnp.full((1,), -1, jnp.int32), r[:-1]])
    next_r = jnp.concatenate([r[1:], jnp.full((1,), -1, jnp.int32)])
    first = real & (r != prev_r)
    last = real & ((r != next_r) | (k == n_real - 1))
    flags = real.astype(jnp.int32) + 2 * first.astype(jnp.int32) + 4 * last.astype(jnp.int32)
    return r, c, flags


def moe_plan(rank_t, cum, total, L):
    tm, tb = MOE_TM, TOK_T
    nb = L // tb
    n_tiles = 2 * L // tm + N_EXPERTS
    counts = total[0, :N_EXPERTS].astype(jnp.int32)
    tiles_e = (counts + tm - 1) // tm
    tile_end = jnp.cumsum(tiles_e)
    tile_start = tile_end - tiles_e
    tid = jnp.arange(n_tiles, dtype=jnp.int32)
    tile_valid = tid < tile_end[-1]
    tile_expert = jnp.minimum(jnp.searchsorted(tile_end, tid, side="right", method="compare_all"),
                              N_EXPERTS - 1).astype(jnp.int32)
    local_row = (tid - tile_start[tile_expert]) * tm
    r = rank_t[:N_EXPERTS].astype(jnp.int32)
    dest = jnp.where(r >= 0, r + (tile_start * tm)[:, None], -1)
    cum_i = cum[:, 0, :N_EXPERTS].astype(jnp.int32)
    cum_next = jnp.concatenate([cum_i[1:], counts[None, :]], axis=0)
    lo = cum_i[:, tile_expert].T
    hi = cum_next[:, tile_expert].T
    hit = tile_valid[:, None] & (lo < (local_row + tm)[:, None]) & (hi > local_row[:, None])
    n_items = n_tiles + N_EXPERTS * nb
    pad_hit = hit | ((~tile_valid)[:, None] & (jnp.arange(nb) == 0)[None, :])
    gather_items = _item_lists(pad_hit, n_items)
    kb, tt, fl = _item_lists(hit.T, n_items)
    return dict(tile_expert=tile_expert, tile_valid=tile_valid.astype(jnp.int32),
                dest_rows=dest, segment_start=(tile_start * tm).astype(jnp.int32),
                gather_items=gather_items, combine_items=(tt, kb, fl), n_tiles=n_tiles, n_items=n_items)


def _moe_gather_kernel(it_ref, ib_ref, if_ref, te_ref, dest_ref, h_ref, o_ref, acc_ref):
    i = pl.program_id(0)
    flag = if_ref[i]
    tile = it_ref[i]
    tm = MOE_TM

    @pl.when((flag & 2) != 0)
    def _():
        acc_ref[...] = jnp.zeros_like(acc_ref)

    @pl.when((flag & 1) != 0)
    def _():
        d = dest_ref[pl.ds(te_ref[tile], 1), :]
        rows = tile * tm + lax.broadcasted_iota(jnp.int32, (tm, 1), 0)
        onehot = jnp.where(d == rows, 1.0, 0.0).astype(BF16)
        acc_ref[...] += jnp.dot(onehot, h_ref[...], preferred_element_type=F32)

    @pl.when((flag & 4) != 0)
    def _():
        o_ref[...] = acc_ref[...].astype(o_ref.dtype)


def moe_gather(h, plan):
    L, d = h.shape
    tm, tb = MOE_TM, TOK_T
    it, ib, fl = plan["gather_items"]
    grid_spec = pltpu.PrefetchScalarGridSpec(
        num_scalar_prefetch=4,
        grid=(plan["n_items"],),
        in_specs=[pl.BlockSpec((N_EXPERTS, tb), lambda i, it, ib, fl, te: (0, ib[i])),
                  pl.BlockSpec((tb, d), lambda i, it, ib, fl, te: (ib[i], 0))],
        out_specs=pl.BlockSpec((tm, d), lambda i, it, ib, fl, te: (it[i], 0)),
        scratch_shapes=[pltpu.VMEM((tm, d), F32)],
    )
    return pl.pallas_call(
        _moe_gather_kernel,
        grid_spec=grid_spec,
        out_shape=jax.ShapeDtypeStruct((plan["n_tiles"] * tm, d), BF16),
        compiler_params=_params("arbitrary"),
        name="moe_gather",
    )(it, ib, fl, plan["tile_expert"], plan["dest_rows"], h)


def _moe_combine_kernel(it_ref, ib_ref, if_ref, te_ref, ts_ref, rank_ref, w_ref, ys_ref, x_ref, gate_ref, fg_ref,
                        o_ref, acc_ref, *, final_norm):
    i = pl.program_id(0)
    flag = if_ref[i]
    tile = it_ref[i]
    tm = MOE_TM

    @pl.when((flag & 2) != 0)
    def _():
        acc_ref[...] = jnp.zeros_like(acc_ref)

    @pl.when((flag & 1) != 0)
    def _():
        e = te_ref[tile]
        mine = lax.broadcasted_iota(jnp.int32, rank_ref.shape, 1) == e
        rank = jnp.sum(jnp.where(mine, rank_ref[...], 0.0), axis=1, keepdims=True)
        w = jnp.sum(jnp.where(mine, w_ref[...], 0.0), axis=1, keepdims=True)
        first = tile * tm - ts_ref[e]
        cols = (first + lax.broadcasted_iota(jnp.int32, (1, tm), 1)).astype(F32)
        onehot = jnp.where(rank == cols, 1.0, 0.0).astype(BF16)
        acc_ref[...] += w * jnp.dot(onehot, ys_ref[...], preferred_element_type=F32)

    @pl.when((flag & 4) != 0)
    def _():
        y = x_ref[...] + gate_ref[...] * acc_ref[...]
        if final_norm:
            y = y * lax.rsqrt(jnp.mean(y * y, axis=-1, keepdims=True) + EPS) * fg_ref[...]
        o_ref[...] = y


def moe_combine(ys, comb, rank, x, gate, plan, final_g=None):
    L, d = x.shape
    final_norm = final_g is not None
    if not final_norm:
        final_g = jnp.ones((1, d), F32)
    tm, tb = MOE_TM, TOK_T
    it, ib, fl = plan["combine_items"]
    tok = lambda i, it, ib, *_: (ib[i], 0)
    fixed = lambda i, *_: (0, 0)
    grid_spec = pltpu.PrefetchScalarGridSpec(
        num_scalar_prefetch=5,
        grid=(plan["n_items"],),
        in_specs=[pl.BlockSpec((tb, LANES), tok), pl.BlockSpec((tb, LANES), tok),
                  pl.BlockSpec((tm, d), lambda i, it, *_: (it[i], 0)),
                  pl.BlockSpec((tb, d), tok), pl.BlockSpec((1, d), fixed), pl.BlockSpec((1, d), fixed)],
        out_specs=pl.BlockSpec((tb, d), tok),
        scratch_shapes=[pltpu.VMEM((tb, d), F32)],
    )
    return pl.pallas_call(
        functools.partial(_moe_combine_kernel, final_norm=final_norm),
        grid_spec=grid_spec,
        out_shape=jax.ShapeDtypeStruct((L, d), F32),
        compiler_params=_params("arbitrary"),
        name="moe_combine",
    )(it, ib, fl, plan["tile_expert"], plan["segment_start"], rank, comb, ys, x, gate, final_g)


def moe_ffn(x, g, sc, sh, gate, router_w, w1, w3, w2, final_g=None):
    L = x.shape[0]
    h, comb, rank, rank_t, cum, total = moe_router(x, g, sc, sh, router_w)
    plan = moe_plan(rank_t, cum, total, L)
    xs = moe_gather(h, plan)
    wplan = weight_plan(plan["tile_expert"], plan["tile_valid"])
    act = grouped_matmul(xs, [w1, w3], wplan, MOE_TM, 1024, BF16, name="swiglu_up")
    ys = grouped_matmul(act, [w2], wplan, MOE_TM, 512, BF16, name="swiglu_down")
    return moe_combine(ys, comb, rank, x, gate, plan, final_g)


def dense_ffn(x, g, sc, sh, gate, w1, w3, w2):
    L = x.shape[0]
    tm = 512
    h = norm_modulate(x, g, sc, sh, BF16)
    wplan = weight_plan(jnp.zeros((L // tm,), jnp.int32), jnp.ones((L // tm,), jnp.int32))
    act = grouped_matmul(h, [w1[None], w3[None]], wplan, tm, 512, BF16, name="swiglu_up")
    return grouped_matmul(act, [w2[None]], wplan, tm, 512, F32, x=x, gate=gate, name="swiglu_down")


def _permute_in_proj_kernel(w_ref, o_ref):
    w = w_ref[...]
    rows = w.shape[0]
    o_ref[:, 0:ORIG_DT] = w[:, 0:ORIG_DT].astype(BF16)
    o_ref[:, ORIG_DT:COL_DT] = w[:, ORIG_DT + SSD_HEADS:].astype(BF16)
    o_ref[:, COL_DT:] = jnp.concatenate([w[:, ORIG_DT:ORIG_DT + SSD_HEADS],
                                         jnp.zeros((rows, LANES - SSD_HEADS), F32)], axis=1).astype(BF16)


def _permute_in_proj(w_in, layer):
    _, d, n = w_in.shape
    tr = 256
    return pl.pallas_call(
        _permute_in_proj_kernel,
        grid=(d // tr,),
        in_specs=[pl.BlockSpec((None, tr, n), lambda i: (layer, i, 0))],
        out_specs=pl.BlockSpec((tr, PROJ_W), lambda i: (i, 0)),
        out_shape=jax.ShapeDtypeStruct((d, PROJ_W), BF16),
        compiler_params=_params("parallel"),
        name="permute_in_proj",
    )(w_in)


def kernel(x, c, positions, rel_bias, w_ada, b_ada, norm_mix_g, w_in, conv_w, conv_b, dt_bias, a_log, d_skip, ssd_norm_g, moba_norm_g, diff_lambda, diff_subln_g, w_out, norm_ffn_g, dense_w1, dense_w3, dense_w2, router_w, expert_w1, expert_w3, expert_w2, final_g):
    batch, L, d = x.shape
    assert batch == 1 and d == D_MODEL
    x = x[0]
    mod = ada_modulation(c, w_ada, b_ada)
    table, pos_rows, pos_cols, near = attention_tables(positions[0], rel_bias)
    row = lambda v: v.reshape(1, -1)
    for layer in range(DEPTH):
        lambda_init = 0.8 - 0.6 * math.exp(-0.3 * layer)
        shift1, scale1, gate1, shift2, scale2, gate2 = (mod[layer, :, j * d:(j + 1) * d] for j in range(6))
        h = norm_modulate(x, row(norm_mix_g[layer]), scale1, shift1, BF16)
        proj, proj16 = in_projection(h, _permute_in_proj(w_in, layer))
        y_ssd = ssd_mixer(proj, conv_w[layer], conv_b[layer], dt_bias[layer], a_log[layer], d_skip[layer],
                          ssd_norm_g[layer])
        qa, ka, va = moba_prep(proj)
        y_moba = moba_attention(qa, ka, va, near, pos_rows, pos_cols, table, moba_norm_g[layer])
        y_diff = diff_attention(proj16, near, pos_rows, pos_cols, table, diff_lambda[layer],
                                diff_subln_g[layer], lambda_init)
        x = out_projection(y_ssd, y_moba, y_diff, w_out, layer, x, gate1)
        i = layer // 2
        g2 = row(norm_ffn_g[layer])
        if layer % 2 == 0:
            x = dense_ffn(x, g2, scale2, shift2, gate2, dense_w1[i], dense_w3[i], dense_w2[i])
        else:
            x = moe_ffn(x, g2, scale2, shift2, gate2, router_w[i], expert_w1[i], expert_w3[i], expert_w2[i],
                        final_g=row(final_g) if layer == DEPTH - 1 else None)
    if (DEPTH - 1) % 2 == 0:
        zero = jnp.zeros((1, d), F32)
        x = norm_modulate(x, row(final_g), zero, zero, F32)
    return x[None]
```

```python
import functools
import math

import jax
import jax.numpy as jnp
from jax import lax
from jax.experimental import pallas as pl
from jax.experimental.pallas import tpu as pltpu

F32 = jnp.float32
BF16 = jnp.bfloat16
HIGHEST = lax.Precision.HIGHEST

D_MODEL = 2048
DEPTH = 2
SSD_HEADS = 16
SSD_HEAD_DIM = 64
SSD_WIDTH = SSD_HEADS * SSD_HEAD_DIM
SSD_GROUPS = 2
SSD_STATE = 128
SSD_CONV = 4
SSD_CHUNK = 256
SSD_BC = 2 * SSD_GROUPS * SSD_STATE
MOBA_HEADS = 8
MOBA_HEAD_DIM = 64
MOBA_WIDTH = MOBA_HEADS * MOBA_HEAD_DIM
MOBA_BLOCK = 256
MOBA_TOPK = 3
DIFF_HEADS = 4
DIFF_QK_DIM = 64
DIFF_V_DIM = 128
DIFF_WIDTH = DIFF_HEADS * DIFF_V_DIM
REL_BUCKETS = 32
REL_MAX_DIST = 128
D_FF_DENSE = 5632
N_EXPERTS = 8
D_FF_EXPERT = 7168
EPS = 1e-6

LANES = 128
SUBLANES = 8
VMEM_LIMIT = 56 * 1024 * 1024

COL_Z = 0
COL_X = SSD_WIDTH
COL_BC = COL_X + SSD_WIDTH
COL_MQ = COL_BC + SSD_BC
COL_MK = COL_MQ + MOBA_WIDTH
COL_MV = COL_MK + MOBA_WIDTH
COL_DQ = COL_MV + MOBA_WIDTH
COL_DK = COL_DQ + DIFF_WIDTH
COL_DV = COL_DK + DIFF_WIDTH
COL_DT = COL_DV + DIFF_WIDTH
PROJ_W = COL_DT + LANES
ORIG_DT = SSD_WIDTH + SSD_WIDTH + SSD_BC

ATT_T = 256
ATT_CHUNK = 64
TOK_T = 512
MOE_TM = 256
NEG_BIG = -1e9
MOBA_MAX_BLOCKS = 32
LOG2E = math.log2(math.e)
MOBA_Q_SCALE = MOBA_HEAD_DIM ** -0.5 * LOG2E
DIFF_Q_SCALE = DIFF_QK_DIM ** -0.5 * LOG2E


def _silu(x):
    return x * (1.0 / (1.0 + jnp.exp(-x)))


def _softplus(x):
    return jnp.maximum(x, 0.0) + jnp.log1p(jnp.exp(-jnp.abs(x)))


def _params(*sem):
    return pltpu.CompilerParams(dimension_semantics=sem, vmem_limit_bytes=VMEM_LIMIT)


def _ada_kernel(c_ref, w_ref, b_ref, o_ref):
    ca = _silu(c_ref[...])
    o_ref[0] = jnp.sum(ca * w_ref[0], axis=0, keepdims=True) + b_ref[0]


def ada_modulation(c, w_ada, b_ada):
    depth, d, n = w_ada.shape
    tn = 2048
    return pl.pallas_call(
        _ada_kernel,
        grid=(depth, n // tn),
        in_specs=[pl.BlockSpec((d, 1), lambda l, j: (0, 0)),
                  pl.BlockSpec((1, d, tn), lambda l, j: (l, 0, j)),
                  pl.BlockSpec((1, 1, tn), lambda l, j: (l, 0, j))],
        out_specs=pl.BlockSpec((1, 1, tn), lambda l, j: (l, 0, j)),
        out_shape=jax.ShapeDtypeStruct((depth, 1, n), F32),
        compiler_params=_params("parallel", "parallel"),
        name="ada_modulation",
    )(c.reshape(d, 1), w_ada, b_ada.reshape(depth, 1, n))


def _norm_mod(x, g, sc, sh):
    ms = jnp.mean(x * x, axis=-1, keepdims=True)
    return (x * lax.rsqrt(ms + EPS) * g) * (1.0 + sc) + sh


def _norm_kernel(x_ref, g_ref, sc_ref, sh_ref, o_ref):
    o_ref[...] = _norm_mod(x_ref[...], g_ref[...], sc_ref[...], sh_ref[...]).astype(o_ref.dtype)


def norm_modulate(x, g, sc, sh, out_dtype):
    L, d = x.shape
    tm = 1024
    vec = pl.BlockSpec((1, d), lambda i: (0, 0))
    return pl.pallas_call(
        _norm_kernel,
        grid=(L // tm,),
        in_specs=[pl.BlockSpec((tm, d), lambda i: (i, 0)), vec, vec, vec],
        out_specs=pl.BlockSpec((tm, d), lambda i: (i, 0)),
        out_shape=jax.ShapeDtypeStruct((L, d), out_dtype),
        compiler_params=_params("parallel"),
        name="norm_modulate",
    )(x, g, sc, sh)


def _inproj_kernel(a_ref, w_ref, o32_ref, o16_ref):
    acc = jnp.dot(a_ref[...], w_ref[...], preferred_element_type=F32)
    o32_ref[...] = acc
    o16_ref[...] = acc.astype(BF16)


def in_projection(h, w):
    L, k = h.shape
    n = w.shape[1]
    tm, tn = 1024, 1920
    return pl.pallas_call(
        _inproj_kernel,
        grid=(n // tn, L // tm),
        in_specs=[pl.BlockSpec((tm, k), lambda j, i: (i, 0)),
                  pl.BlockSpec((k, tn), lambda j, i: (0, j))],
        out_specs=[pl.BlockSpec((tm, tn), lambda j, i: (i, j)),
                   pl.BlockSpec((tm, tn), lambda j, i: (i, j))],
        out_shape=[jax.ShapeDtypeStruct((L, n), F32), jax.ShapeDtypeStruct((L, n), BF16)],
        compiler_params=_params("parallel", "parallel"),
        name="in_projection",
    )(h, w)


def _outproj_kernel(ys_ref, ym_ref, yd_ref, ws_ref, wm_ref, wd_ref, x_ref, gate_ref, o_ref, wb_ref):
    @pl.when(pl.program_id(1) == 0)
    def _():
        wb_ref[0:SSD_WIDTH] = ws_ref[...].astype(BF16)
        wb_ref[SSD_WIDTH:SSD_WIDTH + MOBA_WIDTH] = wm_ref[...].astype(BF16)
        wb_ref[SSD_WIDTH + MOBA_WIDTH:] = wd_ref[...].astype(BF16)

    acc = jnp.dot(ys_ref[...], wb_ref[0:SSD_WIDTH], preferred_element_type=F32)
    acc += jnp.dot(ym_ref[...], wb_ref[SSD_WIDTH:SSD_WIDTH + MOBA_WIDTH], preferred_element_type=F32)
    acc += jnp.dot(yd_ref[...], wb_ref[SSD_WIDTH + MOBA_WIDTH:], preferred_element_type=F32)
    o_ref[...] = x_ref[...] + gate_ref[...] * acc


def out_projection(y_ssd, y_moba, y_diff, w_out, layer, x, gate):
    L, d = x.shape
    tm, tn = 512, 1024
    return pl.pallas_call(
        _outproj_kernel,
        grid=(d // tn, L // tm),
        in_specs=[pl.BlockSpec((tm, SSD_WIDTH), lambda j, i: (i, 0)),
                  pl.BlockSpec((tm, MOBA_WIDTH), lambda j, i: (i, 0)),
                  pl.BlockSpec((tm, DIFF_WIDTH), lambda j, i: (i, 0)),
                  pl.BlockSpec((None, SSD_WIDTH, tn), lambda j, i: (layer, 0, j)),
                  pl.BlockSpec((None, MOBA_WIDTH, tn), lambda j, i: (layer, SSD_WIDTH // MOBA_WIDTH, j)),
                  pl.BlockSpec((None, DIFF_WIDTH, tn),
                               lambda j, i: (layer, (SSD_WIDTH + MOBA_WIDTH) // DIFF_WIDTH, j)),
                  pl.BlockSpec((tm, tn), lambda j, i: (i, j)),
                  pl.BlockSpec((1, tn), lambda j, i: (0, j))],
        out_specs=pl.BlockSpec((tm, tn), lambda j, i: (i, j)),
        out_shape=jax.ShapeDtypeStruct((L, d), F32),
        scratch_shapes=[pltpu.VMEM((w_out.shape[1], tn), BF16)],
        compiler_params=_params("arbitrary", "arbitrary"),
        name="out_projection",
    )(y_ssd, y_moba, y_diff, w_out, w_out, w_out, x, gate)


CAST_COLS = 256
TILE_VALID = 1
TILE_NEW_WEIGHTS = 2


def weight_plan(tile_expert, tile_valid):
    prev = jnp.concatenate([jnp.full((1,), -1, jnp.int32), tile_expert[:-1]])
    first = tile_expert != prev
    ordinal = jnp.cumsum(first.astype(jnp.int32)) - 1
    n_blocks = ordinal[-1] + 1
    block_expert = jnp.zeros_like(tile_expert).at[ordinal].set(tile_expert)
    next_expert = block_expert[(ordinal + 1) % n_blocks]
    flags = tile_valid * TILE_VALID + first.astype(jnp.int32) * TILE_NEW_WEIGHTS
    return tile_expert, flags, ordinal, next_expert, n_blocks.reshape(1)


def _grouped_matmul_kernel(te_ref, tf_ref, to_ref, tx_ref, nb_ref, a_ref, *rest, n_mats, residual):
    w_hbm, rest = rest[:n_mats], rest[n_mats:]
    if residual:
        x_ref, gate_ref, o_ref, wf_ref, wb_ref, sem = rest
    else:
        o_ref, wf_ref, wb_ref, sem = rest
    j = pl.program_id(0)
    t = pl.program_id(1)
    tn = o_ref.shape[1]
    cw = CAST_COLS
    flags = tf_ref[t]
    new_weights = (flags & TILE_NEW_WEIGHTS) != 0
    block = j * nb_ref[0] + to_ref[t]

    def copies(expert, col_tile):
        cols = pl.ds(pl.multiple_of(col_tile * tn, tn), tn)
        return [pltpu.make_async_copy(w.at[expert, :, cols], wf_ref.at[m], sem.at[m]) for m, w in enumerate(w_hbm)]

    @pl.when(new_weights)
    def _():
        @pl.when(block == 0)
        def _():
            for c in copies(te_ref[t], j):
                c.start()

        for c in copies(te_ref[t], j):
            c.wait()

    def epilogue(accs, cols):
        if n_mats == 2:
            y = _silu(accs[0]) * accs[1]
        else:
            y = accs[0]
            if residual:
                y = x_ref[:, cols] + gate_ref[:, cols] * y
        o_ref[:, cols] = y.astype(o_ref.dtype)

    @pl.when(flags == TILE_VALID + TILE_NEW_WEIGHTS)
    def _():
        a = a_ref[...]
        for c in range(0, tn, cw):
            accs = []
            for m in range(n_mats):
                wb = wf_ref[m, :, c:c + cw].astype(BF16)
                wb_ref[m, :, c:c + cw] = wb
                accs.append(jnp.dot(a, wb, preferred_element_type=F32))
            epilogue(accs, slice(c, c + cw))

    @pl.when(flags == TILE_VALID)
    def _():
        a = a_ref[...]
        epilogue([jnp.dot(a, wb_ref[m], preferred_element_type=F32) for m in range(n_mats)], slice(0, tn))

    @pl.when((flags & TILE_VALID) == 0)
    def _():
        o_ref[...] = jnp.zeros_like(o_ref)

    wraps = to_ref[t] == nb_ref[0] - 1

    @pl.when(new_weights & jnp.logical_not(wraps & (j == pl.num_programs(0) - 1)))
    def _():
        for c in copies(tx_ref[t], j + wraps.astype(jnp.int32)):
            c.start()


def grouped_matmul(a, weights, plan, tm, tn, out_dtype, x=None, gate=None, name="grouped_matmul"):
    rows, k = a.shape
    n = weights[0].shape[2]
    n_mats = len(weights)
    residual = x is not None
    idx = lambda f: (lambda j, t, *refs: f(j, t))
    in_specs = [pl.BlockSpec((tm, k), idx(lambda j, t: (t, 0)))] + [pl.BlockSpec(memory_space=pl.ANY)] * n_mats
    args = [a, *weights]
    if residual:
        in_specs += [pl.BlockSpec((tm, tn), idx(lambda j, t: (t, j))), pl.BlockSpec((1, tn), idx(lambda j, t: (0, j)))]
        args += [x, gate]
    grid_spec = pltpu.PrefetchScalarGridSpec(
        num_scalar_prefetch=5,
        grid=(n // tn, rows // tm),
        in_specs=in_specs,
        out_specs=pl.BlockSpec((tm, tn), idx(lambda j, t: (t, j))),
        scratch_shapes=[pltpu.VMEM((n_mats, k, tn), F32), pltpu.VMEM((n_mats, k, tn), BF16),
                        pltpu.SemaphoreType.DMA((n_mats,))],
    )
    return pl.pallas_call(
        functools.partial(_grouped_matmul_kernel, n_mats=n_mats, residual=residual),
        grid_spec=grid_spec,
        out_shape=jax.ShapeDtypeStruct((rows, n), out_dtype),
        compiler_params=_params("arbitrary", "arbitrary"),
        name=name,
    )(*plan, *args)


def _causal_conv(cur, tail_ref, w_ref, b_ref):
    t = cur.shape[0]
    tail = tail_ref[...]
    w = w_ref[...]
    row8 = lax.broadcasted_iota(jnp.int32, (SUBLANES, cur.shape[1]), 0)
    acc = cur * w[SSD_CONV - 1:SSD_CONV]
    top = cur[0:SUBLANES] * w[SSD_CONV - 1:SSD_CONV]
    for s in range(1, SSD_CONV):
        wk = w[SSD_CONV - 1 - s:SSD_CONV - s]
        rolled = pltpu.roll(cur, s, axis=0)
        acc += rolled * wk
        top += jnp.where(row8 < s, pltpu.roll(tail, s, axis=0), rolled[0:SUBLANES]) * wk
    tail_ref[...] = cur[t - SUBLANES:t]
    return jnp.concatenate([top, acc[SUBLANES:]], axis=0) + b_ref[...]


def _bf16_terms(x):
    terms = []
    for _ in range(3):
        part = x.astype(BF16)
        terms.append(part)
        x = x - part.astype(F32)
    return terms


def _dot_exact_rhs01(x, onehot_bf16):
    return sum(jnp.dot(part, onehot_bf16, preferred_element_type=F32) for part in _bf16_terms(x))


def _dot_exact_lhs01(onehot_bf16, x):
    return sum(jnp.dot(onehot_bf16, part, preferred_element_type=F32) for part in _bf16_terms(x))


def _ssd_kernel(z_ref, x_ref, bc_ref, dt_ref, cwx_ref, cwb_ref, cbx_ref, cbb_ref, dtb_ref, alog_ref,
                dskip_ref, ng_ref, expand_ref, o_ref, tailx_ref, tailb_ref, state_ref, ybuf_ref):
    t = SSD_CHUNK
    hg = SSD_HEADS // SSD_GROUPS
    gw = SSD_WIDTH // SSD_GROUPS

    @pl.when(pl.program_id(0) == 0)
    def _():
        tailx_ref[...] = jnp.zeros_like(tailx_ref)
        tailb_ref[...] = jnp.zeros_like(tailb_ref)
        state_ref[...] = jnp.zeros_like(state_ref)

    xs = _silu(_causal_conv(x_ref[...], tailx_ref, cwx_ref, cbx_ref))
    bcm = _silu(_causal_conv(bc_ref[...], tailb_ref, cwb_ref, cbb_ref))
    dt = _softplus(dt_ref[...] + dtb_ref[...])
    a = -jnp.exp(alog_ref[...])
    row = lax.broadcasted_iota(jnp.int32, (t, t), 0)
    col = lax.broadcasted_iota(jnp.int32, (t, t), 1)
    tril = row >= col
    a_cs = _dot_exact_lhs01(jnp.where(tril, 1.0, 0.0).astype(BF16), dt * a)
    a_last = a_cs[t - 1:t]
    per_head = jnp.concatenate(
        [dt, jnp.exp(a_last - a_cs), jnp.exp(a_cs), jnp.broadcast_to(jnp.exp(a_last), (SUBLANES, LANES))], axis=0)
    spread = _dot_exact_rhs01(per_head, expand_ref[...])
    dt_x, to_end_x, ea_x, cd_x = spread[0:t], spread[t:2 * t], spread[2 * t:3 * t], spread[3 * t:3 * t + 1]
    xdt = xs * dt_x
    xdt_b = xdt.astype(BF16)
    xw_b = (xdt * to_end_x).astype(BF16)
    a_cs_t = a_cs.T

    y_off = []
    for g in range(SSD_GROUPS):
        bm = bcm[:, g * SSD_STATE:(g + 1) * SSD_STATE]
        cm_b = bcm[:, (SSD_GROUPS + g) * SSD_STATE:(SSD_GROUPS + g + 1) * SSD_STATE].astype(BF16)
        cb = lax.dot_general(cm_b, bm.astype(BF16), (((1,), (1,)), ((), ())), preferred_element_type=F32)
        h_prev = state_ref[g]
        y_off.append(jnp.dot(cm_b, h_prev.astype(BF16), preferred_element_type=F32)
                     * ea_x[:, g * gw:(g + 1) * gw])
        st_new = jnp.dot(bm.T.astype(BF16), xw_b[:, g * gw:(g + 1) * gw], preferred_element_type=F32)
        state_ref[g] = h_prev * cd_x[:, g * gw:(g + 1) * gw] + st_new
        for r in range(0, hg, 2):
            pair = []
            for h in (g * hg + r, g * hg + r + 1):
                diff = a_cs[:, h:h + 1] - a_cs_t[h:h + 1, :]
                m = (cb * jnp.exp(jnp.where(tril, diff, -jnp.inf))).astype(BF16)
                pair.append(jnp.dot(m, xdt_b[:, h * SSD_HEAD_DIM:(h + 1) * SSD_HEAD_DIM],
                                    preferred_element_type=F32))
            lo = (g * hg + r) * SSD_HEAD_DIM
            ybuf_ref[:, lo:lo + 2 * SSD_HEAD_DIM] = jnp.concatenate(pair, axis=1)

    y = ybuf_ref[...] + jnp.concatenate(y_off, axis=1) + xs * dskip_ref[...]
    y = y * _silu(z_ref[...])
    outs = []
    for g in range(SSD_GROUPS):
        yg = y[:, g * gw:(g + 1) * gw]
        outs.append(yg * lax.rsqrt(jnp.mean(yg * yg, axis=-1, keepdims=True) + EPS))
    o_ref[...] = (jnp.concatenate(outs, axis=1) * ng_ref[...]).astype(o_ref.dtype)


def ssd_mixer(proj, conv_w, conv_b, dt_bias, a_log, d_skip, norm_g):
    L = proj.shape[0]
    t = SSD_CHUNK
    assert L % t == 0

    def pad_lanes(v):
        return jnp.pad(v, (0, LANES - v.shape[0])).reshape(1, LANES)

    expand = (jnp.arange(SSD_WIDTH)[None, :] // SSD_HEAD_DIM == jnp.arange(LANES)[:, None]).astype(BF16)
    full = lambda shape: pl.BlockSpec(shape, lambda c: (0,) * len(shape))
    return pl.pallas_call(
        _ssd_kernel,
        grid=(L // t,),
        in_specs=[pl.BlockSpec((t, SSD_WIDTH), lambda c: (c, COL_Z // SSD_WIDTH)),
                  pl.BlockSpec((t, SSD_WIDTH), lambda c: (c, COL_X // SSD_WIDTH)),
                  pl.BlockSpec((t, SSD_BC), lambda c: (c, COL_BC // SSD_BC)),
                  pl.BlockSpec((t, LANES), lambda c: (c, COL_DT // LANES)),
                  full((SSD_CONV, SSD_WIDTH)), full((SSD_CONV, SSD_BC)),
                  full((1, SSD_WIDTH)), full((1, SSD_BC)),
                  full((1, LANES)), full((1, LANES)), full((1, SSD_WIDTH)), full((1, SSD_WIDTH)),
                  full((LANES, SSD_WIDTH))],
        out_specs=pl.BlockSpec((t, SSD_WIDTH), lambda c: (c, 0)),
        out_shape=jax.ShapeDtypeStruct((L, SSD_WIDTH), BF16),
        scratch_shapes=[pltpu.VMEM((SUBLANES, SSD_WIDTH), F32), pltpu.VMEM((SUBLANES, SSD_BC), F32),
                        pltpu.VMEM((SSD_GROUPS, SSD_STATE, SSD_WIDTH // SSD_GROUPS), F32),
                        pltpu.VMEM((t, SSD_WIDTH), F32)],
        compiler_params=_params("arbitrary"),
        name="ssd_mixer",
    )(proj, proj, proj, proj,
      conv_w[:, :SSD_WIDTH], conv_w[:, SSD_WIDTH:], conv_b[:SSD_WIDTH].reshape(1, -1),
      conv_b[SSD_WIDTH:].reshape(1, -1), pad_lanes(dt_bias), pad_lanes(a_log),
      jnp.repeat(d_skip, SSD_HEAD_DIM).reshape(1, -1), norm_g.reshape(1, -1), expand)


def _moba_prep_kernel(q_ref, k_ref, v_ref, qa_ref, ka_ref, va_ref, kmean_ref):
    own = pl.program_id(0)
    t = MOBA_BLOCK
    dh = MOBA_HEAD_DIM
    nbl = MOBA_MAX_BLOCKS

    @pl.when(own == 0)
    def _():
        kmean_ref[...] = jnp.zeros_like(kmean_ref)

    lane = lax.broadcasted_iota(jnp.int32, (t, LANES), 1)
    onehot = jnp.where(lane == dh + own, 1.0, 0.0)
    blk = lax.broadcasted_iota(jnp.int32, (nbl, t), 0)
    q_t = q_ref[...].T
    k = k_ref[...]
    k_mean = jnp.mean(k, axis=0, keepdims=True)
    va_ref[:, 0] = v_ref[...].T.reshape(MOBA_HEADS, dh, t).astype(BF16)
    for h in range(MOBA_HEADS):
        qh_t = q_t[h * dh:(h + 1) * dh]
        gate = jnp.dot(kmean_ref[h], qh_t, precision=HIGHEST, preferred_element_type=F32)
        gate = jnp.where(blk < own, gate, -jnp.inf)
        sel = blk >= own
        for _ in range(MOBA_TOPK):
            m = jnp.max(gate, axis=0, keepdims=True)
            idx = jnp.min(jnp.where(gate == m, blk, nbl), axis=0, keepdims=True)
            sel = sel | ((blk == idx) & (m > -jnp.inf))
            gate = jnp.where(blk == idx, -jnp.inf, gate)
        offs_t = jnp.where(sel, 0.0, NEG_BIG)
        qa_ref[h] = jnp.concatenate([qh_t * MOBA_Q_SCALE, offs_t, jnp.zeros((LANES - dh - nbl, t), F32)],
                                    axis=0).astype(BF16)
        pair = k[:, (h // 2) * LANES:(h // 2 + 1) * LANES]
        if h % 2:
            pair = pltpu.roll(pair, dh, axis=1)
        ka_ref[h] = jnp.where(lane < dh, pair, onehot).astype(BF16)
        kmean_ref[h, pl.ds(own, 1), :] = k_mean[:, h * dh:(h + 1) * dh]


def moba_prep(proj):
    L = proj.shape[0]
    t = MOBA_BLOCK
    assert L % t == 0 and L // t <= MOBA_MAX_BLOCKS
    return pl.pallas_call(
        _moba_prep_kernel,
        grid=(L // t,),
        in_specs=[pl.BlockSpec((t, MOBA_WIDTH), lambda i: (i, COL_MQ // MOBA_WIDTH)),
                  pl.BlockSpec((t, MOBA_WIDTH), lambda i: (i, COL_MK // MOBA_WIDTH)),
                  pl.BlockSpec((t, MOBA_WIDTH), lambda i: (i, COL_MV // MOBA_WIDTH))],
        out_specs=[pl.BlockSpec((MOBA_HEADS, LANES, t), lambda i: (0, 0, i)),
                   pl.BlockSpec((MOBA_HEADS, t, LANES), lambda i: (0, i, 0)),
                   pl.BlockSpec((MOBA_HEADS, 1, MOBA_HEAD_DIM, t), lambda i: (0, i, 0, 0))],
        out_shape=[jax.ShapeDtypeStruct((MOBA_HEADS, LANES, L), BF16),
                   jax.ShapeDtypeStruct((MOBA_HEADS, L, LANES), BF16),
                   jax.ShapeDtypeStruct((MOBA_HEADS, L // t, MOBA_HEAD_DIM, t), BF16)],
        scratch_shapes=[pltpu.VMEM((MOBA_HEADS, MOBA_MAX_BLOCKS, MOBA_HEAD_DIM), F32)],
        compiler_params=_params("arbitrary"),
        name="moba_prep",
    )(proj, proj, proj)


def _col_reduce(x, op):
    while x.shape[0] > SUBLANES:
        half = x.shape[0] // 2
        x = op(x[:half], x[half:])
    return jnp.max(x, axis=0, keepdims=True) if op is jnp.maximum else jnp.sum(x, axis=0, keepdims=True)


def _attention_kernel(near_ref, q_ref, k_ref, v_ref, posr_ref, posc_ref, tbl_ref, g_ref, lam_ref, o_ref,
                      m_ref, l_ref, al_ref, acc_ref, sa_ref, sb_ref, pa_ref, pb_ref, vt_ref, *, moba, lambda_init):
    hp = pl.program_id(0)
    qi = pl.program_id(1)
    nq = pl.num_programs(1)
    t = ATT_T
    m_ref[...] = jnp.full_like(m_ref, -jnp.inf)
    l_ref[...] = jnp.zeros_like(l_ref)
    acc_ref[...] = jnp.zeros_like(acc_ref)
    if moba:
        q_t = [q_ref[0], q_ref[1]]
        heads = [2 * hp, 2 * hp + 1]
    else:
        q = (q_ref[...].astype(F32) * DIFF_Q_SCALE).T.astype(BF16)
        half = lax.broadcasted_iota(jnp.int32, q.shape, 0) < DIFF_QK_DIM
        q_t = [jnp.where(half, q, jnp.zeros_like(q)), jnp.where(half, jnp.zeros_like(q), q)]
        heads = [MOBA_HEADS + hp]

        @pl.when(qi == 0)
        def _():
            def transpose_tile(i, carry):
                vt_ref[i] = v_ref[pl.ds(pl.multiple_of(i * t, t), t), :].astype(F32).T.astype(BF16)
                return carry

            lax.fori_loop(0, nq, transpose_tile, 0)
    trow = [tbl_ref[pl.ds(hd, 1), :] for hd in heads]
    posq = posr_ref[pl.ds(qi, 1), :]

    def scores(ki, dst_ref):
        rows = pl.ds(pl.multiple_of(ki * t, t), t)
        for s in range(2):
            k = k_ref[s, rows, :] if moba else k_ref[rows, :]
            dst_ref[s] = jnp.dot(k, q_t[s], preferred_element_type=F32)

    def accumulate(ki, p_ref):
        for s in range(2):
            v_t = v_ref[s, ki] if moba else vt_ref[ki]
            pv = jnp.dot(v_t, p_ref[s], preferred_element_type=F32)
            acc_ref[s] = al_ref[s] * acc_ref[s] + pv

    scores(0, sa_ref)
    pb_ref[...] = jnp.zeros_like(pb_ref)
    al_ref[...] = jnp.ones_like(al_ref)

    def tile(ki, general, cur_ref, nxt_ref, p_ref, p_prev_ref):
        scores(jnp.minimum(ki + 1, qi), nxt_ref)
        accumulate(jnp.maximum(ki - 1, 0), p_prev_ref)
        ch = ATT_CHUNK
        mx = [None, None]
        for r in range(0, t, ch):
            if general:
                dist = jnp.clip(posq - posc_ref[ki, r:r + ch, :], 0, LANES - 1)
                bias = [jnp.concatenate(
                    [jnp.take_along_axis(jnp.broadcast_to(tr, (ch, LANES)), dist[:, j * LANES:(j + 1) * LANES],
                                         axis=1) for j in range(t // LANES)], axis=1) for tr in trow]
                key = lax.broadcasted_iota(jnp.int32, (ch, t), 0) + r
                qry = lax.broadcasted_iota(jnp.int32, (ch, t), 1)
                causal = key + ki * t <= qry + qi * t
            for s in range(2):
                blk = cur_ref[s, r:r + ch, :]
                if general:
                    blk = jnp.where(causal, blk + bias[s if moba else 0], -jnp.inf)
                    cur_ref[s, r:r + ch, :] = blk
                mx[s] = blk if mx[s] is None else jnp.maximum(mx[s], blk)
        for s in range(2):
            m_old = m_ref[s]
            m_tile = _col_reduce(mx[s], jnp.maximum)
            if general:
                m_new = jnp.maximum(m_old, m_tile)
                shift = m_new
            else:
                c = trow[s if moba else 0][:, LANES - 1:LANES]
                m_new = jnp.maximum(m_old, m_tile + c)
                shift = m_new - c
            alpha = jnp.exp2(m_old - m_new)
            sm = None
            for r in range(0, t, ch):
                p = jnp.exp2(cur_ref[s, r:r + ch, :] - shift)
                p_ref[s, r:r + ch, :] = p.astype(BF16)
                sm = p if sm is None else sm + p
            l_ref[s] = alpha * l_ref[s] + _col_reduce(sm, jnp.add)
            al_ref[s] = alpha
            m_ref[s] = m_new

    def step(ki, *bufs):
        flag = near_ref[qi * nq + ki]

        @pl.when(flag == 0)
        def _():
            tile(ki, False, *bufs)

        @pl.when(flag != 0)
        def _():
            tile(ki, True, *bufs)

    def pair(j, carry):
        step(2 * j, sa_ref, sb_ref, pa_ref, pb_ref)
        step(2 * j + 1, sb_ref, sa_ref, pb_ref, pa_ref)
        return carry

    lax.fori_loop(0, (qi + 1) // 2, pair, 0)

    @pl.when(qi % 2 == 0)
    def _():
        step(qi, sa_ref, sb_ref, pa_ref, pb_ref)
        accumulate(qi, pa_ref)

    @pl.when(qi % 2 == 1)
    def _():
        accumulate(qi, pb_ref)

    if moba:
        outs = []
        for s in range(2):
            o = acc_ref[s] / l_ref[s]
            ms = jnp.sum(o * o, axis=0, keepdims=True) * (1.0 / MOBA_HEAD_DIM)
            outs.append(o * lax.rsqrt(ms + EPS) * g_ref[s])
        y_t = jnp.concatenate(outs, axis=0)
    else:
        lp = lam_ref[...]
        lam = (jnp.exp(jnp.sum(lp[0:1] * lp[1:2], axis=1, keepdims=True))
               - jnp.exp(jnp.sum(lp[2:3] * lp[3:4], axis=1, keepdims=True)) + lambda_init)
        o = acc_ref[0] / l_ref[0] - lam * (acc_ref[1] / l_ref[1])
        ms = jnp.sum(o * o, axis=0, keepdims=True) * (1.0 / DIFF_V_DIM)
        y_t = (o * lax.rsqrt(ms + EPS) * g_ref[0]) * (1.0 - lambda_init)
    o_ref[...] = y_t.T.astype(o_ref.dtype)


def _attention_call(kernel, steps, dv, L, near, in_specs, args, name, vt_scratch=False):
    t = ATT_T
    full = lambda a: pl.BlockSpec(a.shape, lambda h, i, nr: (0,) * a.ndim)
    grid_spec = pltpu.PrefetchScalarGridSpec(
        num_scalar_prefetch=1,
        grid=(steps, L // t),
        in_specs=in_specs + [full(a) for a in args[len(in_specs):]],
        out_specs=pl.BlockSpec((t, LANES), lambda h, i, nr: (i, h)),
        scratch_shapes=[pltpu.VMEM((2, 1, t), F32), pltpu.VMEM((2, 1, t), F32), pltpu.VMEM((2, 1, t), F32),
                        pltpu.VMEM((2, dv, t), F32), pltpu.VMEM((2, t, t), F32), pltpu.VMEM((2, t, t), F32),
                        pltpu.VMEM((2, t, t), BF16), pltpu.VMEM((2, t, t), BF16),
                        pltpu.VMEM((L // t if vt_scratch else 1, dv, t), BF16)],
    )
    return pl.pallas_call(kernel, grid_spec=grid_spec, out_shape=jax.ShapeDtypeStruct((L, steps * LANES), BF16),
                          compiler_params=_params("parallel", "arbitrary"), name=name)(near, *args)


def moba_attention(qa_t, ka, va_t, near, pos_rows, pos_cols, table, norm_g):
    heads, L, _ = ka.shape
    t = ATT_T
    kernel = functools.partial(_attention_kernel, moba=True, lambda_init=None)
    g = norm_g.reshape(heads, MOBA_HEAD_DIM, 1)
    return _attention_call(
        kernel, heads // 2, MOBA_HEAD_DIM, L, near,
        [pl.BlockSpec((2, LANES, t), lambda h, i, nr: (h, 0, i)),
         pl.BlockSpec((2, L, LANES), lambda h, i, nr: (h, 0, 0)),
         pl.BlockSpec((2, L // t, MOBA_HEAD_DIM, t), lambda h, i, nr: (h, 0, 0, 0)),
         pl.BlockSpec(pos_rows.shape, lambda h, i, nr: (0, 0)),
         pl.BlockSpec(pos_cols.shape, lambda h, i, nr: (0, 0, 0)),
         pl.BlockSpec(table.shape, lambda h, i, nr: (0, 0)),
         pl.BlockSpec((2, MOBA_HEAD_DIM, 1), lambda h, i, nr: (h, 0, 0))],
        [qa_t, ka, va_t, pos_rows, pos_cols, table, g, jnp.zeros((4, DIFF_QK_DIM), F32)], "moba_attention")


def diff_attention(proj16, near, pos_rows, pos_cols, table, lam_params, subln_g, lambda_init):
    L = proj16.shape[0]
    t = ATT_T
    kernel = functools.partial(_attention_kernel, moba=False, lambda_init=lambda_init)
    return _attention_call(
        kernel, DIFF_HEADS, DIFF_V_DIM, L, near,
        [pl.BlockSpec((t, LANES), lambda h, i, nr: (i, COL_DQ // LANES + h)),
         pl.BlockSpec((L, LANES), lambda h, i, nr: (0, COL_DK // LANES + h)),
         pl.BlockSpec((L, LANES), lambda h, i, nr: (0, COL_DV // LANES + h))],
        [proj16, proj16, proj16, pos_rows, pos_cols, table, subln_g.reshape(1, DIFF_V_DIM, 1), lam_params],
        "diff_attention", vt_scratch=True)


def _rel_bucket(dist):
    n = jnp.maximum(dist, 0)
    max_exact = REL_BUCKETS // 2
    nf = jnp.maximum(n, 1).astype(F32)
    large = max_exact + (jnp.log(nf / max_exact) / math.log(REL_MAX_DIST / max_exact)
                         * (REL_BUCKETS - max_exact)).astype(jnp.int32)
    return jnp.where(n < max_exact, n, jnp.minimum(large, REL_BUCKETS - 1))


def attention_tables(positions, rel_bias):
    L = positions.shape[0]
    t = ATT_T
    buckets = _rel_bucket(jnp.arange(LANES, dtype=jnp.int32))
    table = rel_bias[buckets].T * LOG2E
    pos_rows = positions.reshape(L // t, t)
    lo, hi = jnp.min(pos_rows, axis=1), jnp.max(pos_rows, axis=1)
    near = (lo[:, None] - hi[None, :] < LANES) | jnp.eye(L // t, dtype=bool)
    return table, pos_rows, positions.reshape(L // t, t, 1), near.astype(jnp.int32).reshape(-1)


def _router_kernel(x_ref, g_ref, sc_ref, sh_ref, rw_ref, h_ref, comb_ref, rank_ref, rank_t_ref, cum_ref, total_ref,
                   cnt_ref):
    t = TOK_T

    @pl.when(pl.program_id(0) == 0)
    def _():
        cnt_ref[...] = jnp.zeros_like(cnt_ref)

    h = _norm_mod(x_ref[...], g_ref[...], sc_ref[...], sh_ref[...])
    h_ref[...] = h.astype(BF16)
    lane = lax.broadcasted_iota(jnp.int32, (t, LANES), 1)
    logits = jnp.dot(h, rw_ref[...], precision=HIGHEST, preferred_element_type=F32)
    logits = jnp.where(lane < N_EXPERTS, logits, -jnp.inf)
    m1 = jnp.max(logits, axis=1, keepdims=True)
    i1 = jnp.min(jnp.where(logits == m1, lane, LANES), axis=1, keepdims=True)
    rest = jnp.where(lane == i1, -jnp.inf, logits)
    m2 = jnp.max(rest, axis=1, keepdims=True)
    i2 = jnp.min(jnp.where(rest == m2, lane, LANES), axis=1, keepdims=True)
    e2 = jnp.exp(m2 - m1)
    denom = 1.0 + e2
    comb_ref[...] = jnp.where(lane == i1, 1.0 / denom, 0.0) + jnp.where(lane == i2, e2 / denom, 0.0)
    sel = jnp.where((lane == i1) | (lane == i2), 1.0, 0.0)
    row = lax.broadcasted_iota(jnp.int32, (t, t), 0)
    col = lax.broadcasted_iota(jnp.int32, (t, t), 1)
    before = jnp.dot(jnp.where(row > col, 1.0, 0.0).astype(BF16), sel.astype(BF16), preferred_element_type=F32)
    carry = cnt_ref[...]
    cum_ref[0] = carry
    rank = jnp.where(sel > 0.0, before + carry, -1.0)
    rank_ref[...] = rank
    rank_t_ref[...] = rank.T
    carry = carry + jnp.sum(sel, axis=0, keepdims=True)
    cnt_ref[...] = carry
    total_ref[...] = carry


def moe_router(x, g, sc, sh, router_w):
    L, d = x.shape
    t = TOK_T
    rw = jnp.pad(router_w, ((0, 0), (0, LANES - N_EXPERTS)))
    vec = pl.BlockSpec((1, d), lambda i: (0, 0))
    tok = pl.BlockSpec((t, LANES), lambda i: (i, 0))
    return pl.pallas_call(
        _router_kernel,
        grid=(L // t,),
        in_specs=[pl.BlockSpec((t, d), lambda i: (i, 0)), vec, vec, vec,
                  pl.BlockSpec((d, LANES), lambda i: (0, 0))],
        out_specs=[pl.BlockSpec((t, d), lambda i: (i, 0)), tok, tok,
                   pl.BlockSpec((LANES, t), lambda i: (0, i)),
                   pl.BlockSpec((1, 1, LANES), lambda i: (i, 0, 0)),
                   pl.BlockSpec((1, LANES), lambda i: (0, 0))],
        out_shape=[jax.ShapeDtypeStruct((L, d), BF16), jax.ShapeDtypeStruct((L, LANES), F32),
                   jax.ShapeDtypeStruct((L, LANES), F32), jax.ShapeDtypeStruct((LANES, L), F32),
                   jax.ShapeDtypeStruct((L // t, 1, LANES), F32), jax.ShapeDtypeStruct((1, LANES), F32)],
        scratch_shapes=[pltpu.VMEM((1, LANES), F32)],
        compiler_params=_params("arbitrary"),
        name="moe_router",
    )(x, g, sc, sh, rw)


def _item_lists(hit, n_items):
    rows, cols = hit.shape
    running = jnp.cumsum(hit.reshape(-1).astype(jnp.int32))
    n_real = running[-1]
    k = jnp.arange(n_items, dtype=jnp.int32)
    real = k < n_real
    idx = jnp.searchsorted(running, jnp.minimum(k + 1, n_real), side="left", method="compare_all")
    idx = jnp.minimum(idx, rows * cols - 1).astype(jnp.int32)
    r, c = idx // cols, idx % cols
    prev_r = jnp.concatenate([jnp.full((1,), -1, jnp.int32), r[:-1]])
    next_r = jnp.concatenate([r[1:], jnp.full((1,), -1, jnp.int32)])
    first = real & (r != prev_r)
    last = real & ((r != next_r) | (k == n_real - 1))
    flags = real.astype(jnp.int32) + 2 * first.astype(jnp.int32) + 4 * last.astype(jnp.int32)
    return r, c, flags


def moe_plan(rank_t, cum, total, L):
    tm, tb = MOE_TM, TOK_T
    nb = L // tb
    n_tiles = 2 * L // tm + N_EXPERTS
    counts = total[0, :N_EXPERTS].astype(jnp.int32)
    tiles_e = (counts + tm - 1) // tm
    tile_end = jnp.cumsum(tiles_e)
    tile_start = tile_end - tiles_e
    tid = jnp.arange(n_tiles, dtype=jnp.int32)
    tile_valid = tid < tile_end[-1]
    tile_expert = jnp.minimum(jnp.searchsorted(tile_end, tid, side="right", method="compare_all"),
                              N_EXPERTS - 1).astype(jnp.int32)
    local_row = (tid - tile_start[tile_expert]) * tm
    r = rank_t[:N_EXPERTS].astype(jnp.int32)
    dest = jnp.where(r >= 0, r + (tile_start * tm)[:, None], -1)
    cum_i = cum[:, 0, :N_EXPERTS].astype(jnp.int32)
    cum_next = jnp.concatenate([cum_i[1:], counts[None, :]], axis=0)
    lo = cum_i[:, tile_expert].T
    hi = cum_next[:, tile_expert].T
    hit = tile_valid[:, None] & (lo < (local_row + tm)[:, None]) & (hi > local_row[:, None])
    n_items = n_tiles + N_EXPERTS * nb
    pad_hit = hit | ((~tile_valid)[:, None] & (jnp.arange(nb) == 0)[None, :])
    gather_items = _item_lists(pad_hit, n_items)
    kb, tt, fl = _item_lists(hit.T, n_items)
    return dict(tile_expert=tile_expert, tile_valid=tile_valid.astype(jnp.int32),
                dest_rows=dest, segment_start=(tile_start * tm).astype(jnp.int32),
                gather_items=gather_items, combine_items=(tt, kb, fl), n_tiles=n_tiles, n_items=n_items)


def _moe_gather_kernel(it_ref, ib_ref, if_ref, te_ref, dest_ref, h_ref, o_ref, acc_ref):
    i = pl.program_id(0)
    flag = if_ref[i]
    tile = it_ref[i]
    tm = MOE_TM

    @pl.when((flag & 2) != 0)
    def _():
        acc_ref[...] = jnp.zeros_like(acc_ref)

    @pl.when((flag & 1) != 0)
    def _():
        d = dest_ref[pl.ds(te_ref[tile], 1), :]
        rows = tile * tm + lax.broadcasted_iota(jnp.int32, (tm, 1), 0)
        onehot = jnp.where(d == rows, 1.0, 0.0).astype(BF16)
        acc_ref[...] += jnp.dot(onehot, h_ref[...], preferred_element_type=F32)

    @pl.when((flag & 4) != 0)
    def _():
        o_ref[...] = acc_ref[...].astype(o_ref.dtype)


def moe_gather(h, plan):
    L, d = h.shape
    tm, tb = MOE_TM, TOK_T
    it, ib, fl = plan["gather_items"]
    grid_spec = pltpu.PrefetchScalarGridSpec(
        num_scalar_prefetch=4,
        grid=(plan["n_items"],),
        in_specs=[pl.BlockSpec((N_EXPERTS, tb), lambda i, it, ib, fl, te: (0, ib[i])),
                  pl.BlockSpec((tb, d), lambda i, it, ib, fl, te: (ib[i], 0))],
        out_specs=pl.BlockSpec((tm, d), lambda i, it, ib, fl, te: (it[i], 0)),
        scratch_shapes=[pltpu.VMEM((tm, d), F32)],
    )
    return pl.pallas_call(
        _moe_gather_kernel,
        grid_spec=grid_spec,
        out_shape=jax.ShapeDtypeStruct((plan["n_tiles"] * tm, d), BF16),
        compiler_params=_params("arbitrary"),
        name="moe_gather",
    )(it, ib, fl, plan["tile_expert"], plan["dest_rows"], h)


def _moe_combine_kernel(it_ref, ib_ref, if_ref, te_ref, ts_ref, rank_ref, w_ref, ys_ref, x_ref, gate_ref, fg_ref,
                        o_ref, acc_ref, *, final_norm):
    i = pl.program_id(0)
    flag = if_ref[i]
    tile = it_ref[i]
    tm = MOE_TM

    @pl.when((flag & 2) != 0)
    def _():
        acc_ref[...] = jnp.zeros_like(acc_ref)

    @pl.when((flag & 1) != 0)
    def _():
        e = te_ref[tile]
        mine = lax.broadcasted_iota(jnp.int32, rank_ref.shape, 1) == e
        rank = jnp.sum(jnp.where(mine, rank_ref[...], 0.0), axis=1, keepdims=True)
        w = jnp.sum(jnp.where(mine, w_ref[...], 0.0), axis=1, keepdims=True)
        first = tile * tm - ts_ref[e]
        cols = (first + lax.broadcasted_iota(jnp.int32, (1, tm), 1)).astype(F32)
        onehot = jnp.where(rank == cols, 1.0, 0.0).astype(BF16)
        acc_ref[...] += w * jnp.dot(onehot, ys_ref[...], preferred_element_type=F32)

    @pl.when((flag & 4) != 0)
    def _():
        y = x_ref[...] + gate_ref[...] * acc_ref[...]
        if final_norm:
            y = y * lax.rsqrt(jnp.mean(y * y, axis=-1, keepdims=True) + EPS) * fg_ref[...]
        o_ref[...] = y


def moe_combine(ys, comb, rank, x, gate, plan, final_g=None):
    L, d = x.shape
    final_norm = final_g is not None
    if not final_norm:
        final_g = jnp.ones((1, d), F32)
    tm, tb = MOE_TM, TOK_T
    it, ib, fl = plan["combine_items"]
    tok = lambda i, it, ib, *_: (ib[i], 0)
    fixed = lambda i, *_: (0, 0)
    grid_spec = pltpu.PrefetchScalarGridSpec(
        num_scalar_prefetch=5,
        grid=(plan["n_items"],),
        in_specs=[pl.BlockSpec((tb, LANES), tok), pl.BlockSpec((tb, LANES), tok),
                  pl.BlockSpec((tm, d), lambda i, it, *_: (it[i], 0)),
                  pl.BlockSpec((tb, d), tok), pl.BlockSpec((1, d), fixed), pl.BlockSpec((1, d), fixed)],
        out_specs=pl.BlockSpec((tb, d), tok),
        scratch_shapes=[pltpu.VMEM((tb, d), F32)],
    )
    return pl.pallas_call(
        functools.partial(_moe_combine_kernel, final_norm=final_norm),
        grid_spec=grid_spec,
        out_shape=jax.ShapeDtypeStruct((L, d), F32),
        compiler_params=_params("arbitrary"),
        name="moe_combine",
    )(it, ib, fl, plan["tile_expert"], plan["segment_start"], rank, comb, ys, x, gate, final_g)


def moe_ffn(x, g, sc, sh, gate, router_w, w1, w3, w2, final_g=None):
    L = x.shape[0]
    h, comb, rank, rank_t, cum, total = moe_router(x, g, sc, sh, router_w)
    plan = moe_plan(rank_t, cum, total, L)
    xs = moe_gather(h, plan)
    wplan = weight_plan(plan["tile_expert"], plan["tile_valid"])
    act = grouped_matmul(xs, [w1, w3], wplan, MOE_TM, 1792, BF16, name="swiglu_up")
    ys = grouped_matmul(act, [w2], wplan, MOE_TM, 1024, BF16, name="swiglu_down")
    return moe_combine(ys, comb, rank, x, gate, plan, final_g)


def dense_ffn(x, g, sc, sh, gate, w1, w3, w2):
    L = x.shape[0]
    tm = 512
    h = norm_modulate(x, g, sc, sh, BF16)
    wplan = weight_plan(jnp.zeros((L // tm,), jnp.int32), jnp.ones((L // tm,), jnp.int32))
    act = grouped_matmul(h, [w1[None], w3[None]], wplan, tm, 512, BF16, name="swiglu_up")
    return grouped_matmul(act, [w2[None]], wplan, tm, 512, F32, x=x, gate=gate, name="swiglu_down")


def _permute_in_proj_kernel(w_ref, o_ref):
    w = w_ref[...]
    rows = w.shape[0]
    o_ref[:, 0:ORIG_DT] = w[:, 0:ORIG_DT].astype(BF16)
    o_ref[:, ORIG_DT:COL_DT] = w[:, ORIG_DT + SSD_HEADS:].astype(BF16)
    o_ref[:, COL_DT:] = jnp.concatenate([w[:, ORIG_DT:ORIG_DT + SSD_HEADS],
                                         jnp.zeros((rows, LANES - SSD_HEADS), F32)], axis=1).astype(BF16)


def _permute_in_proj(w_in, layer):
    _, d, n = w_in.shape
    tr = 256
    return pl.pallas_call(
        _permute_in_proj_kernel,
        grid=(d // tr,),
        in_specs=[pl.BlockSpec((None, tr, n), lambda i: (layer, i, 0))],
        out_specs=pl.BlockSpec((tr, PROJ_W), lambda i: (i, 0)),
        out_shape=jax.ShapeDtypeStruct((d, PROJ_W), BF16),
        compiler_params=_params("parallel"),
        name="permute_in_proj",
    )(w_in)


def kernel(x, c, positions, rel_bias, w_ada, b_ada, norm_mix_g, w_in, conv_w, conv_b, dt_bias, a_log, d_skip, ssd_norm_g, moba_norm_g, diff_lambda, diff_subln_g, w_out, norm_ffn_g, dense_w1, dense_w3, dense_w2, router_w, expert_w1, expert_w3, expert_w2, final_g):
    batch, L, d = x.shape
    assert batch == 1 and d == D_MODEL
    x = x[0]
    mod = ada_modulation(c, w_ada, b_ada)
    table, pos_rows, pos_cols, near = attention_tables(positions[0], rel_bias)
    row = lambda v: v.reshape(1, -1)
    for layer in range(DEPTH):
        lambda_init = 0.8 - 0.6 * math.exp(-0.3 * layer)
        shift1, scale1, gate1, shift2, scale2, gate2 = (mod[layer, :, j * d:(j + 1) * d] for j in range(6))
        h = norm_modulate(x, row(norm_mix_g[layer]), scale1, shift1, BF16)
        proj, proj16 = in_projection(h, _permute_in_proj(w_in, layer))
        y_ssd = ssd_mixer(proj, conv_w[layer], conv_b[layer], dt_bias[layer], a_log[layer], d_skip[layer],
                          ssd_norm_g[layer])
        qa, ka, va = moba_prep(proj)
        y_moba = moba_attention(qa, ka, va, near, pos_rows, pos_cols, table, moba_norm_g[layer])
        y_diff = diff_attention(proj16, near, pos_rows, pos_cols, table, diff_lambda[layer],
                                diff_subln_g[layer], lambda_init)
        x = out_projection(y_ssd, y_moba, y_diff, w_out, layer, x, gate1)
        i = layer // 2
        g2 = row(norm_ffn_g[layer])
        if layer % 2 == 0:
            x = dense_ffn(x, g2, scale2, shift2, gate2, dense_w1[i], dense_w3[i], dense_w2[i])
        else:
            x = moe_ffn(x, g2, scale2, shift2, gate2, router_w[i], expert_w1[i], expert_w3[i], expert_w2[i],
                        final_g=row(final_g) if layer == DEPTH - 1 else None)
    if (DEPTH - 1) % 2 == 0:
        zero = jnp.zeros((1, d), F32)
        x = norm_modulate(x, row(final_g), zero, zero, F32)
    return x[None]
```

```python
import functools
import math

import jax
import jax.numpy as jnp
from jax import lax
from jax.experimental import pallas as pl
from jax.experimental.pallas import tpu as pltpu

F32 = jnp.float32
BF16 = jnp.bfloat16
HIGHEST = lax.Precision.HIGHEST

D_MODEL = 2048
DEPTH = 2
SSD_HEADS = 16
SSD_HEAD_DIM = 64
SSD_WIDTH = SSD_HEADS * SSD_HEAD_DIM
SSD_GROUPS = 2
SSD_STATE = 128
SSD_CONV = 4
SSD_CHUNK = 256
SSD_BC = 2 * SSD_GROUPS * SSD_STATE
MOBA_HEADS = 8
MOBA_HEAD_DIM = 64
MOBA_WIDTH = MOBA_HEADS * MOBA_HEAD_DIM
MOBA_BLOCK = 256
MOBA_TOPK = 3
DIFF_HEADS = 4
DIFF_QK_DIM = 64
DIFF_V_DIM = 128
DIFF_WIDTH = DIFF_HEADS * DIFF_V_DIM
REL_BUCKETS = 32
REL_MAX_DIST = 128
D_FF_DENSE = 5632
N_EXPERTS = 8
D_FF_EXPERT = 7168
EPS = 1e-6

LANES = 128
SUBLANES = 8
VMEM_LIMIT = 56 * 1024 * 1024

COL_Z = 0
COL_X = SSD_WIDTH
COL_BC = COL_X + SSD_WIDTH
COL_MQ = COL_BC + SSD_BC
COL_MK = COL_MQ + MOBA_WIDTH
COL_MV = COL_MK + MOBA_WIDTH
COL_DQ = COL_MV + MOBA_WIDTH
COL_DK = COL_DQ + DIFF_WIDTH
COL_DV = COL_DK + DIFF_WIDTH
COL_DT = COL_DV + DIFF_WIDTH
PROJ_W = COL_DT + LANES
ORIG_DT = SSD_WIDTH + SSD_WIDTH + SSD_BC

ATT_T = 256
ATT_CHUNK = 64
TOK_T = 512
MOE_TM = 256
NEG_BIG = -1e9
MOBA_MAX_BLOCKS = 32
LOG2E = math.log2(math.e)
MOBA_Q_SCALE = MOBA_HEAD_DIM ** -0.5 * LOG2E
DIFF_Q_SCALE = DIFF_QK_DIM ** -0.5 * LOG2E


def _silu(x):
    return x * (1.0 / (1.0 + jnp.exp(-x)))


def _softplus(x):
    return jnp.maximum(x, 0.0) + jnp.log1p(jnp.exp(-jnp.abs(x)))


def _params(*sem):
    return pltpu.CompilerParams(dimension_semantics=sem, vmem_limit_bytes=VMEM_LIMIT)


def _ada_kernel(c_ref, w_ref, b_ref, o_ref):
    ca = _silu(c_ref[...])
    o_ref[0] = jnp.sum(ca * w_ref[0], axis=0, keepdims=True) + b_ref[0]


def ada_modulation(c, w_ada, b_ada):
    depth, d, n = w_ada.shape
    tn = 2048
    return pl.pallas_call(
        _ada_kernel,
        grid=(depth, n // tn),
        in_specs=[pl.BlockSpec((d, 1), lambda l, j: (0, 0)),
                  pl.BlockSpec((1, d, tn), lambda l, j: (l, 0, j)),
                  pl.BlockSpec((1, 1, tn), lambda l, j: (l, 0, j))],
        out_specs=pl.BlockSpec((1, 1, tn), lambda l, j: (l, 0, j)),
        out_shape=jax.ShapeDtypeStruct((depth, 1, n), F32),
        compiler_params=_params("parallel", "parallel"),
        name="ada_modulation",
    )(c.reshape(d, 1), w_ada, b_ada.reshape(depth, 1, n))


def _norm_mod(x, g, sc, sh):
    ms = jnp.mean(x * x, axis=-1, keepdims=True)
    return (x * lax.rsqrt(ms + EPS) * g) * (1.0 + sc) + sh


def _norm_kernel(x_ref, g_ref, sc_ref, sh_ref, o_ref):
    o_ref[...] = _norm_mod(x_ref[...], g_ref[...], sc_ref[...], sh_ref[...]).astype(o_ref.dtype)


def norm_modulate(x, g, sc, sh, out_dtype):
    L, d = x.shape
    tm = 1024
    vec = pl.BlockSpec((1, d), lambda i: (0, 0))
    return pl.pallas_call(
        _norm_kernel,
        grid=(L // tm,),
        in_specs=[pl.BlockSpec((tm, d), lambda i: (i, 0)), vec, vec, vec],
        out_specs=pl.BlockSpec((tm, d), lambda i: (i, 0)),
        out_shape=jax.ShapeDtypeStruct((L, d), out_dtype),
        compiler_params=_params("parallel"),
        name="norm_modulate",
    )(x, g, sc, sh)


def _inproj_kernel(a_ref, w_ref, o32_ref, o16_ref):
    acc = lax.dot_general(a_ref[...], w_ref[...], (((1,), (1,)), ((), ())), preferred_element_type=F32)
    o32_ref[...] = acc
    o16_ref[...] = acc.astype(BF16)


def in_projection(h, w_t):
    L, k = h.shape
    n = w_t.shape[0]
    tm, tn = 1024, 1920
    return pl.pallas_call(
        _inproj_kernel,
        grid=(n // tn, L // tm),
        in_specs=[pl.BlockSpec((tm, k), lambda j, i: (i, 0)),
                  pl.BlockSpec((tn, k), lambda j, i: (j, 0))],
        out_specs=[pl.BlockSpec((tm, tn), lambda j, i: (i, j)),
                   pl.BlockSpec((tm, tn), lambda j, i: (i, j))],
        out_shape=[jax.ShapeDtypeStruct((L, n), F32), jax.ShapeDtypeStruct((L, n), BF16)],
        compiler_params=_params("parallel", "parallel"),
        name="in_projection",
    )(h, w_t)


def _outproj_kernel(ys_ref, ym_ref, yd_ref, ws_ref, wm_ref, wd_ref, x_ref, gate_ref, o_ref, wb_ref):
    @pl.when(pl.program_id(1) == 0)
    def _():
        wb_ref[0:SSD_WIDTH] = ws_ref[...].astype(BF16)
        wb_ref[SSD_WIDTH:SSD_WIDTH + MOBA_WIDTH] = wm_ref[...].astype(BF16)
        wb_ref[SSD_WIDTH + MOBA_WIDTH:] = wd_ref[...].astype(BF16)

    acc = jnp.dot(ys_ref[...], wb_ref[0:SSD_WIDTH], preferred_element_type=F32)
    acc += jnp.dot(ym_ref[...], wb_ref[SSD_WIDTH:SSD_WIDTH + MOBA_WIDTH], preferred_element_type=F32)
    acc += jnp.dot(yd_ref[...], wb_ref[SSD_WIDTH + MOBA_WIDTH:], preferred_element_type=F32)
    o_ref[...] = x_ref[...] + gate_ref[...] * acc


def out_projection(y_ssd, y_moba, y_diff, w_out, layer, x, gate):
    L, d = x.shape
    tm, tn = 512, 1024
    return pl.pallas_call(
        _outproj_kernel,
        grid=(d // tn, L // tm),
        in_specs=[pl.BlockSpec((tm, SSD_WIDTH), lambda j, i: (i, 0)),
                  pl.BlockSpec((tm, MOBA_WIDTH), lambda j, i: (i, 0)),
                  pl.BlockSpec((tm, DIFF_WIDTH), lambda j, i: (i, 0)),
                  pl.BlockSpec((None, SSD_WIDTH, tn), lambda j, i: (layer, 0, j)),
                  pl.BlockSpec((None, MOBA_WIDTH, tn), lambda j, i: (layer, SSD_WIDTH // MOBA_WIDTH, j)),
                  pl.BlockSpec((None, DIFF_WIDTH, tn),
                               lambda j, i: (layer, (SSD_WIDTH + MOBA_WIDTH) // DIFF_WIDTH, j)),
                  pl.BlockSpec((tm, tn), lambda j, i: (i, j)),
                  pl.BlockSpec((1, tn), lambda j, i: (0, j))],
        out_specs=pl.BlockSpec((tm, tn), lambda j, i: (i, j)),
        out_shape=jax.ShapeDtypeStruct((L, d), F32),
        scratch_shapes=[pltpu.VMEM((w_out.shape[1], tn), BF16)],
        compiler_params=_params("arbitrary", "arbitrary"),
        name="out_projection",
    )(y_ssd, y_moba, y_diff, w_out, w_out, w_out, x, gate)


CAST_COLS = 256
TILE_VALID = 1
TILE_NEW_WEIGHTS = 2


def weight_plan(tile_expert, tile_valid):
    prev = jnp.concatenate([jnp.full((1,), -1, jnp.int32), tile_expert[:-1]])
    first = tile_expert != prev
    ordinal = jnp.cumsum(first.astype(jnp.int32)) - 1
    n_blocks = ordinal[-1] + 1
    block_expert = jnp.zeros_like(tile_expert).at[ordinal].set(tile_expert)
    next_expert = block_expert[(ordinal + 1) % n_blocks]
    flags = tile_valid * TILE_VALID + first.astype(jnp.int32) * TILE_NEW_WEIGHTS
    return tile_expert, flags, ordinal, next_expert, n_blocks.reshape(1)


def _grouped_matmul_kernel(te_ref, tf_ref, to_ref, tx_ref, nb_ref, a_ref, *rest, n_mats, residual):
    w_hbm, rest = rest[:n_mats], rest[n_mats:]
    if residual:
        x_ref, gate_ref, o_ref, wf_ref, wb_ref, sem = rest
    else:
        o_ref, wf_ref, wb_ref, sem = rest
    j = pl.program_id(0)
    t = pl.program_id(1)
    tn = o_ref.shape[1]
    cw = CAST_COLS
    flags = tf_ref[t]
    new_weights = (flags & TILE_NEW_WEIGHTS) != 0
    block = j * nb_ref[0] + to_ref[t]

    def copies(expert, col_tile):
        cols = pl.ds(pl.multiple_of(col_tile * tn, tn), tn)
        return [pltpu.make_async_copy(w.at[expert, :, cols], wf_ref.at[m], sem.at[m]) for m, w in enumerate(w_hbm)]

    @pl.when(new_weights)
    def _():
        @pl.when(block == 0)
        def _():
            for c in copies(te_ref[t], j):
                c.start()

        for c in copies(te_ref[t], j):
            c.wait()

    def epilogue(accs, cols):
        if n_mats == 2:
            y = _silu(accs[0]) * accs[1]
        else:
            y = accs[0]
            if residual:
                y = x_ref[:, cols] + gate_ref[:, cols] * y
        o_ref[:, cols] = y.astype(o_ref.dtype)

    @pl.when(flags == TILE_VALID + TILE_NEW_WEIGHTS)
    def _():
        a = a_ref[...]
        for c in range(0, tn, cw):
            accs = []
            for m in range(n_mats):
                wb = wf_ref[m, :, c:c + cw].astype(BF16)
                wb_ref[m, :, c:c + cw] = wb
                accs.append(jnp.dot(a, wb, preferred_element_type=F32))
            epilogue(accs, slice(c, c + cw))

    @pl.when(flags == TILE_VALID)
    def _():
        a = a_ref[...]
        epilogue([jnp.dot(a, wb_ref[m], preferred_element_type=F32) for m in range(n_mats)], slice(0, tn))

    @pl.when((flags & TILE_VALID) == 0)
    def _():
        o_ref[...] = jnp.zeros_like(o_ref)

    wraps = to_ref[t] == nb_ref[0] - 1

    @pl.when(new_weights & jnp.logical_not(wraps & (j == pl.num_programs(0) - 1)))
    def _():
        for c in copies(tx_ref[t], j + wraps.astype(jnp.int32)):
            c.start()


def grouped_matmul(a, weights, plan, tm, tn, out_dtype, x=None, gate=None, name="grouped_matmul"):
    rows, k = a.shape
    n = weights[0].shape[2]
    n_mats = len(weights)
    residual = x is not None
    idx = lambda f: (lambda j, t, *refs: f(j, t))
    in_specs = [pl.BlockSpec((tm, k), idx(lambda j, t: (t, 0)))] + [pl.BlockSpec(memory_space=pl.ANY)] * n_mats
    args = [a, *weights]
    if residual:
        in_specs += [pl.BlockSpec((tm, tn), idx(lambda j, t: (t, j))), pl.BlockSpec((1, tn), idx(lambda j, t: (0, j)))]
        args += [x, gate]
    grid_spec = pltpu.PrefetchScalarGridSpec(
        num_scalar_prefetch=5,
        grid=(n // tn, rows // tm),
        in_specs=in_specs,
        out_specs=pl.BlockSpec((tm, tn), idx(lambda j, t: (t, j))),
        scratch_shapes=[pltpu.VMEM((n_mats, k, tn), F32), pltpu.VMEM((n_mats, k, tn), BF16),
                        pltpu.SemaphoreType.DMA((n_mats,))],
    )
    return pl.pallas_call(
        functools.partial(_grouped_matmul_kernel, n_mats=n_mats, residual=residual),
        grid_spec=grid_spec,
        out_shape=jax.ShapeDtypeStruct((rows, n), out_dtype),
        compiler_params=_params("arbitrary", "arbitrary"),
        name=name,
    )(*plan, *args)


def _causal_conv(cur, tail_ref, w_ref, b_ref):
    t = cur.shape[0]
    tail = tail_ref[...]
    w = w_ref[...]
    row8 = lax.broadcasted_iota(jnp.int32, (SUBLANES, cur.shape[1]), 0)
    acc = cur * w[SSD_CONV - 1:SSD_CONV]
    top = cur[0:SUBLANES] * w[SSD_CONV - 1:SSD_CONV]
    for s in range(1, SSD_CONV):
        wk = w[SSD_CONV - 1 - s:SSD_CONV - s]
        rolled = pltpu.roll(cur, s, axis=0)
        acc += rolled * wk
        top += jnp.where(row8 < s, pltpu.roll(tail, s, axis=0), rolled[0:SUBLANES]) * wk
    tail_ref[...] = cur[t - SUBLANES:t]
    return jnp.concatenate([top, acc[SUBLANES:]], axis=0) + b_ref[...]


def _bf16_terms(x):
    terms = []
    for _ in range(3):
        part = x.astype(BF16)
        terms.append(part)
        x = x - part.astype(F32)
    return terms


def _dot_exact_rhs01(x, onehot_bf16):
    return sum(jnp.dot(part, onehot_bf16, preferred_element_type=F32) for part in _bf16_terms(x))


def _dot_exact_lhs01(onehot_bf16, x):
    return sum(jnp.dot(onehot_bf16, part, preferred_element_type=F32) for part in _bf16_terms(x))


def _ssd_kernel(z_ref, x_ref, bc_ref, dt_ref, cwx_ref, cwb_ref, cbx_ref, cbb_ref, dtb_ref, alog_ref,
                dskip_ref, ng_ref, expand_ref, o_ref, tailx_ref, tailb_ref, state_ref, ybuf_ref):
    t = SSD_CHUNK
    hg = SSD_HEADS // SSD_GROUPS
    gw = SSD_WIDTH // SSD_GROUPS

    @pl.when(pl.program_id(0) == 0)
    def _():
        tailx_ref[...] = jnp.zeros_like(tailx_ref)
        tailb_ref[...] = jnp.zeros_like(tailb_ref)
        state_ref[...] = jnp.zeros_like(state_ref)

    xs = _silu(_causal_conv(x_ref[...], tailx_ref, cwx_ref, cbx_ref))
    bcm = _silu(_causal_conv(bc_ref[...], tailb_ref, cwb_ref, cbb_ref))
    dt = _softplus(dt_ref[...] + dtb_ref[...])
    a = -jnp.exp(alog_ref[...])
    row = lax.broadcasted_iota(jnp.int32, (t, t), 0)
    col = lax.broadcasted_iota(jnp.int32, (t, t), 1)
    tril = row >= col
    a_cs = _dot_exact_lhs01(jnp.where(tril, 1.0, 0.0).astype(BF16), dt * a)
    a_last = a_cs[t - 1:t]
    per_head = jnp.concatenate(
        [dt, jnp.exp(a_last - a_cs), jnp.exp(a_cs), jnp.broadcast_to(jnp.exp(a_last), (SUBLANES, LANES))], axis=0)
    spread = _dot_exact_rhs01(per_head, expand_ref[...])
    dt_x, to_end_x, ea_x, cd_x = spread[0:t], spread[t:2 * t], spread[2 * t:3 * t], spread[3 * t:3 * t + 1]
    xdt = xs * dt_x
    xdt_b = xdt.astype(BF16)
    xw_b = (xdt * to_end_x).astype(BF16)
    a_cs_t = a_cs.T

    y_off = []
    for g in range(SSD_GROUPS):
        bm = bcm[:, g * SSD_STATE:(g + 1) * SSD_STATE]
        cm_b = bcm[:, (SSD_GROUPS + g) * SSD_STATE:(SSD_GROUPS + g + 1) * SSD_STATE].astype(BF16)
        cb = lax.dot_general(cm_b, bm.astype(BF16), (((1,), (1,)), ((), ())), preferred_element_type=F32)
        h_prev = state_ref[g]
        y_off.append(jnp.dot(cm_b, h_prev.astype(BF16), preferred_element_type=F32)
                     * ea_x[:, g * gw:(g + 1) * gw])
        st_new = jnp.dot(bm.T.astype(BF16), xw_b[:, g * gw:(g + 1) * gw], preferred_element_type=F32)
        state_ref[g] = h_prev * cd_x[:, g * gw:(g + 1) * gw] + st_new
        for r in range(0, hg, 2):
            pair = []
            for h in (g * hg + r, g * hg + r + 1):
                diff = a_cs[:, h:h + 1] - a_cs_t[h:h + 1, :]
                m = (cb * jnp.exp(jnp.where(tril, diff, -jnp.inf))).astype(BF16)
                pair.append(jnp.dot(m, xdt_b[:, h * SSD_HEAD_DIM:(h + 1) * SSD_HEAD_DIM],
                                    preferred_element_type=F32))
            lo = (g * hg + r) * SSD_HEAD_DIM
            ybuf_ref[:, lo:lo + 2 * SSD_HEAD_DIM] = jnp.concatenate(pair, axis=1)

    y = ybuf_ref[...] + jnp.concatenate(y_off, axis=1) + xs * dskip_ref[...]
    y = y * _silu(z_ref[...])
    outs = []
    for g in range(SSD_GROUPS):
        yg = y[:, g * gw:(g + 1) * gw]
        outs.append(yg * lax.rsqrt(jnp.mean(yg * yg, axis=-1, keepdims=True) + EPS))
    o_ref[...] = (jnp.concatenate(outs, axis=1) * ng_ref[...]).astype(o_ref.dtype)


def ssd_mixer(proj, conv_w, conv_b, dt_bias, a_log, d_skip, norm_g):
    L = proj.shape[0]
    t = SSD_CHUNK
    assert L % t == 0

    def pad_lanes(v):
        return jnp.pad(v, (0, LANES - v.shape[0])).reshape(1, LANES)

    expand = (jnp.arange(SSD_WIDTH)[None, :] // SSD_HEAD_DIM == jnp.arange(LANES)[:, None]).astype(BF16)
    full = lambda shape: pl.BlockSpec(shape, lambda c: (0,) * len(shape))
    return pl.pallas_call(
        _ssd_kernel,
        grid=(L // t,),
        in_specs=[pl.BlockSpec((t, SSD_WIDTH), lambda c: (c, COL_Z // SSD_WIDTH)),
                  pl.BlockSpec((t, SSD_WIDTH), lambda c: (c, COL_X // SSD_WIDTH)),
                  pl.BlockSpec((t, SSD_BC), lambda c: (c, COL_BC // SSD_BC)),
                  pl.BlockSpec((t, LANES), lambda c: (c, COL_DT // LANES)),
                  full((SSD_CONV, SSD_WIDTH)), full((SSD_CONV, SSD_BC)),
                  full((1, SSD_WIDTH)), full((1, SSD_BC)),
                  full((1, LANES)), full((1, LANES)), full((1, SSD_WIDTH)), full((1, SSD_WIDTH)),
                  full((LANES, SSD_WIDTH))],
        out_specs=pl.BlockSpec((t, SSD_WIDTH), lambda c: (c, 0)),
        out_shape=jax.ShapeDtypeStruct((L, SSD_WIDTH), BF16),
        scratch_shapes=[pltpu.VMEM((SUBLANES, SSD_WIDTH), F32), pltpu.VMEM((SUBLANES, SSD_BC), F32),
                        pltpu.VMEM((SSD_GROUPS, SSD_STATE, SSD_WIDTH // SSD_GROUPS), F32),
                        pltpu.VMEM((t, SSD_WIDTH), F32)],
        compiler_params=_params("arbitrary"),
        name="ssd_mixer",
    )(proj, proj, proj, proj,
      conv_w[:, :SSD_WIDTH], conv_w[:, SSD_WIDTH:], conv_b[:SSD_WIDTH].reshape(1, -1),
      conv_b[SSD_WIDTH:].reshape(1, -1), pad_lanes(dt_bias), pad_lanes(a_log),
      jnp.repeat(d_skip, SSD_HEAD_DIM).reshape(1, -1), norm_g.reshape(1, -1), expand)


def _moba_prep_kernel(q_ref, k_ref, v_ref, qa_ref, ka_ref, va_ref, kmean_ref):
    own = pl.program_id(0)
    t = MOBA_BLOCK
    dh = MOBA_HEAD_DIM
    nbl = MOBA_MAX_BLOCKS

    @pl.when(own == 0)
    def _():
        kmean_ref[...] = jnp.zeros_like(kmean_ref)

    lane = lax.broadcasted_iota(jnp.int32, (t, LANES), 1)
    onehot = jnp.where(lane == dh + own, 1.0, 0.0)
    blk = lax.broadcasted_iota(jnp.int32, (nbl, t), 0)
    q_t = q_ref[...].T
    k = k_ref[...]
    k_mean = jnp.mean(k, axis=0, keepdims=True)
    va_ref[:, 0] = v_ref[...].T.reshape(MOBA_HEADS, dh, t).astype(BF16)
    for h in range(MOBA_HEADS):
        qh_t = q_t[h * dh:(h + 1) * dh]
        gate = jnp.dot(kmean_ref[h], qh_t, precision=HIGHEST, preferred_element_type=F32)
        gate = jnp.where(blk < own, gate, -jnp.inf)
        sel = blk >= own
        for _ in range(MOBA_TOPK):
            m = jnp.max(gate, axis=0, keepdims=True)
            idx = jnp.min(jnp.where(gate == m, blk, nbl), axis=0, keepdims=True)
            sel = sel | ((blk == idx) & (m > -jnp.inf))
            gate = jnp.where(blk == idx, -jnp.inf, gate)
        offs_t = jnp.where(sel, 0.0, NEG_BIG)
        qa_ref[h] = jnp.concatenate([qh_t * MOBA_Q_SCALE, offs_t, jnp.zeros((LANES - dh - nbl, t), F32)],
                                    axis=0).astype(BF16)
        pair = k[:, (h // 2) * LANES:(h // 2 + 1) * LANES]
        if h % 2:
            pair = pltpu.roll(pair, dh, axis=1)
        ka_ref[h] = jnp.where(lane < dh, pair, onehot).astype(BF16)
        kmean_ref[h, pl.ds(own, 1), :] = k_mean[:, h * dh:(h + 1) * dh]


def moba_prep(proj):
    L = proj.shape[0]
    t = MOBA_BLOCK
    assert L % t == 0 and L // t <= MOBA_MAX_BLOCKS
    return pl.pallas_call(
        _moba_prep_kernel,
        grid=(L // t,),
        in_specs=[pl.BlockSpec((t, MOBA_WIDTH), lambda i: (i, COL_MQ // MOBA_WIDTH)),
                  pl.BlockSpec((t, MOBA_WIDTH), lambda i: (i, COL_MK // MOBA_WIDTH)),
                  pl.BlockSpec((t, MOBA_WIDTH), lambda i: (i, COL_MV // MOBA_WIDTH))],
        out_specs=[pl.BlockSpec((MOBA_HEADS, LANES, t), lambda i: (0, 0, i)),
                   pl.BlockSpec((MOBA_HEADS, t, LANES), lambda i: (0, i, 0)),
                   pl.BlockSpec((MOBA_HEADS, 1, MOBA_HEAD_DIM, t), lambda i: (0, i, 0, 0))],
        out_shape=[jax.ShapeDtypeStruct((MOBA_HEADS, LANES, L), BF16),
                   jax.ShapeDtypeStruct((MOBA_HEADS, L, LANES), BF16),
                   jax.ShapeDtypeStruct((MOBA_HEADS, L // t, MOBA_HEAD_DIM, t), BF16)],
        scratch_shapes=[pltpu.VMEM((MOBA_HEADS, MOBA_MAX_BLOCKS, MOBA_HEAD_DIM), F32)],
        compiler_params=_params("arbitrary"),
        name="moba_prep",
    )(proj, proj, proj)


def _col_reduce(x, op):
    while x.shape[0] > SUBLANES:
        half = x.shape[0] // 2
        x = op(x[:half], x[half:])
    return jnp.max(x, axis=0, keepdims=True) if op is jnp.maximum else jnp.sum(x, axis=0, keepdims=True)


def _attention_kernel(near_ref, q_ref, k_ref, v_ref, posr_ref, posc_ref, tbl_ref, g_ref, lam_ref, o_ref,
                      m_ref, l_ref, al_ref, acc_ref, sa_ref, sb_ref, pa_ref, pb_ref, vt_ref, *, moba, lambda_init):
    hp = pl.program_id(0)
    qi = pl.program_id(1)
    nq = pl.num_programs(1)
    t = ATT_T
    m_ref[...] = jnp.full_like(m_ref, -jnp.inf)
    l_ref[...] = jnp.zeros_like(l_ref)
    acc_ref[...] = jnp.zeros_like(acc_ref)
    if moba:
        q_t = [q_ref[0], q_ref[1]]
        heads = [2 * hp, 2 * hp + 1]
    else:
        q = (q_ref[...].astype(F32) * DIFF_Q_SCALE).T.astype(BF16)
        half = lax.broadcasted_iota(jnp.int32, q.shape, 0) < DIFF_QK_DIM
        q_t = [jnp.where(half, q, jnp.zeros_like(q)), jnp.where(half, jnp.zeros_like(q), q)]
        heads = [MOBA_HEADS + hp]

        @pl.when(qi == 0)
        def _():
            def transpose_tile(i, carry):
                vt_ref[i] = v_ref[pl.ds(pl.multiple_of(i * t, t), t), :].astype(F32).T.astype(BF16)
                return carry

            lax.fori_loop(0, nq, transpose_tile, 0)
    trow = [tbl_ref[pl.ds(hd, 1), :] for hd in heads]
    posq = posr_ref[pl.ds(qi, 1), :]

    def scores(ki, dst_ref):
        rows = pl.ds(pl.multiple_of(ki * t, t), t)
        for s in range(2):
            k = k_ref[s, rows, :] if moba else k_ref[rows, :]
            dst_ref[s] = jnp.dot(k, q_t[s], preferred_element_type=F32)

    def accumulate(ki, p_ref):
        for s in range(2):
            v_t = v_ref[s, ki] if moba else vt_ref[ki]
            pv = jnp.dot(v_t, p_ref[s], preferred_element_type=F32)
            acc_ref[s] = al_ref[s] * acc_ref[s] + pv

    scores(0, sa_ref)
    pb_ref[...] = jnp.zeros_like(pb_ref)
    al_ref[...] = jnp.ones_like(al_ref)

    def tile(ki, general, cur_ref, nxt_ref, p_ref, p_prev_ref):
        scores(jnp.minimum(ki + 1, qi), nxt_ref)
        accumulate(jnp.maximum(ki - 1, 0), p_prev_ref)
        ch = ATT_CHUNK
        mx = [None, None]
        for r in range(0, t, ch):
            if general:
                dist = jnp.clip(posq - posc_ref[ki, r:r + ch, :], 0, LANES - 1)
                bias = [jnp.concatenate(
                    [jnp.take_along_axis(jnp.broadcast_to(tr, (ch, LANES)), dist[:, j * LANES:(j + 1) * LANES],
                                         axis=1) for j in range(t // LANES)], axis=1) for tr in trow]
                key = lax.broadcasted_iota(jnp.int32, (ch, t), 0) + r
                qry = lax.broadcasted_iota(jnp.int32, (ch, t), 1)
                causal = key + ki * t <= qry + qi * t
            for s in range(2):
                blk = cur_ref[s, r:r + ch, :]
                if general:
                    blk = jnp.where(causal, blk + bias[s if moba else 0], -jnp.inf)
                    cur_ref[s, r:r + ch, :] = blk
                mx[s] = blk if mx[s] is None else jnp.maximum(mx[s], blk)
        for s in range(2):
            m_old = m_ref[s]
            m_tile = _col_reduce(mx[s], jnp.maximum)
            if general:
                m_new = jnp.maximum(m_old, m_tile)
                shift = m_new
            else:
                c = trow[s if moba else 0][:, LANES - 1:LANES]
                m_new = jnp.maximum(m_old, m_tile + c)
                shift = m_new - c
            alpha = jnp.exp2(m_old - m_new)
            sm = None
            for r in range(0, t, ch):
                p = jnp.exp2(cur_ref[s, r:r + ch, :] - shift)
                p_ref[s, r:r + ch, :] = p.astype(BF16)
                sm = p if sm is None else sm + p
            l_ref[s] = alpha * l_ref[s] + _col_reduce(sm, jnp.add)
            al_ref[s] = alpha
            m_ref[s] = m_new

    def step(ki, *bufs):
        flag = near_ref[qi * nq + ki]

        @pl.when(flag == 0)
        def _():
            tile(ki, False, *bufs)

        @pl.when(flag != 0)
        def _():
            tile(ki, True, *bufs)

    def pair(j, carry):
        step(2 * j, sa_ref, sb_ref, pa_ref, pb_ref)
        step(2 * j + 1, sb_ref, sa_ref, pb_ref, pa_ref)
        return carry

    lax.fori_loop(0, (qi + 1) // 2, pair, 0)

    @pl.when(qi % 2 == 0)
    def _():
        step(qi, sa_ref, sb_ref, pa_ref, pb_ref)
        accumulate(qi, pa_ref)

    @pl.when(qi % 2 == 1)
    def _():
        accumulate(qi, pb_ref)

    if moba:
        outs = []
        for s in range(2):
            o = acc_ref[s] / l_ref[s]
            ms = jnp.sum(o * o, axis=0, keepdims=True) * (1.0 / MOBA_HEAD_DIM)
            outs.append(o * lax.rsqrt(ms + EPS) * g_ref[s])
        y_t = jnp.concatenate(outs, axis=0)
    else:
        lp = lam_ref[...]
        lam = (jnp.exp(jnp.sum(lp[0:1] * lp[1:2], axis=1, keepdims=True))
               - jnp.exp(jnp.sum(lp[2:3] * lp[3:4], axis=1, keepdims=True)) + lambda_init)
        o = acc_ref[0] / l_ref[0] - lam * (acc_ref[1] / l_ref[1])
        ms = jnp.sum(o * o, axis=0, keepdims=True) * (1.0 / DIFF_V_DIM)
        y_t = (o * lax.rsqrt(ms + EPS) * g_ref[0]) * (1.0 - lambda_init)
    o_ref[...] = y_t.T.astype(o_ref.dtype)


def _attention_call(kernel, steps, dv, L, near, in_specs, args, name, vt_scratch=False):
    t = ATT_T
    full = lambda a: pl.BlockSpec(a.shape, lambda h, i, nr: (0,) * a.ndim)
    grid_spec = pltpu.PrefetchScalarGridSpec(
        num_scalar_prefetch=1,
        grid=(steps, L // t),
        in_specs=in_specs + [full(a) for a in args[len(in_specs):]],
        out_specs=pl.BlockSpec((t, LANES), lambda h, i, nr: (i, h)),
        scratch_shapes=[pltpu.VMEM((2, 1, t), F32), pltpu.VMEM((2, 1, t), F32), pltpu.VMEM((2, 1, t), F32),
                        pltpu.VMEM((2, dv, t), F32), pltpu.VMEM((2, t, t), F32), pltpu.VMEM((2, t, t), F32),
                        pltpu.VMEM((2, t, t), BF16), pltpu.VMEM((2, t, t), BF16),
                        pltpu.VMEM((L // t if vt_scratch else 1, dv, t), BF16)],
    )
    return pl.pallas_call(kernel, grid_spec=grid_spec, out_shape=jax.ShapeDtypeStruct((L, steps * LANES), BF16),
                          compiler_params=_params("parallel", "arbitrary"), name=name)(near, *args)


def moba_attention(qa_t, ka, va_t, near, pos_rows, pos_cols, table, norm_g):
    heads, L, _ = ka.shape
    t = ATT_T
    kernel = functools.partial(_attention_kernel, moba=True, lambda_init=None)
    g = norm_g.reshape(heads, MOBA_HEAD_DIM, 1)
    return _attention_call(
        kernel, heads // 2, MOBA_HEAD_DIM, L, near,
        [pl.BlockSpec((2, LANES, t), lambda h, i, nr: (h, 0, i)),
         pl.BlockSpec((2, L, LANES), lambda h, i, nr: (h, 0, 0)),
         pl.BlockSpec((2, L // t, MOBA_HEAD_DIM, t), lambda h, i, nr: (h, 0, 0, 0)),
         pl.BlockSpec(pos_rows.shape, lambda h, i, nr: (0, 0)),
         pl.BlockSpec(pos_cols.shape, lambda h, i, nr: (0, 0, 0)),
         pl.BlockSpec(table.shape, lambda h, i, nr: (0, 0)),
         pl.BlockSpec((2, MOBA_HEAD_DIM, 1), lambda h, i, nr: (h, 0, 0))],
        [qa_t, ka, va_t, pos_rows, pos_cols, table, g, jnp.zeros((4, DIFF_QK_DIM), F32)], "moba_attention")


def diff_attention(proj16, near, pos_rows, pos_cols, table, lam_params, subln_g, lambda_init):
    L = proj16.shape[0]
    t = ATT_T
    kernel = functools.partial(_attention_kernel, moba=False, lambda_init=lambda_init)
    return _attention_call(
        kernel, DIFF_HEADS, DIFF_V_DIM, L, near,
        [pl.BlockSpec((t, LANES), lambda h, i, nr: (i, COL_DQ // LANES + h)),
         pl.BlockSpec((L, LANES), lambda h, i, nr: (0, COL_DK // LANES + h)),
         pl.BlockSpec((L, LANES), lambda h, i, nr: (0, COL_DV // LANES + h))],
        [proj16, proj16, proj16, pos_rows, pos_cols, table, subln_g.reshape(1, DIFF_V_DIM, 1), lam_params],
        "diff_attention", vt_scratch=True)


def _rel_bucket(dist):
    n = jnp.maximum(dist, 0)
    max_exact = REL_BUCKETS // 2
    nf = jnp.maximum(n, 1).astype(F32)
    large = max_exact + (jnp.log(nf / max_exact) / math.log(REL_MAX_DIST / max_exact)
                         * (REL_BUCKETS - max_exact)).astype(jnp.int32)
    return jnp.where(n < max_exact, n, jnp.minimum(large, REL_BUCKETS - 1))


def attention_tables(positions, rel_bias):
    L = positions.shape[0]
    t = ATT_T
    buckets = _rel_bucket(jnp.arange(LANES, dtype=jnp.int32))
    table = rel_bias[buckets].T * LOG2E
    pos_rows = positions.reshape(L // t, t)
    lo, hi = jnp.min(pos_rows, axis=1), jnp.max(pos_rows, axis=1)
    near = (lo[:, None] - hi[None, :] < LANES) | jnp.eye(L // t, dtype=bool)
    return table, pos_rows, positions.reshape(L // t, t, 1), near.astype(jnp.int32).reshape(-1)


def _router_kernel(x_ref, g_ref, sc_ref, sh_ref, rw_ref, h_ref, comb_ref, rank_ref, rank_t_ref, cum_ref, total_ref,
                   cnt_ref):
    t = TOK_T

    @pl.when(pl.program_id(0) == 0)
    def _():
        cnt_ref[...] = jnp.zeros_like(cnt_ref)

    h = _norm_mod(x_ref[...], g_ref[...], sc_ref[...], sh_ref[...])
    h_ref[...] = h.astype(BF16)
    lane = lax.broadcasted_iota(jnp.int32, (t, LANES), 1)
    logits = jnp.dot(h, rw_ref[...], precision=HIGHEST, preferred_element_type=F32)
    logits = jnp.where(lane < N_EXPERTS, logits, -jnp.inf)
    m1 = jnp.max(logits, axis=1, keepdims=True)
    i1 = jnp.min(jnp.where(logits == m1, lane, LANES), axis=1, keepdims=True)
    rest = jnp.where(lane == i1, -jnp.inf, logits)
    m2 = jnp.max(rest, axis=1, keepdims=True)
    i2 = jnp.min(jnp.where(rest == m2, lane, LANES), axis=1, keepdims=True)
    e2 = jnp.exp(m2 - m1)
    denom = 1.0 + e2
    comb_ref[...] = jnp.where(lane == i1, 1.0 / denom, 0.0) + jnp.where(lane == i2, e2 / denom, 0.0)
    sel = jnp.where((lane == i1) | (lane == i2), 1.0, 0.0)
    row = lax.broadcasted_iota(jnp.int32, (t, t), 0)
    col = lax.broadcasted_iota(jnp.int32, (t, t), 1)
    before = jnp.dot(jnp.where(row > col, 1.0, 0.0).astype(BF16), sel.astype(BF16), preferred_element_type=F32)
    carry = cnt_ref[...]
    cum_ref[0] = carry
    rank = jnp.where(sel > 0.0, before + carry, -1.0)
    rank_ref[...] = rank
    rank_t_ref[...] = rank.T
    carry = carry + jnp.sum(sel, axis=0, keepdims=True)
    cnt_ref[...] = carry
    total_ref[...] = carry


def moe_router(x, g, sc, sh, router_w):
    L, d = x.shape
    t = TOK_T
    rw = jnp.pad(router_w, ((0, 0), (0, LANES - N_EXPERTS)))
    vec = pl.BlockSpec((1, d), lambda i: (0, 0))
    tok = pl.BlockSpec((t, LANES), lambda i: (i, 0))
    return pl.pallas_call(
        _router_kernel,
        grid=(L // t,),
        in_specs=[pl.BlockSpec((t, d), lambda i: (i, 0)), vec, vec, vec,
                  pl.BlockSpec((d, LANES), lambda i: (0, 0))],
        out_specs=[pl.BlockSpec((t, d), lambda i: (i, 0)), tok, tok,
                   pl.BlockSpec((LANES, t), lambda i: (0, i)),
                   pl.BlockSpec((1, 1, LANES), lambda i: (i, 0, 0)),
                   pl.BlockSpec((1, LANES), lambda i: (0, 0))],
        out_shape=[jax.ShapeDtypeStruct((L, d), BF16), jax.ShapeDtypeStruct((L, LANES), F32),
                   jax.ShapeDtypeStruct((L, LANES), F32), jax.ShapeDtypeStruct((LANES, L), F32),
                   jax.ShapeDtypeStruct((L // t, 1, LANES), F32), jax.ShapeDtypeStruct((1, LANES), F32)],
        scratch_shapes=[pltpu.VMEM((1, LANES), F32)],
        compiler_params=_params("arbitrary"),
        name="moe_router",
    )(x, g, sc, sh, rw)


def _item_lists(hit, n_items):
    rows, cols = hit.shape
    running = jnp.cumsum(hit.reshape(-1).astype(jnp.int32))
    n_real = running[-1]
    k = jnp.arange(n_items, dtype=jnp.int32)
    real = k < n_real
    idx = jnp.searchsorted(running, jnp.minimum(k + 1, n_real), side="left", method="compare_all")
    idx = jnp.minimum(idx, rows * cols - 1).astype(jnp.int32)
    r, c = idx // cols, idx % cols
    prev_r = jnp.concatenate([jnp.full((1,), -1, jnp.int32), r[:-1]])
    next_r = jnp.concatenate([r[1:], jnp.full((1,), -1, jnp.int32)])
    first = real & (r != prev_r)
    last = real & ((r != next_r) | (k == n_real - 1))
    flags = real.astype(jnp.int32) + 2 * first.astype(jnp.int32) + 4 * last.astype(jnp.int32)
    return r, c, flags


def moe_plan(rank_t, cum, total, L):
    tm, tb = MOE_TM, TOK_T
    nb = L // tb
    n_tiles = 2 * L // tm + N_EXPERTS
    counts = total[0, :N_EXPERTS].astype(jnp.int32)
    tiles_e = (counts + tm - 1) // tm
    tile_end = jnp.cumsum(tiles_e)
    tile_start = tile_end - tiles_e
    tid = jnp.arange(n_tiles, dtype=jnp.int32)
    tile_valid = tid < tile_end[-1]
    tile_expert = jnp.minimum(jnp.searchsorted(tile_end, tid, side="right", method="compare_all"),
                              N_EXPERTS - 1).astype(jnp.int32)
    local_row = (tid - tile_start[tile_expert]) * tm
    r = rank_t[:N_EXPERTS].astype(jnp.int32)
    dest = jnp.where(r >= 0, r + (tile_start * tm)[:, None], -1)
    cum_i = cum[:, 0, :N_EXPERTS].astype(jnp.int32)
    cum_next = jnp.concatenate([cum_i[1:], counts[None, :]], axis=0)
    lo = cum_i[:, tile_expert].T
    hi = cum_next[:, tile_expert].T
    hit = tile_valid[:, None] & (lo < (local_row + tm)[:, None]) & (hi > local_row[:, None])
    n_items = n_tiles + N_EXPERTS * nb
    pad_hit = hit | ((~tile_valid)[:, None] & (jnp.arange(nb) == 0)[None, :])
    gather_items = _item_lists(pad_hit, n_items)
    kb, tt, fl = _item_lists(hit.T, n_items)
    return dict(tile_expert=tile_expert, tile_valid=tile_valid.astype(jnp.int32),
                dest_rows=dest, segment_start=(tile_start * tm).astype(jnp.int32),
                gather_items=gather_items, combine_items=(tt, kb, fl), n_tiles=n_tiles, n_items=n_items)


def _moe_gather_kernel(it_ref, ib_ref, if_ref, te_ref, dest_ref, h_ref, o_ref, acc_ref):
    i = pl.program_id(0)
    flag = if_ref[i]
    tile = it_ref[i]
    tm = MOE_TM

    @pl.when((flag & 2) != 0)
    def _():
        acc_ref[...] = jnp.zeros_like(acc_ref)

    @pl.when((flag & 1) != 0)
    def _():
        d = dest_ref[pl.ds(te_ref[tile], 1), :]
        rows = tile * tm + lax.broadcasted_iota(jnp.int32, (tm, 1), 0)
        onehot = jnp.where(d == rows, 1.0, 0.0).astype(BF16)
        acc_ref[...] += jnp.dot(onehot, h_ref[...], preferred_element_type=F32)

    @pl.when((flag & 4) != 0)
    def _():
        o_ref[...] = acc_ref[...].astype(o_ref.dtype)


def moe_gather(h, plan):
    L, d = h.shape
    tm, tb = MOE_TM, TOK_T
    it, ib, fl = plan["gather_items"]
    grid_spec = pltpu.PrefetchScalarGridSpec(
        num_scalar_prefetch=4,
        grid=(plan["n_items"],),
        in_specs=[pl.BlockSpec((N_EXPERTS, tb), lambda i, it, ib, fl, te: (0, ib[i])),
                  pl.BlockSpec((tb, d), lambda i, it, ib, fl, te: (ib[i], 0))],
        out_specs=pl.BlockSpec((tm, d), lambda i, it, ib, fl, te: (it[i], 0)),
        scratch_shapes=[pltpu.VMEM((tm, d), F32)],
    )
    return pl.pallas_call(
        _moe_gather_kernel,
        grid_spec=grid_spec,
        out_shape=jax.ShapeDtypeStruct((plan["n_tiles"] * tm, d), BF16),
        compiler_params=_params("arbitrary"),
        name="moe_gather",
    )(it, ib, fl, plan["tile_expert"], plan["dest_rows"], h)


def _moe_combine_kernel(it_ref, ib_ref, if_ref, te_ref, ts_ref, rank_ref, w_ref, ys_ref, x_ref, gate_ref, fg_ref,
                        o_ref, acc_ref, *, final_norm):
    i = pl.program_id(0)
    flag = if_ref[i]
    tile = it_ref[i]
    tm = MOE_TM

    @pl.when((flag & 2) != 0)
    def _():
        acc_ref[...] = jnp.zeros_like(acc_ref)

    @pl.when((flag & 1) != 0)
    def _():
        e = te_ref[tile]
        mine = lax.broadcasted_iota(jnp.int32, rank_ref.shape, 1) == e
        rank = jnp.sum(jnp.where(mine, rank_ref[...], 0.0), axis=1, keepdims=True)
        w = jnp.sum(jnp.where(mine, w_ref[...], 0.0), axis=1, keepdims=True)
        first = tile * tm - ts_ref[e]
        cols = (first + lax.broadcasted_iota(jnp.int32, (1, tm), 1)).astype(F32)
        onehot = jnp.where(rank == cols, 1.0, 0.0).astype(BF16)
        acc_ref[...] += w * jnp.dot(onehot, ys_ref[...], preferred_element_type=F32)

    @pl.when((flag & 4) != 0)
    def _():
        y = x_ref[...] + gate_ref[...] * acc_ref[...]
        if final_norm:
            y = y * lax.rsqrt(jnp.mean(y * y, axis=-1, keepdims=True) + EPS) * fg_ref[...]
        o_ref[...] = y


def moe_combine(ys, comb, rank, x, gate, plan, final_g=None):
    L, d = x.shape
    final_norm = final_g is not None
    if not final_norm:
        final_g = jnp.ones((1, d), F32)
    tm, tb = MOE_TM, TOK_T
    it, ib, fl = plan["combine_items"]
    tok = lambda i, it, ib, *_: (ib[i], 0)
    fixed = lambda i, *_: (0, 0)
    grid_spec = pltpu.PrefetchScalarGridSpec(
        num_scalar_prefetch=5,
        grid=(plan["n_items"],),
        in_specs=[pl.BlockSpec((tb, LANES), tok), pl.BlockSpec((tb, LANES), tok),
                  pl.BlockSpec((tm, d), lambda i, it, *_: (it[i], 0)),
                  pl.BlockSpec((tb, d), tok), pl.BlockSpec((1, d), fixed), pl.BlockSpec((1, d), fixed)],
        out_specs=pl.BlockSpec((tb, d), tok),
        scratch_shapes=[pltpu.VMEM((tb, d), F32)],
    )
    return pl.pallas_call(
        functools.partial(_moe_combine_kernel, final_norm=final_norm),
        grid_spec=grid_spec,
        out_shape=jax.ShapeDtypeStruct((L, d), F32),
        compiler_params=_params("arbitrary"),
        name="moe_combine",
    )(it, ib, fl, plan["tile_expert"], plan["segment_start"], rank, comb, ys, x, gate, final_g)


def moe_ffn(x, g, sc, sh, gate, router_w, w1, w3, w2, final_g=None):
    L = x.shape[0]
    h, comb, rank, rank_t, cum, total = moe_router(x, g, sc, sh, router_w)
    plan = moe_plan(rank_t, cum, total, L)
    xs = moe_gather(h, plan)
    wplan = weight_plan(plan["tile_expert"], plan["tile_valid"])
    act = grouped_matmul(xs, [w1, w3], wplan, MOE_TM, 1024, BF16, name="swiglu_up")
    ys = grouped_matmul(act, [w2], wplan, MOE_TM, 1024, BF16, name="swiglu_down")
    return moe_combine(ys, comb, rank, x, gate, plan, final_g)


def dense_ffn(x, g, sc, sh, gate, w1, w3, w2):
    L = x.shape[0]
    tm = 512
    h = norm_modulate(x, g, sc, sh, BF16)
    wplan = weight_plan(jnp.zeros((L // tm,), jnp.int32), jnp.ones((L // tm,), jnp.int32))
    act = grouped_matmul(h, [w1[None], w3[None]], wplan, tm, 512, BF16, name="swiglu_up")
    return grouped_matmul(act, [w2[None]], wplan, tm, 512, F32, x=x, gate=gate, name="swiglu_down")


def _permute_in_proj_kernel(w_ref, o_ref):
    n = w_ref.shape[0]
    o_ref[0:ORIG_DT] = w_ref[0:ORIG_DT].astype(BF16)
    o_ref[ORIG_DT:COL_DT] = w_ref[ORIG_DT + SSD_HEADS:n].astype(BF16)
    o_ref[COL_DT:COL_DT + SSD_HEADS] = w_ref[ORIG_DT:ORIG_DT + SSD_HEADS].astype(BF16)
    o_ref[COL_DT + SSD_HEADS:] = jnp.zeros((PROJ_W - COL_DT - SSD_HEADS, o_ref.shape[1]), BF16)


def _permute_in_proj(w_in, layer):
    _, d, n = w_in.shape
    tk = 256
    return pl.pallas_call(
        _permute_in_proj_kernel,
        grid=(d // tk,),
        in_specs=[pl.BlockSpec((None, n, tk), lambda i: (layer, 0, i))],
        out_specs=pl.BlockSpec((PROJ_W, tk), lambda i: (0, i)),
        out_shape=jax.ShapeDtypeStruct((PROJ_W, d), BF16),
        compiler_params=_params("parallel"),
        name="permute_in_proj",
    )(jnp.swapaxes(w_in, 1, 2))


def kernel(x, c, positions, rel_bias, w_ada, b_ada, norm_mix_g, w_in, conv_w, conv_b, dt_bias, a_log, d_skip, ssd_norm_g, moba_norm_g, diff_lambda, diff_subln_g, w_out, norm_ffn_g, dense_w1, dense_w3, dense_w2, router_w, expert_w1, expert_w3, expert_w2, final_g):
    batch, L, d = x.shape
    assert batch == 1 and d == D_MODEL
    x = x[0]
    mod = ada_modulation(c, w_ada, b_ada)
    table, pos_rows, pos_cols, near = attention_tables(positions[0], rel_bias)
    row = lambda v: v.reshape(1, -1)
    for layer in range(DEPTH):
        lambda_init = 0.8 - 0.6 * math.exp(-0.3 * layer)
        shift1, scale1, gate1, shift2, scale2, gate2 = (mod[layer, :, j * d:(j + 1) * d] for j in range(6))
        h = norm_modulate(x, row(norm_mix_g[layer]), scale1, shift1, BF16)
        proj, proj16 = in_projection(h, _permute_in_proj(w_in, layer))
        y_ssd = ssd_mixer(proj, conv_w[layer], conv_b[layer], dt_bias[layer], a_log[layer], d_skip[layer],
                          ssd_norm_g[layer])
        qa, ka, va = moba_prep(proj)
        y_moba = moba_attention(qa, ka, va, near, pos_rows, pos_cols, table, moba_norm_g[layer])
        y_diff = diff_attention(proj16, near, pos_rows, pos_cols, table, diff_lambda[layer],
                                diff_subln_g[layer], lambda_init)
        x = out_projection(y_ssd, y_moba, y_diff, w_out, layer, x, gate1)
        i = layer // 2
        g2 = row(norm_ffn_g[layer])
        if layer % 2 == 0:
            x = dense_ffn(x, g2, scale2, shift2, gate2, dense_w1[i], dense_w3[i], dense_w2[i])
        else:
            x = moe_ffn(x, g2, scale2, shift2, gate2, router_w[i], expert_w1[i], expert_w3[i], expert_w2[i],
                        final_g=row(final_g) if layer == DEPTH - 1 else None)
    if (DEPTH - 1) % 2 == 0:
        zero = jnp.zeros((1, d), F32)
        x = norm_modulate(x, row(final_g), zero, zero, F32)
    return x[None]
```

```python
import functools
import math

import jax
import jax.numpy as jnp
from jax import lax
from jax.experimental import pallas as pl
from jax.experimental.pallas import tpu as pltpu

F32 = jnp.float32
BF16 = jnp.bfloat16
HIGHEST = lax.Precision.HIGHEST

D_MODEL = 2048
DEPTH = 2
SSD_HEADS = 16
SSD_HEAD_DIM = 64
SSD_WIDTH = SSD_HEADS * SSD_HEAD_DIM
SSD_GROUPS = 2
SSD_STATE = 128
SSD_CONV = 4
SSD_CHUNK = 256
SSD_BC = 2 * SSD_GROUPS * SSD_STATE
MOBA_HEADS = 8
MOBA_HEAD_DIM = 64
MOBA_WIDTH = MOBA_HEADS * MOBA_HEAD_DIM
MOBA_BLOCK = 256
MOBA_TOPK = 3
DIFF_HEADS = 4
DIFF_QK_DIM = 64
DIFF_V_DIM = 128
DIFF_WIDTH = DIFF_HEADS * DIFF_V_DIM
REL_BUCKETS = 32
REL_MAX_DIST = 128
D_FF_DENSE = 5632
N_EXPERTS = 8
D_FF_EXPERT = 7168
EPS = 1e-6

LANES = 128
SUBLANES = 8
VMEM_LIMIT = 56 * 1024 * 1024

COL_Z = 0
COL_X = SSD_WIDTH
COL_BC = COL_X + SSD_WIDTH
COL_MQ = COL_BC + SSD_BC
COL_MK = COL_MQ + MOBA_WIDTH
COL_MV = COL_MK + MOBA_WIDTH
COL_DQ = COL_MV + MOBA_WIDTH
COL_DK = COL_DQ + DIFF_WIDTH
COL_DV = COL_DK + DIFF_WIDTH
COL_DT = COL_DV + DIFF_WIDTH
PROJ_W = COL_DT + LANES
ORIG_DT = SSD_WIDTH + SSD_WIDTH + SSD_BC

ATT_T = 256
ATT_CHUNK = 64
TOK_T = 512
MOE_TM = 256
NEG_BIG = -1e9
MOBA_MAX_BLOCKS = 32
LOG2E = math.log2(math.e)
MOBA_Q_SCALE = MOBA_HEAD_DIM ** -0.5 * LOG2E
DIFF_Q_SCALE = DIFF_QK_DIM ** -0.5 * LOG2E


def _silu(x):
    return x * (1.0 / (1.0 + jnp.exp(-x)))


def _softplus(x):
    return jnp.maximum(x, 0.0) + jnp.log1p(jnp.exp(-jnp.abs(x)))


def _params(*sem):
    return pltpu.CompilerParams(dimension_semantics=sem, vmem_limit_bytes=VMEM_LIMIT)


def _ada_kernel(c_ref, w_ref, b_ref, o_ref):
    ca = _silu(c_ref[...])
    o_ref[0] = jnp.sum(ca * w_ref[0], axis=0, keepdims=True) + b_ref[0]


def ada_modulation(c, w_ada, b_ada):
    depth, d, n = w_ada.shape
    tn = 2048
    return pl.pallas_call(
        _ada_kernel,
        grid=(depth, n // tn),
        in_specs=[pl.BlockSpec((d, 1), lambda l, j: (0, 0)),
                  pl.BlockSpec((1, d, tn), lambda l, j: (l, 0, j)),
                  pl.BlockSpec((1, 1, tn), lambda l, j: (l, 0, j))],
        out_specs=pl.BlockSpec((1, 1, tn), lambda l, j: (l, 0, j)),
        out_shape=jax.ShapeDtypeStruct((depth, 1, n), F32),
        compiler_params=_params("parallel", "parallel"),
        name="ada_modulation",
    )(c.reshape(d, 1), w_ada, b_ada.reshape(depth, 1, n))


def _norm_mod(x, g, sc, sh):
    ms = jnp.mean(x * x, axis=-1, keepdims=True)
    return (x * lax.rsqrt(ms + EPS) * g) * (1.0 + sc) + sh


def _norm_kernel(x_ref, g_ref, sc_ref, sh_ref, o_ref):
    o_ref[...] = _norm_mod(x_ref[...], g_ref[...], sc_ref[...], sh_ref[...]).astype(o_ref.dtype)


def norm_modulate(x, g, sc, sh, out_dtype):
    L, d = x.shape
    tm = 1024
    vec = pl.BlockSpec((1, d), lambda i: (0, 0))
    return pl.pallas_call(
        _norm_kernel,
        grid=(L // tm,),
        in_specs=[pl.BlockSpec((tm, d), lambda i: (i, 0)), vec, vec, vec],
        out_specs=pl.BlockSpec((tm, d), lambda i: (i, 0)),
        out_shape=jax.ShapeDtypeStruct((L, d), out_dtype),
        compiler_params=_params("parallel"),
        name="norm_modulate",
    )(x, g, sc, sh)


def _inproj_kernel(a_ref, w_ref, o32_ref, o16_ref):
    acc = lax.dot_general(a_ref[...], w_ref[...], (((1,), (1,)), ((), ())), preferred_element_type=F32)
    o32_ref[...] = acc
    o16_ref[...] = acc.astype(BF16)


def in_projection(h, w_t):
    L, k = h.shape
    n = w_t.shape[0]
    tm, tn = 1024, 1920
    return pl.pallas_call(
        _inproj_kernel,
        grid=(n // tn, L // tm),
        in_specs=[pl.BlockSpec((tm, k), lambda j, i: (i, 0)),
                  pl.BlockSpec((tn, k), lambda j, i: (j, 0))],
        out_specs=[pl.BlockSpec((tm, tn), lambda j, i: (i, j)),
                   pl.BlockSpec((tm, tn), lambda j, i: (i, j))],
        out_shape=[jax.ShapeDtypeStruct((L, n), F32), jax.ShapeDtypeStruct((L, n), BF16)],
        compiler_params=_params("parallel", "parallel"),
        name="in_projection",
    )(h, w_t)


def _outproj_kernel(ys_ref, ym_ref, yd_ref, ws_ref, wm_ref, wd_ref, x_ref, gate_ref, o_ref, wb_ref):
    @pl.when(pl.program_id(1) == 0)
    def _():
        wb_ref[0:SSD_WIDTH] = ws_ref[...].astype(BF16)
        wb_ref[SSD_WIDTH:SSD_WIDTH + MOBA_WIDTH] = wm_ref[...].astype(BF16)
        wb_ref[SSD_WIDTH + MOBA_WIDTH:] = wd_ref[...].astype(BF16)

    acc = jnp.dot(ys_ref[...], wb_ref[0:SSD_WIDTH], preferred_element_type=F32)
    acc += jnp.dot(ym_ref[...], wb_ref[SSD_WIDTH:SSD_WIDTH + MOBA_WIDTH], preferred_element_type=F32)
    acc += jnp.dot(yd_ref[...], wb_ref[SSD_WIDTH + MOBA_WIDTH:], preferred_element_type=F32)
    o_ref[...] = x_ref[...] + gate_ref[...] * acc


def out_projection(y_ssd, y_moba, y_diff, w_out, layer, x, gate):
    L, d = x.shape
    tm, tn = 512, 1024
    return pl.pallas_call(
        _outproj_kernel,
        grid=(d // tn, L // tm),
        in_specs=[pl.BlockSpec((tm, SSD_WIDTH), lambda j, i: (i, 0)),
                  pl.BlockSpec((tm, MOBA_WIDTH), lambda j, i: (i, 0)),
                  pl.BlockSpec((tm, DIFF_WIDTH), lambda j, i: (i, 0)),
                  pl.BlockSpec((None, SSD_WIDTH, tn), lambda j, i: (layer, 0, j)),
                  pl.BlockSpec((None, MOBA_WIDTH, tn), lambda j, i: (layer, SSD_WIDTH // MOBA_WIDTH, j)),
                  pl.BlockSpec((None, DIFF_WIDTH, tn),
                               lambda j, i: (layer, (SSD_WIDTH + MOBA_WIDTH) // DIFF_WIDTH, j)),
                  pl.BlockSpec((tm, tn), lambda j, i: (i, j)),
                  pl.BlockSpec((1, tn), lambda j, i: (0, j))],
        out_specs=pl.BlockSpec((tm, tn), lambda j, i: (i, j)),
        out_shape=jax.ShapeDtypeStruct((L, d), F32),
        scratch_shapes=[pltpu.VMEM((w_out.shape[1], tn), BF16)],
        compiler_params=_params("arbitrary", "arbitrary"),
        name="out_projection",
    )(y_ssd, y_moba, y_diff, w_out, w_out, w_out, x, gate)


CAST_COLS = 256
TILE_VALID = 1
TILE_NEW_WEIGHTS = 2


def weight_plan(tile_expert, tile_valid):
    prev = jnp.concatenate([jnp.full((1,), -1, jnp.int32), tile_expert[:-1]])
    first = tile_expert != prev
    ordinal = jnp.cumsum(first.astype(jnp.int32)) - 1
    n_blocks = ordinal[-1] + 1
    block_expert = jnp.zeros_like(tile_expert).at[ordinal].set(tile_expert)
    next_expert = block_expert[(ordinal + 1) % n_blocks]
    flags = tile_valid * TILE_VALID + first.astype(jnp.int32) * TILE_NEW_WEIGHTS
    return tile_expert, flags, ordinal, next_expert, n_blocks.reshape(1)


def _grouped_matmul_kernel(te_ref, tf_ref, to_ref, tx_ref, nb_ref, a_ref, *rest, n_mats, residual):
    w_hbm, rest = rest[:n_mats], rest[n_mats:]
    if residual:
        x_ref, gate_ref, o_ref, wf_ref, wb_ref, sem = rest
    else:
        o_ref, wf_ref, wb_ref, sem = rest
    n_slots = wf_ref.shape[0]
    j = pl.program_id(0)
    t = pl.program_id(1)
    tn = o_ref.shape[1]
    cw = CAST_COLS
    flags = tf_ref[t]
    new_weights = (flags & TILE_NEW_WEIGHTS) != 0
    block = j * nb_ref[0] + to_ref[t]
    slot = block % n_slots
    wraps = to_ref[t] == nb_ref[0] - 1
    has_next = jnp.logical_not(wraps & (j == pl.num_programs(0) - 1))

    def copies(expert, col_tile, dst):
        cols = pl.ds(pl.multiple_of(col_tile * tn, tn), tn)
        return [pltpu.make_async_copy(w.at[expert, :, cols], wf_ref.at[dst, m], sem.at[dst, m])
                for m, w in enumerate(w_hbm)]

    def start_next():
        for c in copies(tx_ref[t], j + wraps.astype(jnp.int32), (block + 1) % n_slots):
            c.start()

    @pl.when(new_weights)
    def _():
        @pl.when(block == 0)
        def _():
            for c in copies(te_ref[t], j, slot):
                c.start()

        if n_slots == 2:
            pl.when(has_next)(start_next)

        for c in copies(te_ref[t], j, slot):
            c.wait()

    def epilogue(accs, cols):
        if n_mats == 2:
            y = _silu(accs[0]) * accs[1]
        else:
            y = accs[0]
            if residual:
                y = x_ref[:, cols] + gate_ref[:, cols] * y
        o_ref[:, cols] = y.astype(o_ref.dtype)

    @pl.when(flags == TILE_VALID + TILE_NEW_WEIGHTS)
    def _():
        a = a_ref[...]
        for c in range(0, tn, cw):
            accs = []
            for m in range(n_mats):
                wb = wf_ref[slot, m, :, c:c + cw].astype(BF16)
                wb_ref[m, :, c:c + cw] = wb
                accs.append(jnp.dot(a, wb, preferred_element_type=F32))
            epilogue(accs, slice(c, c + cw))

    @pl.when(flags == TILE_VALID)
    def _():
        a = a_ref[...]
        epilogue([jnp.dot(a, wb_ref[m], preferred_element_type=F32) for m in range(n_mats)], slice(0, tn))

    @pl.when((flags & TILE_VALID) == 0)
    def _():
        o_ref[...] = jnp.zeros_like(o_ref)

    if n_slots == 1:
        pl.when(new_weights & has_next)(start_next)


def grouped_matmul(a, weights, plan, tm, tn, out_dtype, x=None, gate=None, n_slots=2, name="grouped_matmul"):
    rows, k = a.shape
    n = weights[0].shape[2]
    n_mats = len(weights)
    residual = x is not None
    idx = lambda f: (lambda j, t, *refs: f(j, t))
    in_specs = [pl.BlockSpec((tm, k), idx(lambda j, t: (t, 0)))] + [pl.BlockSpec(memory_space=pl.ANY)] * n_mats
    args = [a, *weights]
    if residual:
        in_specs += [pl.BlockSpec((tm, tn), idx(lambda j, t: (t, j))), pl.BlockSpec((1, tn), idx(lambda j, t: (0, j)))]
        args += [x, gate]
    grid_spec = pltpu.PrefetchScalarGridSpec(
        num_scalar_prefetch=5,
        grid=(n // tn, rows // tm),
        in_specs=in_specs,
        out_specs=pl.BlockSpec((tm, tn), idx(lambda j, t: (t, j))),
        scratch_shapes=[pltpu.VMEM((n_slots, n_mats, k, tn), F32), pltpu.VMEM((n_mats, k, tn), BF16),
                        pltpu.SemaphoreType.DMA((n_slots, n_mats))],
    )
    return pl.pallas_call(
        functools.partial(_grouped_matmul_kernel, n_mats=n_mats, residual=residual),
        grid_spec=grid_spec,
        out_shape=jax.ShapeDtypeStruct((rows, n), out_dtype),
        compiler_params=_params("arbitrary", "arbitrary"),
        name=name,
    )(*plan, *args)


def _causal_conv(cur, tail_ref, w_ref, b_ref):
    t = cur.shape[0]
    tail = tail_ref[...]
    w = w_ref[...]
    row8 = lax.broadcasted_iota(jnp.int32, (SUBLANES, cur.shape[1]), 0)
    acc = cur * w[SSD_CONV - 1:SSD_CONV]
    top = cur[0:SUBLANES] * w[SSD_CONV - 1:SSD_CONV]
    for s in range(1, SSD_CONV):
        wk = w[SSD_CONV - 1 - s:SSD_CONV - s]
        rolled = pltpu.roll(cur, s, axis=0)
        acc += rolled * wk
        top += jnp.where(row8 < s, pltpu.roll(tail, s, axis=0), rolled[0:SUBLANES]) * wk
    tail_ref[...] = cur[t - SUBLANES:t]
    return jnp.concatenate([top, acc[SUBLANES:]], axis=0) + b_ref[...]


def _bf16_terms(x):
    terms = []
    for _ in range(3):
        part = x.astype(BF16)
        terms.append(part)
        x = x - part.astype(F32)
    return terms


def _dot_exact_rhs01(x, onehot_bf16):
    return sum(jnp.dot(part, onehot_bf16, preferred_element_type=F32) for part in _bf16_terms(x))


def _dot_exact_lhs01(onehot_bf16, x):
    return sum(jnp.dot(onehot_bf16, part, preferred_element_type=F32) for part in _bf16_terms(x))


def _ssd_kernel(z_ref, x_ref, bc_ref, dt_ref, cwx_ref, cwb_ref, cbx_ref, cbb_ref, dtb_ref, alog_ref,
                dskip_ref, ng_ref, expand_ref, o_ref, tailx_ref, tailb_ref, state_ref, ybuf_ref):
    t = SSD_CHUNK
    hg = SSD_HEADS // SSD_GROUPS
    gw = SSD_WIDTH // SSD_GROUPS

    @pl.when(pl.program_id(0) == 0)
    def _():
        tailx_ref[...] = jnp.zeros_like(tailx_ref)
        tailb_ref[...] = jnp.zeros_like(tailb_ref)
        state_ref[...] = jnp.zeros_like(state_ref)

    xs = _silu(_causal_conv(x_ref[...], tailx_ref, cwx_ref, cbx_ref))
    bcm = _silu(_causal_conv(bc_ref[...], tailb_ref, cwb_ref, cbb_ref))
    dt = _softplus(dt_ref[...] + dtb_ref[...])
    a = -jnp.exp(alog_ref[...])
    row = lax.broadcasted_iota(jnp.int32, (t, t), 0)
    col = lax.broadcasted_iota(jnp.int32, (t, t), 1)
    tril = row >= col
    a_cs = _dot_exact_lhs01(jnp.where(tril, 1.0, 0.0).astype(BF16), dt * a)
    a_last = a_cs[t - 1:t]
    per_head = jnp.concatenate(
        [dt, jnp.exp(a_last - a_cs), jnp.exp(a_cs), jnp.broadcast_to(jnp.exp(a_last), (SUBLANES, LANES))], axis=0)
    spread = _dot_exact_rhs01(per_head, expand_ref[...])
    dt_x, to_end_x, ea_x, cd_x = spread[0:t], spread[t:2 * t], spread[2 * t:3 * t], spread[3 * t:3 * t + 1]
    xdt = xs * dt_x
    xdt_b = xdt.astype(BF16)
    xw_b = (xdt * to_end_x).astype(BF16)
    a_cs_t = a_cs.T

    y_off = []
    for g in range(SSD_GROUPS):
        bm = bcm[:, g * SSD_STATE:(g + 1) * SSD_STATE]
        cm_b = bcm[:, (SSD_GROUPS + g) * SSD_STATE:(SSD_GROUPS + g + 1) * SSD_STATE].astype(BF16)
        cb = lax.dot_general(cm_b, bm.astype(BF16), (((1,), (1,)), ((), ())), preferred_element_type=F32)
        h_prev = state_ref[g]
        y_off.append(jnp.dot(cm_b, h_prev.astype(BF16), preferred_element_type=F32)
                     * ea_x[:, g * gw:(g + 1) * gw])
        st_new = jnp.dot(bm.T.astype(BF16), xw_b[:, g * gw:(g + 1) * gw], preferred_element_type=F32)
        state_ref[g] = h_prev * cd_x[:, g * gw:(g + 1) * gw] + st_new
        for r in range(0, hg, 2):
            pair = []
            for h in (g * hg + r, g * hg + r + 1):
                diff = a_cs[:, h:h + 1] - a_cs_t[h:h + 1, :]
                m = (cb * jnp.exp(jnp.where(tril, diff, -jnp.inf))).astype(BF16)
                pair.append(jnp.dot(m, xdt_b[:, h * SSD_HEAD_DIM:(h + 1) * SSD_HEAD_DIM],
                                    preferred_element_type=F32))
            lo = (g * hg + r) * SSD_HEAD_DIM
            ybuf_ref[:, lo:lo + 2 * SSD_HEAD_DIM] = jnp.concatenate(pair, axis=1)

    y = ybuf_ref[...] + jnp.concatenate(y_off, axis=1) + xs * dskip_ref[...]
    y = y * _silu(z_ref[...])
    outs = []
    for g in range(SSD_GROUPS):
        yg = y[:, g * gw:(g + 1) * gw]
        outs.append(yg * lax.rsqrt(jnp.mean(yg * yg, axis=-1, keepdims=True) + EPS))
    o_ref[...] = (jnp.concatenate(outs, axis=1) * ng_ref[...]).astype(o_ref.dtype)


def ssd_mixer(proj, conv_w, conv_b, dt_bias, a_log, d_skip, norm_g):
    L = proj.shape[0]
    t = SSD_CHUNK
    assert L % t == 0

    def pad_lanes(v):
        return jnp.pad(v, (0, LANES - v.shape[0])).reshape(1, LANES)

    expand = (jnp.arange(SSD_WIDTH)[None, :] // SSD_HEAD_DIM == jnp.arange(LANES)[:, None]).astype(BF16)
    full = lambda shape: pl.BlockSpec(shape, lambda c: (0,) * len(shape))
    return pl.pallas_call(
        _ssd_kernel,
        grid=(L // t,),
        in_specs=[pl.BlockSpec((t, SSD_WIDTH), lambda c: (c, COL_Z // SSD_WIDTH)),
                  pl.BlockSpec((t, SSD_WIDTH), lambda c: (c, COL_X // SSD_WIDTH)),
                  pl.BlockSpec((t, SSD_BC), lambda c: (c, COL_BC // SSD_BC)),
                  pl.BlockSpec((t, LANES), lambda c: (c, COL_DT // LANES)),
                  full((SSD_CONV, SSD_WIDTH)), full((SSD_CONV, SSD_BC)),
                  full((1, SSD_WIDTH)), full((1, SSD_BC)),
                  full((1, LANES)), full((1, LANES)), full((1, SSD_WIDTH)), full((1, SSD_WIDTH)),
                  full((LANES, SSD_WIDTH))],
        out_specs=pl.BlockSpec((t, SSD_WIDTH), lambda c: (c, 0)),
        out_shape=jax.ShapeDtypeStruct((L, SSD_WIDTH), BF16),
        scratch_shapes=[pltpu.VMEM((SUBLANES, SSD_WIDTH), F32), pltpu.VMEM((SUBLANES, SSD_BC), F32),
                        pltpu.VMEM((SSD_GROUPS, SSD_STATE, SSD_WIDTH // SSD_GROUPS), F32),
                        pltpu.VMEM((t, SSD_WIDTH), F32)],
        compiler_params=_params("arbitrary"),
        name="ssd_mixer",
    )(proj, proj, proj, proj,
      conv_w[:, :SSD_WIDTH], conv_w[:, SSD_WIDTH:], conv_b[:SSD_WIDTH].reshape(1, -1),
      conv_b[SSD_WIDTH:].reshape(1, -1), pad_lanes(dt_bias), pad_lanes(a_log),
      jnp.repeat(d_skip, SSD_HEAD_DIM).reshape(1, -1), norm_g.reshape(1, -1), expand)


def _moba_prep_kernel(q_ref, k_ref, v_ref, qa_ref, ka_ref, va_ref, kmean_ref):
    own = pl.program_id(0)
    t = MOBA_BLOCK
    dh = MOBA_HEAD_DIM
    nbl = MOBA_MAX_BLOCKS

    @pl.when(own == 0)
    def _():
        kmean_ref[...] = jnp.zeros_like(kmean_ref)

    lane = lax.broadcasted_iota(jnp.int32, (t, LANES), 1)
    onehot = jnp.where(lane == dh + own, 1.0, 0.0)
    blk = lax.broadcasted_iota(jnp.int32, (nbl, t), 0)
    q_t = q_ref[...].T
    k = k_ref[...]
    k_mean = jnp.mean(k, axis=0, keepdims=True)
    va_ref[:, 0] = v_ref[...].T.reshape(MOBA_HEADS, dh, t).astype(BF16)
    for h in range(MOBA_HEADS):
        qh_t = q_t[h * dh:(h + 1) * dh]
        gate = jnp.dot(kmean_ref[h], qh_t, precision=HIGHEST, preferred_element_type=F32)
        gate = jnp.where(blk < own, gate, -jnp.inf)
        sel = blk >= own
        for _ in range(MOBA_TOPK):
            m = jnp.max(gate, axis=0, keepdims=True)
            idx = jnp.min(jnp.where(gate == m, blk, nbl), axis=0, keepdims=True)
            sel = sel | ((blk == idx) & (m > -jnp.inf))
            gate = jnp.where(blk == idx, -jnp.inf, gate)
        offs_t = jnp.where(sel, 0.0, NEG_BIG)
        qa_ref[h] = jnp.concatenate([qh_t * MOBA_Q_SCALE, offs_t, jnp.zeros((LANES - dh - nbl, t), F32)],
                                    axis=0).astype(BF16)
        pair = k[:, (h // 2) * LANES:(h // 2 + 1) * LANES]
        if h % 2:
            pair = pltpu.roll(pair, dh, axis=1)
        ka_ref[h] = jnp.where(lane < dh, pair, onehot).astype(BF16)
        kmean_ref[h, pl.ds(own, 1), :] = k_mean[:, h * dh:(h + 1) * dh]


def moba_prep(proj):
    L = proj.shape[0]
    t = MOBA_BLOCK
    assert L % t == 0 and L // t <= MOBA_MAX_BLOCKS
    return pl.pallas_call(
        _moba_prep_kernel,
        grid=(L // t,),
        in_specs=[pl.BlockSpec((t, MOBA_WIDTH), lambda i: (i, COL_MQ // MOBA_WIDTH)),
                  pl.BlockSpec((t, MOBA_WIDTH), lambda i: (i, COL_MK // MOBA_WIDTH)),
                  pl.BlockSpec((t, MOBA_WIDTH), lambda i: (i, COL_MV // MOBA_WIDTH))],
        out_specs=[pl.BlockSpec((MOBA_HEADS, LANES, t), lambda i: (0, 0, i)),
                   pl.BlockSpec((MOBA_HEADS, t, LANES), lambda i: (0, i, 0)),
                   pl.BlockSpec((MOBA_HEADS, 1, MOBA_HEAD_DIM, t), lambda i: (0, i, 0, 0))],
        out_shape=[jax.ShapeDtypeStruct((MOBA_HEADS, LANES, L), BF16),
                   jax.ShapeDtypeStruct((MOBA_HEADS, L, LANES), BF16),
                   jax.ShapeDtypeStruct((MOBA_HEADS, L // t, MOBA_HEAD_DIM, t), BF16)],
        scratch_shapes=[pltpu.VMEM((MOBA_HEADS, MOBA_MAX_BLOCKS, MOBA_HEAD_DIM), F32)],
        compiler_params=_params("arbitrary"),
        name="moba_prep",
    )(proj, proj, proj)


def _col_reduce(x, op):
    while x.shape[0] > SUBLANES:
        half = x.shape[0] // 2
        x = op(x[:half], x[half:])
    return jnp.max(x, axis=0, keepdims=True) if op is jnp.maximum else jnp.sum(x, axis=0, keepdims=True)


def _attention_kernel(near_ref, q_ref, k_ref, v_ref, posr_ref, posc_ref, tbl_ref, g_ref, lam_ref, o_ref,
                      m_ref, l_ref, al_ref, acc_ref, sa_ref, sb_ref, pa_ref, pb_ref, vt_ref, *, moba, lambda_init):
    hp = pl.program_id(0)
    qi = pl.program_id(1)
    nq = pl.num_programs(1)
    t = ATT_T
    m_ref[...] = jnp.full_like(m_ref, -jnp.inf)
    l_ref[...] = jnp.zeros_like(l_ref)
    acc_ref[...] = jnp.zeros_like(acc_ref)
    if moba:
        q_t = [q_ref[0], q_ref[1]]
        heads = [2 * hp, 2 * hp + 1]
    else:
        q = (q_ref[...].astype(F32) * DIFF_Q_SCALE).T.astype(BF16)
        half = lax.broadcasted_iota(jnp.int32, q.shape, 0) < DIFF_QK_DIM
        q_t = [jnp.where(half, q, jnp.zeros_like(q)), jnp.where(half, jnp.zeros_like(q), q)]
        heads = [MOBA_HEADS + hp]

        @pl.when(qi == 0)
        def _():
            def transpose_tile(i, carry):
                vt_ref[i] = v_ref[pl.ds(pl.multiple_of(i * t, t), t), :].astype(F32).T.astype(BF16)
                return carry

            lax.fori_loop(0, nq, transpose_tile, 0)
    trow = [tbl_ref[pl.ds(hd, 1), :] for hd in heads]
    posq = posr_ref[pl.ds(qi, 1), :]

    def scores(ki, dst_ref):
        rows = pl.ds(pl.multiple_of(ki * t, t), t)
        for s in range(2):
            k = k_ref[s, rows, :] if moba else k_ref[rows, :]
            dst_ref[s] = jnp.dot(k, q_t[s], preferred_element_type=F32)

    def accumulate(ki, p_ref):
        for s in range(2):
            v_t = v_ref[s, ki] if moba else vt_ref[ki]
            pv = jnp.dot(v_t, p_ref[s], preferred_element_type=F32)
            acc_ref[s] = al_ref[s] * acc_ref[s] + pv

    scores(0, sa_ref)
    pb_ref[...] = jnp.zeros_like(pb_ref)
    al_ref[...] = jnp.ones_like(al_ref)

    def tile(ki, general, cur_ref, nxt_ref, p_ref, p_prev_ref):
        scores(jnp.minimum(ki + 1, qi), nxt_ref)
        accumulate(jnp.maximum(ki - 1, 0), p_prev_ref)
        ch = ATT_CHUNK
        mx = [None, None]
        for r in range(0, t, ch):
            if general:
                dist = jnp.clip(posq - posc_ref[ki, r:r + ch, :], 0, LANES - 1)
                bias = [jnp.concatenate(
                    [jnp.take_along_axis(jnp.broadcast_to(tr, (ch, LANES)), dist[:, j * LANES:(j + 1) * LANES],
                                         axis=1) for j in range(t // LANES)], axis=1) for tr in trow]
                key = lax.broadcasted_iota(jnp.int32, (ch, t), 0) + r
                qry = lax.broadcasted_iota(jnp.int32, (ch, t), 1)
                causal = key + ki * t <= qry + qi * t
            for s in range(2):
                blk = cur_ref[s, r:r + ch, :]
                if general:
                    blk = jnp.where(causal, blk + bias[s if moba else 0], -jnp.inf)
                    cur_ref[s, r:r + ch, :] = blk
                mx[s] = blk if mx[s] is None else jnp.maximum(mx[s], blk)
        for s in range(2):
            m_old = m_ref[s]
            m_tile = _col_reduce(mx[s], jnp.maximum)
            if general:
                m_new = jnp.maximum(m_old, m_tile)
                shift = m_new
            else:
                c = trow[s if moba else 0][:, LANES - 1:LANES]
                m_new = jnp.maximum(m_old, m_tile + c)
                shift = m_new - c
            alpha = jnp.exp2(m_old - m_new)
            sm = None
            for r in range(0, t, ch):
                p = jnp.exp2(cur_ref[s, r:r + ch, :] - shift)
                p_ref[s, r:r + ch, :] = p.astype(BF16)
                sm = p if sm is None else sm + p
            l_ref[s] = alpha * l_ref[s] + _col_reduce(sm, jnp.add)
            al_ref[s] = alpha
            m_ref[s] = m_new

    def step(ki, *bufs):
        flag = near_ref[qi * nq + ki]

        @pl.when(flag == 0)
        def _():
            tile(ki, False, *bufs)

        @pl.when(flag != 0)
        def _():
            tile(ki, True, *bufs)

    def pair(j, carry):
        step(2 * j, sa_ref, sb_ref, pa_ref, pb_ref)
        step(2 * j + 1, sb_ref, sa_ref, pb_ref, pa_ref)
        return carry

    lax.fori_loop(0, (qi + 1) // 2, pair, 0)

    @pl.when(qi % 2 == 0)
    def _():
        step(qi, sa_ref, sb_ref, pa_ref, pb_ref)
        accumulate(qi, pa_ref)

    @pl.when(qi % 2 == 1)
    def _():
        accumulate(qi, pb_ref)

    if moba:
        outs = []
        for s in range(2):
            o = acc_ref[s] / l_ref[s]
            ms = jnp.sum(o * o, axis=0, keepdims=True) * (1.0 / MOBA_HEAD_DIM)
            outs.append(o * lax.rsqrt(ms + EPS) * g_ref[s])
        y_t = jnp.concatenate(outs, axis=0)
    else:
        lp = lam_ref[...]
        lam = (jnp.exp(jnp.sum(lp[0:1] * lp[1:2], axis=1, keepdims=True))
               - jnp.exp(jnp.sum(lp[2:3] * lp[3:4], axis=1, keepdims=True)) + lambda_init)
        o = acc_ref[0] / l_ref[0] - lam * (acc_ref[1] / l_ref[1])
        ms = jnp.sum(o * o, axis=0, keepdims=True) * (1.0 / DIFF_V_DIM)
        y_t = (o * lax.rsqrt(ms + EPS) * g_ref[0]) * (1.0 - lambda_init)
    o_ref[...] = y_t.T.astype(o_ref.dtype)


def _attention_call(kernel, steps, dv, L, near, in_specs, args, name, vt_scratch=False):
    t = ATT_T
    full = lambda a: pl.BlockSpec(a.shape, lambda h, i, nr: (0,) * a.ndim)
    grid_spec = pltpu.PrefetchScalarGridSpec(
        num_scalar_prefetch=1,
        grid=(steps, L // t),
        in_specs=in_specs + [full(a) for a in args[len(in_specs):]],
        out_specs=pl.BlockSpec((t, LANES), lambda h, i, nr: (i, h)),
        scratch_shapes=[pltpu.VMEM((2, 1, t), F32), pltpu.VMEM((2, 1, t), F32), pltpu.VMEM((2, 1, t), F32),
                        pltpu.VMEM((2, dv, t), F32), pltpu.VMEM((2, t, t), F32), pltpu.VMEM((2, t, t), F32),
                        pltpu.VMEM((2, t, t), BF16), pltpu.VMEM((2, t, t), BF16),
                        pltpu.VMEM((L // t if vt_scratch else 1, dv, t), BF16)],
    )
    return pl.pallas_call(kernel, grid_spec=grid_spec, out_shape=jax.ShapeDtypeStruct((L, steps * LANES), BF16),
                          compiler_params=_params("parallel", "arbitrary"), name=name)(near, *args)


def moba_attention(qa_t, ka, va_t, near, pos_rows, pos_cols, table, norm_g):
    heads, L, _ = ka.shape
    t = ATT_T
    kernel = functools.partial(_attention_kernel, moba=True, lambda_init=None)
    g = norm_g.reshape(heads, MOBA_HEAD_DIM, 1)
    return _attention_call(
        kernel, heads // 2, MOBA_HEAD_DIM, L, near,
        [pl.BlockSpec((2, LANES, t), lambda h, i, nr: (h, 0, i)),
         pl.BlockSpec((2, L, LANES), lambda h, i, nr: (h, 0, 0)),
         pl.BlockSpec((2, L // t, MOBA_HEAD_DIM, t), lambda h, i, nr: (h, 0, 0, 0)),
         pl.BlockSpec(pos_rows.shape, lambda h, i, nr: (0, 0)),
         pl.BlockSpec(pos_cols.shape, lambda h, i, nr: (0, 0, 0)),
         pl.BlockSpec(table.shape, lambda h, i, nr: (0, 0)),
         pl.BlockSpec((2, MOBA_HEAD_DIM, 1), lambda h, i, nr: (h, 0, 0))],
        [qa_t, ka, va_t, pos_rows, pos_cols, table, g, jnp.zeros((4, DIFF_QK_DIM), F32)], "moba_attention")


def diff_attention(proj16, near, pos_rows, pos_cols, table, lam_params, subln_g, lambda_init):
    L = proj16.shape[0]
    t = ATT_T
    kernel = functools.partial(_attention_kernel, moba=False, lambda_init=lambda_init)
    return _attention_call(
        kernel, DIFF_HEADS, DIFF_V_DIM, L, near,
        [pl.BlockSpec((t, LANES), lambda h, i, nr: (i, COL_DQ // LANES + h)),
         pl.BlockSpec((L, LANES), lambda h, i, nr: (0, COL_DK // LANES + h)),
         pl.BlockSpec((L, LANES), lambda h, i, nr: (0, COL_DV // LANES + h))],
        [proj16, proj16, proj16, pos_rows, pos_cols, table, subln_g.reshape(1, DIFF_V_DIM, 1), lam_params],
        "diff_attention", vt_scratch=True)


def _rel_bucket(dist):
    n = jnp.maximum(dist, 0)
    max_exact = REL_BUCKETS // 2
    nf = jnp.maximum(n, 1).astype(F32)
    large = max_exact + (jnp.log(nf / max_exact) / math.log(REL_MAX_DIST / max_exact)
                         * (REL_BUCKETS - max_exact)).astype(jnp.int32)
    return jnp.where(n < max_exact, n, jnp.minimum(large, REL_BUCKETS - 1))


def attention_tables(positions, rel_bias):
    L = positions.shape[0]
    t = ATT_T
    buckets = _rel_bucket(jnp.arange(LANES, dtype=jnp.int32))
    table = rel_bias[buckets].T * LOG2E
    pos_rows = positions.reshape(L // t, t)
    lo, hi = jnp.min(pos_rows, axis=1), jnp.max(pos_rows, axis=1)
    near = (lo[:, None] - hi[None, :] < LANES) | jnp.eye(L // t, dtype=bool)
    return table, pos_rows, positions.reshape(L // t, t, 1), near.astype(jnp.int32).reshape(-1)


def _router_kernel(x_ref, g_ref, sc_ref, sh_ref, rw_ref, h_ref, comb_ref, rank_ref, rank_t_ref, cum_ref, total_ref,
                   cnt_ref):
    t = TOK_T

    @pl.when(pl.program_id(0) == 0)
    def _():
        cnt_ref[...] = jnp.zeros_like(cnt_ref)

    h = _norm_mod(x_ref[...], g_ref[...], sc_ref[...], sh_ref[...])
    h_ref[...] = h.astype(BF16)
    lane = lax.broadcasted_iota(jnp.int32, (t, LANES), 1)
    logits = jnp.dot(h, rw_ref[...], precision=HIGHEST, preferred_element_type=F32)
    logits = jnp.where(lane < N_EXPERTS, logits, -jnp.inf)
    m1 = jnp.max(logits, axis=1, keepdims=True)
    i1 = jnp.min(jnp.where(logits == m1, lane, LANES), axis=1, keepdims=True)
    rest = jnp.where(lane == i1, -jnp.inf, logits)
    m2 = jnp.max(rest, axis=1, keepdims=True)
    i2 = jnp.min(jnp.where(rest == m2, lane, LANES), axis=1, keepdims=True)
    e2 = jnp.exp(m2 - m1)
    denom = 1.0 + e2
    comb_ref[...] = jnp.where(lane == i1, 1.0 / denom, 0.0) + jnp.where(lane == i2, e2 / denom, 0.0)
    sel = jnp.where((lane == i1) | (lane == i2), 1.0, 0.0)
    row = lax.broadcasted_iota(jnp.int32, (t, t), 0)
    col = lax.broadcasted_iota(jnp.int32, (t, t), 1)
    before = jnp.dot(jnp.where(row > col, 1.0, 0.0).astype(BF16), sel.astype(BF16), preferred_element_type=F32)
    carry = cnt_ref[...]
    cum_ref[0] = carry
    rank = jnp.where(sel > 0.0, before + carry, -1.0)
    rank_ref[...] = rank
    rank_t_ref[...] = rank.T
    carry = carry + jnp.sum(sel, axis=0, keepdims=True)
    cnt_ref[...] = carry
    total_ref[...] = carry


def moe_router(x, g, sc, sh, router_w):
    L, d = x.shape
    t = TOK_T
    rw = jnp.pad(router_w, ((0, 0), (0, LANES - N_EXPERTS)))
    vec = pl.BlockSpec((1, d), lambda i: (0, 0))
    tok = pl.BlockSpec((t, LANES), lambda i: (i, 0))
    return pl.pallas_call(
        _router_kernel,
        grid=(L // t,),
        in_specs=[pl.BlockSpec((t, d), lambda i: (i, 0)), vec, vec, vec,
                  pl.BlockSpec((d, LANES), lambda i: (0, 0))],
        out_specs=[pl.BlockSpec((t, d), lambda i: (i, 0)), tok, tok,
                   pl.BlockSpec((LANES, t), lambda i: (0, i)),
                   pl.BlockSpec((1, 1, LANES), lambda i: (i, 0, 0)),
                   pl.BlockSpec((1, LANES), lambda i: (0, 0))],
        out_shape=[jax.ShapeDtypeStruct((L, d), BF16), jax.ShapeDtypeStruct((L, LANES), F32),
                   jax.ShapeDtypeStruct((L, LANES), F32), jax.ShapeDtypeStruct((LANES, L), F32),
                   jax.ShapeDtypeStruct((L // t, 1, LANES), F32), jax.ShapeDtypeStruct((1, LANES), F32)],
        scratch_shapes=[pltpu.VMEM((1, LANES), F32)],
        compiler_params=_params("arbitrary"),
        name="moe_router",
    )(x, g, sc, sh, rw)


def _item_lists(hit, n_items):
    rows, cols = hit.shape
    running = jnp.cumsum(hit.reshape(-1).astype(jnp.int32))
    n_real = running[-1]
    k = jnp.arange(n_items, dtype=jnp.int32)
    real = k < n_real
    idx = jnp.searchsorted(running, jnp.minimum(k + 1, n_real), side="left", method="compare_all")
    idx = jnp.minimum(idx, rows * cols - 1).astype(jnp.int32)
    r, c = idx // cols, idx % cols
    prev_r = jnp.concatenate([jnp.full((1,), -1, jnp.int32), r[:-1]])
    next_r = jnp.concatenate([r[1:], jnp.full((1,), -1, jnp.int32)])
    first = real & (r != prev_r)
    last = real & ((r != next_r) | (k == n_real - 1))
    flags = real.astype(jnp.int32) + 2 * first.astype(jnp.int32) + 4 * last.astype(jnp.int32)
    return r, c, flags


def moe_plan(rank_t, cum, total, L):
    tm, tb = MOE_TM, TOK_T
    nb = L // tb
    n_tiles = 2 * L // tm + N_EXPERTS
    counts = total[0, :N_EXPERTS].astype(jnp.int32)
    tiles_e = (counts + tm - 1) // tm
    tile_end = jnp.cumsum(tiles_e)
    tile_start = tile_end - tiles_e
    tid = jnp.arange(n_tiles, dtype=jnp.int32)
    tile_valid = tid < tile_end[-1]
    tile_expert = jnp.minimum(jnp.searchsorted(tile_end, tid, side="right", method="compare_all"),
                              N_EXPERTS - 1).astype(jnp.int32)
    local_row = (tid - tile_start[tile_expert]) * tm
    r = rank_t[:N_EXPERTS].astype(jnp.int32)
    dest = jnp.where(r >= 0, r + (tile_start * tm)[:, None], -1)
    cum_i = cum[:, 0, :N_EXPERTS].astype(jnp.int32)
    cum_next = jnp.concatenate([cum_i[1:], counts[None, :]], axis=0)
    lo = cum_i[:, tile_expert].T
    hi = cum_next[:, tile_expert].T
    hit = tile_valid[:, None] & (lo < (local_row + tm)[:, None]) & (hi > local_row[:, None])
    n_items = n_tiles + N_EXPERTS * nb
    pad_hit = hit | ((~tile_valid)[:, None] & (jnp.arange(nb) == 0)[None, :])
    gather_items = _item_lists(pad_hit, n_items)
    kb, tt, fl = _item_lists(hit.T, n_items)
    return dict(tile_expert=tile_expert, tile_valid=tile_valid.astype(jnp.int32),
                dest_rows=dest, segment_start=(tile_start * tm).astype(jnp.int32),
                gather_items=gather_items, combine_items=(tt, kb, fl), n_tiles=n_tiles, n_items=n_items)


def _moe_gather_kernel(it_ref, ib_ref, if_ref, te_ref, dest_ref, h_ref, o_ref, acc_ref):
    i = pl.program_id(0)
    flag = if_ref[i]
    tile = it_ref[i]
    tm = MOE_TM

    @pl.when((flag & 2) != 0)
    def _():
        acc_ref[...] = jnp.zeros_like(acc_ref)

    @pl.when((flag & 1) != 0)
    def _():
        d = dest_ref[pl.ds(te_ref[tile], 1), :]
        rows = tile * tm + lax.broadcasted_iota(jnp.int32, (tm, 1), 0)
        onehot = jnp.where(d == rows, 1.0, 0.0).astype(BF16)
        acc_ref[...] += jnp.dot(onehot, h_ref[...], preferred_element_type=F32)

    @pl.when((flag & 4) != 0)
    def _():
        o_ref[...] = acc_ref[...].astype(o_ref.dtype)


def moe_gather(h, plan):
    L, d = h.shape
    tm, tb = MOE_TM, TOK_T
    it, ib, fl = plan["gather_items"]
    grid_spec = pltpu.PrefetchScalarGridSpec(
        num_scalar_prefetch=4,
        grid=(plan["n_items"],),
        in_specs=[pl.BlockSpec((N_EXPERTS, tb), lambda i, it, ib, fl, te: (0, ib[i])),
                  pl.BlockSpec((tb, d), lambda i, it, ib, fl, te: (ib[i], 0))],
        out_specs=pl.BlockSpec((tm, d), lambda i, it, ib, fl, te: (it[i], 0)),
        scratch_shapes=[pltpu.VMEM((tm, d), F32)],
    )
    return pl.pallas_call(
        _moe_gather_kernel,
        grid_spec=grid_spec,
        out_shape=jax.ShapeDtypeStruct((plan["n_tiles"] * tm, d), BF16),
        compiler_params=_params("arbitrary"),
        name="moe_gather",
    )(it, ib, fl, plan["tile_expert"], plan["dest_rows"], h)


def _moe_combine_kernel(it_ref, ib_ref, if_ref, te_ref, ts_ref, rank_ref, w_ref, ys_ref, x_ref, gate_ref, fg_ref,
                        o_ref, acc_ref, *, final_norm):
    i = pl.program_id(0)
    flag = if_ref[i]
    tile = it_ref[i]
    tm = MOE_TM

    @pl.when((flag & 2) != 0)
    def _():
        acc_ref[...] = jnp.zeros_like(acc_ref)

    @pl.when((flag & 1) != 0)
    def _():
        e = te_ref[tile]
        mine = lax.broadcasted_iota(jnp.int32, rank_ref.shape, 1) == e
        rank = jnp.sum(jnp.where(mine, rank_ref[...], 0.0), axis=1, keepdims=True)
        w = jnp.sum(jnp.where(mine, w_ref[...], 0.0), axis=1, keepdims=True)
        first = tile * tm - ts_ref[e]
        cols = (first + lax.broadcasted_iota(jnp.int32, (1, tm), 1)).astype(F32)
        onehot = jnp.where(rank == cols, 1.0, 0.0).astype(BF16)
        acc_ref[...] += w * jnp.dot(onehot, ys_ref[...], preferred_element_type=F32)

    @pl.when((flag & 4) != 0)
    def _():
        y = x_ref[...] + gate_ref[...] * acc_ref[...]
        if final_norm:
            y = y * lax.rsqrt(jnp.mean(y * y, axis=-1, keepdims=True) + EPS) * fg_ref[...]
        o_ref[...] = y


def moe_combine(ys, comb, rank, x, gate, plan, final_g=None):
    L, d = x.shape
    final_norm = final_g is not None
    if not final_norm:
        final_g = jnp.ones((1, d), F32)
    tm, tb = MOE_TM, TOK_T
    it, ib, fl = plan["combine_items"]
    tok = lambda i, it, ib, *_: (ib[i], 0)
    fixed = lambda i, *_: (0, 0)
    grid_spec = pltpu.PrefetchScalarGridSpec(
        num_scalar_prefetch=5,
        grid=(plan["n_items"],),
        in_specs=[pl.BlockSpec((tb, LANES), tok), pl.BlockSpec((tb, LANES), tok),
                  pl.BlockSpec((tm, d), lambda i, it, *_: (it[i], 0)),
                  pl.BlockSpec((tb, d), tok), pl.BlockSpec((1, d), fixed), pl.BlockSpec((1, d), fixed)],
        out_specs=pl.BlockSpec((tb, d), tok),
        scratch_shapes=[pltpu.VMEM((tb, d), F32)],
    )
    return pl.pallas_call(
        functools.partial(_moe_combine_kernel, final_norm=final_norm),
        grid_spec=grid_spec,
        out_shape=jax.ShapeDtypeStruct((L, d), F32),
        compiler_params=_params("arbitrary"),
        name="moe_combine",
    )(it, ib, fl, plan["tile_expert"], plan["segment_start"], rank, comb, ys, x, gate, final_g)


def moe_ffn(x, g, sc, sh, gate, router_w, w1, w3, w2, final_g=None):
    L = x.shape[0]
    h, comb, rank, rank_t, cum, total = moe_router(x, g, sc, sh, router_w)
    plan = moe_plan(rank_t, cum, total, L)
    xs = moe_gather(h, plan)
    wplan = weight_plan(plan["tile_expert"], plan["tile_valid"])
    act = grouped_matmul(xs, [w1, w3], wplan, MOE_TM, 1024, BF16, name="swiglu_up")
    ys = grouped_matmul(act, [w2], wplan, MOE_TM, 1024, BF16, n_slots=1, name="swiglu_down")
    return moe_combine(ys, comb, rank, x, gate, plan, final_g)


def dense_ffn(x, g, sc, sh, gate, w1, w3, w2):
    L = x.shape[0]
    tm = 512
    h = norm_modulate(x, g, sc, sh, BF16)
    wplan = weight_plan(jnp.zeros((L // tm,), jnp.int32), jnp.ones((L // tm,), jnp.int32))
    act = grouped_matmul(h, [w1[None], w3[None]], wplan, tm, 512, BF16, name="swiglu_up")
    return grouped_matmul(act, [w2[None]], wplan, tm, 512, F32, x=x, gate=gate, name="swiglu_down")


def _permute_in_proj_kernel(w_ref, o_ref):
    n = w_ref.shape[0]
    o_ref[0:ORIG_DT] = w_ref[0:ORIG_DT].astype(BF16)
    o_ref[ORIG_DT:COL_DT] = w_ref[ORIG_DT + SSD_HEADS:n].astype(BF16)
    o_ref[COL_DT:COL_DT + SSD_HEADS] = w_ref[ORIG_DT:ORIG_DT + SSD_HEADS].astype(BF16)
    o_ref[COL_DT + SSD_HEADS:] = jnp.zeros((PROJ_W - COL_DT - SSD_HEADS, o_ref.shape[1]), BF16)


def _permute_in_proj(w_in, layer):
    _, d, n = w_in.shape
    tk = 256
    return pl.pallas_call(
        _permute_in_proj_kernel,
        grid=(d // tk,),
        in_specs=[pl.BlockSpec((None, n, tk), lambda i: (layer, 0, i))],
        out_specs=pl.BlockSpec((PROJ_W, tk), lambda i: (0, i)),
        out_shape=jax.ShapeDtypeStruct((PROJ_W, d), BF16),
        compiler_params=_params("parallel"),
        name="permute_in_proj",
    )(jnp.swapaxes(w_in, 1, 2))


def kernel(x, c, positions, rel_bias, w_ada, b_ada, norm_mix_g, w_in, conv_w, conv_b, dt_bias, a_log, d_skip, ssd_norm_g, moba_norm_g, diff_lambda, diff_subln_g, w_out, norm_ffn_g, dense_w1, dense_w3, dense_w2, router_w, expert_w1, expert_w3, expert_w2, final_g):
    batch, L, d = x.shape
    assert batch == 1 and d == D_MODEL
    x = x[0]
    mod = ada_modulation(c, w_ada, b_ada)
    table, pos_rows, pos_cols, near = attention_tables(positions[0], rel_bias)
    row = lambda v: v.reshape(1, -1)
    for layer in range(DEPTH):
        lambda_init = 0.8 - 0.6 * math.exp(-0.3 * layer)
        shift1, scale1, gate1, shift2, scale2, gate2 = (mod[layer, :, j * d:(j + 1) * d] for j in range(6))
        h = norm_modulate(x, row(norm_mix_g[layer]), scale1, shift1, BF16)
        proj, proj16 = in_projection(h, _permute_in_proj(w_in, layer))
        y_ssd = ssd_mixer(proj, conv_w[layer], conv_b[layer], dt_bias[layer], a_log[layer], d_skip[layer],
                          ssd_norm_g[layer])
        qa, ka, va = moba_prep(proj)
        y_moba = moba_attention(qa, ka, va, near, pos_rows, pos_cols, table, moba_norm_g[layer])
        y_diff = diff_attention(proj16, near, pos_rows, pos_cols, table, diff_lambda[layer],
                                diff_subln_g[layer], lambda_init)
        x = out_projection(y_ssd, y_moba, y_diff, w_out, layer, x, gate1)
        i = layer // 2
        g2 = row(norm_ffn_g[layer])
        if layer % 2 == 0:
            x = dense_ffn(x, g2, scale2, shift2, gate2, dense_w1[i], dense_w3[i], dense_w2[i])
        else:
            x = moe_ffn(x, g2, scale2, shift2, gate2, router_w[i], expert_w1[i], expert_w3[i], expert_w2[i],
                        final_g=row(final_g) if layer == DEPTH - 1 else None)
    if (DEPTH - 1) % 2 == 0:
        zero = jnp.zeros((1, d), F32)
        x = norm_modulate(x, row(final_g), zero, zero, F32)
    return x[None]
```

```python
import functools
import math

import jax
import jax.numpy as jnp
from jax import lax
from jax.experimental import pallas as pl
from jax.experimental.pallas import tpu as pltpu

F32 = jnp.float32
BF16 = jnp.bfloat16
HIGHEST = lax.Precision.HIGHEST

D_MODEL = 2048
DEPTH = 2
SSD_HEADS = 16
SSD_HEAD_DIM = 64
SSD_WIDTH = SSD_HEADS * SSD_HEAD_DIM
SSD_GROUPS = 2
SSD_STATE = 128
SSD_CONV = 4
SSD_CHUNK = 256
SSD_BC = 2 * SSD_GROUPS * SSD_STATE
MOBA_HEADS = 8
MOBA_HEAD_DIM = 64
MOBA_WIDTH = MOBA_HEADS * MOBA_HEAD_DIM
MOBA_BLOCK = 256
MOBA_TOPK = 3
DIFF_HEADS = 4
DIFF_QK_DIM = 64
DIFF_V_DIM = 128
DIFF_WIDTH = DIFF_HEADS * DIFF_V_DIM
REL_BUCKETS = 32
REL_MAX_DIST = 128
D_FF_DENSE = 5632
N_EXPERTS = 8
D_FF_EXPERT = 7168
EPS = 1e-6

LANES = 128
SUBLANES = 8
VMEM_LIMIT = 56 * 1024 * 1024

COL_Z = 0
COL_X = SSD_WIDTH
COL_BC = COL_X + SSD_WIDTH
COL_MQ = COL_BC + SSD_BC
COL_MK = COL_MQ + MOBA_WIDTH
COL_MV = COL_MK + MOBA_WIDTH
COL_DQ = COL_MV + MOBA_WIDTH
COL_DK = COL_DQ + DIFF_WIDTH
COL_DV = COL_DK + DIFF_WIDTH
COL_DT = COL_DV + DIFF_WIDTH
PROJ_W = COL_DT + LANES
ORIG_DT = SSD_WIDTH + SSD_WIDTH + SSD_BC

ATT_T = 256
ATT_CHUNK = 64
TOK_T = 512
MOE_TM = 256
NEG_BIG = -1e9
MOBA_MAX_BLOCKS = 32
LOG2E = math.log2(math.e)
MOBA_Q_SCALE = MOBA_HEAD_DIM ** -0.5 * LOG2E
DIFF_Q_SCALE = DIFF_QK_DIM ** -0.5 * LOG2E


def _silu(x):
    return x * (1.0 / (1.0 + jnp.exp(-x)))


def _softplus(x):
    return jnp.maximum(x, 0.0) + jnp.log1p(jnp.exp(-jnp.abs(x)))


def _params(*sem):
    return pltpu.CompilerParams(dimension_semantics=sem, vmem_limit_bytes=VMEM_LIMIT)


def _ada_kernel(c_ref, w_ref, b_ref, o_ref):
    ca = _silu(c_ref[...])
    o_ref[0] = jnp.sum(ca * w_ref[0], axis=0, keepdims=True) + b_ref[0]


def ada_modulation(c, w_ada, b_ada):
    depth, d, n = w_ada.shape
    tn = 2048
    return pl.pallas_call(
        _ada_kernel,
        grid=(depth, n // tn),
        in_specs=[pl.BlockSpec((d, 1), lambda l, j: (0, 0)),
                  pl.BlockSpec((1, d, tn), lambda l, j: (l, 0, j)),
                  pl.BlockSpec((1, 1, tn), lambda l, j: (l, 0, j))],
        out_specs=pl.BlockSpec((1, 1, tn), lambda l, j: (l, 0, j)),
        out_shape=jax.ShapeDtypeStruct((depth, 1, n), F32),
        compiler_params=_params("parallel", "parallel"),
        name="ada_modulation",
    )(c.reshape(d, 1), w_ada, b_ada.reshape(depth, 1, n))


def _norm_mod(x, g, sc, sh):
    ms = jnp.mean(x * x, axis=-1, keepdims=True)
    return (x * lax.rsqrt(ms + EPS) * g) * (1.0 + sc) + sh


def _norm_kernel(x_ref, g_ref, sc_ref, sh_ref, o_ref):
    o_ref[...] = _norm_mod(x_ref[...], g_ref[...], sc_ref[...], sh_ref[...]).astype(o_ref.dtype)


def norm_modulate(x, g, sc, sh, out_dtype):
    L, d = x.shape
    tm = 1024
    vec = pl.BlockSpec((1, d), lambda i: (0, 0))
    return pl.pallas_call(
        _norm_kernel,
        grid=(L // tm,),
        in_specs=[pl.BlockSpec((tm, d), lambda i: (i, 0)), vec, vec, vec],
        out_specs=pl.BlockSpec((tm, d), lambda i: (i, 0)),
        out_shape=jax.ShapeDtypeStruct((L, d), out_dtype),
        compiler_params=_params("parallel"),
        name="norm_modulate",
    )(x, g, sc, sh)


def _inproj_kernel(a_ref, w_ref, o32_ref, o16_ref):
    acc = lax.dot_general(a_ref[...], w_ref[...], (((1,), (1,)), ((), ())), preferred_element_type=F32)
    o32_ref[...] = acc
    o16_ref[...] = acc.astype(BF16)


def in_projection(h, w_t):
    L, k = h.shape
    n = w_t.shape[0]
    tm, tn = 1024, 1920
    return pl.pallas_call(
        _inproj_kernel,
        grid=(n // tn, L // tm),
        in_specs=[pl.BlockSpec((tm, k), lambda j, i: (i, 0)),
                  pl.BlockSpec((tn, k), lambda j, i: (j, 0))],
        out_specs=[pl.BlockSpec((tm, tn), lambda j, i: (i, j)),
                   pl.BlockSpec((tm, tn), lambda j, i: (i, j))],
        out_shape=[jax.ShapeDtypeStruct((L, n), F32), jax.ShapeDtypeStruct((L, n), BF16)],
        compiler_params=_params("parallel", "parallel"),
        name="in_projection",
    )(h, w_t)


def _outproj_kernel(ys_ref, ym_ref, yd_ref, ws_ref, wm_ref, wd_ref, x_ref, gate_ref, o_ref, wb_ref):
    @pl.when(pl.program_id(1) == 0)
    def _():
        wb_ref[0:SSD_WIDTH] = ws_ref[...].astype(BF16)
        wb_ref[SSD_WIDTH:SSD_WIDTH + MOBA_WIDTH] = wm_ref[...].astype(BF16)
        wb_ref[SSD_WIDTH + MOBA_WIDTH:] = wd_ref[...].astype(BF16)

    acc = jnp.dot(ys_ref[...], wb_ref[0:SSD_WIDTH], preferred_element_type=F32)
    acc += jnp.dot(ym_ref[...], wb_ref[SSD_WIDTH:SSD_WIDTH + MOBA_WIDTH], preferred_element_type=F32)
    acc += jnp.dot(yd_ref[...], wb_ref[SSD_WIDTH + MOBA_WIDTH:], preferred_element_type=F32)
    o_ref[...] = x_ref[...] + gate_ref[...] * acc


def out_projection(y_ssd, y_moba, y_diff, w_out, layer, x, gate):
    L, d = x.shape
    tm, tn = 512, 1024
    return pl.pallas_call(
        _outproj_kernel,
        grid=(d // tn, L // tm),
        in_specs=[pl.BlockSpec((tm, SSD_WIDTH), lambda j, i: (i, 0)),
                  pl.BlockSpec((tm, MOBA_WIDTH), lambda j, i: (i, 0)),
                  pl.BlockSpec((tm, DIFF_WIDTH), lambda j, i: (i, 0)),
                  pl.BlockSpec((None, SSD_WIDTH, tn), lambda j, i: (layer, 0, j)),
                  pl.BlockSpec((None, MOBA_WIDTH, tn), lambda j, i: (layer, SSD_WIDTH // MOBA_WIDTH, j)),
                  pl.BlockSpec((None, DIFF_WIDTH, tn),
                               lambda j, i: (layer, (SSD_WIDTH + MOBA_WIDTH) // DIFF_WIDTH, j)),
                  pl.BlockSpec((tm, tn), lambda j, i: (i, j)),
                  pl.BlockSpec((1, tn), lambda j, i: (0, j))],
        out_specs=pl.BlockSpec((tm, tn), lambda j, i: (i, j)),
        out_shape=jax.ShapeDtypeStruct((L, d), F32),
        scratch_shapes=[pltpu.VMEM((w_out.shape[1], tn), BF16)],
        compiler_params=_params("arbitrary", "arbitrary"),
        name="out_projection",
    )(y_ssd, y_moba, y_diff, w_out, w_out, w_out, x, gate)


CAST_COLS = 256
TILE_VALID = 1
TILE_NEW_WEIGHTS = 2


def weight_plan(tile_expert, tile_valid):
    prev = jnp.concatenate([jnp.full((1,), -1, jnp.int32), tile_expert[:-1]])
    first = tile_expert != prev
    ordinal = jnp.cumsum(first.astype(jnp.int32)) - 1
    n_blocks = ordinal[-1] + 1
    block_expert = jnp.zeros_like(tile_expert).at[ordinal].set(tile_expert)
    next_expert = block_expert[(ordinal + 1) % n_blocks]
    flags = tile_valid * TILE_VALID + first.astype(jnp.int32) * TILE_NEW_WEIGHTS
    return tile_expert, flags, ordinal, next_expert, n_blocks.reshape(1)


def _grouped_matmul_kernel(te_ref, tf_ref, to_ref, tx_ref, nb_ref, a_ref, *rest, n_mats, residual):
    w_hbm, rest = rest[:n_mats], rest[n_mats:]
    if residual:
        x_ref, gate_ref, o_ref, wf_ref, wb_ref, sem = rest
    else:
        o_ref, wf_ref, wb_ref, sem = rest
    n_slots = wf_ref.shape[0]
    j = pl.program_id(0)
    t = pl.program_id(1)
    tn = o_ref.shape[1]
    cw = CAST_COLS
    flags = tf_ref[t]
    new_weights = (flags & TILE_NEW_WEIGHTS) != 0
    block = j * nb_ref[0] + to_ref[t]
    slot = block % n_slots
    wraps = to_ref[t] == nb_ref[0] - 1
    has_next = jnp.logical_not(wraps & (j == pl.num_programs(0) - 1))

    def copies(expert, col_tile, dst):
        cols = pl.ds(pl.multiple_of(col_tile * tn, tn), tn)
        return [pltpu.make_async_copy(w.at[expert, :, cols], wf_ref.at[dst, m], sem.at[dst, m])
                for m, w in enumerate(w_hbm)]

    def start_next():
        for c in copies(tx_ref[t], j + wraps.astype(jnp.int32), (block + 1) % n_slots):
            c.start()

    @pl.when(new_weights)
    def _():
        @pl.when(block == 0)
        def _():
            for c in copies(te_ref[t], j, slot):
                c.start()

        if n_slots == 2:
            pl.when(has_next)(start_next)

        for c in copies(te_ref[t], j, slot):
            c.wait()

    def epilogue(accs, cols):
        if n_mats == 2:
            y = _silu(accs[0]) * accs[1]
        else:
            y = accs[0]
            if residual:
                y = x_ref[:, cols] + gate_ref[:, cols] * y
        o_ref[:, cols] = y.astype(o_ref.dtype)

    @pl.when(flags == TILE_VALID + TILE_NEW_WEIGHTS)
    def _():
        a = a_ref[...]
        for c in range(0, tn, cw):
            accs = []
            for m in range(n_mats):
                wb = wf_ref[slot, m, :, c:c + cw].astype(BF16)
                wb_ref[m, :, c:c + cw] = wb
                accs.append(jnp.dot(a, wb, preferred_element_type=F32))
            epilogue(accs, slice(c, c + cw))

    @pl.when(flags == TILE_VALID)
    def _():
        a = a_ref[...]
        epilogue([jnp.dot(a, wb_ref[m], preferred_element_type=F32) for m in range(n_mats)], slice(0, tn))

    @pl.when((flags & TILE_VALID) == 0)
    def _():
        o_ref[...] = jnp.zeros_like(o_ref)

    if n_slots == 1:
        pl.when(new_weights & has_next)(start_next)


def grouped_matmul(a, weights, plan, tm, tn, out_dtype, x=None, gate=None, n_slots=2, name="grouped_matmul"):
    rows, k = a.shape
    n = weights[0].shape[2]
    n_mats = len(weights)
    residual = x is not None
    idx = lambda f: (lambda j, t, *refs: f(j, t))
    in_specs = [pl.BlockSpec((tm, k), idx(lambda j, t: (t, 0)))] + [pl.BlockSpec(memory_space=pl.ANY)] * n_mats
    args = [a, *weights]
    if residual:
        in_specs += [pl.BlockSpec((tm, tn), idx(lambda j, t: (t, j))), pl.BlockSpec((1, tn), idx(lambda j, t: (0, j)))]
        args += [x, gate]
    grid_spec = pltpu.PrefetchScalarGridSpec(
        num_scalar_prefetch=5,
        grid=(n // tn, rows // tm),
        in_specs=in_specs,
        out_specs=pl.BlockSpec((tm, tn), idx(lambda j, t: (t, j))),
        scratch_shapes=[pltpu.VMEM((n_slots, n_mats, k, tn), F32), pltpu.VMEM((n_mats, k, tn), BF16),
                        pltpu.SemaphoreType.DMA((n_slots, n_mats))],
    )
    return pl.pallas_call(
        functools.partial(_grouped_matmul_kernel, n_mats=n_mats, residual=residual),
        grid_spec=grid_spec,
        out_shape=jax.ShapeDtypeStruct((rows, n), out_dtype),
        compiler_params=_params("arbitrary", "arbitrary"),
        name=name,
    )(*plan, *args)


def _causal_conv(cur, tail_ref, w_ref, b_ref):
    t = cur.shape[0]
    tail = tail_ref[...]
    w = w_ref[...]
    row8 = lax.broadcasted_iota(jnp.int32, (SUBLANES, cur.shape[1]), 0)
    acc = cur * w[SSD_CONV - 1:SSD_CONV]
    top = cur[0:SUBLANES] * w[SSD_CONV - 1:SSD_CONV]
    for s in range(1, SSD_CONV):
        wk = w[SSD_CONV - 1 - s:SSD_CONV - s]
        rolled = pltpu.roll(cur, s, axis=0)
        acc += rolled * wk
        top += jnp.where(row8 < s, pltpu.roll(tail, s, axis=0), rolled[0:SUBLANES]) * wk
    tail_ref[...] = cur[t - SUBLANES:t]
    return jnp.concatenate([top, acc[SUBLANES:]], axis=0) + b_ref[...]


def _bf16_terms(x):
    terms = []
    for _ in range(3):
        part = x.astype(BF16)
        terms.append(part)
        x = x - part.astype(F32)
    return terms


def _dot_exact_rhs01(x, onehot_bf16):
    return sum(jnp.dot(part, onehot_bf16, preferred_element_type=F32) for part in _bf16_terms(x))


def _dot_exact_lhs01(onehot_bf16, x):
    return sum(jnp.dot(onehot_bf16, part, preferred_element_type=F32) for part in _bf16_terms(x))


def _ssd_kernel(z_ref, x_ref, bc_ref, dt_ref, cwx_ref, cwb_ref, cbx_ref, cbb_ref, dtb_ref, alog_ref,
                dskip_ref, ng_ref, expand_ref, o_ref, tailx_ref, tailb_ref, state_ref, ybuf_ref):
    t = SSD_CHUNK
    hg = SSD_HEADS // SSD_GROUPS
    gw = SSD_WIDTH // SSD_GROUPS

    @pl.when(pl.program_id(0) == 0)
    def _():
        tailx_ref[...] = jnp.zeros_like(tailx_ref)
        tailb_ref[...] = jnp.zeros_like(tailb_ref)
        state_ref[...] = jnp.zeros_like(state_ref)

    xs = _silu(_causal_conv(x_ref[...], tailx_ref, cwx_ref, cbx_ref))
    bcm = _silu(_causal_conv(bc_ref[...], tailb_ref, cwb_ref, cbb_ref))
    dt = _softplus(dt_ref[...] + dtb_ref[...])
    a = -jnp.exp(alog_ref[...])
    row = lax.broadcasted_iota(jnp.int32, (t, t), 0)
    col = lax.broadcasted_iota(jnp.int32, (t, t), 1)
    tril = row >= col
    a_cs = _dot_exact_lhs01(jnp.where(tril, 1.0, 0.0).astype(BF16), dt * a)
    a_last = a_cs[t - 1:t]
    per_head = jnp.concatenate(
        [dt, jnp.exp(a_last - a_cs), jnp.exp(a_cs), jnp.broadcast_to(jnp.exp(a_last), (SUBLANES, LANES))], axis=0)
    spread = _dot_exact_rhs01(per_head, expand_ref[...])
    dt_x, to_end_x, ea_x, cd_x = spread[0:t], spread[t:2 * t], spread[2 * t:3 * t], spread[3 * t:3 * t + 1]
    xdt = xs * dt_x
    xdt_b = xdt.astype(BF16)
    xw_b = (xdt * to_end_x).astype(BF16)
    a_cs_t = a_cs.T

    y_off = []
    for g in range(SSD_GROUPS):
        bm = bcm[:, g * SSD_STATE:(g + 1) * SSD_STATE]
        cm_b = bcm[:, (SSD_GROUPS + g) * SSD_STATE:(SSD_GROUPS + g + 1) * SSD_STATE].astype(BF16)
        cb = lax.dot_general(cm_b, bm.astype(BF16), (((1,), (1,)), ((), ())), preferred_element_type=F32)
        h_prev = state_ref[g]
        y_off.append(jnp.dot(cm_b, h_prev.astype(BF16), preferred_element_type=F32)
                     * ea_x[:, g * gw:(g + 1) * gw])
        st_new = jnp.dot(bm.T.astype(BF16), xw_b[:, g * gw:(g + 1) * gw], preferred_element_type=F32)
        state_ref[g] = h_prev * cd_x[:, g * gw:(g + 1) * gw] + st_new
        for r in range(0, hg, 2):
            pair = []
            for h in (g * hg + r, g * hg + r + 1):
                diff = a_cs[:, h:h + 1] - a_cs_t[h:h + 1, :]
                m = (cb * jnp.exp(jnp.where(tril, diff, -jnp.inf))).astype(BF16)
                pair.append(jnp.dot(m, xdt_b[:, h * SSD_HEAD_DIM:(h + 1) * SSD_HEAD_DIM],
                                    preferred_element_type=F32))
            lo = (g * hg + r) * SSD_HEAD_DIM
            ybuf_ref[:, lo:lo + 2 * SSD_HEAD_DIM] = jnp.concatenate(pair, axis=1)

    y = ybuf_ref[...] + jnp.concatenate(y_off, axis=1) + xs * dskip_ref[...]
    y = y * _silu(z_ref[...])
    outs = []
    for g in range(SSD_GROUPS):
        yg = y[:, g * gw:(g + 1) * gw]
        outs.append(yg * lax.rsqrt(jnp.mean(yg * yg, axis=-1, keepdims=True) + EPS))
    o_ref[...] = (jnp.concatenate(outs, axis=1) * ng_ref[...]).astype(o_ref.dtype)


def ssd_mixer(proj, conv_w, conv_b, dt_bias, a_log, d_skip, norm_g):
    L = proj.shape[0]
    t = SSD_CHUNK
    assert L % t == 0

    def pad_lanes(v):
        return jnp.pad(v, (0, LANES - v.shape[0])).reshape(1, LANES)

    expand = (jnp.arange(SSD_WIDTH)[None, :] // SSD_HEAD_DIM == jnp.arange(LANES)[:, None]).astype(BF16)
    full = lambda shape: pl.BlockSpec(shape, lambda c: (0,) * len(shape))
    return pl.pallas_call(
        _ssd_kernel,
        grid=(L // t,),
        in_specs=[pl.BlockSpec((t, SSD_WIDTH), lambda c: (c, COL_Z // SSD_WIDTH)),
                  pl.BlockSpec((t, SSD_WIDTH), lambda c: (c, COL_X // SSD_WIDTH)),
                  pl.BlockSpec((t, SSD_BC), lambda c: (c, COL_BC // SSD_BC)),
                  pl.BlockSpec((t, LANES), lambda c: (c, COL_DT // LANES)),
                  full((SSD_CONV, SSD_WIDTH)), full((SSD_CONV, SSD_BC)),
                  full((1, SSD_WIDTH)), full((1, SSD_BC)),
                  full((1, LANES)), full((1, LANES)), full((1, SSD_WIDTH)), full((1, SSD_WIDTH)),
                  full((LANES, SSD_WIDTH))],
        out_specs=pl.BlockSpec((t, SSD_WIDTH), lambda c: (c, 0)),
        out_shape=jax.ShapeDtypeStruct((L, SSD_WIDTH), BF16),
        scratch_shapes=[pltpu.VMEM((SUBLANES, SSD_WIDTH), F32), pltpu.VMEM((SUBLANES, SSD_BC), F32),
                        pltpu.VMEM((SSD_GROUPS, SSD_STATE, SSD_WIDTH // SSD_GROUPS), F32),
                        pltpu.VMEM((t, SSD_WIDTH), F32)],
        compiler_params=_params("arbitrary"),
        name="ssd_mixer",
    )(proj, proj, proj, proj,
      conv_w[:, :SSD_WIDTH], conv_w[:, SSD_WIDTH:], conv_b[:SSD_WIDTH].reshape(1, -1),
      conv_b[SSD_WIDTH:].reshape(1, -1), pad_lanes(dt_bias), pad_lanes(a_log),
      jnp.repeat(d_skip, SSD_HEAD_DIM).reshape(1, -1), norm_g.reshape(1, -1), expand)


def _moba_prep_kernel(q_ref, k_ref, v_ref, qa_ref, ka_ref, va_ref, kmean_ref):
    own = pl.program_id(0)
    t = MOBA_BLOCK
    dh = MOBA_HEAD_DIM
    nbl = MOBA_MAX_BLOCKS

    @pl.when(own == 0)
    def _():
        kmean_ref[...] = jnp.zeros_like(kmean_ref)

    lane = lax.broadcasted_iota(jnp.int32, (t, LANES), 1)
    onehot = jnp.where(lane == dh + own, 1.0, 0.0)
    blk = lax.broadcasted_iota(jnp.int32, (nbl, t), 0)
    q_t = q_ref[...].T
    k = k_ref[...]
    k_mean = jnp.mean(k, axis=0, keepdims=True)
    va_ref[:, 0] = v_ref[...].T.reshape(MOBA_HEADS, dh, t).astype(BF16)
    for h in range(MOBA_HEADS):
        qh_t = q_t[h * dh:(h + 1) * dh]
        gate = jnp.dot(kmean_ref[h], qh_t, precision=HIGHEST, preferred_element_type=F32)
        gate = jnp.where(blk < own, gate, -jnp.inf)
        sel = blk >= own
        for _ in range(MOBA_TOPK):
            m = jnp.max(gate, axis=0, keepdims=True)
            idx = jnp.min(jnp.where(gate == m, blk, nbl), axis=0, keepdims=True)
            sel = sel | ((blk == idx) & (m > -jnp.inf))
            gate = jnp.where(blk == idx, -jnp.inf, gate)
        offs_t = jnp.where(sel, 0.0, NEG_BIG)
        qa_ref[h] = jnp.concatenate([qh_t * MOBA_Q_SCALE, offs_t, jnp.zeros((LANES - dh - nbl, t), F32)],
                                    axis=0).astype(BF16)
        pair = k[:, (h // 2) * LANES:(h // 2 + 1) * LANES]
        if h % 2:
            pair = pltpu.roll(pair, dh, axis=1)
        ka_ref[h] = jnp.where(lane < dh, pair, onehot).astype(BF16)
        kmean_ref[h, pl.ds(own, 1), :] = k_mean[:, h * dh:(h + 1) * dh]


def moba_prep(proj):
    L = proj.shape[0]
    t = MOBA_BLOCK
    assert L % t == 0 and L // t <= MOBA_MAX_BLOCKS
    return pl.pallas_call(
        _moba_prep_kernel,
        grid=(L // t,),
        in_specs=[pl.BlockSpec((t, MOBA_WIDTH), lambda i: (i, COL_MQ // MOBA_WIDTH)),
                  pl.BlockSpec((t, MOBA_WIDTH), lambda i: (i, COL_MK // MOBA_WIDTH)),
                  pl.BlockSpec((t, MOBA_WIDTH), lambda i: (i, COL_MV // MOBA_WIDTH))],
        out_specs=[pl.BlockSpec((MOBA_HEADS, LANES, t), lambda i: (0, 0, i)),
                   pl.BlockSpec((MOBA_HEADS, t, LANES), lambda i: (0, i, 0)),
                   pl.BlockSpec((MOBA_HEADS, 1, MOBA_HEAD_DIM, t), lambda i: (0, i, 0, 0))],
        out_shape=[jax.ShapeDtypeStruct((MOBA_HEADS, LANES, L), BF16),
                   jax.ShapeDtypeStruct((MOBA_HEADS, L, LANES), BF16),
                   jax.ShapeDtypeStruct((MOBA_HEADS, L // t, MOBA_HEAD_DIM, t), BF16)],
        scratch_shapes=[pltpu.VMEM((MOBA_HEADS, MOBA_MAX_BLOCKS, MOBA_HEAD_DIM), F32)],
        compiler_params=_params("arbitrary"),
        name="moba_prep",
    )(proj, proj, proj)


def _col_reduce(x, op):
    while x.shape[0] > SUBLANES:
        half = x.shape[0] // 2
        x = op(x[:half], x[half:])
    return jnp.max(x, axis=0, keepdims=True) if op is jnp.maximum else jnp.sum(x, axis=0, keepdims=True)


def _attention_kernel(near_ref, q_ref, k_ref, v_ref, posr_ref, posc_ref, tbl_ref, g_ref, lam_ref, o_ref,
                      m_ref, l_ref, al_ref, acc_ref, sa_ref, sb_ref, pa_ref, pb_ref, vt_ref, *, moba, lambda_init):
    hp = pl.program_id(0)
    qi = pl.program_id(1)
    nq = pl.num_programs(1)
    t = ATT_T
    m_ref[...] = jnp.full_like(m_ref, -jnp.inf)
    l_ref[...] = jnp.zeros_like(l_ref)
    acc_ref[...] = jnp.zeros_like(acc_ref)
    if moba:
        q_t = [q_ref[0], q_ref[1]]
        heads = [2 * hp, 2 * hp + 1]
    else:
        q = (q_ref[...].astype(F32) * DIFF_Q_SCALE).T.astype(BF16)
        half = lax.broadcasted_iota(jnp.int32, q.shape, 0) < DIFF_QK_DIM
        q_t = [jnp.where(half, q, jnp.zeros_like(q)), jnp.where(half, jnp.zeros_like(q), q)]
        heads = [MOBA_HEADS + hp]

        @pl.when(qi == 0)
        def _():
            def transpose_tile(i, carry):
                vt_ref[i] = v_ref[pl.ds(pl.multiple_of(i * t, t), t), :].astype(F32).T.astype(BF16)
                return carry

            lax.fori_loop(0, nq, transpose_tile, 0)
    trow = [tbl_ref[pl.ds(hd, 1), :] for hd in heads]
    posq = posr_ref[pl.ds(qi, 1), :]

    def scores(ki, dst_ref):
        rows = pl.ds(pl.multiple_of(ki * t, t), t)
        for s in range(2):
            k = k_ref[s, rows, :] if moba else k_ref[rows, :]
            dst_ref[s] = jnp.dot(k, q_t[s], preferred_element_type=F32)

    def accumulate(ki, p_ref):
        for s in range(2):
            v_t = v_ref[s, ki] if moba else vt_ref[ki]
            pv = jnp.dot(v_t, p_ref[s], preferred_element_type=F32)
            acc_ref[s] = al_ref[s] * acc_ref[s] + pv

    scores(0, sa_ref)
    pb_ref[...] = jnp.zeros_like(pb_ref)
    al_ref[...] = jnp.ones_like(al_ref)

    def tile(ki, general, cur_ref, nxt_ref, p_ref, p_prev_ref, diagonal=False):
        scores(jnp.minimum(ki + 1, qi), nxt_ref)
        accumulate(jnp.maximum(ki - 1, 0), p_prev_ref)
        ch = ATT_CHUNK
        mx = [None, None]
        for r in range(0, t, ch):
            if general:
                dist = jnp.clip(posq - posc_ref[ki, r:r + ch, :], 0, LANES - 1)
                bias = [jnp.concatenate(
                    [jnp.take_along_axis(jnp.broadcast_to(tr, (ch, LANES)), dist[:, j * LANES:(j + 1) * LANES],
                                         axis=1) for j in range(t // LANES)], axis=1) for tr in trow]
                if diagonal:
                    key = lax.broadcasted_iota(jnp.int32, (ch, t), 0) + r
                    qry = lax.broadcasted_iota(jnp.int32, (ch, t), 1)
                    causal = key <= qry
            for s in range(2):
                blk = cur_ref[s, r:r + ch, :]
                if general:
                    blk = blk + bias[s if moba else 0]
                    if diagonal:
                        blk = jnp.where(causal, blk, -jnp.inf)
                    cur_ref[s, r:r + ch, :] = blk
                mx[s] = blk if mx[s] is None else jnp.maximum(mx[s], blk)
        for s in range(2):
            m_old = m_ref[s]
            m_tile = _col_reduce(mx[s], jnp.maximum)
            if general:
                m_new = jnp.maximum(m_old, m_tile)
                shift = m_new
            else:
                c = trow[s if moba else 0][:, LANES - 1:LANES]
                m_new = jnp.maximum(m_old, m_tile + c)
                shift = m_new - c
            alpha = jnp.exp2(m_old - m_new)
            sm = None
            for r in range(0, t, ch):
                p = jnp.exp2(cur_ref[s, r:r + ch, :] - shift)
                p_ref[s, r:r + ch, :] = p.astype(BF16)
                sm = p if sm is None else sm + p
            l_ref[s] = alpha * l_ref[s] + _col_reduce(sm, jnp.add)
            al_ref[s] = alpha
            m_ref[s] = m_new

    def step(ki, *bufs):
        flag = near_ref[qi * nq + ki]

        @pl.when(flag == 0)
        def _():
            tile(ki, False, *bufs)

        @pl.when((flag != 0) & (ki < qi))
        def _():
            tile(ki, True, *bufs)

        @pl.when(ki == qi)
        def _():
            tile(ki, True, *bufs, diagonal=True)

    def pair(j, carry):
        step(2 * j, sa_ref, sb_ref, pa_ref, pb_ref)
        step(2 * j + 1, sb_ref, sa_ref, pb_ref, pa_ref)
        return carry

    lax.fori_loop(0, (qi + 1) // 2, pair, 0)

    @pl.when(qi % 2 == 0)
    def _():
        step(qi, sa_ref, sb_ref, pa_ref, pb_ref)
        accumulate(qi, pa_ref)

    @pl.when(qi % 2 == 1)
    def _():
        accumulate(qi, pb_ref)

    if moba:
        outs = []
        for s in range(2):
            o = acc_ref[s] / l_ref[s]
            ms = jnp.sum(o * o, axis=0, keepdims=True) * (1.0 / MOBA_HEAD_DIM)
            outs.append(o * lax.rsqrt(ms + EPS) * g_ref[s])
        y_t = jnp.concatenate(outs, axis=0)
    else:
        lp = lam_ref[...]
        lam = (jnp.exp(jnp.sum(lp[0:1] * lp[1:2], axis=1, keepdims=True))
               - jnp.exp(jnp.sum(lp[2:3] * lp[3:4], axis=1, keepdims=True)) + lambda_init)
        o = acc_ref[0] / l_ref[0] - lam * (acc_ref[1] / l_ref[1])
        ms = jnp.sum(o * o, axis=0, keepdims=True) * (1.0 / DIFF_V_DIM)
        y_t = (o * lax.rsqrt(ms + EPS) * g_ref[0]) * (1.0 - lambda_init)
    o_ref[...] = y_t.T.astype(o_ref.dtype)


def _attention_call(kernel, steps, dv, L, near, in_specs, args, name, vt_scratch=False):
    t = ATT_T
    full = lambda a: pl.BlockSpec(a.shape, lambda h, i, nr: (0,) * a.ndim)
    grid_spec = pltpu.PrefetchScalarGridSpec(
        num_scalar_prefetch=1,
        grid=(steps, L // t),
        in_specs=in_specs + [full(a) for a in args[len(in_specs):]],
        out_specs=pl.BlockSpec((t, LANES), lambda h, i, nr: (i, h)),
        scratch_shapes=[pltpu.VMEM((2, 1, t), F32), pltpu.VMEM((2, 1, t), F32), pltpu.VMEM((2, 1, t), F32),
                        pltpu.VMEM((2, dv, t), F32), pltpu.VMEM((2, t, t), F32), pltpu.VMEM((2, t, t), F32),
                        pltpu.VMEM((2, t, t), BF16), pltpu.VMEM((2, t, t), BF16),
                        pltpu.VMEM((L // t if vt_scratch else 1, dv, t), BF16)],
    )
    return pl.pallas_call(kernel, grid_spec=grid_spec, out_shape=jax.ShapeDtypeStruct((L, steps * LANES), BF16),
                          compiler_params=_params("parallel", "arbitrary"), name=name)(near, *args)


def moba_attention(qa_t, ka, va_t, near, pos_rows, pos_cols, table, norm_g):
    heads, L, _ = ka.shape
    t = ATT_T
    kernel = functools.partial(_attention_kernel, moba=True, lambda_init=None)
    g = norm_g.reshape(heads, MOBA_HEAD_DIM, 1)
    return _attention_call(
        kernel, heads // 2, MOBA_HEAD_DIM, L, near,
        [pl.BlockSpec((2, LANES, t), lambda h, i, nr: (h, 0, i)),
         pl.BlockSpec((2, L, LANES), lambda h, i, nr: (h, 0, 0)),
         pl.BlockSpec((2, L // t, MOBA_HEAD_DIM, t), lambda h, i, nr: (h, 0, 0, 0)),
         pl.BlockSpec(pos_rows.shape, lambda h, i, nr: (0, 0)),
         pl.BlockSpec(pos_cols.shape, lambda h, i, nr: (0, 0, 0)),
         pl.BlockSpec(table.shape, lambda h, i, nr: (0, 0)),
         pl.BlockSpec((2, MOBA_HEAD_DIM, 1), lambda h, i, nr: (h, 0, 0))],
        [qa_t, ka, va_t, pos_rows, pos_cols, table, g, jnp.zeros((4, DIFF_QK_DIM), F32)], "moba_attention")


def diff_attention(proj16, near, pos_rows, pos_cols, table, lam_params, subln_g, lambda_init):
    L = proj16.shape[0]
    t = ATT_T
    kernel = functools.partial(_attention_kernel, moba=False, lambda_init=lambda_init)
    return _attention_call(
        kernel, DIFF_HEADS, DIFF_V_DIM, L, near,
        [pl.BlockSpec((t, LANES), lambda h, i, nr: (i, COL_DQ // LANES + h)),
         pl.BlockSpec((L, LANES), lambda h, i, nr: (0, COL_DK // LANES + h)),
         pl.BlockSpec((L, LANES), lambda h, i, nr: (0, COL_DV // LANES + h))],
        [proj16, proj16, proj16, pos_rows, pos_cols, table, subln_g.reshape(1, DIFF_V_DIM, 1), lam_params],
        "diff_attention", vt_scratch=True)


def _rel_bucket(dist):
    n = jnp.maximum(dist, 0)
    max_exact = REL_BUCKETS // 2
    nf = jnp.maximum(n, 1).astype(F32)
    large = max_exact + (jnp.log(nf / max_exact) / math.log(REL_MAX_DIST / max_exact)
                         * (REL_BUCKETS - max_exact)).astype(jnp.int32)
    return jnp.where(n < max_exact, n, jnp.minimum(large, REL_BUCKETS - 1))


def attention_tables(positions, rel_bias):
    L = positions.shape[0]
    t = ATT_T
    buckets = _rel_bucket(jnp.arange(LANES, dtype=jnp.int32))
    table = rel_bias[buckets].T * LOG2E
    pos_rows = positions.reshape(L // t, t)
    lo, hi = jnp.min(pos_rows, axis=1), jnp.max(pos_rows, axis=1)
    near = (lo[:, None] - hi[None, :] < LANES) | jnp.eye(L // t, dtype=bool)
    return table, pos_rows, positions.reshape(L // t, t, 1), near.astype(jnp.int32).reshape(-1)


def _router_kernel(x_ref, g_ref, sc_ref, sh_ref, rw_ref, h_ref, comb_ref, rank_ref, rank_t_ref, cum_ref, total_ref,
                   cnt_ref):
    t = TOK_T

    @pl.when(pl.program_id(0) == 0)
    def _():
        cnt_ref[...] = jnp.zeros_like(cnt_ref)

    h = _norm_mod(x_ref[...], g_ref[...], sc_ref[...], sh_ref[...])
    h_ref[...] = h.astype(BF16)
    lane = lax.broadcasted_iota(jnp.int32, (t, LANES), 1)
    logits = jnp.dot(h, rw_ref[...], precision=HIGHEST, preferred_element_type=F32)
    logits = jnp.where(lane < N_EXPERTS, logits, -jnp.inf)
    m1 = jnp.max(logits, axis=1, keepdims=True)
    i1 = jnp.min(jnp.where(logits == m1, lane, LANES), axis=1, keepdims=True)
    rest = jnp.where(lane == i1, -jnp.inf, logits)
    m2 = jnp.max(rest, axis=1, keepdims=True)
    i2 = jnp.min(jnp.where(rest == m2, lane, LANES), axis=1, keepdims=True)
    e2 = jnp.exp(m2 - m1)
    denom = 1.0 + e2
    comb_ref[...] = jnp.where(lane == i1, 1.0 / denom, 0.0) + jnp.where(lane == i2, e2 / denom, 0.0)
    sel = jnp.where((lane == i1) | (lane == i2), 1.0, 0.0)
    row = lax.broadcasted_iota(jnp.int32, (t, t), 0)
    col = lax.broadcasted_iota(jnp.int32, (t, t), 1)
    before = jnp.dot(jnp.where(row > col, 1.0, 0.0).astype(BF16), sel.astype(BF16), preferred_element_type=F32)
    carry = cnt_ref[...]
    cum_ref[0] = carry
    rank = jnp.where(sel > 0.0, before + carry, -1.0)
    rank_ref[...] = rank
    rank_t_ref[...] = rank.T
    carry = carry + jnp.sum(sel, axis=0, keepdims=True)
    cnt_ref[...] = carry
    total_ref[...] = carry


def moe_router(x, g, sc, sh, router_w):
    L, d = x.shape
    t = TOK_T
    rw = jnp.pad(router_w, ((0, 0), (0, LANES - N_EXPERTS)))
    vec = pl.BlockSpec((1, d), lambda i: (0, 0))
    tok = pl.BlockSpec((t, LANES), lambda i: (i, 0))
    return pl.pallas_call(
        _router_kernel,
        grid=(L // t,),
        in_specs=[pl.BlockSpec((t, d), lambda i: (i, 0)), vec, vec, vec,
                  pl.BlockSpec((d, LANES), lambda i: (0, 0))],
        out_specs=[pl.BlockSpec((t, d), lambda i: (i, 0)), tok, tok,
                   pl.BlockSpec((LANES, t), lambda i: (0, i)),
                   pl.BlockSpec((1, 1, LANES), lambda i: (i, 0, 0)),
                   pl.BlockSpec((1, LANES), lambda i: (0, 0))],
        out_shape=[jax.ShapeDtypeStruct((L, d), BF16), jax.ShapeDtypeStruct((L, LANES), F32),
                   jax.ShapeDtypeStruct((L, LANES), F32), jax.ShapeDtypeStruct((LANES, L), F32),
                   jax.ShapeDtypeStruct((L // t, 1, LANES), F32), jax.ShapeDtypeStruct((1, LANES), F32)],
        scratch_shapes=[pltpu.VMEM((1, LANES), F32)],
        compiler_params=_params("arbitrary"),
        name="moe_router",
    )(x, g, sc, sh, rw)


def _item_lists(hit, n_items):
    rows, cols = hit.shape
    running = jnp.cumsum(hit.reshape(-1).astype(jnp.int32))
    n_real = running[-1]
    k = jnp.arange(n_items, dtype=jnp.int32)
    real = k < n_real
    idx = jnp.searchsorted(running, jnp.minimum(k + 1, n_real), side="left", method="compare_all")
    idx = jnp.minimum(idx, rows * cols - 1).astype(jnp.int32)
    r, c = idx // cols, idx % cols
    prev_r = jnp.concatenate([jnp.full((1,), -1, jnp.int32), r[:-1]])
    next_r = jnp.concatenate([r[1:], jnp.full((1,), -1, jnp.int32)])
    first = real & (r != prev_r)
    last = real & ((r != next_r) | (k == n_real - 1))
    flags = real.astype(jnp.int32) + 2 * first.astype(jnp.int32) + 4 * last.astype(jnp.int32)
    return r, c, flags


def moe_plan(rank_t, cum, total, L):
    tm, tb = MOE_TM, TOK_T
    nb = L // tb
    n_tiles = 2 * L // tm + N_EXPERTS
    counts = total[0, :N_EXPERTS].astype(jnp.int32)
    tiles_e = (counts + tm - 1) // tm
    tile_end = jnp.cumsum(tiles_e)
    tile_start = tile_end - tiles_e
    tid = jnp.arange(n_tiles, dtype=jnp.int32)
    tile_valid = tid < tile_end[-1]
    tile_expert = jnp.minimum(jnp.searchsorted(tile_end, tid, side="right", method="compare_all"),
                              N_EXPERTS - 1).astype(jnp.int32)
    local_row = (tid - tile_start[tile_expert]) * tm
    r = rank_t[:N_EXPERTS].astype(jnp.int32)
    dest = jnp.where(r >= 0, r + (tile_start * tm)[:, None], -1)
    cum_i = cum[:, 0, :N_EXPERTS].astype(jnp.int32)
    cum_next = jnp.concatenate([cum_i[1:], counts[None, :]], axis=0)
    lo = cum_i[:, tile_expert].T
    hi = cum_next[:, tile_expert].T
    hit = tile_valid[:, None] & (lo < (local_row + tm)[:, None]) & (hi > local_row[:, None])
    n_items = n_tiles + N_EXPERTS * nb
    pad_hit = hit | ((~tile_valid)[:, None] & (jnp.arange(nb) == 0)[None, :])
    gather_items = _item_lists(pad_hit, n_items)
    kb, tt, fl = _item_lists(hit.T, n_items)
    return dict(tile_expert=tile_expert, tile_valid=tile_valid.astype(jnp.int32),
                dest_rows=dest, segment_start=(tile_start * tm).astype(jnp.int32),
                gather_items=gather_items, combine_items=(tt, kb, fl), n_tiles=n_tiles, n_items=n_items)


def _moe_gather_kernel(it_ref, ib_ref, if_ref, te_ref, dest_ref, h_ref, o_ref, acc_ref):
    i = pl.program_id(0)
    flag = if_ref[i]
    tile = it_ref[i]
    tm = MOE_TM

    @pl.when((flag & 2) != 0)
    def _():
        acc_ref[...] = jnp.zeros_like(acc_ref)

    @pl.when((flag & 1) != 0)
    def _():
        d = dest_ref[pl.ds(te_ref[tile], 1), :]
        rows = tile * tm + lax.broadcasted_iota(jnp.int32, (tm, 1), 0)
        onehot = jnp.where(d == rows, 1.0, 0.0).astype(BF16)
        acc_ref[...] += jnp.dot(onehot, h_ref[...], preferred_element_type=F32)

    @pl.when((flag & 4) != 0)
    def _():
        o_ref[...] = acc_ref[...].astype(o_ref.dtype)


def moe_gather(h, plan):
    L, d = h.shape
    tm, tb = MOE_TM, TOK_T
    it, ib, fl = plan["gather_items"]
    grid_spec = pltpu.PrefetchScalarGridSpec(
        num_scalar_prefetch=4,
        grid=(plan["n_items"],),
        in_specs=[pl.BlockSpec((N_EXPERTS, tb), lambda i, it, ib, fl, te: (0, ib[i])),
                  pl.BlockSpec((tb, d), lambda i, it, ib, fl, te: (ib[i], 0))],
        out_specs=pl.BlockSpec((tm, d), lambda i, it, ib, fl, te: (it[i], 0)),
        scratch_shapes=[pltpu.VMEM((tm, d), F32)],
    )
    return pl.pallas_call(
        _moe_gather_kernel,
        grid_spec=grid_spec,
        out_shape=jax.ShapeDtypeStruct((plan["n_tiles"] * tm, d), BF16),
        compiler_params=_params("arbitrary"),
        name="moe_gather",
    )(it, ib, fl, plan["tile_expert"], plan["dest_rows"], h)


def _moe_combine_kernel(it_ref, ib_ref, if_ref, te_ref, ts_ref, rank_ref, w_ref, ys_ref, x_ref, gate_ref, fg_ref,
                        o_ref, acc_ref, *, final_norm):
    i = pl.program_id(0)
    flag = if_ref[i]
    tile = it_ref[i]
    tm = MOE_TM

    @pl.when((flag & 2) != 0)
    def _():
        acc_ref[...] = jnp.zeros_like(acc_ref)

    @pl.when((flag & 1) != 0)
    def _():
        e = te_ref[tile]
        mine = lax.broadcasted_iota(jnp.int32, rank_ref.shape, 1) == e
        rank = jnp.sum(jnp.where(mine, rank_ref[...], 0.0), axis=1, keepdims=True)
        w = jnp.sum(jnp.where(mine, w_ref[...], 0.0), axis=1, keepdims=True)
        first = tile * tm - ts_ref[e]
        cols = (first + lax.broadcasted_iota(jnp.int32, (1, tm), 1)).astype(F32)
        onehot = jnp.where(rank == cols, 1.0, 0.0).astype(BF16)
        acc_ref[...] += w * jnp.dot(onehot, ys_ref[...], preferred_element_type=F32)

    @pl.when((flag & 4) != 0)
    def _():
        y = x_ref[...] + gate_ref[...] * acc_ref[...]
        if final_norm:
            y = y * lax.rsqrt(jnp.mean(y * y, axis=-1, keepdims=True) + EPS) * fg_ref[...]
        o_ref[...] = y


def moe_combine(ys, comb, rank, x, gate, plan, final_g=None):
    L, d = x.shape
    final_norm = final_g is not None
    if not final_norm:
        final_g = jnp.ones((1, d), F32)
    tm, tb = MOE_TM, TOK_T
    it, ib, fl = plan["combine_items"]
    tok = lambda i, it, ib, *_: (ib[i], 0)
    fixed = lambda i, *_: (0, 0)
    grid_spec = pltpu.PrefetchScalarGridSpec(
        num_scalar_prefetch=5,
        grid=(plan["n_items"],),
        in_specs=[pl.BlockSpec((tb, LANES), tok), pl.BlockSpec((tb, LANES), tok),
                  pl.BlockSpec((tm, d), lambda i, it, *_: (it[i], 0)),
                  pl.BlockSpec((tb, d), tok), pl.BlockSpec((1, d), fixed), pl.BlockSpec((1, d), fixed)],
        out_specs=pl.BlockSpec((tb, d), tok),
        scratch_shapes=[pltpu.VMEM((tb, d), F32)],
    )
    return pl.pallas_call(
        functools.partial(_moe_combine_kernel, final_norm=final_norm),
        grid_spec=grid_spec,
        out_shape=jax.ShapeDtypeStruct((L, d), F32),
        compiler_params=_params("arbitrary"),
        name="moe_combine",
    )(it, ib, fl, plan["tile_expert"], plan["segment_start"], rank, comb, ys, x, gate, final_g)


def moe_ffn(x, g, sc, sh, gate, router_w, w1, w3, w2, final_g=None):
    L = x.shape[0]
    h, comb, rank, rank_t, cum, total = moe_router(x, g, sc, sh, router_w)
    plan = moe_plan(rank_t, cum, total, L)
    xs = moe_gather(h, plan)
    wplan = weight_plan(plan["tile_expert"], plan["tile_valid"])
    act = grouped_matmul(xs, [w1, w3], wplan, MOE_TM, 1024, BF16, name="swiglu_up")
    ys = grouped_matmul(act, [w2], wplan, MOE_TM, 1024, BF16, n_slots=1, name="swiglu_down")
    return moe_combine(ys, comb, rank, x, gate, plan, final_g)


def dense_ffn(x, g, sc, sh, gate, w1, w3, w2):
    L = x.shape[0]
    tm = 512
    h = norm_modulate(x, g, sc, sh, BF16)
    wplan = weight_plan(jnp.zeros((L // tm,), jnp.int32), jnp.ones((L // tm,), jnp.int32))
    act = grouped_matmul(h, [w1[None], w3[None]], wplan, tm, 512, BF16, name="swiglu_up")
    return grouped_matmul(act, [w2[None]], wplan, tm, 512, F32, x=x, gate=gate, name="swiglu_down")


def _permute_in_proj_kernel(w_ref, o_ref):
    n = w_ref.shape[0]
    o_ref[0:ORIG_DT] = w_ref[0:ORIG_DT].astype(BF16)
    o_ref[ORIG_DT:COL_DT] = w_ref[ORIG_DT + SSD_HEADS:n].astype(BF16)
    o_ref[COL_DT:COL_DT + SSD_HEADS] = w_ref[ORIG_DT:ORIG_DT + SSD_HEADS].astype(BF16)
    o_ref[COL_DT + SSD_HEADS:] = jnp.zeros((PROJ_W - COL_DT - SSD_HEADS, o_ref.shape[1]), BF16)


def _permute_in_proj(w_in, layer):
    _, d, n = w_in.shape
    tk = 256
    return pl.pallas_call(
        _permute_in_proj_kernel,
        grid=(d // tk,),
        in_specs=[pl.BlockSpec((None, n, tk), lambda i: (layer, 0, i))],
        out_specs=pl.BlockSpec((PROJ_W, tk), lambda i: (0, i)),
        out_shape=jax.ShapeDtypeStruct((PROJ_W, d), BF16),
        compiler_params=_params("parallel"),
        name="permute_in_proj",
    )(jnp.swapaxes(w_in, 1, 2))


def kernel(x, c, positions, rel_bias, w_ada, b_ada, norm_mix_g, w_in, conv_w, conv_b, dt_bias, a_log, d_skip, ssd_norm_g, moba_norm_g, diff_lambda, diff_subln_g, w_out, norm_ffn_g, dense_w1, dense_w3, dense_w2, router_w, expert_w1, expert_w3, expert_w2, final_g):
    batch, L, d = x.shape
    assert batch == 1 and d == D_MODEL
    x = x[0]
    mod = ada_modulation(c, w_ada, b_ada)
    table, pos_rows, pos_cols, near = attention_tables(positions[0], rel_bias)
    row = lambda v: v.reshape(1, -1)
    for layer in range(DEPTH):
        lambda_init = 0.8 - 0.6 * math.exp(-0.3 * layer)
        shift1, scale1, gate1, shift2, scale2, gate2 = (mod[layer, :, j * d:(j + 1) * d] for j in range(6))
        h = norm_modulate(x, row(norm_mix_g[layer]), scale1, shift1, BF16)
        proj, proj16 = in_projection(h, _permute_in_proj(w_in, layer))
        y_ssd = ssd_mixer(proj, conv_w[layer], conv_b[layer], dt_bias[layer], a_log[layer], d_skip[layer],
                          ssd_norm_g[layer])
        qa, ka, va = moba_prep(proj)
        y_moba = moba_attention(qa, ka, va, near, pos_rows, pos_cols, table, moba_norm_g[layer])
        y_diff = diff_attention(proj16, near, pos_rows, pos_cols, table, diff_lambda[layer],
                                diff_subln_g[layer], lambda_init)
        x = out_projection(y_ssd, y_moba, y_diff, w_out, layer, x, gate1)
        i = layer // 2
        g2 = row(norm_ffn_g[layer])
        if layer % 2 == 0:
            x = dense_ffn(x, g2, scale2, shift2, gate2, dense_w1[i], dense_w3[i], dense_w2[i])
        else:
            x = moe_ffn(x, g2, scale2, shift2, gate2, router_w[i], expert_w1[i], expert_w3[i], expert_w2[i],
                        final_g=row(final_g) if layer == DEPTH - 1 else None)
    if (DEPTH - 1) % 2 == 0:
        zero = jnp.zeros((1, d), F32)
        x = norm_modulate(x, row(final_g), zero, zero, F32)
    return x[None]
```
